```python
import math
import jax, jax.numpy as jnp
from jax import lax
import numpy as np

D_MODEL = 1024
BATCH = 16
SEQ = 256
DEPTH = 2
DEC_BATCH = 8
DEC_SEQ = 1024
PAST_LEN = 256

GRID_W = 64
N_EVEN = (DEPTH + 1) // 2
N_ODD = DEPTH // 2
EPS = 1e-6
Q_BLOCK = 128
ROPE_THETA = 10000.0

SSD_HEADS = 16
SSD_HEADDIM = 64
SSD_INNER = SSD_HEADS * SSD_HEADDIM
SSD_GROUPS = 2
SSD_STATE = 64
SSD_CONV = 3
SSD_CHUNK = 128
SSD_CONV_DIM = SSD_INNER + 2 * SSD_GROUPS * SSD_STATE

MLA_HEADS = 8
MLA_Q_LORA = 256
MLA_KV_LORA = 128
MLA_NOPE = 64
MLA_ROPE = 32
MLA_V = 64
MLA_QK = MLA_NOPE + MLA_ROPE

EVEN_SPLITS = (SSD_INNER,
               SSD_INNER + SSD_CONV_DIM,
               SSD_INNER + SSD_CONV_DIM + SSD_HEADS,
               SSD_INNER + SSD_CONV_DIM + SSD_HEADS + MLA_Q_LORA)
E_IN = EVEN_SPLITS[-1] + MLA_KV_LORA + MLA_ROPE
E_OUT = SSD_INNER + MLA_HEADS * MLA_V

GQA_HEADS = 16
GQA_KV_HEADS = 4
GQA_HEAD_DIM = 64
ODD_SPLITS = (GQA_HEADS * GQA_HEAD_DIM, (GQA_HEADS + GQA_KV_HEADS) * GQA_HEAD_DIM)
O_IN = (GQA_HEADS + 2 * GQA_KV_HEADS) * GQA_HEAD_DIM
O_OUT = GQA_HEADS * GQA_HEAD_DIM

N_EXPERTS = 16
N_GROUPS = 4
GROUP_SIZE = N_EXPERTS // N_GROUPS
TOP_K = 2
D_EXPERT = 256

kernel_name = 'hybrid_ssd_mla_gqa_moe_diffusion_step'


def rms_norm(x, g):
    xf = x.astype(jnp.float32)
    y = xf * lax.rsqrt(jnp.mean(xf * xf, axis=-1, keepdims=True) + EPS)
    return (y * g.astype(jnp.float32)).astype(x.dtype)


def split_heads(x, n):
    return x.reshape(x.shape[:-1] + (n, x.shape[-1] // n))


def rope_1d(x, pos):
    half = x.shape[-1] // 2
    inv = jnp.power(ROPE_THETA, -jnp.arange(half, dtype=jnp.float32) / half)
    ang = pos.astype(jnp.float32)[:, None] * inv[None, :]
    cos = jnp.cos(ang)[:, None, :]
    sin = jnp.sin(ang)[:, None, :]
    x1, x2 = x[..., :half], x[..., half:]
    return jnp.concatenate([x1 * cos - x2 * sin, x2 * cos + x1 * sin], axis=-1).astype(x.dtype)


def rope_2d(x, row, col):
    a = x.shape[-1] // 2
    return jnp.concatenate([rope_1d(x[..., :a], row), rope_1d(x[..., a:], col)], axis=-1)


def rope_tail(x, n, row, col):
    return jnp.concatenate([x[..., :-n], rope_2d(x[..., -n:], row, col)], axis=-1)


def block_attention(q, k, v):
    b, lq, h, d = q.shape
    hk = k.shape[2]
    r = h // hk
    nb = lq // Q_BLOCK
    scale = d ** -0.5
    qb = q.reshape(b, nb, Q_BLOCK, hk, r, d).transpose(1, 0, 2, 3, 4, 5)

    def one_block(qi):
        s = jnp.einsum('bqgrd,bkgd->bgrqk', qi, k, preferred_element_type=jnp.float32) * scale
        p = jax.nn.softmax(s, axis=-1).astype(v.dtype)
        return jnp.einsum('bgrqk,bkgv->bqgrv', p, v)

    o = lax.map(one_block, qb)
    return o.transpose(1, 0, 2, 3, 4, 5).reshape(b, lq, h, v.shape[-1])


def dw_conv(x, w, bias):
    out = lax.conv_general_dilated(
        x, w[:, None, :].astype(x.dtype), window_strides=(1,),
        padding=[(SSD_CONV // 2, SSD_CONV // 2)],
        dimension_numbers=('NWC', 'WIO', 'NWC'), feature_group_count=x.shape[-1])
    return out + bias


def ssd_scan(x, dt, a, bmat, cmat, h0):
    b, l, h, p = x.shape
    g, n = bmat.shape[2], bmat.shape[3]
    nc = l // SSD_CHUNK
    q = SSD_CHUNK
    rep = h // g
    bh = jnp.repeat(bmat, rep, axis=2).astype(jnp.float32).reshape(b, nc, q, h, n)
    ch = jnp.repeat(cmat, rep, axis=2).astype(jnp.float32).reshape(b, nc, q, h, n)
    dtc = dt.reshape(b, nc, q, h)
    xdt = x.astype(jnp.float32).reshape(b, nc, q, h, p) * dtc[..., None]
    cum = jnp.cumsum(dtc * a, axis=2)
    seg = cum[:, :, :, None, :] - cum[:, :, None, :, :]
    mask = jnp.tril(jnp.ones((q, q), dtype=bool))[:, :, None]
    decay = jnp.exp(jnp.where(mask, seg, -jnp.inf))
    scores = jnp.einsum('bcihn,bcjhn->bcijh', ch, bh) * decay
    y_diag = jnp.einsum('bcijh,bcjhp->bcihp', scores, xdt)
    decay_end = jnp.exp(cum[:, :, -1:, :] - cum)
    states = jnp.einsum('bcjhn,bcjh,bcjhp->bchpn', bh, decay_end, xdt)
    chunk_decay = jnp.exp(cum[:, :, -1, :])

    def step(s, inp):
        st, dc = inp
        return s * dc[:, :, None, None] + st, s

    final, s_in = lax.scan(step, h0.astype(jnp.float32),
                           (jnp.moveaxis(states, 1, 0), jnp.moveaxis(chunk_decay, 1, 0)))
    s_in = jnp.moveaxis(s_in, 0, 1)
    y_off = jnp.einsum('bcihn,bchpn,bcih->bcihp', ch, s_in, jnp.exp(cum))
    return (y_diag + y_off).reshape(b, l, h, p), final


def ssd_mixer(z, xbc, dt_raw, p, h0_f, h0_b):
    b, l, _ = z.shape
    xbc = jax.nn.silu(dw_conv(xbc, p['conv_w'], p['conv_b']))
    xs, bm, cm = jnp.split(xbc, (SSD_INNER, SSD_INNER + SSD_GROUPS * SSD_STATE), axis=-1)
    x = xs.reshape(b, l, SSD_HEADS, SSD_HEADDIM)
    bm = bm.reshape(b, l, SSD_GROUPS, SSD_STATE)
    cm = cm.reshape(b, l, SSD_GROUPS, SSD_STATE)
    dtf = dt_raw.astype(jnp.float32)
    a_f = -jnp.exp(p['a_log_f'].astype(jnp.float32))
    a_b = -jnp.exp(p['a_log_b'].astype(jnp.float32))
    y_f, s_f = ssd_scan(x, jax.nn.softplus(dtf + p['dt_bias_f'].astype(jnp.float32)), a_f, bm, cm, h0_f)
    rev = lambda t: jnp.flip(t, axis=1)
    y_b, s_b = ssd_scan(rev(x), jax.nn.softplus(rev(dtf) + p['dt_bias_b'].astype(jnp.float32)),
                        a_b, rev(bm), rev(cm), h0_b)
    y = y_f + rev(y_b) + x.astype(jnp.float32) * p['d_skip'].astype(jnp.float32)[:, None]
    y = y.reshape(b, l, SSD_INNER).astype(z.dtype)
    y = rms_norm(y * jax.nn.silu(z), p['ssd_norm'])
    return y, s_f.astype(z.dtype), s_b.astype(z.dtype)


def mla_queries(q_lat, p):
    q = split_heads(rms_norm(q_lat, p['q_a_norm']) @ p['w_q_b'], MLA_HEADS)
    return rms_norm(q, p['q_norm'])


def mla_keys_values(ckv, kpe, p):
    kv = split_heads(ckv @ p['w_kv_b'], MLA_HEADS)
    k_nope, v = jnp.split(kv, (MLA_NOPE,), axis=-1)
    k_pe = jnp.broadcast_to(kpe[:, :, None, :], k_nope.shape[:-1] + (MLA_ROPE,))
    k = rms_norm(jnp.concatenate([k_nope, k_pe], axis=-1), p['k_norm'])
    return k, v


def even_mixer(h, p, pos, ctx):
    b, l, _ = h.shape
    z, xbc, dt_raw, q_lat, kv_a = jnp.split(h @ p['w_in'], EVEN_SPLITS, axis=-1)
    if ctx is None:
        h0 = jnp.zeros((b, SSD_HEADS, SSD_HEADDIM, SSD_STATE), jnp.float32)
        h0_f, h0_b = h0, h0
    else:
        h0_f, h0_b = ctx[0], ctx[1]
    y_ssd, s_f, s_b = ssd_mixer(z, xbc, dt_raw, p, h0_f, h0_b)
    ckv = rms_norm(kv_a[..., :MLA_KV_LORA], p['kv_a_norm'])
    kpe = kv_a[..., MLA_KV_LORA:]
    q = mla_queries(q_lat, p)
    k, v = mla_keys_values(ckv, kpe, p)
    if ctx is None:
        state = (s_f, s_b, ckv, kpe)
    else:
        row, col = pos
        q = rope_tail(q, MLA_ROPE, row, col)
        k = rope_tail(k, MLA_ROPE, row, col)
        k_ctx, v_ctx = mla_keys_values(ctx[2], ctx[3], p)
        k = jnp.concatenate([k_ctx, k], axis=1)
        v = jnp.concatenate([v_ctx, v], axis=1)
        state = None
    o = block_attention(q, k, v).reshape(b, l, MLA_HEADS * MLA_V)
    return jnp.concatenate([y_ssd, o], axis=-1) @ p['w_out'], state


def odd_mixer(h, p, pos, ctx):
    b, l, _ = h.shape
    q, k, v = jnp.split(h @ p['w_in'], ODD_SPLITS, axis=-1)
    q = rms_norm(split_heads(q, GQA_HEADS), p['q_norm'])
    k = rms_norm(split_heads(k, GQA_KV_HEADS), p['k_norm'])
    v = split_heads(v, GQA_KV_HEADS)
    if ctx is None:
        state = (k, v)
    else:
        row, col = pos
        q = rope_2d(q, row, col)
        k = jnp.concatenate([ctx[0], rope_2d(k, row, col)], axis=1)
        v = jnp.concatenate([ctx[1], v], axis=1)
        state = None
    o = block_attention(q, k, v).reshape(b, l, O_OUT)
    return o @ p['w_out'], state


def moe_ffn(h, w_router, router_bias, w_gate, w_up, w_down):
    scores = jax.nn.sigmoid(jnp.einsum('bld,de->ble', h, w_router, preferred_element_type=jnp.float32))
    sel = (scores + router_bias.astype(jnp.float32)).reshape(scores.shape[:-1] + (N_GROUPS, GROUP_SIZE))
    group_score = lax.top_k(sel, TOP_K)[0].sum(axis=-1)
    best = jnp.argmax(group_score, axis=-1)
    in_group = jnp.arange(N_GROUPS) == best[..., None]
    masked = jnp.where(in_group[..., None], sel, -jnp.inf).reshape(scores.shape)
    _, idx = lax.top_k(masked, TOP_K)
    w = jnp.take_along_axis(scores, idx, axis=-1)
    w = w / jnp.sum(w, axis=-1, keepdims=True)
    gates = jnp.sum(jax.nn.one_hot(idx, N_EXPERTS, dtype=jnp.float32) * w[..., None], axis=-2).astype(h.dtype)
    a = jnp.einsum('bld,edf->blef', h, w_gate)
    u = jnp.einsum('bld,edf->blef', h, w_up)
    return jnp.einsum('blef,efd->bld', jax.nn.silu(a) * u * gates[..., None], w_down)


def ada_mod(cond, w, b):
    return (jax.nn.silu(cond) @ w + b)[:, None, :]


def trunk_layer(x, mod, g_mix, g_ffn, mixer, p, moe_p, pos, ctx):
    sh1, sc1, gt1, sh2, sc2, gt2 = jnp.split(mod, 6, axis=-1)
    h = rms_norm(x, g_mix) * (1 + sc1) + sh1
    out, state = mixer(h, p, pos, ctx)
    x = x + gt1 * out
    h = rms_norm(x, g_ffn) * (1 + sc2) + sh2
    x = x + gt2 * moe_ffn(h, *moe_p)
    return x, state


def setup_inputs(seed: int = 0) -> dict:
    key = jax.random.key(seed)
    ks = iter(jax.random.split(key, 64))
    D = D_MODEL

    def nrm(shape, scale):
        return jax.random.normal(next(ks), shape, jnp.float32) * scale

    def gain(shape):
        return 1.0 + nrm(shape, 0.02)

    def a_log():
        return jnp.log(jax.random.uniform(next(ks), (N_EVEN, SSD_HEADS), jnp.float32, 1.0, 16.0))

    def dt_bias():
        dt = jnp.exp(jax.random.uniform(next(ks), (N_EVEN, SSD_HEADS), jnp.float32,
                                        math.log(1e-3), math.log(1e-1)))
        return dt + jnp.log(-jnp.expm1(-dt))

    return {
        'x_prompt': nrm((BATCH, SEQ, D), 1.0),
        'x_sample': nrm((DEC_BATCH, DEC_SEQ, D), 1.0),
        'c': nrm((DEC_BATCH, D), 1.0),
        'state_ssd_fwd': nrm((DEC_BATCH, N_EVEN, SSD_HEADS, SSD_HEADDIM, SSD_STATE), 0.1),
        'state_ssd_bwd': nrm((DEC_BATCH, N_EVEN, SSD_HEADS, SSD_HEADDIM, SSD_STATE), 0.1),
        'cache_mla_ckv': nrm((DEC_BATCH, N_EVEN, PAST_LEN, MLA_KV_LORA), 1.0),
        'cache_mla_kpe': nrm((DEC_BATCH, N_EVEN, PAST_LEN, MLA_ROPE), 1.0),
        'cache_gqa_k': nrm((DEC_BATCH, N_ODD, PAST_LEN, GQA_KV_HEADS, GQA_HEAD_DIM), 1.0),
        'cache_gqa_v': nrm((DEC_BATCH, N_ODD, PAST_LEN, GQA_KV_HEADS, GQA_HEAD_DIM), 1.0),
        'c_ctx': nrm((D,), 1.0),
        'g_mix': gain((DEPTH, D)),
        'g_ffn': gain((DEPTH, D)),
        'w_ada': nrm((DEPTH, D, 6 * D), 0.5 * D ** -0.5),
        'b_ada': nrm((DEPTH, 6 * D), 0.02),
        'w_router': nrm((D, N_EXPERTS), D ** -0.5),
        'router_bias': nrm((N_EXPERTS,), 0.01),
        'w_exp_gate': nrm((DEPTH, N_EXPERTS, D, D_EXPERT), D ** -0.5),
        'w_exp_up': nrm((DEPTH, N_EXPERTS, D, D_EXPERT), D ** -0.5),
        'w_exp_down': nrm((DEPTH, N_EXPERTS, D_EXPERT, D), D_EXPERT ** -0.5),
        'w_in_even': nrm((N_EVEN, D, E_IN), D ** -0.5),
        'ssd_conv_w': nrm((N_EVEN, SSD_CONV, SSD_CONV_DIM), SSD_CONV ** -0.5),
        'ssd_conv_b': nrm((N_EVEN, SSD_CONV_DIM), 0.02),
        'ssd_a_log_fwd': a_log(),
        'ssd_a_log_bwd': a_log(),
        'ssd_dt_bias_fwd': dt_bias(),
        'ssd_dt_bias_bwd': dt_bias(),
        'ssd_d': gain((N_EVEN, SSD_HEADS)),
        'ssd_norm': gain((N_EVEN, SSD_INNER)),
        'mla_q_a_norm': gain((N_EVEN, MLA_Q_LORA)),
        'mla_w_q_b': nrm((N_EVEN, MLA_Q_LORA, MLA_HEADS * MLA_QK), MLA_Q_LORA ** -0.5),
        'mla_kv_a_norm': gain((N_EVEN, MLA_KV_LORA)),
        'mla_w_kv_b': nrm((N_EVEN, MLA_KV_LORA, MLA_HEADS * (MLA_NOPE + MLA_V)), MLA_KV_LORA ** -0.5),
        'mla_q_norm': gain((N_EVEN, MLA_QK)),
        'mla_k_norm': gain((N_EVEN, MLA_QK)),
        'w_out_even': nrm((N_EVEN, E_OUT, D), E_OUT ** -0.5),
        'w_in_odd': nrm((N_ODD, D, O_IN), D ** -0.5),
        'gqa_q_norm': gain((N_ODD, GQA_HEAD_DIM)),
        'gqa_k_norm': gain((N_ODD, GQA_HEAD_DIM)),
        'w_out_odd': nrm((N_ODD, O_OUT, D), O_OUT ** -0.5),
    }


def reference(x_prompt, x_sample, c, state_ssd_fwd, state_ssd_bwd, cache_mla_ckv, cache_mla_kpe,
              cache_gqa_k, cache_gqa_v, c_ctx, g_mix, g_ffn, w_ada, b_ada, w_router, router_bias,
              w_exp_gate, w_exp_up, w_exp_down, w_in_even, ssd_conv_w, ssd_conv_b, ssd_a_log_fwd,
              ssd_a_log_bwd, ssd_dt_bias_fwd, ssd_dt_bias_bwd, ssd_d, ssd_norm, mla_q_a_norm,
              mla_w_q_b, mla_kv_a_norm, mla_w_kv_b, mla_q_norm, mla_k_norm, w_out_even, w_in_odd,
              gqa_q_norm, gqa_k_norm, w_out_odd):
    rows = x_sample.shape[1] // GRID_W
    pos = (jnp.repeat(jnp.arange(rows), GRID_W), jnp.tile(jnp.arange(GRID_W), rows))
    xc, xl = x_prompt, x_sample
    even_states, odd_states = [], []
    for i in range(DEPTH):
        j = i // 2
        if i % 2 == 0:
            mixer = even_mixer
            p = {'w_in': w_in_even[j], 'conv_w': ssd_conv_w[j], 'conv_b': ssd_conv_b[j],
                 'a_log_f': ssd_a_log_fwd[j], 'a_log_b': ssd_a_log_bwd[j],
                 'dt_bias_f': ssd_dt_bias_fwd[j], 'dt_bias_b': ssd_dt_bias_bwd[j],
                 'd_skip': ssd_d[j], 'ssd_norm': ssd_norm[j], 'q_a_norm': mla_q_a_norm[j],
                 'w_q_b': mla_w_q_b[j], 'kv_a_norm': mla_kv_a_norm[j], 'w_kv_b': mla_w_kv_b[j],
                 'q_norm': mla_q_norm[j], 'k_norm': mla_k_norm[j], 'w_out': w_out_even[j]}
            lat_ctx = (state_ssd_fwd[:, j], state_ssd_bwd[:, j], cache_mla_ckv[:, j], cache_mla_kpe[:, j])
        else:
            mixer = odd_mixer
            p = {'w_in': w_in_odd[j], 'q_norm': gqa_q_norm[j], 'k_norm': gqa_k_norm[j],
                 'w_out': w_out_odd[j]}
            lat_ctx = (cache_gqa_k[:, j], cache_gqa_v[:, j])
        moe_p = (w_router, router_bias, w_exp_gate[i], w_exp_up[i], w_exp_down[i])
        mod_ctx = ada_mod(c_ctx[None, :], w_ada[i], b_ada[i])
        mod_lat = ada_mod(c, w_ada[i], b_ada[i])
        xc, st = trunk_layer(xc, mod_ctx, g_mix[i], g_ffn[i], mixer, p, moe_p, None, None)
        xl, _ = trunk_layer(xl, mod_lat, g_mix[i], g_ffn[i], mixer, p, moe_p, pos, lat_ctx)
        if i % 2 == 0:
            even_states.append(st)
        else:
            odd_states.append(st)
    new_ssd_fwd = jnp.stack([s[0] for s in even_states], axis=1)
    new_ssd_bwd = jnp.stack([s[1] for s in even_states], axis=1)
    new_mla_ckv = jnp.stack([s[2] for s in even_states], axis=1)
    new_mla_kpe = jnp.stack([s[3] for s in even_states], axis=1)
    new_gqa_k = jnp.stack([s[0] for s in odd_states], axis=1)
    new_gqa_v = jnp.stack([s[1] for s in odd_states], axis=1)
    return (xc, xl, new_ssd_fwd, new_ssd_bwd, new_mla_ckv, new_mla_kpe, new_gqa_k, new_gqa_v)
```

```python
import functools
import math

import jax
import jax.numpy as jnp
from jax import lax
from jax.experimental import pallas as pl
from jax.experimental.pallas import tpu as pltpu

F32 = jnp.float32
BF16 = jnp.bfloat16

D = 1024
EPS = 1e-6
GRID_W = 64
ROPE_THETA = 10000.0
SSD_HEADS = 16
SSD_HEADDIM = 64
SSD_INNER = 1024
SSD_STATE = 64
SSD_GROUPS = 2
SSD_CONV_DIM = 1280
SSD_CHUNK = 128
MLA_HEADS = 8
MLA_Q_LORA = 256
MLA_KV_LORA = 128
MLA_NOPE = 64
MLA_ROPE = 32
MLA_V = 64
MLA_QK = 96
GQA_HEADS = 16
GQA_KV_HEADS = 4
GQA_HEAD_DIM = 64
N_EXPERTS = 16
N_GROUPS = 4
GROUP_SIZE = 4
D_EXPERT = 256

LANE = 128
Q_BLOCK = 128
TOKEN_TILE = 512
MOE_TILE = 1024
ADA_TILE = 1536
VMEM_LIMIT = 56 * 1024 * 1024


def _cparams(n_axes):
    return pltpu.CompilerParams(dimension_semantics=("arbitrary",) * n_axes,
                                vmem_limit_bytes=VMEM_LIMIT)


def _dot(a, b):
    return jnp.dot(a, b, preferred_element_type=F32)


def _dot_nt(a, b):
    return lax.dot_general(a, b, (((1,), (1,)), ((), ())), preferred_element_type=F32)


def _split3(x):
    hi = x.astype(BF16)
    r1 = x - hi.astype(F32)
    mid = r1.astype(BF16)
    lo = (r1 - mid.astype(F32)).astype(BF16)
    return hi, mid, lo


def _dot_x3(x, e):
    hi, mid, lo = _split3(x)
    return _dot(hi, e) + _dot(mid, e) + _dot(lo, e)


def _dot_3x(e, x):
    hi, mid, lo = _split3(x)
    return _dot(e, hi) + _dot(e, mid) + _dot(e, lo)


def _dot_f32(a, b):
    a0, a1, a2 = _split3(a)
    b0, b1, b2 = _split3(b)
    return (_dot(a0, b0) + (_dot(a0, b1) + _dot(a1, b0))
            + (_dot(a1, b1) + _dot(a0, b2) + _dot(a2, b0)))


def _sigmoid(x):
    return 1.0 / (1.0 + jnp.exp(-x))


def _silu(x):
    return x * _sigmoid(x)


def _iota(shape, axis):
    return lax.broadcasted_iota(jnp.int32, shape, axis)


def _ada_kernel(c_ref, w_ref, b_ref, o_ref):
    a = _silu(c_ref[...]).astype(BF16)
    o_ref[0] = _dot(a, w_ref[0].astype(BF16)) + b_ref[0]


def _ada_mod(cond, w_ada, b_ada):
    depth, _, n = w_ada.shape
    rows = cond.shape[0]
    return pl.pallas_call(
        _ada_kernel,
        grid=(depth, n // ADA_TILE),
        in_specs=[pl.BlockSpec((rows, D), lambda l, j: (0, 0)),
                  pl.BlockSpec((1, D, ADA_TILE), lambda l, j: (l, 0, j)),
                  pl.BlockSpec((1, 1, ADA_TILE), lambda l, j: (l, 0, j))],
        out_specs=pl.BlockSpec((1, rows, ADA_TILE), lambda l, j: (l, 0, j)),
        out_shape=jax.ShapeDtypeStruct((depth, rows, n), F32),
        compiler_params=_cparams(2),
        name="ada_mod",
    )(cond, w_ada, b_ada.reshape(depth, 1, n))


def _mod_row(i, tile, n_ctx_tokens, lat_len):
    n_ctx_tiles = n_ctx_tokens // tile
    per_seq = lat_len // tile
    return jnp.where(i < n_ctx_tiles, 0, 1 + (i - n_ctx_tiles) // per_seq)


def _norm_mod(x, g, shift, scale):
    ms = jnp.mean(x * x, axis=-1, keepdims=True)
    y = x * lax.rsqrt(ms + EPS) * g
    return y * (1.0 + scale) + shift


def _inproj_kernel(x_ref, mod_ref, g_ref, w_ref, *o_refs, widths):
    m = mod_ref[0]
    h = _norm_mod(x_ref[...], g_ref[...], m[:, 0:D], m[:, D:2 * D]).astype(BF16)
    off = 0
    for o_ref, wd in zip(o_refs, widths):
        o_ref[...] = _dot(h, w_ref[:, off:off + wd]).astype(o_ref.dtype)
        off += wd


def _inproj(x, mod, g, w, widths, n_ctx_tokens, lat_len):
    t = x.shape[0]
    n = w.shape[1]
    row = functools.partial(_mod_row, tile=TOKEN_TILE, n_ctx_tokens=n_ctx_tokens, lat_len=lat_len)
    return pl.pallas_call(
        functools.partial(_inproj_kernel, widths=widths),
        grid=(t // TOKEN_TILE,),
        in_specs=[pl.BlockSpec((TOKEN_TILE, D), lambda i: (i, 0)),
                  pl.BlockSpec((1, 1, 6 * D), lambda i: (row(i), 0, 0)),
                  pl.BlockSpec((1, D), lambda i: (0, 0)),
                  pl.BlockSpec((D, n), lambda i: (0, 0))],
        out_specs=[pl.BlockSpec((TOKEN_TILE, wd), lambda i: (i, 0)) for wd in widths],
        out_shape=[jax.ShapeDtypeStruct((t, wd), F32) for wd in widths],
        compiler_params=_cparams(1),
        name="inproj",
    )(x, mod, g, w)


def _ssd_kernel(*refs, seq, has_h0, want_state):
    it = iter(refs)
    z_ref, xbc_ref, dt_ref, cw_ref, cb_ref, hp_ref, nw_ref = [next(it) for _ in range(7)]
    h0_refs = (next(it), next(it)) if has_h0 else None
    y_ref = next(it)
    s_refs = (next(it), next(it)) if want_state else None
    xs_ref, bc_ref, yacc_ref, st_ref = [next(it) for _ in range(4)]

    q = SSD_CHUNK
    nc = seq // q
    gn = SSD_GROUPS * SSD_STATE

    cw = cw_ref[...]
    cb = cb_ref[...]
    rid = _iota((q, SSD_CONV_DIM), 0)

    def conv_body(c, carry):
        r0 = pl.multiple_of(c * q, q)
        xc = xbc_ref[pl.ds(r0, q), :]
        prev = xbc_ref[pl.ds(jnp.maximum(r0 - 1, 0), 1), :]
        nxt = xbc_ref[pl.ds(jnp.minimum(r0 + q, seq - 1), 1), :]
        prev = jnp.where(c > 0, prev, 0.0)
        nxt = jnp.where(c < nc - 1, nxt, 0.0)
        xp = jnp.where(rid == 0, prev, pltpu.roll(xc, 1, axis=0))
        xn = jnp.where(rid == q - 1, nxt, pltpu.roll(xc, q - 1, axis=0))
        v = _silu(xp * cw[0:1] + xc * cw[1:2] + xn * cw[2:3] + cb)
        xs_ref[pl.ds(r0, q), :] = v[:, :SSD_INNER]
        bc_ref[pl.ds(r0, q), :] = v[:, SSD_INNER:]
        return carry

    lax.fori_loop(0, nc, conv_body, 0)

    hp = hp_ref[...]
    expand = (_iota((LANE, SSD_INNER), 1) // SSD_HEADDIM == _iota((LANE, SSD_INNER), 0)).astype(BF16)
    blockmask = (_iota((gn, SSD_INNER), 0) // SSD_STATE
                 == _iota((gn, SSD_INNER), 1) // (SSD_INNER // SSD_GROUPS))
    ri = _iota((q, q), 0)
    ci = _iota((q, q), 1)
    lane = _iota((q, LANE), 1)
    lo_half = lane < SSD_STATE

    def scan_pass(fwd):
        a = -jnp.exp(hp[0:1] if fwd else hp[1:2])
        bias = hp[2:3] if fwd else hp[3:4]
        causal = (ri >= ci) if fwd else (ri <= ci)
        tri = causal.astype(BF16)
        if has_h0:
            st_ref[...] = h0_refs[0 if fwd else 1][0]
        else:
            st_ref[...] = jnp.zeros((gn, SSD_INNER), F32)

        def body(k, carry):
            c = k if fwd else nc - 1 - k
            r0 = pl.multiple_of(c * q, q)
            xs = xs_ref[pl.ds(r0, q), :]
            bc = bc_ref[pl.ds(r0, q), :]
            bm = bc[:, :gn]
            cm = bc[:, gn:]
            xdt_raw = dt_ref[pl.ds(r0, q), :] + bias
            dt = jnp.maximum(xdt_raw, 0.0) + jnp.log1p(jnp.exp(-jnp.abs(xdt_raw)))
            cum = _dot_3x(tri, dt * a)
            cum_t = cum.T
            last = cum[q - 1:q] if fwd else cum[0:1]
            stacked = jnp.concatenate([dt, jnp.exp(cum), jnp.exp(last - cum)], axis=0)
            ex = _dot_x3(stacked, expand)
            dt_e, ecum_e, dend_e = ex[0:q], ex[q:2 * q], ex[2 * q:3 * q]
            cdec_e = ecum_e[q - 1:q] if fwd else ecum_e[0:1]
            xdt = xs * dt_e
            st = st_ref[...]
            cmb = cm.astype(BF16)
            bmb = bm.astype(BF16)
            y = _dot(cmb, st.astype(BF16)) * ecum_e
            cb0 = _dot_nt(jnp.where(lo_half, cm, 0.0).astype(BF16), bmb)
            cb1 = _dot_nt(jnp.where(lo_half, 0.0, cm).astype(BF16), bmb)
            pairs = []
            for j in range(SSD_HEADS // 2):
                cbg = cb0 if j < SSD_HEADS // 4 else cb1
                xp = xdt[:, j * LANE:(j + 1) * LANE]
                acc = None
                for half in range(2):
                    h = 2 * j + half
                    seg = cum[:, h:h + 1] - cum_t[h:h + 1, :]
                    dec = jnp.exp(jnp.where(causal, seg, -jnp.inf))
                    sc = (cbg * dec).astype(BF16)
                    keep = lo_half if half == 0 else jnp.logical_not(lo_half)
                    part = _dot(sc, jnp.where(keep, xp, 0.0).astype(BF16))
                    acc = part if acc is None else acc + part
                pairs.append(acc)
            y = y + jnp.concatenate(pairs, axis=1)
            if fwd:
                yacc_ref[pl.ds(r0, q), :] = y
            else:
                yacc_ref[pl.ds(r0, q), :] += y
            new = _dot(bm.T.astype(BF16), (xdt * dend_e).astype(BF16))
            st_ref[...] = jnp.where(blockmask, st * cdec_e + new, 0.0)
            return carry

        lax.fori_loop(0, nc, body, 0)
        if want_state:
            for j in range(SSD_INNER // LANE):
                g = (j * LANE) // (SSD_INNER // SSD_GROUPS)
                t = st_ref[:, j * LANE:(j + 1) * LANE].T
                s_refs[0 if fwd else 1][0, j * LANE:(j + 1) * LANE, :] = t[:, g * SSD_STATE:(g + 1) * SSD_STATE]

    scan_pass(True)
    scan_pass(False)

    d_e = _dot_x3(jnp.broadcast_to(hp[4:5], (8, LANE)), expand)[0:1]
    nw = nw_ref[...]

    def out_body(c, carry):
        r0 = pl.multiple_of(c * q, q)
        y = yacc_ref[pl.ds(r0, q), :] + xs_ref[pl.ds(r0, q), :] * d_e
        y = y * _silu(z_ref[pl.ds(r0, q), :])
        ms = jnp.mean(y * y, axis=-1, keepdims=True)
        y_ref[pl.ds(r0, q), :] = (y * lax.rsqrt(ms + EPS) * nw).astype(y_ref.dtype)
        return carry

    lax.fori_loop(0, nc, out_body, 0)


def _ssd(z, xbc, dt, cw, cb, hp, nw, h0, y_prev, *, seq, n_seq, blk0):
    t = z.shape[0]
    has_h0 = h0 is not None
    want_state = not has_h0
    gn = SSD_GROUPS * SSD_STATE
    tok = lambda w: pl.BlockSpec((seq, w), lambda b: (blk0 + b, 0))
    full = lambda a: pl.BlockSpec(a.shape, lambda b: (0,) * a.ndim)
    in_specs = [tok(SSD_INNER), tok(SSD_CONV_DIM), tok(LANE), full(cw), full(cb), full(hp), full(nw)]
    args = [z, xbc, dt, cw, cb, hp, nw]
    if has_h0:
        in_specs += [pl.BlockSpec((1, gn, SSD_INNER), lambda b: (b, 0, 0))] * 2
        args += list(h0)
    out_shape = [jax.ShapeDtypeStruct((t, SSD_INNER), BF16)]
    out_specs = [tok(SSD_INNER)]
    aliases = {}
    if y_prev is not None:
        in_specs.append(pl.BlockSpec(memory_space=pl.ANY))
        args.append(y_prev)
        aliases = {len(args) - 1: 0}
    if want_state:
        out_shape += [jax.ShapeDtypeStruct((n_seq, SSD_INNER, SSD_STATE), F32)] * 2
        out_specs += [pl.BlockSpec((1, SSD_INNER, SSD_STATE), lambda b: (b, 0, 0))] * 2

    def kern(*refs):
        refs = list(refs)
        if y_prev is not None:
            del refs[len(args) - 1]
        _ssd_kernel(*refs, seq=seq, has_h0=has_h0, want_state=want_state)

    return pl.pallas_call(
        kern,
        grid=(n_seq,),
        in_specs=in_specs,
        out_specs=out_specs,
        out_shape=out_shape,
        scratch_shapes=[pltpu.VMEM((seq, SSD_INNER), F32),
                        pltpu.VMEM((seq, 2 * gn), F32),
                        pltpu.VMEM((seq, SSD_INNER), F32),
                        pltpu.VMEM((gn, SSD_INNER), F32)],
        input_output_aliases=aliases,
        compiler_params=_cparams(1),
        name="ssd_lat" if has_h0 else "ssd_ctx",
    )(*args)


def _rope(x, cos, sin, first_half, dist):
    partner = jnp.where(first_half, pltpu.roll(x, LANE - dist, axis=1), pltpu.roll(x, dist, axis=1))
    return x * cos + partner * sin


def _attend(qb, keys, vals, scale):
    ss = [_dot_nt(qb, k) * scale for k in keys]
    m = ss[0].max(axis=-1, keepdims=True)
    for s in ss[1:]:
        m = jnp.maximum(m, s.max(axis=-1, keepdims=True))
    ps = [jnp.exp(s - m) for s in ss]
    den = ps[0].sum(axis=-1, keepdims=True)
    for p in ps[1:]:
        den = den + p.sum(axis=-1, keepdims=True)
    o = _dot(ps[0].astype(BF16), vals[0])
    for p, v in zip(ps[1:], vals[1:]):
        o = o + _dot(p.astype(BF16), v)
    return o / den


def _mla_kernel(*refs, seq, lat):
    it = iter(refs)
    qa_ref, kv_ref, qan_ref, kvn_ref, wq_ref, wk_ref, wv_ref, qn_ref, kn_ref = [next(it) for _ in range(9)]
    if lat:
        cckv_ref, ckpe_ref, cos_ref, sin_ref = [next(it) for _ in range(4)]
    o_ref = next(it)
    if not lat:
        ckv_out, kpe_out = next(it), next(it)
    q_s, k_s, v_s, qh_s, kh_s, oacc = [next(it) for _ in range(6)]
    if lat:
        kc_s, vc_s, kch_s = [next(it) for _ in range(3)]

    scale = MLA_QK ** -0.5
    kv = kv_ref[...]
    kpe = kv[:, LANE:]
    ckv_raw = kv[:, :LANE]
    ckv = ckv_raw * lax.rsqrt(jnp.mean(ckv_raw * ckv_raw, axis=-1, keepdims=True) + EPS) * kvn_ref[...]
    qa = qa_ref[...]
    qa = qa * lax.rsqrt(jnp.mean(qa * qa, axis=-1, keepdims=True) + EPS) * qan_ref[...]
    if not lat:
        ckv_out[...] = ckv
        kpe_out[...] = kpe
    ckvb = ckv.astype(BF16)
    q_s[...] = _dot(qa.astype(BF16), wq_ref[...])
    k_s[...] = _dot(ckvb, wk_ref[...])
    v_s[...] = _dot(ckvb, wv_ref[...]).astype(BF16)
    if lat:
        ccb = cckv_ref[0].astype(BF16)
        kc_s[...] = _dot(ccb, wk_ref[...])
        vc_s[...] = _dot(ccb, wv_ref[...]).astype(BF16)
        ckpe = ckpe_ref[0]
        cos = cos_ref[...]
        sin = sin_ref[...]
        lane = _iota((seq, LANE), 1)
        first_half = (lane % (MLA_ROPE // 2)) < (MLA_ROPE // 4)
    qn = qn_ref[...]
    kn = kn_ref[...]
    lane_q = _iota((Q_BLOCK, LANE), 1)

    def head_norm(x, g):
        return x * lax.rsqrt(jnp.sum(x * x, axis=-1, keepdims=True) * (1.0 / MLA_QK) + EPS) * g

    for h in range(MLA_HEADS):
        sl = slice(h * LANE, (h + 1) * LANE)
        qh = head_norm(q_s[:, sl], qn)
        kh = head_norm(k_s[:, sl] + kpe, kn)
        if lat:
            qh = _rope(qh, cos, sin, first_half, MLA_ROPE // 4)
            kh = _rope(kh, cos, sin, first_half, MLA_ROPE // 4)
            kch_s[...] = head_norm(kc_s[:, sl] + ckpe, kn).astype(BF16)
        qh_s[...] = qh.astype(BF16)
        kh_s[...] = kh.astype(BF16)
        pair = slice((h // 2) * LANE, (h // 2 + 1) * LANE)
        keep = (lane_q < MLA_V) if h % 2 == 0 else (lane_q >= MLA_V)

        def qblock(i, carry):
            r0 = pl.multiple_of(i * Q_BLOCK, Q_BLOCK)
            qb = qh_s[pl.ds(r0, Q_BLOCK), :]
            if lat:
                o = _attend(qb, [kch_s[...], kh_s[...]], [vc_s[:, pair], v_s[:, pair]], scale)
            else:
                o = _attend(qb, [kh_s[...]], [v_s[:, pair]], scale)
            o = jnp.where(keep, o, 0.0)
            if h % 2 == 0:
                oacc[pl.ds(r0, Q_BLOCK), pair] = o
            else:
                oacc[pl.ds(r0, Q_BLOCK), pair] += o
            return carry

        lax.fori_loop(0, seq // Q_BLOCK, qblock, 0)
    o_ref[...] = oacc[...].astype(o_ref.dtype)


def _mla(qa, kv, qan, kvn, wq, wk, wv, qn, kn, cache, tables, o_prev, *, seq, n_seq, blk0):
    t = qa.shape[0]
    lat = cache is not None
    hv = MLA_HEADS * MLA_V
    tok = lambda w: pl.BlockSpec((seq, w), lambda b: (blk0 + b, 0))
    full = lambda a: pl.BlockSpec(a.shape, lambda b: (0,) * a.ndim)
    args = [qa, kv, qan, kvn, wq, wk, wv, qn, kn]
    in_specs = [tok(MLA_Q_LORA), tok(2 * LANE)] + [full(a) for a in args[2:]]
    past = 0
    if lat:
        past = cache[0].shape[1]
        in_specs += [pl.BlockSpec((1, past, LANE), lambda b: (b, 0, 0))] * 2 + [full(tables[0]), full(tables[1])]
        args += [cache[0], cache[1], tables[0], tables[1]]
    out_shape = [jax.ShapeDtypeStruct((t, hv), BF16)]
    out_specs = [tok(hv)]
    aliases = {}
    if o_prev is not None:
        in_specs.append(pl.BlockSpec(memory_space=pl.ANY))
        args.append(o_prev)
        aliases = {len(args) - 1: 0}
    if not lat:
        out_shape += [jax.ShapeDtypeStruct((n_seq * seq, LANE), F32)] * 2
        out_specs += [pl.BlockSpec((seq, LANE), lambda b: (b, 0))] * 2
    scratch = [pltpu.VMEM((seq, MLA_HEADS * LANE), F32),
               pltpu.VMEM((seq, MLA_HEADS * LANE), F32),
               pltpu.VMEM((seq, hv), BF16),
               pltpu.VMEM((seq, LANE), BF16),
               pltpu.VMEM((seq, LANE), BF16),
               pltpu.VMEM((seq, hv), F32)]
    if lat:
        scratch += [pltpu.VMEM((past, MLA_HEADS * LANE), F32),
                    pltpu.VMEM((past, hv), BF16),
                    pltpu.VMEM((past, LANE), BF16)]

    def kern(*refs):
        refs = list(refs)
        if o_prev is not None:
            del refs[len(args) - 1]
        _mla_kernel(*refs, seq=seq, lat=lat)

    return pl.pallas_call(
        kern,
        grid=(n_seq,),
        in_specs=in_specs,
        out_specs=out_specs,
        out_shape=out_shape,
        scratch_shapes=scratch,
        input_output_aliases=aliases,
        compiler_params=_cparams(1),
        name="mla_lat" if lat else "mla_ctx",
    )(*args)


def _gqa_kernel(*refs, seq, lat):
    it = iter(refs)
    q_ref, k_ref, v_ref, qn_ref, kn_ref = [next(it) for _ in range(5)]
    if lat:
        ck_ref, cv_ref, cos_ref, sin_ref = [next(it) for _ in range(4)]
    o_ref = next(it)
    if not lat:
        kn_out = next(it)
    qn_s, kn_s, vb_s, qh_s, oacc = [next(it) for _ in range(5)]
    if lat:
        ckb_s, cvb_s = next(it), next(it)

    scale = GQA_HEAD_DIM ** -0.5
    hd = GQA_HEAD_DIM
    nq = GQA_HEADS * hd
    nk = GQA_KV_HEADS * LANE

    def half_norm(x, g, width):
        n_half = width // hd
        ind = (_iota((width, LANE), 0) // hd == _iota((width, LANE), 1)).astype(BF16)
        ind_t = (_iota((LANE, width), 1) // hd == _iota((LANE, width), 0)).astype(BF16)
        del n_half
        ss = _dot_x3(x * x, ind)
        r = lax.rsqrt(ss * (1.0 / hd) + EPS)
        return x * _dot_x3(r, ind_t) * g

    qn = half_norm(q_ref[...], qn_ref[...], nq)
    kn = half_norm(k_ref[...], kn_ref[...], nk)
    if not lat:
        kn_out[...] = kn
    if lat:
        cos = cos_ref[...]
        sin = sin_ref[...]
        lane = _iota((seq, LANE), 1)
        first_half = (lane % (hd // 2)) < (hd // 4)
        for j in range(nq // LANE):
            sl = slice(j * LANE, (j + 1) * LANE)
            qn_s[:, sl] = _rope(qn[:, sl], cos, sin, first_half, hd // 4)
        for j in range(nk // LANE):
            sl = slice(j * LANE, (j + 1) * LANE)
            kn_s[:, sl] = _rope(kn[:, sl], cos, sin, first_half, hd // 4).astype(BF16)
        ckb_s[...] = ck_ref[0].astype(BF16)
        cvb_s[...] = cv_ref[0].astype(BF16)
    else:
        qn_s[...] = qn
        kn_s[...] = kn.astype(BF16)
    vb_s[...] = v_ref[...].astype(BF16)
    lane_s = _iota((seq, LANE), 1)
    lane_q = _iota((Q_BLOCK, LANE), 1)

    for h in range(GQA_HEADS):
        pair = slice((h // 2) * LANE, (h // 2 + 1) * LANE)
        g = h // (GQA_HEADS // GQA_KV_HEADS)
        gs = slice(g * LANE, (g + 1) * LANE)
        keep_s = (lane_s < hd) if h % 2 == 0 else (lane_s >= hd)
        keep_q = (lane_q < hd) if h % 2 == 0 else (lane_q >= hd)
        qh_s[...] = jnp.where(keep_s, qn_s[:, pair], 0.0).astype(BF16)

        def qblock(i, carry):
            r0 = pl.multiple_of(i * Q_BLOCK, Q_BLOCK)
            qb = qh_s[pl.ds(r0, Q_BLOCK), :]
            if lat:
                o = _attend(qb, [ckb_s[:, gs], kn_s[:, gs]], [cvb_s[:, gs], vb_s[:, gs]], scale)
            else:
                o = _attend(qb, [kn_s[:, gs]], [vb_s[:, gs]], scale)
            o = jnp.where(keep_q, o, 0.0)
            if h % 2 == 0:
                oacc[pl.ds(r0, Q_BLOCK), pair] = o
            else:
                oacc[pl.ds(r0, Q_BLOCK), pair] += o
            return carry

        lax.fori_loop(0, seq // Q_BLOCK, qblock, 0)
    o_ref[...] = oacc[...].astype(o_ref.dtype)


def _gqa(q, k, v, qn, kn, cache, tables, o_prev, *, seq, n_seq, blk0):
    t = q.shape[0]
    lat = cache is not None
    nq = GQA_HEADS * GQA_HEAD_DIM
    nk = GQA_KV_HEADS * LANE
    tok = lambda w: pl.BlockSpec((seq, w), lambda b: (blk0 + b, 0))
    full = lambda a: pl.BlockSpec(a.shape, lambda b: (0,) * a.ndim)
    args = [q, k, v, qn, kn]
    in_specs = [tok(nq), tok(nk), tok(nk), full(qn), full(kn)]
    past = 0
    if lat:
        past = cache[0].shape[1]
        in_specs += [pl.BlockSpec((1, past, nk), lambda b: (b, 0, 0))] * 2 + [full(tables[0]), full(tables[1])]
        args += [cache[0], cache[1], tables[0], tables[1]]
    out_shape = [jax.ShapeDtypeStruct((t, nq), BF16)]
    out_specs = [tok(nq)]
    aliases = {}
    if o_prev is not None:
        in_specs.append(pl.BlockSpec(memory_space=pl.ANY))
        args.append(o_prev)
        aliases = {len(args) - 1: 0}
    if not lat:
        out_shape += [jax.ShapeDtypeStruct((n_seq * seq, nk), F32)]
        out_specs += [pl.BlockSpec((seq, nk), lambda b: (b, 0))]
    scratch = [pltpu.VMEM((seq, nq), F32),
               pltpu.VMEM((seq, nk), BF16),
               pltpu.VMEM((seq, nk), BF16),
               pltpu.VMEM((seq, LANE), BF16),
               pltpu.VMEM((seq, nq), F32)]
    if lat:
        scratch += [pltpu.VMEM((past, nk), BF16), pltpu.VMEM((past, nk), BF16)]

    def kern(*refs):
        refs = list(refs)
        if o_prev is not None:
            del refs[len(args) - 1]
        _gqa_kernel(*refs, seq=seq, lat=lat)

    return pl.pallas_call(
        kern,
        grid=(n_seq,),
        in_specs=in_specs,
        out_specs=out_specs,
        out_shape=out_shape,
        scratch_shapes=scratch,
        input_output_aliases=aliases,
        compiler_params=_cparams(1),
        name="gqa_lat" if lat else "gqa_ctx",
    )(*args)


def _route(sel, sc):
    gs = []
    for g in range(N_GROUPS):
        r = sel[g * GROUP_SIZE:(g + 1) * GROUP_SIZE]
        best = None
        for i in range(GROUP_SIZE):
            for j in range(i + 1, GROUP_SIZE):
                s = r[i] + r[j]
                best = s if best is None else jnp.maximum(best, s)
        gs.append(best)
    cur = gs[0]
    grp = jnp.zeros_like(cur, dtype=jnp.int32)
    for g in range(1, N_GROUPS):
        upd = gs[g] > cur
        grp = jnp.where(upd, g, grp)
        cur = jnp.where(upd, gs[g], cur)

    def pick(rows, i):
        v = rows[i]
        for g in range(1, N_GROUPS):
            v = jnp.where(grp == g, rows[g * GROUP_SIZE + i], v)
        return v

    cand = [pick(sel, i) for i in range(GROUP_SIZE)]
    csc = [pick(sc, i) for i in range(GROUP_SIZE)]
    v1, i1, w1 = cand[0], jnp.zeros_like(grp), csc[0]
    for i in range(1, GROUP_SIZE):
        upd = cand[i] > v1
        v1 = jnp.where(upd, cand[i], v1)
        i1 = jnp.where(upd, i, i1)
        w1 = jnp.where(upd, csc[i], w1)
    v2 = i2 = w2 = None
    for i in range(GROUP_SIZE):
        ok = i1 != i
        if v2 is None:
            v2 = jnp.where(ok, cand[i], -jnp.inf)
            i2 = jnp.zeros_like(grp)
            w2 = csc[i]
        else:
            upd = ok & (cand[i] > v2)
            v2 = jnp.where(upd, cand[i], v2)
            i2 = jnp.where(upd, i, i2)
            w2 = jnp.where(upd, csc[i], w2)
    tot = w1 + w2
    e1 = grp * GROUP_SIZE + i1
    e2 = grp * GROUP_SIZE + i2
    return [jnp.where(e1 == e, w1 / tot, 0.0) + jnp.where(e2 == e, w2 / tot, 0.0)
            for e in range(N_EXPERTS)]


def _outproj_kernel(*refs, n_y):
    x_ref, mod_ref = refs[0], refs[1]
    y_refs = refs[2:2 + n_y]
    w_ref, g_ref, wr_ref, rb_ref, x1_ref, h2_ref, gate_ref = refs[2 + n_y:]
    m = mod_ref[0]
    off = 0
    out = None
    for y_ref in y_refs:
        k = y_ref.shape[1]
        part = _dot(y_ref[...], w_ref[off:off + k, :])
        out = part if out is None else out + part
        off += k
    x1 = x_ref[...] + m[:, 2 * D:3 * D] * out
    x1_ref[...] = x1
    h2 = _norm_mod(x1, g_ref[...], m[:, 3 * D:4 * D], m[:, 4 * D:5 * D])
    h2_ref[...] = h2.astype(BF16)
    logits = _dot_f32(h2, wr_ref[...])
    sc_t = _sigmoid(logits.T[0:N_EXPERTS, :])
    sel_t = sc_t + rb_ref[...]
    gates = _route([sel_t[e:e + 1] for e in range(N_EXPERTS)], [sc_t[e:e + 1] for e in range(N_EXPERTS)])
    t = x1.shape[0]
    gates_t = jnp.concatenate(gates + [jnp.zeros((LANE - N_EXPERTS, t), F32)], axis=0)
    gate_ref[...] = gates_t.T


def _outproj(x, mod, ys, w, g, wr, rb, n_ctx_tokens, lat_len):
    t = x.shape[0]
    row = functools.partial(_mod_row, tile=TOKEN_TILE, n_ctx_tokens=n_ctx_tokens, lat_len=lat_len)
    full = lambda a: pl.BlockSpec(a.shape, lambda i: (0,) * a.ndim)
    return pl.pallas_call(
        functools.partial(_outproj_kernel, n_y=len(ys)),
        grid=(t // TOKEN_TILE,),
        in_specs=[pl.BlockSpec((TOKEN_TILE, D), lambda i: (i, 0)),
                  pl.BlockSpec((1, 1, 6 * D), lambda i: (row(i), 0, 0))]
                 + [pl.BlockSpec((TOKEN_TILE, y.shape[1]), lambda i: (i, 0)) for y in ys]
                 + [full(w), full(g), full(wr), full(rb)],
        out_specs=[pl.BlockSpec((TOKEN_TILE, D), lambda i: (i, 0)),
                   pl.BlockSpec((TOKEN_TILE, D), lambda i: (i, 0)),
                   pl.BlockSpec((TOKEN_TILE, LANE), lambda i: (i, 0))],
        out_shape=[jax.ShapeDtypeStruct((t, D), F32),
                   jax.ShapeDtypeStruct((t, D), BF16),
                   jax.ShapeDtypeStruct((t, LANE), F32)],
        compiler_params=_cparams(1),
        name="outproj_router",
    )(x, mod, *ys, w, g, wr, rb)


def _moe_kernel(x_ref, mod_ref, h_ref, gate_ref, wg_ref, wu_ref, wd_ref, o_ref, acc_ref):
    e = pl.program_id(1)

    @pl.when(e == 0)
    def _():
        acc_ref[...] = jnp.zeros_like(acc_ref)

    h = h_ref[...]
    a = _dot(h, wg_ref[0])
    u = _dot(h, wu_ref[0])
    gates = gate_ref[...]
    gcol = jnp.sum(jnp.where(_iota(gates.shape, 1) == e, gates, 0.0), axis=-1, keepdims=True)
    act = (_silu(a) * u * gcol).astype(BF16)
    acc_ref[...] += _dot(act, wd_ref[0])

    @pl.when(e == N_EXPERTS - 1)
    def _():
        o_ref[...] = x_ref[...] + mod_ref[0][:, 5 * D:6 * D] * acc_ref[...]


def _moe(x1, mod, h2, gates, wg, wu, wd, n_ctx_tokens, lat_len):
    t = x1.shape[0]
    row = functools.partial(_mod_row, tile=MOE_TILE, n_ctx_tokens=n_ctx_tokens, lat_len=lat_len)
    return pl.pallas_call(
        _moe_kernel,
        grid=(t // MOE_TILE, N_EXPERTS),
        in_specs=[pl.BlockSpec((MOE_TILE, D), lambda i, e: (i, 0)),
                  pl.BlockSpec((1, 1, 6 * D), lambda i, e: (row(i), 0, 0)),
                  pl.BlockSpec((MOE_TILE, D), lambda i, e: (i, 0)),
                  pl.BlockSpec((MOE_TILE, LANE), lambda i, e: (i, 0)),
                  pl.BlockSpec((1, D, D_EXPERT), lambda i, e: (e, 0, 0)),
                  pl.BlockSpec((1, D, D_EXPERT), lambda i, e: (e, 0, 0)),
                  pl.BlockSpec((1, D_EXPERT, D), lambda i, e: (e, 0, 0))],
        out_specs=pl.BlockSpec((MOE_TILE, D), lambda i, e: (i, 0)),
        out_shape=jax.ShapeDtypeStruct((t, D), F32),
        scratch_shapes=[pltpu.VMEM((MOE_TILE, D), F32)],
        compiler_params=_cparams(2),
        name="moe_experts",
    )(x1, mod, h2, gates, wg, wu, wd)


def _rope_angles(pos, half):
    inv = jnp.power(ROPE_THETA, -jnp.arange(half, dtype=F32) / half)
    ang = pos.astype(F32)[:, None] * inv[None, :]
    return jnp.cos(ang), jnp.sin(ang)


def _rope_tables(seq, rot, lead, lane_tiles):
    rows = seq // GRID_W
    row = jnp.repeat(jnp.arange(rows), GRID_W)
    col = jnp.tile(jnp.arange(GRID_W), rows)
    cr, sr = _rope_angles(row, rot // 4)
    cc, sc = _rope_angles(col, rot // 4)
    cos = jnp.concatenate([cr, cr, cc, cc], axis=1)
    sin = jnp.concatenate([-sr, sr, -sc, sc], axis=1)
    width = LANE // lane_tiles
    pad = width - lead - rot
    cos = jnp.concatenate([jnp.ones((seq, lead), F32), cos, jnp.ones((seq, pad), F32)], axis=1)
    sin = jnp.concatenate([jnp.zeros((seq, lead), F32), sin, jnp.zeros((seq, pad), F32)], axis=1)
    return jnp.tile(cos, (1, lane_tiles)), jnp.tile(sin, (1, lane_tiles))


def _pad_lanes(a, width):
    return jnp.pad(a, [(0, 0)] * (a.ndim - 1) + [(0, width - a.shape[-1])])


def _even_weights(w_in, w_q_b, w_kv_b):
    z = w_in[:, 0:1024]
    xbc = w_in[:, 1024:2304]
    dt = w_in[:, 2304:2320]
    qa = w_in[:, 2320:2576]
    ckv = w_in[:, 2576:2704]
    kpe = w_in[:, 2704:2736]
    zeros = lambda n: jnp.zeros((D, n), w_in.dtype)
    w = jnp.concatenate([z, xbc, qa, ckv, zeros(MLA_NOPE), kpe, zeros(LANE - MLA_QK), dt,
                         zeros(LANE - SSD_HEADS)], axis=1).astype(BF16)
    wq = _pad_lanes(w_q_b.reshape(MLA_Q_LORA, MLA_HEADS, MLA_QK), LANE).reshape(MLA_Q_LORA, MLA_HEADS * LANE)
    kvb = w_kv_b.reshape(MLA_KV_LORA, MLA_HEADS, MLA_NOPE + MLA_V)
    wk = _pad_lanes(kvb[:, :, :MLA_NOPE], LANE).reshape(MLA_KV_LORA, MLA_HEADS * LANE)
    wv = kvb[:, :, MLA_NOPE:].reshape(MLA_KV_LORA, MLA_HEADS * MLA_V)
    return w, wq.astype(BF16), wk.astype(BF16), wv.astype(BF16)


def _dup_heads(a):
    s = a.shape[:-1]
    a = a.reshape(s + (GQA_KV_HEADS, 1, GQA_HEAD_DIM))
    return jnp.broadcast_to(a, s + (GQA_KV_HEADS, 2, GQA_HEAD_DIM)).reshape(s + (GQA_KV_HEADS * LANE,))


def _undup_heads(a):
    s = a.shape[:-1]
    return a.reshape(s + (GQA_KV_HEADS, 2, GQA_HEAD_DIM))[..., 0, :]


def _state_to_kernel(h0):
    b = h0.shape[0]
    t = jnp.transpose(h0, (0, 3, 1, 2)).reshape(b, SSD_STATE, SSD_INNER)
    half = SSD_INNER // SSD_GROUPS
    col = jnp.arange(SSD_INNER) // half
    parts = [jnp.where(col == g, t, 0.0) for g in range(SSD_GROUPS)]
    return jnp.concatenate(parts, axis=1)


def _state_from_kernel(st):
    return st.reshape(st.shape[0], SSD_HEADS, SSD_HEADDIM, SSD_STATE)


def kernel(x_prompt, x_sample, c, state_ssd_fwd, state_ssd_bwd, cache_mla_ckv, cache_mla_kpe, cache_gqa_k, cache_gqa_v, c_ctx, g_mix, g_ffn, w_ada, b_ada, w_router, router_bias, w_exp_gate, w_exp_up, w_exp_down, w_in_even, ssd_conv_w, ssd_conv_b, ssd_a_log_fwd, ssd_a_log_bwd, ssd_dt_bias_fwd, ssd_dt_bias_bwd, ssd_d, ssd_norm, mla_q_a_norm, mla_w_q_b, mla_kv_a_norm, mla_w_kv_b, mla_q_norm, mla_k_norm, w_out_even, w_in_odd, gqa_q_norm, gqa_k_norm, w_out_odd):
    nb, ls, _ = x_prompt.shape
    db, dl, _ = x_sample.shape
    depth = w_ada.shape[0]
    n_ctx = nb * ls
    n_tok = n_ctx + db * dl
    ctx_blk0 = 0
    lat_blk0 = n_ctx // dl

    x = jnp.concatenate([x_prompt.reshape(n_ctx, D), x_sample.reshape(db * dl, D)], axis=0)
    rows = 16
    cond = jnp.concatenate([c_ctx[None, :], c, jnp.zeros((rows - 1 - db, D), F32)], axis=0)
    mod_all = _ada_mod(cond, w_ada, b_ada)[:, :1 + db].reshape(depth, 1 + db, 1, 6 * D)

    wr = _pad_lanes(w_router, LANE)
    rb = router_bias.reshape(N_EXPERTS, 1)
    wg_all = w_exp_gate.astype(BF16)
    wu_all = w_exp_up.astype(BF16)
    wd_all = w_exp_down.astype(BF16)

    outs = {}
    for i in range(depth):
        j = i // 2
        mod = mod_all[i]
        g1 = g_mix[i].reshape(1, D)
        g2 = g_ffn[i].reshape(1, D)
        if i % 2 == 0:
            w, wq, wk, wv = _even_weights(w_in_even[j], mla_w_q_b[j], mla_w_kv_b[j])
            z, xbc, qa, kv, dt = _inproj(x, mod, g1, w, (SSD_INNER, SSD_CONV_DIM, MLA_Q_LORA, 2 * LANE, LANE),
                                         n_ctx, dl)
            cw = jnp.pad(ssd_conv_w[j], ((0, 5), (0, 0)))
            cb = ssd_conv_b[j].reshape(1, SSD_CONV_DIM)
            hp = _pad_lanes(jnp.stack([ssd_a_log_fwd[j], ssd_a_log_bwd[j], ssd_dt_bias_fwd[j],
                                       ssd_dt_bias_bwd[j], ssd_d[j]]), LANE)
            hp = jnp.pad(hp, ((0, 3), (0, 0)))
            nw = ssd_norm[j].reshape(1, SSD_INNER)
            y, sf, sb = _ssd(z, xbc, dt, cw, cb, hp, nw, None, None, seq=ls, n_seq=nb, blk0=ctx_blk0)
            h0 = (_state_to_kernel(state_ssd_fwd[:, j]), _state_to_kernel(state_ssd_bwd[:, j]))
            y = _ssd(z, xbc, dt, cw, cb, hp, nw, h0, y, seq=dl, n_seq=db, blk0=lat_blk0)
            if isinstance(y, (list, tuple)):
                y = y[0]
            qan = mla_q_a_norm[j].reshape(1, MLA_Q_LORA)
            kvn = mla_kv_a_norm[j].reshape(1, MLA_KV_LORA)
            qn = _pad_lanes(mla_q_norm[j].reshape(1, MLA_QK), LANE)
            kn = _pad_lanes(mla_k_norm[j].reshape(1, MLA_QK), LANE)
            o, ckv_new, kpe_new = _mla(qa, kv, qan, kvn, wq, wk, wv, qn, kn, None, None, None,
                                       seq=ls, n_seq=nb, blk0=ctx_blk0)
            ckpe = jnp.pad(cache_mla_kpe[:, j], ((0, 0), (0, 0), (MLA_NOPE, LANE - MLA_QK)))
            tables = _rope_tables(dl, MLA_ROPE, MLA_NOPE, 1)
            o = _mla(qa, kv, qan, kvn, wq, wk, wv, qn, kn, (cache_mla_ckv[:, j], ckpe), tables, o,
                     seq=dl, n_seq=db, blk0=lat_blk0)
            if isinstance(o, (list, tuple)):
                o = o[0]
            outs.setdefault("ssd_f", []).append(_state_from_kernel(sf))
            outs.setdefault("ssd_b", []).append(_state_from_kernel(sb))
            outs.setdefault("ckv", []).append(ckv_new.reshape(nb, ls, MLA_KV_LORA))
            outs.setdefault("kpe", []).append(kpe_new[:, MLA_NOPE:MLA_QK].reshape(nb, ls, MLA_ROPE))
            ys = (y, o)
            w_out = w_out_even[j].astype(BF16)
        else:
            wi = w_in_odd[j]
            nq = GQA_HEADS * GQA_HEAD_DIM
            nkv = GQA_KV_HEADS * GQA_HEAD_DIM
            w = jnp.concatenate([wi[:, :nq], _dup_heads(wi[:, nq:nq + nkv]), _dup_heads(wi[:, nq + nkv:])],
                                axis=1).astype(BF16)
            nk = GQA_KV_HEADS * LANE
            q, k, v = _inproj(x, mod, g1, w, (nq, nk, nk), n_ctx, dl)
            qn = jnp.tile(gqa_q_norm[j].reshape(1, GQA_HEAD_DIM), (1, nq // GQA_HEAD_DIM))
            kn = jnp.tile(gqa_k_norm[j].reshape(1, GQA_HEAD_DIM), (1, nk // GQA_HEAD_DIM))
            o, k_new = _gqa(q, k, v, qn, kn, None, None, None, seq=ls, n_seq=nb, blk0=ctx_blk0)
            cache = (_dup_heads(cache_gqa_k[:, j].reshape(db, -1, nkv)),
                     _dup_heads(cache_gqa_v[:, j].reshape(db, -1, nkv)))
            tables = _rope_tables(dl, GQA_HEAD_DIM, 0, 2)
            o = _gqa(q, k, v, qn, kn, cache, tables, o, seq=dl, n_seq=db, blk0=lat_blk0)
            if isinstance(o, (list, tuple)):
                o = o[0]
            outs.setdefault("gk", []).append(
                _undup_heads(k_new).reshape(nb, ls, GQA_KV_HEADS, GQA_HEAD_DIM))
            outs.setdefault("gv", []).append(
                _undup_heads(v[:n_ctx]).reshape(nb, ls, GQA_KV_HEADS, GQA_HEAD_DIM))
            ys = (o,)
            w_out = w_out_odd[j].astype(BF16)
        x1, h2, gates = _outproj(x, mod, ys, w_out, g2, wr, rb, n_ctx, dl)
        x = _moe(x1, mod, h2, gates, wg_all[i], wu_all[i], wd_all[i], n_ctx, dl)

    stack = lambda key: jnp.stack(outs[key], axis=1)
    return (x[:n_ctx].reshape(nb, ls, D), x[n_ctx:].reshape(db, dl, D),
            stack("ssd_f"), stack("ssd_b"), stack("ckv"), stack("kpe"), stack("gk"), stack("gv"))
```

```python
import functools
import math

import jax
import jax.numpy as jnp
from jax import lax
from jax.experimental import pallas as pl
from jax.experimental.pallas import tpu as pltpu

F32 = jnp.float32
BF16 = jnp.bfloat16

D = 1024
EPS = 1e-6
GRID_W = 64
ROPE_THETA = 10000.0
SSD_HEADS = 16
SSD_HEADDIM = 64
SSD_INNER = 1024
SSD_STATE = 64
SSD_GROUPS = 2
SSD_CONV_DIM = 1280
SSD_CHUNK = 128
MLA_HEADS = 8
MLA_Q_LORA = 256
MLA_KV_LORA = 128
MLA_NOPE = 64
MLA_ROPE = 32
MLA_V = 64
MLA_QK = 96
GQA_HEADS = 16
GQA_KV_HEADS = 4
GQA_HEAD_DIM = 64
N_EXPERTS = 16
N_GROUPS = 4
GROUP_SIZE = 4
D_EXPERT = 256

LANE = 128
Q_BLOCK = 128
LOG2E = 1.4426950408889634
TOKEN_TILE = 512
MOE_TILE = 1024
ADA_TILE = 1536
VMEM_LIMIT = 56 * 1024 * 1024


def _cparams(n_axes):
    return pltpu.CompilerParams(dimension_semantics=("arbitrary",) * n_axes,
                                vmem_limit_bytes=VMEM_LIMIT)


def _dot(a, b):
    return jnp.dot(a, b, preferred_element_type=F32)


def _dot_nt(a, b):
    return lax.dot_general(a, b, (((1,), (1,)), ((), ())), preferred_element_type=F32)


def _split3(x):
    hi = x.astype(BF16)
    r1 = x - hi.astype(F32)
    mid = r1.astype(BF16)
    lo = (r1 - mid.astype(F32)).astype(BF16)
    return hi, mid, lo


def _dot_x3(x, e):
    hi, mid, lo = _split3(x)
    return _dot(hi, e) + _dot(mid, e) + _dot(lo, e)


def _dot_3x(e, x):
    hi, mid, lo = _split3(x)
    return _dot(e, hi) + _dot(e, mid) + _dot(e, lo)


def _dot_f32(a, b):
    a0, a1, a2 = _split3(a)
    b0, b1, b2 = _split3(b)
    return (_dot(a0, b0) + (_dot(a0, b1) + _dot(a1, b0))
            + (_dot(a1, b1) + _dot(a0, b2) + _dot(a2, b0)))


def _sigmoid(x):
    return 1.0 / (1.0 + jnp.exp(-x))


def _silu(x):
    return x * _sigmoid(x)


def _iota(shape, axis):
    return lax.broadcasted_iota(jnp.int32, shape, axis)


def _ada_kernel(c_ref, w_ref, b_ref, o_ref):
    a = _silu(c_ref[...]).astype(BF16)
    o_ref[0] = _dot(a, w_ref[0].astype(BF16)) + b_ref[0]


def _ada_mod(cond, w_ada, b_ada):
    depth, _, n = w_ada.shape
    rows = cond.shape[0]
    return pl.pallas_call(
        _ada_kernel,
        grid=(depth, n // ADA_TILE),
        in_specs=[pl.BlockSpec((rows, D), lambda l, j: (0, 0)),
                  pl.BlockSpec((1, D, ADA_TILE), lambda l, j: (l, 0, j)),
                  pl.BlockSpec((1, 1, ADA_TILE), lambda l, j: (l, 0, j))],
        out_specs=pl.BlockSpec((1, rows, ADA_TILE), lambda l, j: (l, 0, j)),
        out_shape=jax.ShapeDtypeStruct((depth, rows, n), F32),
        compiler_params=_cparams(2),
        name="ada_mod",
    )(cond, w_ada, b_ada.reshape(depth, 1, n))


def _mod_row(i, tile, n_ctx_tokens, lat_len):
    n_ctx_tiles = n_ctx_tokens // tile
    per_seq = lat_len // tile
    return jnp.where(i < n_ctx_tiles, 0, 1 + (i - n_ctx_tiles) // per_seq)


def _norm_mod(x, g, shift, scale):
    ms = jnp.mean(x * x, axis=-1, keepdims=True)
    y = x * lax.rsqrt(ms + EPS) * g
    return y * (1.0 + scale) + shift


def _inproj_kernel(x_ref, mod_ref, g_ref, w_ref, *o_refs, widths):
    m = mod_ref[0]
    h = _norm_mod(x_ref[...], g_ref[...], m[:, 0:D], m[:, D:2 * D]).astype(BF16)
    off = 0
    for o_ref, wd in zip(o_refs, widths):
        o_ref[...] = _dot(h, w_ref[:, off:off + wd]).astype(o_ref.dtype)
        off += wd


def _inproj(x, mod, g, w, widths, n_ctx_tokens, lat_len):
    t = x.shape[0]
    n = w.shape[1]
    row = functools.partial(_mod_row, tile=TOKEN_TILE, n_ctx_tokens=n_ctx_tokens, lat_len=lat_len)
    return pl.pallas_call(
        functools.partial(_inproj_kernel, widths=widths),
        grid=(t // TOKEN_TILE,),
        in_specs=[pl.BlockSpec((TOKEN_TILE, D), lambda i: (i, 0)),
                  pl.BlockSpec((1, 1, 6 * D), lambda i: (row(i), 0, 0)),
                  pl.BlockSpec((1, D), lambda i: (0, 0)),
                  pl.BlockSpec((D, n), lambda i: (0, 0))],
        out_specs=[pl.BlockSpec((TOKEN_TILE, wd), lambda i: (i, 0)) for wd in widths],
        out_shape=[jax.ShapeDtypeStruct((t, wd), F32) for wd in widths],
        compiler_params=_cparams(1),
        name="inproj",
    )(x, mod, g, w)


def _ssd_kernel(*refs, seq, has_h0, want_state):
    it = iter(refs)
    z_ref, xbc_ref, dt_ref, cw_ref, cb_ref, hp_ref, nw_ref = [next(it) for _ in range(7)]
    h0_refs = (next(it), next(it)) if has_h0 else None
    y_ref = next(it)
    s_refs = (next(it), next(it)) if want_state else None
    xs_ref, bc_ref, yacc_ref, st_ref = [next(it) for _ in range(4)]

    q = SSD_CHUNK
    nc = seq // q
    gn = SSD_GROUPS * SSD_STATE

    cw = cw_ref[...]
    cb = cb_ref[...]
    rid = _iota((q, SSD_CONV_DIM), 0)

    def conv_body(c, carry):
        r0 = pl.multiple_of(c * q, q)
        xc = xbc_ref[pl.ds(r0, q), :]
        prev = xbc_ref[pl.ds(jnp.maximum(r0 - 1, 0), 1), :]
        nxt = xbc_ref[pl.ds(jnp.minimum(r0 + q, seq - 1), 1), :]
        prev = jnp.where(c > 0, prev, 0.0)
        nxt = jnp.where(c < nc - 1, nxt, 0.0)
        xp = jnp.where(rid == 0, prev, pltpu.roll(xc, 1, axis=0))
        xn = jnp.where(rid == q - 1, nxt, pltpu.roll(xc, q - 1, axis=0))
        v = _silu(xp * cw[0:1] + xc * cw[1:2] + xn * cw[2:3] + cb)
        xs_ref[pl.ds(r0, q), :] = v[:, :SSD_INNER]
        bc_ref[pl.ds(r0, q), :] = v[:, SSD_INNER:]
        return carry

    lax.fori_loop(0, nc, conv_body, 0)

    hp = hp_ref[...]
    expand = (_iota((LANE, SSD_INNER), 1) // SSD_HEADDIM == _iota((LANE, SSD_INNER), 0)).astype(BF16)
    blockmask = (_iota((gn, SSD_INNER), 0) // SSD_STATE
                 == _iota((gn, SSD_INNER), 1) // (SSD_INNER // SSD_GROUPS))
    ri = _iota((q, q), 0)
    ci = _iota((q, q), 1)
    lane = _iota((q, LANE), 1)
    lo_half = lane < SSD_STATE

    def scan_pass(fwd):
        a = -jnp.exp(hp[0:1] if fwd else hp[1:2])
        bias = hp[2:3] if fwd else hp[3:4]
        causal = (ri >= ci) if fwd else (ri <= ci)
        tri = causal.astype(BF16)
        if has_h0:
            st_ref[...] = h0_refs[0 if fwd else 1][0]
        else:
            st_ref[...] = jnp.zeros((gn, SSD_INNER), F32)

        def body(k, carry):
            c = k if fwd else nc - 1 - k
            r0 = pl.multiple_of(c * q, q)
            xs = xs_ref[pl.ds(r0, q), :]
            bc = bc_ref[pl.ds(r0, q), :]
            bm = bc[:, :gn]
            cm = bc[:, gn:]
            xdt_raw = dt_ref[pl.ds(r0, q), :] + bias
            dt = jnp.maximum(xdt_raw, 0.0) + jnp.log1p(jnp.exp(-jnp.abs(xdt_raw)))
            cum = _dot_3x(tri, dt * a)
            cum_t = cum.T
            last = cum[q - 1:q] if fwd else cum[0:1]
            stacked = jnp.concatenate([dt, jnp.exp(cum), jnp.exp(last - cum)], axis=0)
            ex = _dot_x3(stacked, expand)
            dt_e, ecum_e, dend_e = ex[0:q], ex[q:2 * q], ex[2 * q:3 * q]
            cdec_e = ecum_e[q - 1:q] if fwd else ecum_e[0:1]
            xdt = xs * dt_e
            st = st_ref[...]
            cmb = cm.astype(BF16)
            bmb = bm.astype(BF16)
            y = _dot(cmb, st.astype(BF16)) * ecum_e
            cb0 = _dot_nt(jnp.where(lo_half, cm, 0.0).astype(BF16), bmb)
            cb1 = _dot_nt(jnp.where(lo_half, 0.0, cm).astype(BF16), bmb)
            pairs = []
            for j in range(SSD_HEADS // 2):
                cbg = cb0 if j < SSD_HEADS // 4 else cb1
                xp = xdt[:, j * LANE:(j + 1) * LANE]
                acc = None
                for half in range(2):
                    h = 2 * j + half
                    seg = cum[:, h:h + 1] - cum_t[h:h + 1, :]
                    dec = jnp.exp(jnp.where(causal, seg, -jnp.inf))
                    sc = (cbg * dec).astype(BF16)
                    keep = lo_half if half == 0 else jnp.logical_not(lo_half)
                    part = _dot(sc, jnp.where(keep, xp, 0.0).astype(BF16))
                    acc = part if acc is None else acc + part
                pairs.append(acc)
            y = y + jnp.concatenate(pairs, axis=1)
            if fwd:
                yacc_ref[pl.ds(r0, q), :] = y
            else:
                yacc_ref[pl.ds(r0, q), :] += y
            new = _dot(bm.T.astype(BF16), (xdt * dend_e).astype(BF16))
            st_ref[...] = jnp.where(blockmask, st * cdec_e + new, 0.0)
            return carry

        lax.fori_loop(0, nc, body, 0)
        if want_state:
            for j in range(SSD_INNER // LANE):
                g = (j * LANE) // (SSD_INNER // SSD_GROUPS)
                t = st_ref[:, j * LANE:(j + 1) * LANE].T
                s_refs[0 if fwd else 1][0, j * LANE:(j + 1) * LANE, :] = t[:, g * SSD_STATE:(g + 1) * SSD_STATE]

    scan_pass(True)
    scan_pass(False)

    d_e = _dot_x3(jnp.broadcast_to(hp[4:5], (8, LANE)), expand)[0:1]
    nw = nw_ref[...]

    def out_body(c, carry):
        r0 = pl.multiple_of(c * q, q)
        y = yacc_ref[pl.ds(r0, q), :] + xs_ref[pl.ds(r0, q), :] * d_e
        y = y * _silu(z_ref[pl.ds(r0, q), :])
        ms = jnp.mean(y * y, axis=-1, keepdims=True)
        y_ref[pl.ds(r0, q), :] = (y * lax.rsqrt(ms + EPS) * nw).astype(y_ref.dtype)
        return carry

    lax.fori_loop(0, nc, out_body, 0)


def _ssd(z, xbc, dt, cw, cb, hp, nw, h0, y_prev, *, seq, n_seq, blk0):
    t = z.shape[0]
    has_h0 = h0 is not None
    want_state = not has_h0
    gn = SSD_GROUPS * SSD_STATE
    tok = lambda w: pl.BlockSpec((seq, w), lambda b: (blk0 + b, 0))
    full = lambda a: pl.BlockSpec(a.shape, lambda b: (0,) * a.ndim)
    in_specs = [tok(SSD_INNER), tok(SSD_CONV_DIM), tok(LANE), full(cw), full(cb), full(hp), full(nw)]
    args = [z, xbc, dt, cw, cb, hp, nw]
    if has_h0:
        in_specs += [pl.BlockSpec((1, gn, SSD_INNER), lambda b: (b, 0, 0))] * 2
        args += list(h0)
    out_shape = [jax.ShapeDtypeStruct((t, SSD_INNER), BF16)]
    out_specs = [tok(SSD_INNER)]
    aliases = {}
    if y_prev is not None:
        in_specs.append(pl.BlockSpec(memory_space=pl.ANY))
        args.append(y_prev)
        aliases = {len(args) - 1: 0}
    if want_state:
        out_shape += [jax.ShapeDtypeStruct((n_seq, SSD_INNER, SSD_STATE), F32)] * 2
        out_specs += [pl.BlockSpec((1, SSD_INNER, SSD_STATE), lambda b: (b, 0, 0))] * 2

    def kern(*refs):
        refs = list(refs)
        if y_prev is not None:
            del refs[len(args) - 1]
        _ssd_kernel(*refs, seq=seq, has_h0=has_h0, want_state=want_state)

    return pl.pallas_call(
        kern,
        grid=(n_seq,),
        in_specs=in_specs,
        out_specs=out_specs,
        out_shape=out_shape,
        scratch_shapes=[pltpu.VMEM((seq, SSD_INNER), F32),
                        pltpu.VMEM((seq, 2 * gn), F32),
                        pltpu.VMEM((seq, SSD_INNER), F32),
                        pltpu.VMEM((gn, SSD_INNER), F32)],
        input_output_aliases=aliases,
        compiler_params=_cparams(1),
        name="ssd_lat" if has_h0 else "ssd_ctx",
    )(*args)


def _rope(x, cos, sin, first_half, dist):
    partner = jnp.where(first_half, pltpu.roll(x, LANE - dist, axis=1), pltpu.roll(x, dist, axis=1))
    return x * cos + partner * sin


def _softmax_pv(ss, vals):
    m = ss[0].max(axis=-1, keepdims=True)
    for s in ss[1:]:
        m = jnp.maximum(m, s.max(axis=-1, keepdims=True))
    ps = [jnp.exp2(s - m) for s in ss]
    den = ps[0].sum(axis=-1, keepdims=True)
    for p in ps[1:]:
        den = den + p.sum(axis=-1, keepdims=True)
    o = _dot(ps[0].astype(BF16), vals[0]())
    for p, v in zip(ps[1:], vals[1:]):
        o = o + _dot(p.astype(BF16), v())
    return o / den


def _attend_blocks(seq, units, emit, s_scr):
    nblk = seq // Q_BLOCK

    def scores(u, blk, static):
        q_ref, rows, keys, _ = units[u]
        r = blk * rows if static else pl.multiple_of(blk * rows, rows)
        qb = q_ref[pl.ds(r, rows), :]
        return [_dot_nt(qb, k()) for k in keys]

    if s_scr is None:
        for i in range(nblk):
            emit(i * Q_BLOCK, [_softmax_pv(scores(u, i, True), units[u][3]) for u in range(len(units))])
        return

    def put(slot, blk):
        for u in range(len(units)):
            off = 0
            for s in scores(u, blk, False):
                s_scr[slot][u][:, off:off + s.shape[1]] = s
                off += s.shape[1]

    def take(slot, blk):
        outs = []
        for u, (_, _, keys, vals) in enumerate(units):
            off = 0
            ss = []
            for k in keys:
                n = k().shape[0]
                ss.append(s_scr[slot][u][:, off:off + n])
                off += n
            outs.append(_softmax_pv(ss, vals))
        emit(pl.multiple_of(blk * Q_BLOCK, Q_BLOCK), outs)

    put(0, 0)

    def body(ii, carry):
        b0 = 2 * ii
        put(1, b0 + 1)
        take(0, b0)
        put(0, jnp.where(b0 + 2 >= nblk, 0, b0 + 2))
        take(1, b0 + 1)
        return carry

    lax.fori_loop(0, nblk // 2, body, 0)


def _mla_kernel(*refs, seq, lat):
    it = iter(refs)
    qa_ref, kv_ref, qan_ref, kvn_ref, wq_ref, wk_ref, wv_ref, qn_ref, kn_ref = [next(it) for _ in range(9)]
    if lat:
        cckv_ref, ckpe_ref, cos_ref, sin_ref = [next(it) for _ in range(4)]
    o_ref = next(it)
    if not lat:
        ckv_out, kpe_out = next(it), next(it)
    q_s, k_s, v_s, qh0_s, qh1_s, kh0_s, kh1_s = [next(it) for _ in range(7)]
    s_scr = None
    if lat:
        kc_s, vc_s, kch0_s, kch1_s = [next(it) for _ in range(4)]
        s_scr = [[next(it), next(it)], [next(it), next(it)]]

    qscale = MLA_QK ** -0.5 * LOG2E
    kv = kv_ref[...]
    kpe = kv[:, LANE:]
    ckv_raw = kv[:, :LANE]
    ckv = ckv_raw * lax.rsqrt(jnp.mean(ckv_raw * ckv_raw, axis=-1, keepdims=True) + EPS) * kvn_ref[...]
    qa = qa_ref[...]
    qa = qa * lax.rsqrt(jnp.mean(qa * qa, axis=-1, keepdims=True) + EPS) * qan_ref[...]
    if not lat:
        ckv_out[...] = ckv
        kpe_out[...] = kpe
    ckvb = ckv.astype(BF16)
    q_s[...] = _dot(qa.astype(BF16), wq_ref[...])
    k_s[...] = _dot(ckvb, wk_ref[...])
    v_s[...] = _dot(ckvb, wv_ref[...]).astype(BF16)
    if lat:
        ccb = cckv_ref[0].astype(BF16)
        kc_s[...] = _dot(ccb, wk_ref[...])
        vc_s[...] = _dot(ccb, wv_ref[...]).astype(BF16)
        ckpe = ckpe_ref[0]
        cos = cos_ref[...]
        sin = sin_ref[...]
        lane = _iota((seq, LANE), 1)
        first_half = (lane % (MLA_ROPE // 2)) < (MLA_ROPE // 4)
    qn = qn_ref[...]
    kn = kn_ref[...]
    lo_lanes = _iota((Q_BLOCK, LANE), 1) < MLA_V

    def head_norm(x, g):
        return x * lax.rsqrt(jnp.sum(x * x, axis=-1, keepdims=True) * (1.0 / MLA_QK) + EPS) * g

    for j in range(MLA_HEADS // 2):
        pair = slice(j * LANE, (j + 1) * LANE)
        for half, (qh_s, kh_s) in enumerate(((qh0_s, kh0_s), (qh1_s, kh1_s))):
            sl = slice((2 * j + half) * LANE, (2 * j + half + 1) * LANE)
            qh = head_norm(q_s[:, sl], qn)
            kh = head_norm(k_s[:, sl] + kpe, kn)
            if lat:
                qh = _rope(qh, cos, sin, first_half, MLA_ROPE // 4)
                kh = _rope(kh, cos, sin, first_half, MLA_ROPE // 4)
                (kch0_s, kch1_s)[half][...] = head_norm(kc_s[:, sl] + ckpe, kn).astype(BF16)
            qh_s[...] = (qh * qscale).astype(BF16)
            kh_s[...] = kh.astype(BF16)

        def emit(r0, outs, pair=pair):
            o_ref[pl.ds(r0, Q_BLOCK), pair] = jnp.where(lo_lanes, outs[0], outs[1]).astype(o_ref.dtype)

        units = []
        for half, (qh_s, kh_s) in enumerate(((qh0_s, kh0_s), (qh1_s, kh1_s))):
            keys = [lambda r=kh_s: r[...]]
            vals = [lambda pair=pair: v_s[:, pair]]
            if lat:
                keys = [lambda r=(kch0_s, kch1_s)[half]: r[...]] + keys
                vals = [lambda pair=pair: vc_s[:, pair]] + vals
            units.append((qh_s, Q_BLOCK, keys, vals))
        _attend_blocks(seq, units, emit, s_scr)


def _mla(qa, kv, qan, kvn, wq, wk, wv, qn, kn, cache, tables, o_prev, *, seq, n_seq, blk0):
    t = qa.shape[0]
    lat = cache is not None
    hv = MLA_HEADS * MLA_V
    tok = lambda w: pl.BlockSpec((seq, w), lambda b: (blk0 + b, 0))
    full = lambda a: pl.BlockSpec(a.shape, lambda b: (0,) * a.ndim)
    args = [qa, kv, qan, kvn, wq, wk, wv, qn, kn]
    in_specs = [tok(MLA_Q_LORA), tok(2 * LANE)] + [full(a) for a in args[2:]]
    past = 0
    if lat:
        past = cache[0].shape[1]
        in_specs += [pl.BlockSpec((1, past, LANE), lambda b: (b, 0, 0))] * 2 + [full(tables[0]), full(tables[1])]
        args += [cache[0], cache[1], tables[0], tables[1]]
    out_shape = [jax.ShapeDtypeStruct((t, hv), BF16)]
    out_specs = [tok(hv)]
    aliases = {}
    if o_prev is not None:
        in_specs.append(pl.BlockSpec(memory_space=pl.ANY))
        args.append(o_prev)
        aliases = {len(args) - 1: 0}
    if not lat:
        out_shape += [jax.ShapeDtypeStruct((n_seq * seq, LANE), F32)] * 2
        out_specs += [pl.BlockSpec((seq, LANE), lambda b: (b, 0))] * 2
    scratch = [pltpu.VMEM((seq, MLA_HEADS * LANE), F32),
               pltpu.VMEM((seq, MLA_HEADS * LANE), F32),
               pltpu.VMEM((seq, hv), BF16)] + [pltpu.VMEM((seq, LANE), BF16)] * 4
    if lat:
        scratch += [pltpu.VMEM((past, MLA_HEADS * LANE), F32),
                    pltpu.VMEM((past, hv), BF16),
                    pltpu.VMEM((past, LANE), BF16),
                    pltpu.VMEM((past, LANE), BF16)]
        scratch += [pltpu.VMEM((Q_BLOCK, past + seq), F32)] * 4

    def kern(*refs):
        refs = list(refs)
        if o_prev is not None:
            del refs[len(args) - 1]
        _mla_kernel(*refs, seq=seq, lat=lat)

    return pl.pallas_call(
        kern,
        grid=(n_seq,),
        in_specs=in_specs,
        out_specs=out_specs,
        out_shape=out_shape,
        scratch_shapes=scratch,
        input_output_aliases=aliases,
        compiler_params=_cparams(1),
        name="mla_lat" if lat else "mla_ctx",
    )(*args)


def _gqa_kernel(*refs, seq, lat):
    it = iter(refs)
    q_ref, k_ref, v_ref, qn_ref, kn_ref = [next(it) for _ in range(5)]
    if lat:
        ck_ref, cv_ref, cos_ref, sin_ref = [next(it) for _ in range(4)]
    o_ref = next(it)
    if not lat:
        kn_out = next(it)
    qn_s, kn_s, vb_s, q2_s = [next(it) for _ in range(4)]
    s_scr = None
    if lat:
        ckb_s, cvb_s = next(it), next(it)
        s_scr = [[next(it)], [next(it)]]

    qscale = GQA_HEAD_DIM ** -0.5 * LOG2E
    hd = GQA_HEAD_DIM
    nq = GQA_HEADS * hd
    nk = GQA_KV_HEADS * LANE

    def half_norm(x, g, width):
        n_half = width // hd
        ind = (_iota((width, LANE), 0) // hd == _iota((width, LANE), 1)).astype(BF16)
        ind_t = (_iota((LANE, width), 1) // hd == _iota((LANE, width), 0)).astype(BF16)
        del n_half
        ss = _dot_x3(x * x, ind)
        r = lax.rsqrt(ss * (1.0 / hd) + EPS)
        return x * _dot_x3(r, ind_t) * g

    qn = half_norm(q_ref[...], qn_ref[...], nq)
    kn = half_norm(k_ref[...], kn_ref[...], nk)
    if not lat:
        kn_out[...] = kn
    if lat:
        cos = cos_ref[...]
        sin = sin_ref[...]
        lane = _iota((seq, LANE), 1)
        first_half = (lane % (hd // 2)) < (hd // 4)
        for j in range(nq // LANE):
            sl = slice(j * LANE, (j + 1) * LANE)
            qn_s[:, sl] = _rope(qn[:, sl], cos, sin, first_half, hd // 4)
        for j in range(nk // LANE):
            sl = slice(j * LANE, (j + 1) * LANE)
            kn_s[:, sl] = _rope(kn[:, sl], cos, sin, first_half, hd // 4).astype(BF16)
        ckb_s[...] = ck_ref[0].astype(BF16)
        cvb_s[...] = cv_ref[0].astype(BF16)
    else:
        qn_s[...] = qn
        kn_s[...] = kn.astype(BF16)
    vb_s[...] = v_ref[...].astype(BF16)
    lo_q = _iota((Q_BLOCK, LANE), 1) < hd

    for j in range(GQA_HEADS // 2):
        pair = slice(j * LANE, (j + 1) * LANE)
        g = (2 * j) // (GQA_HEADS // GQA_KV_HEADS)
        gs = slice(g * LANE, (g + 1) * LANE)
        for i in range(seq // Q_BLOCK):
            qp = qn_s[i * Q_BLOCK:(i + 1) * Q_BLOCK, pair] * qscale
            q2_s[2 * i * Q_BLOCK:(2 * i + 1) * Q_BLOCK, :] = jnp.where(lo_q, qp, 0.0).astype(BF16)
            q2_s[(2 * i + 1) * Q_BLOCK:(2 * i + 2) * Q_BLOCK, :] = jnp.where(lo_q, 0.0, qp).astype(BF16)

        def emit(r0, outs, pair=pair):
            o = outs[0]
            o_ref[pl.ds(r0, Q_BLOCK), pair] = jnp.where(lo_q, o[:Q_BLOCK], o[Q_BLOCK:]).astype(o_ref.dtype)

        keys = [lambda gs=gs: kn_s[:, gs]]
        vals = [lambda gs=gs: vb_s[:, gs]]
        if lat:
            keys = [lambda gs=gs: ckb_s[:, gs]] + keys
            vals = [lambda gs=gs: cvb_s[:, gs]] + vals
        _attend_blocks(seq, [(q2_s, 2 * Q_BLOCK, keys, vals)], emit, s_scr)


def _gqa(q, k, v, qn, kn, cache, tables, o_prev, *, seq, n_seq, blk0):
    t = q.shape[0]
    lat = cache is not None
    nq = GQA_HEADS * GQA_HEAD_DIM
    nk = GQA_KV_HEADS * LANE
    tok = lambda w: pl.BlockSpec((seq, w), lambda b: (blk0 + b, 0))
    full = lambda a: pl.BlockSpec(a.shape, lambda b: (0,) * a.ndim)
    args = [q, k, v, qn, kn]
    in_specs = [tok(nq), tok(nk), tok(nk), full(qn), full(kn)]
    past = 0
    if lat:
        past = cache[0].shape[1]
        in_specs += [pl.BlockSpec((1, past, nk), lambda b: (b, 0, 0))] * 2 + [full(tables[0]), full(tables[1])]
        args += [cache[0], cache[1], tables[0], tables[1]]
    out_shape = [jax.ShapeDtypeStruct((t, nq), BF16)]
    out_specs = [tok(nq)]
    aliases = {}
    if o_prev is not None:
        in_specs.append(pl.BlockSpec(memory_space=pl.ANY))
        args.append(o_prev)
        aliases = {len(args) - 1: 0}
    if not lat:
        out_shape += [jax.ShapeDtypeStruct((n_seq * seq, nk), F32)]
        out_specs += [pl.BlockSpec((seq, nk), lambda b: (b, 0))]
    scratch = [pltpu.VMEM((seq, nq), F32),
               pltpu.VMEM((seq, nk), BF16),
               pltpu.VMEM((seq, nk), BF16),
               pltpu.VMEM((2 * seq, LANE), BF16)]
    if lat:
        scratch += [pltpu.VMEM((past, nk), BF16), pltpu.VMEM((past, nk), BF16)]
        scratch += [pltpu.VMEM((2 * Q_BLOCK, past + seq), F32)] * 2

    def kern(*refs):
        refs = list(refs)
        if o_prev is not None:
            del refs[len(args) - 1]
        _gqa_kernel(*refs, seq=seq, lat=lat)

    return pl.pallas_call(
        kern,
        grid=(n_seq,),
        in_specs=in_specs,
        out_specs=out_specs,
        out_shape=out_shape,
        scratch_shapes=scratch,
        input_output_aliases=aliases,
        compiler_params=_cparams(1),
        name="gqa_lat" if lat else "gqa_ctx",
    )(*args)


def _route(sel, sc):
    gs = []
    for g in range(N_GROUPS):
        r = sel[g * GROUP_SIZE:(g + 1) * GROUP_SIZE]
        best = None
        for i in range(GROUP_SIZE):
            for j in range(i + 1, GROUP_SIZE):
                s = r[i] + r[j]
                best = s if best is None else jnp.maximum(best, s)
        gs.append(best)
    cur = gs[0]
    grp = jnp.zeros_like(cur, dtype=jnp.int32)
    for g in range(1, N_GROUPS):
        upd = gs[g] > cur
        grp = jnp.where(upd, g, grp)
        cur = jnp.where(upd, gs[g], cur)

    def pick(rows, i):
        v = rows[i]
        for g in range(1, N_GROUPS):
            v = jnp.where(grp == g, rows[g * GROUP_SIZE + i], v)
        return v

    cand = [pick(sel, i) for i in range(GROUP_SIZE)]
    csc = [pick(sc, i) for i in range(GROUP_SIZE)]
    v1, i1, w1 = cand[0], jnp.zeros_like(grp), csc[0]
    for i in range(1, GROUP_SIZE):
        upd = cand[i] > v1
        v1 = jnp.where(upd, cand[i], v1)
        i1 = jnp.where(upd, i, i1)
        w1 = jnp.where(upd, csc[i], w1)
    v2 = i2 = w2 = None
    for i in range(GROUP_SIZE):
        ok = i1 != i
        if v2 is None:
            v2 = jnp.where(ok, cand[i], -jnp.inf)
            i2 = jnp.zeros_like(grp)
            w2 = csc[i]
        else:
            upd = ok & (cand[i] > v2)
            v2 = jnp.where(upd, cand[i], v2)
            i2 = jnp.where(upd, i, i2)
            w2 = jnp.where(upd, csc[i], w2)
    tot = w1 + w2
    e1 = grp * GROUP_SIZE + i1
    e2 = grp * GROUP_SIZE + i2
    return [jnp.where(e1 == e, w1 / tot, 0.0) + jnp.where(e2 == e, w2 / tot, 0.0)
            for e in range(N_EXPERTS)]


def _outproj_kernel(*refs, n_y):
    x_ref, mod_ref = refs[0], refs[1]
    y_refs = refs[2:2 + n_y]
    w_ref, g_ref, wr_ref, rb_ref, x1_ref, h2_ref, gate_ref = refs[2 + n_y:]
    m = mod_ref[0]
    off = 0
    out = None
    for y_ref in y_refs:
        k = y_ref.shape[1]
        part = _dot(y_ref[...], w_ref[off:off + k, :])
        out = part if out is None else out + part
        off += k
    x1 = x_ref[...] + m[:, 2 * D:3 * D] * out
    x1_ref[...] = x1
    h2 = _norm_mod(x1, g_ref[...], m[:, 3 * D:4 * D], m[:, 4 * D:5 * D])
    h2_ref[...] = h2.astype(BF16)
    logits = _dot_f32(h2, wr_ref[...])
    sc_t = _sigmoid(logits.T[0:N_EXPERTS, :])
    sel_t = sc_t + rb_ref[...]
    gates = _route([sel_t[e:e + 1] for e in range(N_EXPERTS)], [sc_t[e:e + 1] for e in range(N_EXPERTS)])
    t = x1.shape[0]
    gates_t = jnp.concatenate(gates + [jnp.zeros((LANE - N_EXPERTS, t), F32)], axis=0)
    gate_ref[...] = gates_t.T


def _outproj(x, mod, ys, w, g, wr, rb, n_ctx_tokens, lat_len):
    t = x.shape[0]
    row = functools.partial(_mod_row, tile=TOKEN_TILE, n_ctx_tokens=n_ctx_tokens, lat_len=lat_len)
    full = lambda a: pl.BlockSpec(a.shape, lambda i: (0,) * a.ndim)
    return pl.pallas_call(
        functools.partial(_outproj_kernel, n_y=len(ys)),
        grid=(t // TOKEN_TILE,),
        in_specs=[pl.BlockSpec((TOKEN_TILE, D), lambda i: (i, 0)),
                  pl.BlockSpec((1, 1, 6 * D), lambda i: (row(i), 0, 0))]
                 + [pl.BlockSpec((TOKEN_TILE, y.shape[1]), lambda i: (i, 0)) for y in ys]
                 + [full(w), full(g), full(wr), full(rb)],
        out_specs=[pl.BlockSpec((TOKEN_TILE, D), lambda i: (i, 0)),
                   pl.BlockSpec((TOKEN_TILE, D), lambda i: (i, 0)),
                   pl.BlockSpec((TOKEN_TILE, LANE), lambda i: (i, 0))],
        out_shape=[jax.ShapeDtypeStruct((t, D), F32),
                   jax.ShapeDtypeStruct((t, D), BF16),
                   jax.ShapeDtypeStruct((t, LANE), F32)],
        compiler_params=_cparams(1),
        name="outproj_router",
    )(x, mod, *ys, w, g, wr, rb)


def _moe_kernel(x_ref, mod_ref, h_ref, gate_ref, wg_ref, wu_ref, wd_ref, o_ref, acc_ref):
    e = pl.program_id(1)

    @pl.when(e == 0)
    def _():
        acc_ref[...] = jnp.zeros_like(acc_ref)

    h = h_ref[...]
    a = _dot(h, wg_ref[0])
    u = _dot(h, wu_ref[0])
    gates = gate_ref[...]
    gcol = jnp.sum(jnp.where(_iota(gates.shape, 1) == e, gates, 0.0), axis=-1, keepdims=True)
    act = (_silu(a) * u * gcol).astype(BF16)
    acc_ref[...] += _dot(act, wd_ref[0])

    @pl.when(e == N_EXPERTS - 1)
    def _():
        o_ref[...] = x_ref[...] + mod_ref[0][:, 5 * D:6 * D] * acc_ref[...]


def _moe(x1, mod, h2, gates, wg, wu, wd, n_ctx_tokens, lat_len):
    t = x1.shape[0]
    row = functools.partial(_mod_row, tile=MOE_TILE, n_ctx_tokens=n_ctx_tokens, lat_len=lat_len)
    return pl.pallas_call(
        _moe_kernel,
        grid=(t // MOE_TILE, N_EXPERTS),
        in_specs=[pl.BlockSpec((MOE_TILE, D), lambda i, e: (i, 0)),
                  pl.BlockSpec((1, 1, 6 * D), lambda i, e: (row(i), 0, 0)),
                  pl.BlockSpec((MOE_TILE, D), lambda i, e: (i, 0)),
                  pl.BlockSpec((MOE_TILE, LANE), lambda i, e: (i, 0)),
                  pl.BlockSpec((1, D, D_EXPERT), lambda i, e: (e, 0, 0)),
                  pl.BlockSpec((1, D, D_EXPERT), lambda i, e: (e, 0, 0)),
                  pl.BlockSpec((1, D_EXPERT, D), lambda i, e: (e, 0, 0))],
        out_specs=pl.BlockSpec((MOE_TILE, D), lambda i, e: (i, 0)),
        out_shape=jax.ShapeDtypeStruct((t, D), F32),
        scratch_shapes=[pltpu.VMEM((MOE_TILE, D), F32)],
        compiler_params=_cparams(2),
        name="moe_experts",
    )(x1, mod, h2, gates, wg, wu, wd)


def _rope_angles(pos, half):
    inv = jnp.power(ROPE_THETA, -jnp.arange(half, dtype=F32) / half)
    ang = pos.astype(F32)[:, None] * inv[None, :]
    return jnp.cos(ang), jnp.sin(ang)


def _rope_tables(seq, rot, lead, lane_tiles):
    rows = seq // GRID_W
    row = jnp.repeat(jnp.arange(rows), GRID_W)
    col = jnp.tile(jnp.arange(GRID_W), rows)
    cr, sr = _rope_angles(row, rot // 4)
    cc, sc = _rope_angles(col, rot // 4)
    cos = jnp.concatenate([cr, cr, cc, cc], axis=1)
    sin = jnp.concatenate([-sr, sr, -sc, sc], axis=1)
    width = LANE // lane_tiles
    pad = width - lead - rot
    cos = jnp.concatenate([jnp.ones((seq, lead), F32), cos, jnp.ones((seq, pad), F32)], axis=1)
    sin = jnp.concatenate([jnp.zeros((seq, lead), F32), sin, jnp.zeros((seq, pad), F32)], axis=1)
    return jnp.tile(cos, (1, lane_tiles)), jnp.tile(sin, (1, lane_tiles))


def _pad_lanes(a, width):
    return jnp.pad(a, [(0, 0)] * (a.ndim - 1) + [(0, width - a.shape[-1])])


def _even_weights(w_in, w_q_b, w_kv_b):
    z = w_in[:, 0:1024]
    xbc = w_in[:, 1024:2304]
    dt = w_in[:, 2304:2320]
    qa = w_in[:, 2320:2576]
    ckv = w_in[:, 2576:2704]
    kpe = w_in[:, 2704:2736]
    zeros = lambda n: jnp.zeros((D, n), w_in.dtype)
    w = jnp.concatenate([z, xbc, qa, ckv, zeros(MLA_NOPE), kpe, zeros(LANE - MLA_QK), dt,
                         zeros(LANE - SSD_HEADS)], axis=1).astype(BF16)
    wq = _pad_lanes(w_q_b.reshape(MLA_Q_LORA, MLA_HEADS, MLA_QK), LANE).reshape(MLA_Q_LORA, MLA_HEADS * LANE)
    kvb = w_kv_b.reshape(MLA_KV_LORA, MLA_HEADS, MLA_NOPE + MLA_V)
    wk = _pad_lanes(kvb[:, :, :MLA_NOPE], LANE).reshape(MLA_KV_LORA, MLA_HEADS * LANE)
    wv = kvb[:, :, MLA_NOPE:].reshape(MLA_KV_LORA, MLA_HEADS * MLA_V)
    return w, wq.astype(BF16), wk.astype(BF16), wv.astype(BF16)


def _dup_heads(a):
    s = a.shape[:-1]
    a = a.reshape(s + (GQA_KV_HEADS, 1, GQA_HEAD_DIM))
    return jnp.broadcast_to(a, s + (GQA_KV_HEADS, 2, GQA_HEAD_DIM)).reshape(s + (GQA_KV_HEADS * LANE,))


def _undup_heads(a):
    s = a.shape[:-1]
    return a.reshape(s + (GQA_KV_HEADS, 2, GQA_HEAD_DIM))[..., 0, :]


def _state_to_kernel(h0):
    b = h0.shape[0]
    t = jnp.transpose(h0, (0, 3, 1, 2)).reshape(b, SSD_STATE, SSD_INNER)
    half = SSD_INNER // SSD_GROUPS
    col = jnp.arange(SSD_INNER) // half
    parts = [jnp.where(col == g, t, 0.0) for g in range(SSD_GROUPS)]
    return jnp.concatenate(parts, axis=1)


def _state_from_kernel(st):
    return st.reshape(st.shape[0], SSD_HEADS, SSD_HEADDIM, SSD_STATE)


def kernel(x_prompt, x_sample, c, state_ssd_fwd, state_ssd_bwd, cache_mla_ckv, cache_mla_kpe, cache_gqa_k, cache_gqa_v, c_ctx, g_mix, g_ffn, w_ada, b_ada, w_router, router_bias, w_exp_gate, w_exp_up, w_exp_down, w_in_even, ssd_conv_w, ssd_conv_b, ssd_a_log_fwd, ssd_a_log_bwd, ssd_dt_bias_fwd, ssd_dt_bias_bwd, ssd_d, ssd_norm, mla_q_a_norm, mla_w_q_b, mla_kv_a_norm, mla_w_kv_b, mla_q_norm, mla_k_norm, w_out_even, w_in_odd, gqa_q_norm, gqa_k_norm, w_out_odd):
    nb, ls, _ = x_prompt.shape
    db, dl, _ = x_sample.shape
    depth = w_ada.shape[0]
    n_ctx = nb * ls
    n_tok = n_ctx + db * dl
    ctx_blk0 = 0
    lat_blk0 = n_ctx // dl

    x = jnp.concatenate([x_prompt.reshape(n_ctx, D), x_sample.reshape(db * dl, D)], axis=0)
    rows = 16
    cond = jnp.concatenate([c_ctx[None, :], c, jnp.zeros((rows - 1 - db, D), F32)], axis=0)
    mod_all = _ada_mod(cond, w_ada, b_ada)[:, :1 + db].reshape(depth, 1 + db, 1, 6 * D)

    wr = _pad_lanes(w_router, LANE)
    rb = router_bias.reshape(N_EXPERTS, 1)
    wg_all = w_exp_gate.astype(BF16)
    wu_all = w_exp_up.astype(BF16)
    wd_all = w_exp_down.astype(BF16)

    outs = {}
    for i in range(depth):
        j = i // 2
        mod = mod_all[i]
        g1 = g_mix[i].reshape(1, D)
        g2 = g_ffn[i].reshape(1, D)
        if i % 2 == 0:
            w, wq, wk, wv = _even_weights(w_in_even[j], mla_w_q_b[j], mla_w_kv_b[j])
            z, xbc, qa, kv, dt = _inproj(x, mod, g1, w, (SSD_INNER, SSD_CONV_DIM, MLA_Q_LORA, 2 * LANE, LANE),
                                         n_ctx, dl)
            cw = jnp.pad(ssd_conv_w[j], ((0, 5), (0, 0)))
            cb = ssd_conv_b[j].reshape(1, SSD_CONV_DIM)
            hp = _pad_lanes(jnp.stack([ssd_a_log_fwd[j], ssd_a_log_bwd[j], ssd_dt_bias_fwd[j],
                                       ssd_dt_bias_bwd[j], ssd_d[j]]), LANE)
            hp = jnp.pad(hp, ((0, 3), (0, 0)))
            nw = ssd_norm[j].reshape(1, SSD_INNER)
            y, sf, sb = _ssd(z, xbc, dt, cw, cb, hp, nw, None, None, seq=ls, n_seq=nb, blk0=ctx_blk0)
            h0 = (_state_to_kernel(state_ssd_fwd[:, j]), _state_to_kernel(state_ssd_bwd[:, j]))
            y = _ssd(z, xbc, dt, cw, cb, hp, nw, h0, y, seq=dl, n_seq=db, blk0=lat_blk0)
            if isinstance(y, (list, tuple)):
                y = y[0]
            qan = mla_q_a_norm[j].reshape(1, MLA_Q_LORA)
            kvn = mla_kv_a_norm[j].reshape(1, MLA_KV_LORA)
            qn = _pad_lanes(mla_q_norm[j].reshape(1, MLA_QK), LANE)
            kn = _pad_lanes(mla_k_norm[j].reshape(1, MLA_QK), LANE)
            o, ckv_new, kpe_new = _mla(qa, kv, qan, kvn, wq, wk, wv, qn, kn, None, None, None,
                                       seq=ls, n_seq=nb, blk0=ctx_blk0)
            ckpe = jnp.pad(cache_mla_kpe[:, j], ((0, 0), (0, 0), (MLA_NOPE, LANE - MLA_QK)))
            tables = _rope_tables(dl, MLA_ROPE, MLA_NOPE, 1)
            o = _mla(qa, kv, qan, kvn, wq, wk, wv, qn, kn, (cache_mla_ckv[:, j], ckpe), tables, o,
                     seq=dl, n_seq=db, blk0=lat_blk0)
            if isinstance(o, (list, tuple)):
                o = o[0]
            outs.setdefault("ssd_f", []).append(_state_from_kernel(sf))
            outs.setdefault("ssd_b", []).append(_state_from_kernel(sb))
            outs.setdefault("ckv", []).append(ckv_new.reshape(nb, ls, MLA_KV_LORA))
            outs.setdefault("kpe", []).append(kpe_new[:, MLA_NOPE:MLA_QK].reshape(nb, ls, MLA_ROPE))
            ys = (y, o)
            w_out = w_out_even[j].astype(BF16)
        else:
            wi = w_in_odd[j]
            nq = GQA_HEADS * GQA_HEAD_DIM
            nkv = GQA_KV_HEADS * GQA_HEAD_DIM
            w = jnp.concatenate([wi[:, :nq], _dup_heads(wi[:, nq:nq + nkv]), _dup_heads(wi[:, nq + nkv:])],
                                axis=1).astype(BF16)
            nk = GQA_KV_HEADS * LANE
            q, k, v = _inproj(x, mod, g1, w, (nq, nk, nk), n_ctx, dl)
            qn = jnp.tile(gqa_q_norm[j].reshape(1, GQA_HEAD_DIM), (1, nq // GQA_HEAD_DIM))
            kn = jnp.tile(gqa_k_norm[j].reshape(1, GQA_HEAD_DIM), (1, nk // GQA_HEAD_DIM))
            o, k_new = _gqa(q, k, v, qn, kn, None, None, None, seq=ls, n_seq=nb, blk0=ctx_blk0)
            cache = (_dup_heads(cache_gqa_k[:, j].reshape(db, -1, nkv)),
                     _dup_heads(cache_gqa_v[:, j].reshape(db, -1, nkv)))
            tables = _rope_tables(dl, GQA_HEAD_DIM, 0, 2)
            o = _gqa(q, k, v, qn, kn, cache, tables, o, seq=dl, n_seq=db, blk0=lat_blk0)
            if isinstance(o, (list, tuple)):
                o = o[0]
            outs.setdefault("gk", []).append(
                _undup_heads(k_new).reshape(nb, ls, GQA_KV_HEADS, GQA_HEAD_DIM))
            outs.setdefault("gv", []).append(
                _undup_heads(v[:n_ctx]).reshape(nb, ls, GQA_KV_HEADS, GQA_HEAD_DIM))
            ys = (o,)
            w_out = w_out_odd[j].astype(BF16)
        x1, h2, gates = _outproj(x, mod, ys, w_out, g2, wr, rb, n_ctx, dl)
        x = _moe(x1, mod, h2, gates, wg_all[i], wu_all[i], wd_all[i], n_ctx, dl)

    stack = lambda key: jnp.stack(outs[key], axis=1)
    return (x[:n_ctx].reshape(nb, ls, D), x[n_ctx:].reshape(db, dl, D),
            stack("ssd_f"), stack("ssd_b"), stack("ckv"), stack("kpe"), stack("gk"), stack("gv"))
```

```python
import functools
import math

import jax
import jax.numpy as jnp
from jax import lax
from jax.experimental import pallas as pl
from jax.experimental.pallas import tpu as pltpu

F32 = jnp.float32
BF16 = jnp.bfloat16

D = 1024
EPS = 1e-6
GRID_W = 64
ROPE_THETA = 10000.0
SSD_HEADS = 16
SSD_HEADDIM = 64
SSD_INNER = 1024
SSD_STATE = 64
SSD_GROUPS = 2
SSD_CONV_DIM = 1280
SSD_CHUNK = 128
MLA_HEADS = 8
MLA_Q_LORA = 256
MLA_KV_LORA = 128
MLA_NOPE = 64
MLA_ROPE = 32
MLA_V = 64
MLA_QK = 96
GQA_HEADS = 16
GQA_KV_HEADS = 4
GQA_HEAD_DIM = 64
N_EXPERTS = 16
N_GROUPS = 4
GROUP_SIZE = 4
D_EXPERT = 256

LANE = 128
Q_BLOCK = 128
LOG2E = 1.4426950408889634
TOKEN_TILE = 512
MOE_TILE = 1024
ADA_TILE = 1536
VMEM_LIMIT = 56 * 1024 * 1024


def _cparams(n_axes):
    return pltpu.CompilerParams(dimension_semantics=("arbitrary",) * n_axes,
                                vmem_limit_bytes=VMEM_LIMIT)


def _dot(a, b):
    return jnp.dot(a, b, preferred_element_type=F32)


def _dot_nt(a, b):
    return lax.dot_general(a, b, (((1,), (1,)), ((), ())), preferred_element_type=F32)


def _split3(x):
    hi = x.astype(BF16)
    r1 = x - hi.astype(F32)
    mid = r1.astype(BF16)
    lo = (r1 - mid.astype(F32)).astype(BF16)
    return hi, mid, lo


def _dot_x3(x, e):
    hi, mid, lo = _split3(x)
    return _dot(hi, e) + _dot(mid, e) + _dot(lo, e)


def _dot_x2(x, e):
    hi = x.astype(BF16)
    mid = (x - hi.astype(F32)).astype(BF16)
    return _dot(hi, e) + _dot(mid, e)


def _dot_3x(e, x):
    hi, mid, lo = _split3(x)
    return _dot(e, hi) + _dot(e, mid) + _dot(e, lo)


def _dot_f32(a, b):
    a0, a1, a2 = _split3(a)
    b0, b1, b2 = _split3(b)
    return (_dot(a0, b0) + (_dot(a0, b1) + _dot(a1, b0))
            + (_dot(a1, b1) + _dot(a0, b2) + _dot(a2, b0)))


def _sigmoid(x):
    return 1.0 / (1.0 + jnp.exp(-x))


def _silu(x):
    return x * _sigmoid(x)


def _iota(shape, axis):
    return lax.broadcasted_iota(jnp.int32, shape, axis)


def _ada_kernel(c_ref, w_ref, b_ref, o_ref):
    a = _silu(c_ref[...]).astype(BF16)
    o_ref[0] = _dot(a, w_ref[0].astype(BF16)) + b_ref[0]


def _ada_mod(cond, w_ada, b_ada):
    depth, _, n = w_ada.shape
    rows = cond.shape[0]
    return pl.pallas_call(
        _ada_kernel,
        grid=(depth, n // ADA_TILE),
        in_specs=[pl.BlockSpec((rows, D), lambda l, j: (0, 0)),
                  pl.BlockSpec((1, D, ADA_TILE), lambda l, j: (l, 0, j)),
                  pl.BlockSpec((1, 1, ADA_TILE), lambda l, j: (l, 0, j))],
        out_specs=pl.BlockSpec((1, rows, ADA_TILE), lambda l, j: (l, 0, j)),
        out_shape=jax.ShapeDtypeStruct((depth, rows, n), F32),
        compiler_params=_cparams(2),
        name="ada_mod",
    )(cond, w_ada, b_ada.reshape(depth, 1, n))


def _mod_spec(mod, n_tokens, tile, n_axes=1):
    per_row = n_tokens // mod.shape[0]
    if n_axes == 1:
        return pl.BlockSpec((1, 1, 6 * D), lambda i: ((i * tile) // per_row, 0, 0))
    return pl.BlockSpec((1, 1, 6 * D), lambda i, e: ((i * tile) // per_row, 0, 0))


def _norm_mod(x, g, shift, scale):
    ms = jnp.mean(x * x, axis=-1, keepdims=True)
    y = x * lax.rsqrt(ms + EPS) * g
    return y * (1.0 + scale) + shift


def _inproj_kernel(x_ref, mod_ref, g_ref, w_ref, *o_refs, widths):
    m = mod_ref[0]
    h = _norm_mod(x_ref[...], g_ref[...], m[:, 0:D], m[:, D:2 * D]).astype(BF16)
    off = 0
    for o_ref, wd in zip(o_refs, widths):
        o_ref[...] = _dot(h, w_ref[:, off:off + wd]).astype(o_ref.dtype)
        off += wd


def _inproj(x, mod, g, w, widths):
    t = x.shape[0]
    n = w.shape[1]
    return pl.pallas_call(
        functools.partial(_inproj_kernel, widths=widths),
        grid=(t // TOKEN_TILE,),
        in_specs=[pl.BlockSpec((TOKEN_TILE, D), lambda i: (i, 0)),
                  _mod_spec(mod, t, TOKEN_TILE),
                  pl.BlockSpec((1, D), lambda i: (0, 0)),
                  pl.BlockSpec((D, n), lambda i: (0, 0))],
        out_specs=[pl.BlockSpec((TOKEN_TILE, wd), lambda i: (i, 0)) for wd in widths],
        out_shape=[jax.ShapeDtypeStruct((t, wd), F32) for wd in widths],
        compiler_params=_cparams(1),
        name="inproj",
    )(x, mod, g, w)


def _ssd_kernel(*refs, seq, has_h0, want_state):
    it = iter(refs)
    z_ref, xbc_ref, dt_ref, cw_ref, cb_ref, hp_ref, nw_ref = [next(it) for _ in range(7)]
    h0_refs = (next(it), next(it)) if has_h0 else None
    y_ref = next(it)
    s_refs = (next(it), next(it)) if want_state else None
    xs_ref, bc_ref, yacc_ref, st_ref = [next(it) for _ in range(4)]

    q = SSD_CHUNK
    nc = seq // q
    gn = SSD_GROUPS * SSD_STATE

    cw = cw_ref[...]
    cb = cb_ref[...]
    rid = _iota((q, SSD_CONV_DIM), 0)

    def conv_body(c, carry):
        r0 = pl.multiple_of(c * q, q)
        xc = xbc_ref[pl.ds(r0, q), :]
        prev = xbc_ref[pl.ds(jnp.maximum(r0 - 1, 0), 1), :]
        nxt = xbc_ref[pl.ds(jnp.minimum(r0 + q, seq - 1), 1), :]
        prev = jnp.where(c > 0, prev, 0.0)
        nxt = jnp.where(c < nc - 1, nxt, 0.0)
        xp = jnp.where(rid == 0, prev, pltpu.roll(xc, 1, axis=0))
        xn = jnp.where(rid == q - 1, nxt, pltpu.roll(xc, q - 1, axis=0))
        v = _silu(xp * cw[0:1] + xc * cw[1:2] + xn * cw[2:3] + cb)
        xs_ref[pl.ds(r0, q), :] = v[:, :SSD_INNER]
        bc_ref[pl.ds(r0, q), :] = v[:, SSD_INNER:]
        return carry

    lax.fori_loop(0, nc, conv_body, 0)

    hp = hp_ref[...]
    expand = (_iota((LANE, SSD_INNER), 1) // SSD_HEADDIM == _iota((LANE, SSD_INNER), 0)).astype(BF16)
    blockmask = (_iota((gn, SSD_INNER), 0) // SSD_STATE
                 == _iota((gn, SSD_INNER), 1) // (SSD_INNER // SSD_GROUPS))
    ri = _iota((q, q), 0)
    ci = _iota((q, q), 1)
    lane = _iota((q, LANE), 1)
    lo_half = lane < SSD_STATE

    def scan_pass(fwd):
        a = -jnp.exp(hp[0:1] if fwd else hp[1:2])
        bias = hp[2:3] if fwd else hp[3:4]
        causal = (ri >= ci) if fwd else (ri <= ci)
        tri = causal.astype(BF16)
        if has_h0:
            st_ref[...] = h0_refs[0 if fwd else 1][0]
        else:
            st_ref[...] = jnp.zeros((gn, SSD_INNER), F32)

        def body(k, carry):
            c = k if fwd else nc - 1 - k
            r0 = pl.multiple_of(c * q, q)
            xs = xs_ref[pl.ds(r0, q), :]
            bc = bc_ref[pl.ds(r0, q), :]
            bm = bc[:, :gn]
            cm = bc[:, gn:]
            xdt_raw = dt_ref[pl.ds(r0, q), :] + bias
            dt = jnp.maximum(xdt_raw, 0.0) + jnp.log1p(jnp.exp(-jnp.abs(xdt_raw)))
            cum = _dot_3x(tri, dt * a)
            cum_t = cum.T
            last = cum[q - 1:q] if fwd else cum[0:1]
            stacked = jnp.concatenate([dt, jnp.exp(cum), jnp.exp(last - cum)], axis=0)
            ex = _dot_x2(stacked, expand)
            dt_e, ecum_e, dend_e = ex[0:q], ex[q:2 * q], ex[2 * q:3 * q]
            cdec_e = ecum_e[q - 1:q] if fwd else ecum_e[0:1]
            xdt = xs * dt_e
            st = st_ref[...]
            cmb = cm.astype(BF16)
            bmb = bm.astype(BF16)
            y = _dot(cmb, st.astype(BF16)) * ecum_e
            cb0 = _dot_nt(jnp.where(lo_half, cm, 0.0).astype(BF16), bmb)
            cb1 = _dot_nt(jnp.where(lo_half, 0.0, cm).astype(BF16), bmb)
            pairs = []
            for j in range(SSD_HEADS // 2):
                cbg = cb0 if j < SSD_HEADS // 4 else cb1
                xp = xdt[:, j * LANE:(j + 1) * LANE]
                scs = []
                for half in range(2):
                    h = 2 * j + half
                    seg = cum[:, h:h + 1] - cum_t[h:h + 1, :]
                    dec = jnp.exp(jnp.where(causal, seg, -jnp.inf))
                    scs.append((cbg * dec).astype(BF16))
                rhs = jnp.concatenate([jnp.where(lo_half, xp, 0.0), jnp.where(lo_half, 0.0, xp)], axis=0)
                pairs.append(_dot(jnp.concatenate(scs, axis=1), rhs.astype(BF16)))
            y = y + jnp.concatenate(pairs, axis=1)
            if fwd:
                yacc_ref[pl.ds(r0, q), :] = y
            else:
                yacc_ref[pl.ds(r0, q), :] += y
            new = _dot(bm.T.astype(BF16), (xdt * dend_e).astype(BF16))
            st_ref[...] = jnp.where(blockmask, st * cdec_e + new, 0.0)
            return carry

        lax.fori_loop(0, nc, body, 0)
        if want_state:
            for j in range(SSD_INNER // LANE):
                g = (j * LANE) // (SSD_INNER // SSD_GROUPS)
                t = st_ref[:, j * LANE:(j + 1) * LANE].T
                s_refs[0 if fwd else 1][0, j * LANE:(j + 1) * LANE, :] = t[:, g * SSD_STATE:(g + 1) * SSD_STATE]

    scan_pass(True)
    scan_pass(False)

    d_e = _dot_x3(jnp.broadcast_to(hp[4:5], (8, LANE)), expand)[0:1]
    nw = nw_ref[...]

    def out_body(c, carry):
        r0 = pl.multiple_of(c * q, q)
        y = yacc_ref[pl.ds(r0, q), :] + xs_ref[pl.ds(r0, q), :] * d_e
        y = y * _silu(z_ref[pl.ds(r0, q), :])
        ms = jnp.mean(y * y, axis=-1, keepdims=True)
        y_ref[pl.ds(r0, q), :] = (y * lax.rsqrt(ms + EPS) * nw).astype(y_ref.dtype)
        return carry

    lax.fori_loop(0, nc, out_body, 0)


def _ssd(z, xbc, dt, cw, cb, hp, nw, h0, *, seq):
    t = z.shape[0]
    n_seq = t // seq
    has_h0 = h0 is not None
    want_state = not has_h0
    gn = SSD_GROUPS * SSD_STATE
    tok = lambda w: pl.BlockSpec((seq, w), lambda b: (b, 0))
    full = lambda a: pl.BlockSpec(a.shape, lambda b: (0,) * a.ndim)
    in_specs = [tok(SSD_INNER), tok(SSD_CONV_DIM), tok(LANE), full(cw), full(cb), full(hp), full(nw)]
    args = [z, xbc, dt, cw, cb, hp, nw]
    if has_h0:
        in_specs += [pl.BlockSpec((1, gn, SSD_INNER), lambda b: (b, 0, 0))] * 2
        args += list(h0)
    out_shape = [jax.ShapeDtypeStruct((t, SSD_INNER), BF16)]
    out_specs = [tok(SSD_INNER)]
    if want_state:
        out_shape += [jax.ShapeDtypeStruct((n_seq, SSD_INNER, SSD_STATE), F32)] * 2
        out_specs += [pl.BlockSpec((1, SSD_INNER, SSD_STATE), lambda b: (b, 0, 0))] * 2

    return pl.pallas_call(
        functools.partial(_ssd_kernel, seq=seq, has_h0=has_h0, want_state=want_state),
        grid=(n_seq,),
        in_specs=in_specs,
        out_specs=out_specs,
        out_shape=out_shape,
        scratch_shapes=[pltpu.VMEM((seq, SSD_INNER), F32),
                        pltpu.VMEM((seq, 2 * gn), F32),
                        pltpu.VMEM((seq, SSD_INNER), F32),
                        pltpu.VMEM((gn, SSD_INNER), F32)],
        compiler_params=_cparams(1),
        name="ssd_lat" if has_h0 else "ssd_ctx",
    )(*args)


def _rope(x, cos, sin, first_half, dist):
    partner = jnp.where(first_half, pltpu.roll(x, LANE - dist, axis=1), pltpu.roll(x, dist, axis=1))
    return x * cos + partner * sin


def _softmax_pv(ss, vals):
    m = ss[0].max(axis=-1, keepdims=True)
    for s in ss[1:]:
        m = jnp.maximum(m, s.max(axis=-1, keepdims=True))
    ps = [jnp.exp2(s - m) for s in ss]
    den = ps[0].sum(axis=-1, keepdims=True)
    for p in ps[1:]:
        den = den + p.sum(axis=-1, keepdims=True)
    o = _dot(ps[0].astype(BF16), vals[0]())
    for p, v in zip(ps[1:], vals[1:]):
        o = o + _dot(p.astype(BF16), v())
    return o / den


def _attend_blocks(seq, units, emit, s_scr):
    nblk = seq // Q_BLOCK

    def scores(u, blk, static):
        q_ref, rows, keys, _ = units[u]
        r = blk * rows if static else pl.multiple_of(blk * rows, rows)
        qb = q_ref[pl.ds(r, rows), :]
        return [_dot_nt(qb, k()) for k in keys]

    if s_scr is None:
        for i in range(nblk):
            emit(i * Q_BLOCK, [_softmax_pv(scores(u, i, True), units[u][3]) for u in range(len(units))])
        return

    def put(slot, blk):
        for u in range(len(units)):
            off = 0
            for s in scores(u, blk, False):
                s_scr[slot][u][:, off:off + s.shape[1]] = s
                off += s.shape[1]

    def take(slot, blk):
        outs = []
        for u, (_, _, keys, vals) in enumerate(units):
            off = 0
            ss = []
            for k in keys:
                n = k().shape[0]
                ss.append(s_scr[slot][u][:, off:off + n])
                off += n
            outs.append(_softmax_pv(ss, vals))
        emit(pl.multiple_of(blk * Q_BLOCK, Q_BLOCK), outs)

    put(0, 0)

    def body(ii, carry):
        b0 = 2 * ii
        put(1, b0 + 1)
        take(0, b0)
        put(0, jnp.where(b0 + 2 >= nblk, 0, b0 + 2))
        take(1, b0 + 1)
        return carry

    lax.fori_loop(0, nblk // 2, body, 0)


def _mla_kernel(*refs, seq, lat):
    it = iter(refs)
    qa_ref, kv_ref, qan_ref, kvn_ref, wq_ref, wk_ref, wv_ref, qn_ref, kn_ref = [next(it) for _ in range(9)]
    if lat:
        cckv_ref, ckpe_ref, cos_ref, sin_ref = [next(it) for _ in range(4)]
    o_ref = next(it)
    if not lat:
        ckv_out, kpe_out = next(it), next(it)
    q_s, k_s, v_s, qh0_s, qh1_s, kh0_s, kh1_s = [next(it) for _ in range(7)]
    s_scr = None
    if lat:
        kc_s, vc_s, kch0_s, kch1_s = [next(it) for _ in range(4)]
        s_scr = [[next(it), next(it)], [next(it), next(it)]]

    qscale = MLA_QK ** -0.5 * LOG2E
    kv = kv_ref[...]
    kpe = kv[:, LANE:]
    ckv_raw = kv[:, :LANE]
    ckv = ckv_raw * lax.rsqrt(jnp.mean(ckv_raw * ckv_raw, axis=-1, keepdims=True) + EPS) * kvn_ref[...]
    qa = qa_ref[...]
    qa = qa * lax.rsqrt(jnp.mean(qa * qa, axis=-1, keepdims=True) + EPS) * qan_ref[...]
    if not lat:
        ckv_out[...] = ckv
        kpe_out[...] = kpe
    ckvb = ckv.astype(BF16)
    q_s[...] = _dot(qa.astype(BF16), wq_ref[...])
    k_s[...] = _dot(ckvb, wk_ref[...])
    v_s[...] = _dot(ckvb, wv_ref[...]).astype(BF16)
    if lat:
        ccb = cckv_ref[0].astype(BF16)
        kc_s[...] = _dot(ccb, wk_ref[...])
        vc_s[...] = _dot(ccb, wv_ref[...]).astype(BF16)
        ckpe = ckpe_ref[0]
        cos = cos_ref[...]
        sin = sin_ref[...]
        lane = _iota((seq, LANE), 1)
        first_half = (lane % (MLA_ROPE // 2)) < (MLA_ROPE // 4)
    qn = qn_ref[...]
    kn = kn_ref[...]
    lo_lanes = _iota((Q_BLOCK, LANE), 1) < MLA_V

    def head_norm(x, g):
        return x * lax.rsqrt(jnp.sum(x * x, axis=-1, keepdims=True) * (1.0 / MLA_QK) + EPS) * g

    for j in range(MLA_HEADS // 2):
        pair = slice(j * LANE, (j + 1) * LANE)
        for half, (qh_s, kh_s) in enumerate(((qh0_s, kh0_s), (qh1_s, kh1_s))):
            sl = slice((2 * j + half) * LANE, (2 * j + half + 1) * LANE)
            qh = head_norm(q_s[:, sl], qn)
            kh = head_norm(k_s[:, sl] + kpe, kn)
            if lat:
                qh = _rope(qh, cos, sin, first_half, MLA_ROPE // 4)
                kh = _rope(kh, cos, sin, first_half, MLA_ROPE // 4)
                (kch0_s, kch1_s)[half][...] = head_norm(kc_s[:, sl] + ckpe, kn).astype(BF16)
            qh_s[...] = (qh * qscale).astype(BF16)
            kh_s[...] = kh.astype(BF16)

        def emit(r0, outs, pair=pair):
            o_ref[pl.ds(r0, Q_BLOCK), pair] = jnp.where(lo_lanes, outs[0], outs[1]).astype(o_ref.dtype)

        units = []
        for half, (qh_s, kh_s) in enumerate(((qh0_s, kh0_s), (qh1_s, kh1_s))):
            keys = [lambda r=kh_s: r[...]]
            vals = [lambda pair=pair: v_s[:, pair]]
            if lat:
                keys = [lambda r=(kch0_s, kch1_s)[half]: r[...]] + keys
                vals = [lambda pair=pair: vc_s[:, pair]] + vals
            units.append((qh_s, Q_BLOCK, keys, vals))
        _attend_blocks(seq, units, emit, s_scr)


def _mla(qa, kv, qan, kvn, wq, wk, wv, qn, kn, cache, tables, *, seq):
    t = qa.shape[0]
    n_seq = t // seq
    lat = cache is not None
    hv = MLA_HEADS * MLA_V
    tok = lambda w: pl.BlockSpec((seq, w), lambda b: (b, 0))
    full = lambda a: pl.BlockSpec(a.shape, lambda b: (0,) * a.ndim)
    args = [qa, kv, qan, kvn, wq, wk, wv, qn, kn]
    in_specs = [tok(MLA_Q_LORA), tok(2 * LANE)] + [full(a) for a in args[2:]]
    past = 0
    if lat:
        past = cache[0].shape[1]
        in_specs += [pl.BlockSpec((1, past, LANE), lambda b: (b, 0, 0))] * 2 + [full(tables[0]), full(tables[1])]
        args += [cache[0], cache[1], tables[0], tables[1]]
    out_shape = [jax.ShapeDtypeStruct((t, hv), BF16)]
    out_specs = [tok(hv)]
    if not lat:
        out_shape += [jax.ShapeDtypeStruct((t, LANE), F32)] * 2
        out_specs += [pl.BlockSpec((seq, LANE), lambda b: (b, 0))] * 2
    scratch = [pltpu.VMEM((seq, MLA_HEADS * LANE), F32),
               pltpu.VMEM((seq, MLA_HEADS * LANE), F32),
               pltpu.VMEM((seq, hv), BF16)] + [pltpu.VMEM((seq, LANE), BF16)] * 4
    if lat:
        scratch += [pltpu.VMEM((past, MLA_HEADS * LANE), F32),
                    pltpu.VMEM((past, hv), BF16),
                    pltpu.VMEM((past, LANE), BF16),
                    pltpu.VMEM((past, LANE), BF16)]
        scratch += [pltpu.VMEM((Q_BLOCK, past + seq), F32)] * 4

    return pl.pallas_call(
        functools.partial(_mla_kernel, seq=seq, lat=lat),
        grid=(n_seq,),
        in_specs=in_specs,
        out_specs=out_specs,
        out_shape=out_shape,
        scratch_shapes=scratch,
        compiler_params=_cparams(1),
        name="mla_lat" if lat else "mla_ctx",
    )(*args)


def _gqa_kernel(*refs, seq, lat):
    it = iter(refs)
    q_ref, k_ref, v_ref, qn_ref, kn_ref = [next(it) for _ in range(5)]
    if lat:
        ck_ref, cv_ref, cos_ref, sin_ref = [next(it) for _ in range(4)]
    o_ref = next(it)
    if not lat:
        kn_out = next(it)
    qn_s, kn_s, vb_s, q2_s = [next(it) for _ in range(4)]
    s_scr = None
    if lat:
        ckb_s, cvb_s = next(it), next(it)
        s_scr = [[next(it)], [next(it)]]

    qscale = GQA_HEAD_DIM ** -0.5 * LOG2E
    hd = GQA_HEAD_DIM
    nq = GQA_HEADS * hd
    nk = GQA_KV_HEADS * LANE

    def half_norm(x, g, width):
        n_half = width // hd
        ind = (_iota((width, LANE), 0) // hd == _iota((width, LANE), 1)).astype(BF16)
        ind_t = (_iota((LANE, width), 1) // hd == _iota((LANE, width), 0)).astype(BF16)
        del n_half
        ss = _dot_x3(x * x, ind)
        r = lax.rsqrt(ss * (1.0 / hd) + EPS)
        return x * _dot_x3(r, ind_t) * g

    qn = half_norm(q_ref[...], qn_ref[...], nq)
    kn = half_norm(k_ref[...], kn_ref[...], nk)
    if not lat:
        kn_out[...] = kn
    if lat:
        cos = cos_ref[...]
        sin = sin_ref[...]
        lane = _iota((seq, LANE), 1)
        first_half = (lane % (hd // 2)) < (hd // 4)
        for j in range(nq // LANE):
            sl = slice(j * LANE, (j + 1) * LANE)
            qn_s[:, sl] = _rope(qn[:, sl], cos, sin, first_half, hd // 4)
        for j in range(nk // LANE):
            sl = slice(j * LANE, (j + 1) * LANE)
            kn_s[:, sl] = _rope(kn[:, sl], cos, sin, first_half, hd // 4).astype(BF16)
        ckb_s[...] = ck_ref[0].astype(BF16)
        cvb_s[...] = cv_ref[0].astype(BF16)
    else:
        qn_s[...] = qn
        kn_s[...] = kn.astype(BF16)
    vb_s[...] = v_ref[...].astype(BF16)
    lo_q = _iota((Q_BLOCK, LANE), 1) < hd

    for j in range(GQA_HEADS // 2):
        pair = slice(j * LANE, (j + 1) * LANE)
        g = (2 * j) // (GQA_HEADS // GQA_KV_HEADS)
        gs = slice(g * LANE, (g + 1) * LANE)
        for i in range(seq // Q_BLOCK):
            qp = qn_s[i * Q_BLOCK:(i + 1) * Q_BLOCK, pair] * qscale
            q2_s[2 * i * Q_BLOCK:(2 * i + 1) * Q_BLOCK, :] = jnp.where(lo_q, qp, 0.0).astype(BF16)
            q2_s[(2 * i + 1) * Q_BLOCK:(2 * i + 2) * Q_BLOCK, :] = jnp.where(lo_q, 0.0, qp).astype(BF16)

        def emit(r0, outs, pair=pair):
            o = outs[0]
            o_ref[pl.ds(r0, Q_BLOCK), pair] = jnp.where(lo_q, o[:Q_BLOCK], o[Q_BLOCK:]).astype(o_ref.dtype)

        keys = [lambda gs=gs: kn_s[:, gs]]
        vals = [lambda gs=gs: vb_s[:, gs]]
        if lat:
            keys = [lambda gs=gs: ckb_s[:, gs]] + keys
            vals = [lambda gs=gs: cvb_s[:, gs]] + vals
        _attend_blocks(seq, [(q2_s, 2 * Q_BLOCK, keys, vals)], emit, s_scr)


def _gqa(q, k, v, qn, kn, cache, tables, *, seq):
    t = q.shape[0]
    n_seq = t // seq
    lat = cache is not None
    nq = GQA_HEADS * GQA_HEAD_DIM
    nk = GQA_KV_HEADS * LANE
    tok = lambda w: pl.BlockSpec((seq, w), lambda b: (b, 0))
    full = lambda a: pl.BlockSpec(a.shape, lambda b: (0,) * a.ndim)
    args = [q, k, v, qn, kn]
    in_specs = [tok(nq), tok(nk), tok(nk), full(qn), full(kn)]
    past = 0
    if lat:
        past = cache[0].shape[1]
        in_specs += [pl.BlockSpec((1, past, nk), lambda b: (b, 0, 0))] * 2 + [full(tables[0]), full(tables[1])]
        args += [cache[0], cache[1], tables[0], tables[1]]
    out_shape = [jax.ShapeDtypeStruct((t, nq), BF16)]
    out_specs = [tok(nq)]
    if not lat:
        out_shape += [jax.ShapeDtypeStruct((t, nk), F32)]
        out_specs += [pl.BlockSpec((seq, nk), lambda b: (b, 0))]
    scratch = [pltpu.VMEM((seq, nq), F32),
               pltpu.VMEM((seq, nk), BF16),
               pltpu.VMEM((seq, nk), BF16),
               pltpu.VMEM((2 * seq, LANE), BF16)]
    if lat:
        scratch += [pltpu.VMEM((past, nk), BF16), pltpu.VMEM((past, nk), BF16)]
        scratch += [pltpu.VMEM((2 * Q_BLOCK, past + seq), F32)] * 2

    return pl.pallas_call(
        functools.partial(_gqa_kernel, seq=seq, lat=lat),
        grid=(n_seq,),
        in_specs=in_specs,
        out_specs=out_specs,
        out_shape=out_shape,
        scratch_shapes=scratch,
        compiler_params=_cparams(1),
        name="gqa_lat" if lat else "gqa_ctx",
    )(*args)


def _route(sel, sc):
    gs = []
    for g in range(N_GROUPS):
        r = sel[g * GROUP_SIZE:(g + 1) * GROUP_SIZE]
        best = None
        for i in range(GROUP_SIZE):
            for j in range(i + 1, GROUP_SIZE):
                s = r[i] + r[j]
                best = s if best is None else jnp.maximum(best, s)
        gs.append(best)
    cur = gs[0]
    grp = jnp.zeros_like(cur, dtype=jnp.int32)
    for g in range(1, N_GROUPS):
        upd = gs[g] > cur
        grp = jnp.where(upd, g, grp)
        cur = jnp.where(upd, gs[g], cur)

    def pick(rows, i):
        v = rows[i]
        for g in range(1, N_GROUPS):
            v = jnp.where(grp == g, rows[g * GROUP_SIZE + i], v)
        return v

    cand = [pick(sel, i) for i in range(GROUP_SIZE)]
    csc = [pick(sc, i) for i in range(GROUP_SIZE)]
    v1, i1, w1 = cand[0], jnp.zeros_like(grp), csc[0]
    for i in range(1, GROUP_SIZE):
        upd = cand[i] > v1
        v1 = jnp.where(upd, cand[i], v1)
        i1 = jnp.where(upd, i, i1)
        w1 = jnp.where(upd, csc[i], w1)
    v2 = i2 = w2 = None
    for i in range(GROUP_SIZE):
        ok = i1 != i
        if v2 is None:
            v2 = jnp.where(ok, cand[i], -jnp.inf)
            i2 = jnp.zeros_like(grp)
            w2 = csc[i]
        else:
            upd = ok & (cand[i] > v2)
            v2 = jnp.where(upd, cand[i], v2)
            i2 = jnp.where(upd, i, i2)
            w2 = jnp.where(upd, csc[i], w2)
    tot = w1 + w2
    e1 = grp * GROUP_SIZE + i1
    e2 = grp * GROUP_SIZE + i2
    return [jnp.where(e1 == e, w1 / tot, 0.0) + jnp.where(e2 == e, w2 / tot, 0.0)
            for e in range(N_EXPERTS)]


def _outproj_kernel(*refs, n_y):
    x_ref, mod_ref = refs[0], refs[1]
    y_refs = refs[2:2 + n_y]
    w_ref, g_ref, wr_ref, rb_ref, x1_ref, h2_ref, gate_ref = refs[2 + n_y:]
    m = mod_ref[0]
    off = 0
    out = None
    for y_ref in y_refs:
        k = y_ref.shape[1]
        part = _dot(y_ref[...], w_ref[off:off + k, :])
        out = part if out is None else out + part
        off += k
    x1 = x_ref[...] + m[:, 2 * D:3 * D] * out
    x1_ref[...] = x1
    h2 = _norm_mod(x1, g_ref[...], m[:, 3 * D:4 * D], m[:, 4 * D:5 * D])
    h2_hi = h2.astype(BF16)
    h2_ref[...] = h2_hi
    h2_mid = (h2 - h2_hi.astype(F32)).astype(BF16)
    parts = _dot(h2_hi, wr_ref[...]) + _dot(h2_mid, wr_ref[...])
    logits = (parts + pltpu.roll(parts, LANE - N_EXPERTS, axis=1)
              + pltpu.roll(parts, LANE - 2 * N_EXPERTS, axis=1))
    sc_t = _sigmoid(logits.T[0:N_EXPERTS, :])
    sel_t = sc_t + rb_ref[...]
    gates = _route([sel_t[e:e + 1] for e in range(N_EXPERTS)], [sc_t[e:e + 1] for e in range(N_EXPERTS)])
    t = x1.shape[0]
    gates_t = jnp.concatenate(gates + [jnp.zeros((LANE - N_EXPERTS, t), F32)], axis=0)
    gate_ref[...] = gates_t.T


def _outproj(x, mod, ys, w, g, wr, rb):
    t = x.shape[0]
    full = lambda a: pl.BlockSpec(a.shape, lambda i: (0,) * a.ndim)
    return pl.pallas_call(
        functools.partial(_outproj_kernel, n_y=len(ys)),
        grid=(t // TOKEN_TILE,),
        in_specs=[pl.BlockSpec((TOKEN_TILE, D), lambda i: (i, 0)),
                  _mod_spec(mod, t, TOKEN_TILE)]
                 + [pl.BlockSpec((TOKEN_TILE, y.shape[1]), lambda i: (i, 0)) for y in ys]
                 + [full(w), full(g), full(wr), full(rb)],
        out_specs=[pl.BlockSpec((TOKEN_TILE, D), lambda i: (i, 0)),
                   pl.BlockSpec((TOKEN_TILE, D), lambda i: (i, 0)),
                   pl.BlockSpec((TOKEN_TILE, LANE), lambda i: (i, 0))],
        out_shape=[jax.ShapeDtypeStruct((t, D), F32),
                   jax.ShapeDtypeStruct((t, D), BF16),
                   jax.ShapeDtypeStruct((t, LANE), F32)],
        compiler_params=_cparams(1),
        name="outproj_router",
    )(x, mod, *ys, w, g, wr, rb)


def _moe_kernel(x_ref, mod_ref, h_ref, gate_ref, wg_ref, wu_ref, wd_ref, o_ref, acc_ref):
    e = pl.program_id(1)

    @pl.when(e == 0)
    def _():
        acc_ref[...] = jnp.zeros_like(acc_ref)

    h = h_ref[...]
    a = _dot(h, wg_ref[0])
    u = _dot(h, wu_ref[0])
    gates = gate_ref[...]
    gcol = jnp.sum(jnp.where(_iota(gates.shape, 1) == e, gates, 0.0), axis=-1, keepdims=True)
    act = (_silu(a) * u * gcol).astype(BF16)
    acc_ref[...] += _dot(act, wd_ref[0])

    @pl.when(e == N_EXPERTS - 1)
    def _():
        o_ref[...] = x_ref[...] + mod_ref[0][:, 5 * D:6 * D] * acc_ref[...]


def _moe(x1, mod, h2, gates, wg, wu, wd, layer):
    t = x1.shape[0]
    return pl.pallas_call(
        _moe_kernel,
        grid=(t // MOE_TILE, N_EXPERTS),
        in_specs=[pl.BlockSpec((MOE_TILE, D), lambda i, e: (i, 0)),
                  _mod_spec(mod, t, MOE_TILE, n_axes=2),
                  pl.BlockSpec((MOE_TILE, D), lambda i, e: (i, 0)),
                  pl.BlockSpec((MOE_TILE, LANE), lambda i, e: (i, 0)),
                  pl.BlockSpec((None, 1, D, D_EXPERT), lambda i, e: (layer, e, 0, 0)),
                  pl.BlockSpec((None, 1, D, D_EXPERT), lambda i, e: (layer, e, 0, 0)),
                  pl.BlockSpec((None, 1, D_EXPERT, D), lambda i, e: (layer, e, 0, 0))],
        out_specs=pl.BlockSpec((MOE_TILE, D), lambda i, e: (i, 0)),
        out_shape=jax.ShapeDtypeStruct((t, D), F32),
        scratch_shapes=[pltpu.VMEM((MOE_TILE, D), F32)],
        compiler_params=_cparams(2),
        name="moe_experts",
    )(x1, mod, h2, gates, wg, wu, wd)


def _rope_angles(pos, half):
    inv = jnp.power(ROPE_THETA, -jnp.arange(half, dtype=F32) / half)
    ang = pos.astype(F32)[:, None] * inv[None, :]
    return jnp.cos(ang), jnp.sin(ang)


def _rope_tables(seq, rot, lead, lane_tiles):
    rows = seq // GRID_W
    row = jnp.repeat(jnp.arange(rows), GRID_W)
    col = jnp.tile(jnp.arange(GRID_W), rows)
    cr, sr = _rope_angles(row, rot // 4)
    cc, sc = _rope_angles(col, rot // 4)
    cos = jnp.concatenate([cr, cr, cc, cc], axis=1)
    sin = jnp.concatenate([-sr, sr, -sc, sc], axis=1)
    width = LANE // lane_tiles
    pad = width - lead - rot
    cos = jnp.concatenate([jnp.ones((seq, lead), F32), cos, jnp.ones((seq, pad), F32)], axis=1)
    sin = jnp.concatenate([jnp.zeros((seq, lead), F32), sin, jnp.zeros((seq, pad), F32)], axis=1)
    return jnp.tile(cos, (1, lane_tiles)), jnp.tile(sin, (1, lane_tiles))


def _pad_lanes(a, width):
    return jnp.pad(a, [(0, 0)] * (a.ndim - 1) + [(0, width - a.shape[-1])])


def _even_weights(w_in, w_q_b, w_kv_b):
    z = w_in[:, 0:1024]
    xbc = w_in[:, 1024:2304]
    dt = w_in[:, 2304:2320]
    qa = w_in[:, 2320:2576]
    ckv = w_in[:, 2576:2704]
    kpe = w_in[:, 2704:2736]
    zeros = lambda n: jnp.zeros((D, n), w_in.dtype)
    w = jnp.concatenate([z, xbc, qa, ckv, zeros(MLA_NOPE), kpe, zeros(LANE - MLA_QK), dt,
                         zeros(LANE - SSD_HEADS)], axis=1).astype(BF16)
    wq = _pad_lanes(w_q_b.reshape(MLA_Q_LORA, MLA_HEADS, MLA_QK), LANE).reshape(MLA_Q_LORA, MLA_HEADS * LANE)
    kvb = w_kv_b.reshape(MLA_KV_LORA, MLA_HEADS, MLA_NOPE + MLA_V)
    wk = _pad_lanes(kvb[:, :, :MLA_NOPE], LANE).reshape(MLA_KV_LORA, MLA_HEADS * LANE)
    wv = kvb[:, :, MLA_NOPE:].reshape(MLA_KV_LORA, MLA_HEADS * MLA_V)
    return w, wq.astype(BF16), wk.astype(BF16), wv.astype(BF16)


def _dup_heads(a):
    s = a.shape[:-1]
    a = a.reshape(s + (GQA_KV_HEADS, 1, GQA_HEAD_DIM))
    return jnp.broadcast_to(a, s + (GQA_KV_HEADS, 2, GQA_HEAD_DIM)).reshape(s + (GQA_KV_HEADS * LANE,))


def _undup_heads(a):
    s = a.shape[:-1]
    return a.reshape(s + (GQA_KV_HEADS, 2, GQA_HEAD_DIM))[..., 0, :]


def _state_to_kernel(h0):
    b = h0.shape[0]
    t = jnp.transpose(h0, (0, 3, 1, 2)).reshape(b, SSD_STATE, SSD_INNER)
    half = SSD_INNER // SSD_GROUPS
    col = jnp.arange(SSD_INNER) // half
    parts = [jnp.where(col == g, t, 0.0) for g in range(SSD_GROUPS)]
    return jnp.concatenate(parts, axis=1)


def _state_from_kernel(st):
    return st.reshape(st.shape[0], SSD_HEADS, SSD_HEADDIM, SSD_STATE)


def kernel(x_prompt, x_sample, c, state_ssd_fwd, state_ssd_bwd, cache_mla_ckv, cache_mla_kpe, cache_gqa_k, cache_gqa_v, c_ctx, g_mix, g_ffn, w_ada, b_ada, w_router, router_bias, w_exp_gate, w_exp_up, w_exp_down, w_in_even, ssd_conv_w, ssd_conv_b, ssd_a_log_fwd, ssd_a_log_bwd, ssd_dt_bias_fwd, ssd_dt_bias_bwd, ssd_d, ssd_norm, mla_q_a_norm, mla_w_q_b, mla_kv_a_norm, mla_w_kv_b, mla_q_norm, mla_k_norm, w_out_even, w_in_odd, gqa_q_norm, gqa_k_norm, w_out_odd):
    nb, ls, _ = x_prompt.shape
    db, dl, _ = x_sample.shape
    depth = w_ada.shape[0]
    n_ctx = nb * ls

    xs = [x_prompt.reshape(n_ctx, D), x_sample.reshape(db * dl, D)]
    seqs = (ls, dl)
    rows = 16
    cond = jnp.concatenate([c_ctx[None, :], c, jnp.zeros((rows - 1 - db, D), F32)], axis=0)
    mod_all = _ada_mod(cond, w_ada, b_ada)

    wr = _pad_lanes(jnp.concatenate(_split3(w_router), axis=1), LANE)
    rb = router_bias.reshape(N_EXPERTS, 1)
    wg_all = w_exp_gate.astype(BF16)
    wu_all = w_exp_up.astype(BF16)
    wd_all = w_exp_down.astype(BF16)

    outs = {}
    for i in range(depth):
        j = i // 2
        mods = [mod_all[i, 0:1].reshape(1, 1, 6 * D), mod_all[i, 1:1 + db].reshape(db, 1, 6 * D)]
        g1 = g_mix[i].reshape(1, D)
        g2 = g_ffn[i].reshape(1, D)
        if i % 2 == 0:
            w, wq, wk, wv = _even_weights(w_in_even[j], mla_w_q_b[j], mla_w_kv_b[j])
            cw = jnp.pad(ssd_conv_w[j], ((0, 5), (0, 0)))
            cb = ssd_conv_b[j].reshape(1, SSD_CONV_DIM)
            hp = _pad_lanes(jnp.stack([ssd_a_log_fwd[j], ssd_a_log_bwd[j], ssd_dt_bias_fwd[j],
                                       ssd_dt_bias_bwd[j], ssd_d[j]]), LANE)
            hp = jnp.pad(hp, ((0, 3), (0, 0)))
            nw = ssd_norm[j].reshape(1, SSD_INNER)
            qan = mla_q_a_norm[j].reshape(1, MLA_Q_LORA)
            kvn = mla_kv_a_norm[j].reshape(1, MLA_KV_LORA)
            qn = _pad_lanes(mla_q_norm[j].reshape(1, MLA_QK), LANE)
            kn = _pad_lanes(mla_k_norm[j].reshape(1, MLA_QK), LANE)
            h0 = (_state_to_kernel(state_ssd_fwd[:, j]), _state_to_kernel(state_ssd_bwd[:, j]))
            ckpe = jnp.pad(cache_mla_kpe[:, j], ((0, 0), (0, 0), (MLA_NOPE, LANE - MLA_QK)))
            tables = _rope_tables(dl, MLA_ROPE, MLA_NOPE, 1)
            w_out = w_out_even[j].astype(BF16)
            ys = []
            for s in range(2):
                z, xbc, qa, kv, dt = _inproj(xs[s], mods[s], g1, w,
                                             (SSD_INNER, SSD_CONV_DIM, MLA_Q_LORA, 2 * LANE, LANE))
                if s == 0:
                    y, sf, sb = _ssd(z, xbc, dt, cw, cb, hp, nw, None, seq=seqs[s])
                    o, ckv_new, kpe_new = _mla(qa, kv, qan, kvn, wq, wk, wv, qn, kn, None, None, seq=seqs[s])
                    outs.setdefault("ssd_f", []).append(_state_from_kernel(sf))
                    outs.setdefault("ssd_b", []).append(_state_from_kernel(sb))
                    outs.setdefault("ckv", []).append(ckv_new.reshape(nb, ls, MLA_KV_LORA))
                    outs.setdefault("kpe", []).append(kpe_new[:, MLA_NOPE:MLA_QK].reshape(nb, ls, MLA_ROPE))
                else:
                    y, = _ssd(z, xbc, dt, cw, cb, hp, nw, h0, seq=seqs[s])
                    o, = _mla(qa, kv, qan, kvn, wq, wk, wv, qn, kn, (cache_mla_ckv[:, j], ckpe), tables,
                              seq=seqs[s])
                ys.append((y, o))
        else:
            wi = w_in_odd[j]
            nq = GQA_HEADS * GQA_HEAD_DIM
            nkv = GQA_KV_HEADS * GQA_HEAD_DIM
            nk = GQA_KV_HEADS * LANE
            w = jnp.concatenate([wi[:, :nq], _dup_heads(wi[:, nq:nq + nkv]), _dup_heads(wi[:, nq + nkv:])],
                                axis=1).astype(BF16)
            qn = jnp.tile(gqa_q_norm[j].reshape(1, GQA_HEAD_DIM), (1, nq // GQA_HEAD_DIM))
            kn = jnp.tile(gqa_k_norm[j].reshape(1, GQA_HEAD_DIM), (1, nk // GQA_HEAD_DIM))
            cache = (_dup_heads(cache_gqa_k[:, j].reshape(db, -1, nkv)),
                     _dup_heads(cache_gqa_v[:, j].reshape(db, -1, nkv)))
            tables = _rope_tables(dl, GQA_HEAD_DIM, 0, 2)
            w_out = w_out_odd[j].astype(BF16)
            ys = []
            for s in range(2):
                q, k, v = _inproj(xs[s], mods[s], g1, w, (nq, nk, nk))
                if s == 0:
                    o, k_new = _gqa(q, k, v, qn, kn, None, None, seq=seqs[s])
                    outs.setdefault("gk", []).append(
                        _undup_heads(k_new).reshape(nb, ls, GQA_KV_HEADS, GQA_HEAD_DIM))
                    outs.setdefault("gv", []).append(
                        _undup_heads(v).reshape(nb, ls, GQA_KV_HEADS, GQA_HEAD_DIM))
                else:
                    o, = _gqa(q, k, v, qn, kn, cache, tables, seq=seqs[s])
                ys.append((o,))
        for s in range(2):
            x1, h2, gates = _outproj(xs[s], mods[s], ys[s], w_out, g2, wr, rb)
            xs[s] = _moe(x1, mods[s], h2, gates, wg_all, wu_all, wd_all, i)

    stack = lambda key: jnp.stack(outs[key], axis=1)
    return (xs[0].reshape(nb, ls, D), xs[1].reshape(db, dl, D),
            stack("ssd_f"), stack("ssd_b"), stack("ckv"), stack("kpe"), stack("gk"), stack("gv"))
```

```python
import functools
import math

import jax
import jax.numpy as jnp
from jax import lax
from jax.experimental import pallas as pl
from jax.experimental.pallas import tpu as pltpu

F32 = jnp.float32
BF16 = jnp.bfloat16

D = 1024
EPS = 1e-6
GRID_W = 64
ROPE_THETA = 10000.0
SSD_HEADS = 16
SSD_HEADDIM = 64
SSD_INNER = 1024
SSD_STATE = 64
SSD_GROUPS = 2
SSD_CONV_DIM = 1280
SSD_CHUNK = 128
MLA_HEADS = 8
MLA_Q_LORA = 256
MLA_KV_LORA = 128
MLA_NOPE = 64
MLA_ROPE = 32
MLA_V = 64
MLA_QK = 96
GQA_HEADS = 16
GQA_KV_HEADS = 4
GQA_HEAD_DIM = 64
N_EXPERTS = 16
N_GROUPS = 4
GROUP_SIZE = 4
D_EXPERT = 256

LANE = 128
Q_BLOCK = 128
LOG2E = 1.4426950408889634
TOKEN_TILE = 512
EXPERT_TILE = 512
SEG_ALIGN = 16
GROUP_LANE = 16
DISPATCH_ROWS = TOKEN_TILE + LANE
ADA_TILE = 1536
VMEM_LIMIT = 56 * 1024 * 1024


def _cparams(n_axes):
    return pltpu.CompilerParams(dimension_semantics=("arbitrary",) * n_axes,
                                vmem_limit_bytes=VMEM_LIMIT)


def _dot(a, b):
    return jnp.dot(a, b, preferred_element_type=F32)


def _dot_nt(a, b):
    return lax.dot_general(a, b, (((1,), (1,)), ((), ())), preferred_element_type=F32)


def _split3(x):
    hi = x.astype(BF16)
    r1 = x - hi.astype(F32)
    mid = r1.astype(BF16)
    lo = (r1 - mid.astype(F32)).astype(BF16)
    return hi, mid, lo


def _dot_x3(x, e):
    hi, mid, lo = _split3(x)
    return _dot(hi, e) + _dot(mid, e) + _dot(lo, e)


def _dot_x2(x, e):
    hi = x.astype(BF16)
    mid = (x - hi.astype(F32)).astype(BF16)
    return _dot(hi, e) + _dot(mid, e)


def _dot_3x(e, x):
    hi, mid, lo = _split3(x)
    return _dot(e, hi) + _dot(e, mid) + _dot(e, lo)


def _dot_f32(a, b):
    a0, a1, a2 = _split3(a)
    b0, b1, b2 = _split3(b)
    return (_dot(a0, b0) + (_dot(a0, b1) + _dot(a1, b0))
            + (_dot(a1, b1) + _dot(a0, b2) + _dot(a2, b0)))


def _sigmoid(x):
    return 1.0 / (1.0 + jnp.exp(-x))


def _silu(x):
    return x * _sigmoid(x)


def _iota(shape, axis):
    return lax.broadcasted_iota(jnp.int32, shape, axis)


def _ada_kernel(c_ref, w_ref, b_ref, o_ref):
    a = _silu(c_ref[...]).astype(BF16)
    o_ref[0] = _dot(a, w_ref[0].astype(BF16)) + b_ref[0]


def _ada_mod(cond, w_ada, b_ada):
    depth, _, n = w_ada.shape
    rows = cond.shape[0]
    return pl.pallas_call(
        _ada_kernel,
        grid=(depth, n // ADA_TILE),
        in_specs=[pl.BlockSpec((rows, D), lambda l, j: (0, 0)),
                  pl.BlockSpec((1, D, ADA_TILE), lambda l, j: (l, 0, j)),
                  pl.BlockSpec((1, 1, ADA_TILE), lambda l, j: (l, 0, j))],
        out_specs=pl.BlockSpec((1, rows, ADA_TILE), lambda l, j: (l, 0, j)),
        out_shape=jax.ShapeDtypeStruct((depth, rows, n), F32),
        compiler_params=_cparams(2),
        name="ada_mod",
    )(cond, w_ada, b_ada.reshape(depth, 1, n))


def _mod_spec(mod, n_tokens, tile, n_axes=1):
    per_row = n_tokens // mod.shape[0]
    if n_axes == 1:
        return pl.BlockSpec((1, 1, 6 * D), lambda i: ((i * tile) // per_row, 0, 0))
    return pl.BlockSpec((1, 1, 6 * D), lambda i, e: ((i * tile) // per_row, 0, 0))


def _norm_mod(x, g, shift, scale):
    ms = jnp.mean(x * x, axis=-1, keepdims=True)
    y = x * lax.rsqrt(ms + EPS) * g
    return y * (1.0 + scale) + shift


def _inproj_kernel(x_ref, mod_ref, g_ref, w_ref, *o_refs, widths):
    m = mod_ref[0]
    h = _norm_mod(x_ref[...], g_ref[...], m[:, 0:D], m[:, D:2 * D]).astype(BF16)
    off = 0
    for o_ref, wd in zip(o_refs, widths):
        o_ref[...] = _dot(h, w_ref[:, off:off + wd]).astype(o_ref.dtype)
        off += wd


def _inproj(x, mod, g, w, widths):
    t = x.shape[0]
    n = w.shape[1]
    return pl.pallas_call(
        functools.partial(_inproj_kernel, widths=widths),
        grid=(t // TOKEN_TILE,),
        in_specs=[pl.BlockSpec((TOKEN_TILE, D), lambda i: (i, 0)),
                  _mod_spec(mod, t, TOKEN_TILE),
                  pl.BlockSpec((1, D), lambda i: (0, 0)),
                  pl.BlockSpec((D, n), lambda i: (0, 0))],
        out_specs=[pl.BlockSpec((TOKEN_TILE, wd), lambda i: (i, 0)) for wd in widths],
        out_shape=[jax.ShapeDtypeStruct((t, wd), F32) for wd in widths],
        compiler_params=_cparams(1),
        name="inproj",
    )(x, mod, g, w)


def _ssd_kernel(*refs, seq, has_h0, want_state):
    it = iter(refs)
    z_ref, xbc_ref, dt_ref, cw_ref, cb_ref, hp_ref, nw_ref = [next(it) for _ in range(7)]
    h0_refs = (next(it), next(it)) if has_h0 else None
    y_ref = next(it)
    s_refs = (next(it), next(it)) if want_state else None
    xs_ref, bc_ref, yacc_ref, st_ref = [next(it) for _ in range(4)]

    q = SSD_CHUNK
    nc = seq // q
    gn = SSD_GROUPS * SSD_STATE

    cw = cw_ref[...]
    cb = cb_ref[...]
    rid = _iota((q, SSD_CONV_DIM), 0)

    def conv_body(c, carry):
        r0 = pl.multiple_of(c * q, q)
        xc = xbc_ref[pl.ds(r0, q), :]
        prev = xbc_ref[pl.ds(jnp.maximum(r0 - 1, 0), 1), :]
        nxt = xbc_ref[pl.ds(jnp.minimum(r0 + q, seq - 1), 1), :]
        prev = jnp.where(c > 0, prev, 0.0)
        nxt = jnp.where(c < nc - 1, nxt, 0.0)
        xp = jnp.where(rid == 0, prev, pltpu.roll(xc, 1, axis=0))
        xn = jnp.where(rid == q - 1, nxt, pltpu.roll(xc, q - 1, axis=0))
        v = _silu(xp * cw[0:1] + xc * cw[1:2] + xn * cw[2:3] + cb)
        xs_ref[pl.ds(r0, q), :] = v[:, :SSD_INNER]
        bc_ref[pl.ds(r0, q), :] = v[:, SSD_INNER:]
        return carry

    lax.fori_loop(0, nc, conv_body, 0)

    hp = hp_ref[...]
    expand = (_iota((LANE, SSD_INNER), 1) // SSD_HEADDIM == _iota((LANE, SSD_INNER), 0)).astype(BF16)
    blockmask = (_iota((gn, SSD_INNER), 0) // SSD_STATE
                 == _iota((gn, SSD_INNER), 1) // (SSD_INNER // SSD_GROUPS))
    ri = _iota((q, q), 0)
    ci = _iota((q, q), 1)
    lane = _iota((q, LANE), 1)
    lo_half = lane < SSD_STATE

    def scan_pass(fwd):
        a = -jnp.exp(hp[0:1] if fwd else hp[1:2])
        bias = hp[2:3] if fwd else hp[3:4]
        causal = (ri >= ci) if fwd else (ri <= ci)
        tri = causal.astype(BF16)
        if has_h0:
            st_ref[...] = h0_refs[0 if fwd else 1][0]
        else:
            st_ref[...] = jnp.zeros((gn, SSD_INNER), F32)

        def body(k, carry):
            c = k if fwd else nc - 1 - k
            r0 = pl.multiple_of(c * q, q)
            xs = xs_ref[pl.ds(r0, q), :]
            bc = bc_ref[pl.ds(r0, q), :]
            bm = bc[:, :gn]
            cm = bc[:, gn:]
            xdt_raw = dt_ref[pl.ds(r0, q), :] + bias
            dt = jnp.maximum(xdt_raw, 0.0) + jnp.log1p(jnp.exp(-jnp.abs(xdt_raw)))
            cum = _dot_3x(tri, dt * a)
            cum_t = cum.T
            last = cum[q - 1:q] if fwd else cum[0:1]
            stacked = jnp.concatenate([dt, jnp.exp(cum), jnp.exp(last - cum)], axis=0)
            ex = _dot_x2(stacked, expand)
            dt_e, ecum_e, dend_e = ex[0:q], ex[q:2 * q], ex[2 * q:3 * q]
            cdec_e = ecum_e[q - 1:q] if fwd else ecum_e[0:1]
            xdt = xs * dt_e
            st = st_ref[...]
            cmb = cm.astype(BF16)
            bmb = bm.astype(BF16)
            y = _dot(cmb, st.astype(BF16)) * ecum_e
            cb0 = _dot_nt(jnp.where(lo_half, cm, 0.0).astype(BF16), bmb)
            cb1 = _dot_nt(jnp.where(lo_half, 0.0, cm).astype(BF16), bmb)
            pairs = []
            for j in range(SSD_HEADS // 2):
                cbg = cb0 if j < SSD_HEADS // 4 else cb1
                xp = xdt[:, j * LANE:(j + 1) * LANE]
                scs = []
                for half in range(2):
                    h = 2 * j + half
                    seg = cum[:, h:h + 1] - cum_t[h:h + 1, :]
                    dec = jnp.exp(jnp.where(causal, seg, -jnp.inf))
                    scs.append((cbg * dec).astype(BF16))
                rhs = jnp.concatenate([jnp.where(lo_half, xp, 0.0), jnp.where(lo_half, 0.0, xp)], axis=0)
                pairs.append(_dot(jnp.concatenate(scs, axis=1), rhs.astype(BF16)))
            y = y + jnp.concatenate(pairs, axis=1)
            if fwd:
                yacc_ref[pl.ds(r0, q), :] = y
            else:
                yacc_ref[pl.ds(r0, q), :] += y
            new = _dot(bm.T.astype(BF16), (xdt * dend_e).astype(BF16))
            st_ref[...] = jnp.where(blockmask, st * cdec_e + new, 0.0)
            return carry

        lax.fori_loop(0, nc, body, 0)
        if want_state:
            for j in range(SSD_INNER // LANE):
                g = (j * LANE) // (SSD_INNER // SSD_GROUPS)
                t = st_ref[:, j * LANE:(j + 1) * LANE].T
                s_refs[0 if fwd else 1][0, j * LANE:(j + 1) * LANE, :] = t[:, g * SSD_STATE:(g + 1) * SSD_STATE]

    scan_pass(True)
    scan_pass(False)

    d_e = _dot_x3(jnp.broadcast_to(hp[4:5], (8, LANE)), expand)[0:1]
    nw = nw_ref[...]

    def out_body(c, carry):
        r0 = pl.multiple_of(c * q, q)
        y = yacc_ref[pl.ds(r0, q), :] + xs_ref[pl.ds(r0, q), :] * d_e
        y = y * _silu(z_ref[pl.ds(r0, q), :])
        ms = jnp.mean(y * y, axis=-1, keepdims=True)
        y_ref[pl.ds(r0, q), :] = (y * lax.rsqrt(ms + EPS) * nw).astype(y_ref.dtype)
        return carry

    lax.fori_loop(0, nc, out_body, 0)


def _ssd(z, xbc, dt, cw, cb, hp, nw, h0, *, seq):
    t = z.shape[0]
    n_seq = t // seq
    has_h0 = h0 is not None
    want_state = not has_h0
    gn = SSD_GROUPS * SSD_STATE
    tok = lambda w: pl.BlockSpec((seq, w), lambda b: (b, 0))
    full = lambda a: pl.BlockSpec(a.shape, lambda b: (0,) * a.ndim)
    in_specs = [tok(SSD_INNER), tok(SSD_CONV_DIM), tok(LANE), full(cw), full(cb), full(hp), full(nw)]
    args = [z, xbc, dt, cw, cb, hp, nw]
    if has_h0:
        in_specs += [pl.BlockSpec((1, gn, SSD_INNER), lambda b: (b, 0, 0))] * 2
        args += list(h0)
    out_shape = [jax.ShapeDtypeStruct((t, SSD_INNER), BF16)]
    out_specs = [tok(SSD_INNER)]
    if want_state:
        out_shape += [jax.ShapeDtypeStruct((n_seq, SSD_INNER, SSD_STATE), F32)] * 2
        out_specs += [pl.BlockSpec((1, SSD_INNER, SSD_STATE), lambda b: (b, 0, 0))] * 2

    return pl.pallas_call(
        functools.partial(_ssd_kernel, seq=seq, has_h0=has_h0, want_state=want_state),
        grid=(n_seq,),
        in_specs=in_specs,
        out_specs=out_specs,
        out_shape=out_shape,
        scratch_shapes=[pltpu.VMEM((seq, SSD_INNER), F32),
                        pltpu.VMEM((seq, 2 * gn), F32),
                        pltpu.VMEM((seq, SSD_INNER), F32),
                        pltpu.VMEM((gn, SSD_INNER), F32)],
        compiler_params=_cparams(1),
        name="ssd_lat" if has_h0 else "ssd_ctx",
    )(*args)


def _rope(x, cos, sin, first_half, dist):
    partner = jnp.where(first_half, pltpu.roll(x, LANE - dist, axis=1), pltpu.roll(x, dist, axis=1))
    return x * cos + partner * sin


def _softmax_pv(ss, vals):
    m = ss[0].max(axis=-1, keepdims=True)
    for s in ss[1:]:
        m = jnp.maximum(m, s.max(axis=-1, keepdims=True))
    ps = [jnp.exp2(s - m) for s in ss]
    den = ps[0].sum(axis=-1, keepdims=True)
    for p in ps[1:]:
        den = den + p.sum(axis=-1, keepdims=True)
    o = _dot(ps[0].astype(BF16), vals[0]())
    for p, v in zip(ps[1:], vals[1:]):
        o = o + _dot(p.astype(BF16), v())
    return o / den


def _attend_blocks(seq, units, emit, s_scr):
    nblk = seq // Q_BLOCK

    def scores(u, blk, static):
        q_ref, rows, keys, _ = units[u]
        r = blk * rows if static else pl.multiple_of(blk * rows, rows)
        qb = q_ref[pl.ds(r, rows), :]
        return [_dot_nt(qb, k()) for k in keys]

    if s_scr is None:
        for i in range(nblk):
            emit(i * Q_BLOCK, [_softmax_pv(scores(u, i, True), units[u][3]) for u in range(len(units))])
        return

    def put(slot, blk):
        for u in range(len(units)):
            off = 0
            for s in scores(u, blk, False):
                s_scr[slot][u][:, off:off + s.shape[1]] = s
                off += s.shape[1]

    def take(slot, blk):
        outs = []
        for u, (_, _, keys, vals) in enumerate(units):
            off = 0
            ss = []
            for k in keys:
                n = k().shape[0]
                ss.append(s_scr[slot][u][:, off:off + n])
                off += n
            outs.append(_softmax_pv(ss, vals))
        emit(pl.multiple_of(blk * Q_BLOCK, Q_BLOCK), outs)

    put(0, 0)

    def body(ii, carry):
        b0 = 2 * ii
        put(1, b0 + 1)
        take(0, b0)
        put(0, jnp.where(b0 + 2 >= nblk, 0, b0 + 2))
        take(1, b0 + 1)
        return carry

    lax.fori_loop(0, nblk // 2, body, 0)


def _mla_kernel(*refs, seq, lat):
    it = iter(refs)
    qa_ref, kv_ref, qan_ref, kvn_ref, wq_ref, wk_ref, wv_ref, qn_ref, kn_ref = [next(it) for _ in range(9)]
    if lat:
        cckv_ref, ckpe_ref, cos_ref, sin_ref = [next(it) for _ in range(4)]
    o_ref = next(it)
    if not lat:
        ckv_out, kpe_out = next(it), next(it)
    q_s, k_s, v_s, qh0_s, qh1_s, kh0_s, kh1_s = [next(it) for _ in range(7)]
    s_scr = None
    if lat:
        kc_s, vc_s, kch0_s, kch1_s = [next(it) for _ in range(4)]
        s_scr = [[next(it), next(it)], [next(it), next(it)]]

    qscale = MLA_QK ** -0.5 * LOG2E
    kv = kv_ref[...]
    kpe = kv[:, LANE:]
    ckv_raw = kv[:, :LANE]
    ckv = ckv_raw * lax.rsqrt(jnp.mean(ckv_raw * ckv_raw, axis=-1, keepdims=True) + EPS) * kvn_ref[...]
    qa = qa_ref[...]
    qa = qa * lax.rsqrt(jnp.mean(qa * qa, axis=-1, keepdims=True) + EPS) * qan_ref[...]
    if not lat:
        ckv_out[...] = ckv
        kpe_out[...] = kpe
    ckvb = ckv.astype(BF16)
    q_s[...] = _dot(qa.astype(BF16), wq_ref[...])
    k_s[...] = _dot(ckvb, wk_ref[...])
    v_s[...] = _dot(ckvb, wv_ref[...]).astype(BF16)
    if lat:
        ccb = cckv_ref[0].astype(BF16)
        kc_s[...] = _dot(ccb, wk_ref[...])
        vc_s[...] = _dot(ccb, wv_ref[...]).astype(BF16)
        ckpe = ckpe_ref[0]
        cos = cos_ref[...]
        sin = sin_ref[...]
        lane = _iota((seq, LANE), 1)
        first_half = (lane % (MLA_ROPE // 2)) < (MLA_ROPE // 4)
    qn = qn_ref[...]
    kn = kn_ref[...]
    lo_lanes = _iota((Q_BLOCK, LANE), 1) < MLA_V

    def head_norm(x, g):
        return x * lax.rsqrt(jnp.sum(x * x, axis=-1, keepdims=True) * (1.0 / MLA_QK) + EPS) * g

    for j in range(MLA_HEADS // 2):
        pair = slice(j * LANE, (j + 1) * LANE)
        for half, (qh_s, kh_s) in enumerate(((qh0_s, kh0_s), (qh1_s, kh1_s))):
            sl = slice((2 * j + half) * LANE, (2 * j + half + 1) * LANE)
            qh = head_norm(q_s[:, sl], qn)
            kh = head_norm(k_s[:, sl] + kpe, kn)
            if lat:
                qh = _rope(qh, cos, sin, first_half, MLA_ROPE // 4)
                kh = _rope(kh, cos, sin, first_half, MLA_ROPE // 4)
                (kch0_s, kch1_s)[half][...] = head_norm(kc_s[:, sl] + ckpe, kn).astype(BF16)
            qh_s[...] = (qh * qscale).astype(BF16)
            kh_s[...] = kh.astype(BF16)

        def emit(r0, outs, pair=pair):
            o_ref[pl.ds(r0, Q_BLOCK), pair] = jnp.where(lo_lanes, outs[0], outs[1]).astype(o_ref.dtype)

        units = []
        for half, (qh_s, kh_s) in enumerate(((qh0_s, kh0_s), (qh1_s, kh1_s))):
            keys = [lambda r=kh_s: r[...]]
            vals = [lambda pair=pair: v_s[:, pair]]
            if lat:
                keys = [lambda r=(kch0_s, kch1_s)[half]: r[...]] + keys
                vals = [lambda pair=pair: vc_s[:, pair]] + vals
            units.append((qh_s, Q_BLOCK, keys, vals))
        _attend_blocks(seq, units, emit, s_scr)


def _mla(qa, kv, qan, kvn, wq, wk, wv, qn, kn, cache, tables, *, seq):
    t = qa.shape[0]
    n_seq = t // seq
    lat = cache is not None
    hv = MLA_HEADS * MLA_V
    tok = lambda w: pl.BlockSpec((seq, w), lambda b: (b, 0))
    full = lambda a: pl.BlockSpec(a.shape, lambda b: (0,) * a.ndim)
    args = [qa, kv, qan, kvn, wq, wk, wv, qn, kn]
    in_specs = [tok(MLA_Q_LORA), tok(2 * LANE)] + [full(a) for a in args[2:]]
    past = 0
    if lat:
        past = cache[0].shape[1]
        in_specs += [pl.BlockSpec((1, past, LANE), lambda b: (b, 0, 0))] * 2 + [full(tables[0]), full(tables[1])]
        args += [cache[0], cache[1], tables[0], tables[1]]
    out_shape = [jax.ShapeDtypeStruct((t, hv), BF16)]
    out_specs = [tok(hv)]
    if not lat:
        out_shape += [jax.ShapeDtypeStruct((t, LANE), F32)] * 2
        out_specs += [pl.BlockSpec((seq, LANE), lambda b: (b, 0))] * 2
    scratch = [pltpu.VMEM((seq, MLA_HEADS * LANE), F32),
               pltpu.VMEM((seq, MLA_HEADS * LANE), F32),
               pltpu.VMEM((seq, hv), BF16)] + [pltpu.VMEM((seq, LANE), BF16)] * 4
    if lat:
        scratch += [pltpu.VMEM((past, MLA_HEADS * LANE), F32),
                    pltpu.VMEM((past, hv), BF16),
                    pltpu.VMEM((past, LANE), BF16),
                    pltpu.VMEM((past, LANE), BF16)]
        scratch += [pltpu.VMEM((Q_BLOCK, past + seq), F32)] * 4

    return pl.pallas_call(
        functools.partial(_mla_kernel, seq=seq, lat=lat),
        grid=(n_seq,),
        in_specs=in_specs,
        out_specs=out_specs,
        out_shape=out_shape,
        scratch_shapes=scratch,
        compiler_params=_cparams(1),
        name="mla_lat" if lat else "mla_ctx",
    )(*args)


def _gqa_kernel(*refs, seq, lat):
    it = iter(refs)
    q_ref, k_ref, v_ref, qn_ref, kn_ref = [next(it) for _ in range(5)]
    if lat:
        ck_ref, cv_ref, cos_ref, sin_ref = [next(it) for _ in range(4)]
    o_ref = next(it)
    if not lat:
        kn_out = next(it)
    qn_s, kn_s, vb_s, q2_s = [next(it) for _ in range(4)]
    s_scr = None
    if lat:
        ckb_s, cvb_s = next(it), next(it)
        s_scr = [[next(it)], [next(it)]]

    qscale = GQA_HEAD_DIM ** -0.5 * LOG2E
    hd = GQA_HEAD_DIM
    nq = GQA_HEADS * hd
    nk = GQA_KV_HEADS * LANE

    def half_norm(x, g, width):
        n_half = width // hd
        ind = (_iota((width, LANE), 0) // hd == _iota((width, LANE), 1)).astype(BF16)
        ind_t = (_iota((LANE, width), 1) // hd == _iota((LANE, width), 0)).astype(BF16)
        del n_half
        ss = _dot_x3(x * x, ind)
        r = lax.rsqrt(ss * (1.0 / hd) + EPS)
        return x * _dot_x3(r, ind_t) * g

    qn = half_norm(q_ref[...], qn_ref[...], nq)
    kn = half_norm(k_ref[...], kn_ref[...], nk)
    if not lat:
        kn_out[...] = kn
    if lat:
        cos = cos_ref[...]
        sin = sin_ref[...]
        lane = _iota((seq, LANE), 1)
        first_half = (lane % (hd // 2)) < (hd // 4)
        for j in range(nq // LANE):
            sl = slice(j * LANE, (j + 1) * LANE)
            qn_s[:, sl] = _rope(qn[:, sl], cos, sin, first_half, hd // 4)
        for j in range(nk // LANE):
            sl = slice(j * LANE, (j + 1) * LANE)
            kn_s[:, sl] = _rope(kn[:, sl], cos, sin, first_half, hd // 4).astype(BF16)
        ckb_s[...] = ck_ref[0].astype(BF16)
        cvb_s[...] = cv_ref[0].astype(BF16)
    else:
        qn_s[...] = qn
        kn_s[...] = kn.astype(BF16)
    vb_s[...] = v_ref[...].astype(BF16)
    lo_q = _iota((Q_BLOCK, LANE), 1) < hd

    for j in range(GQA_HEADS // 2):
        pair = slice(j * LANE, (j + 1) * LANE)
        g = (2 * j) // (GQA_HEADS // GQA_KV_HEADS)
        gs = slice(g * LANE, (g + 1) * LANE)
        for i in range(seq // Q_BLOCK):
            qp = qn_s[i * Q_BLOCK:(i + 1) * Q_BLOCK, pair] * qscale
            q2_s[2 * i * Q_BLOCK:(2 * i + 1) * Q_BLOCK, :] = jnp.where(lo_q, qp, 0.0).astype(BF16)
            q2_s[(2 * i + 1) * Q_BLOCK:(2 * i + 2) * Q_BLOCK, :] = jnp.where(lo_q, 0.0, qp).astype(BF16)

        def emit(r0, outs, pair=pair):
            o = outs[0]
            o_ref[pl.ds(r0, Q_BLOCK), pair] = jnp.where(lo_q, o[:Q_BLOCK], o[Q_BLOCK:]).astype(o_ref.dtype)

        keys = [lambda gs=gs: kn_s[:, gs]]
        vals = [lambda gs=gs: vb_s[:, gs]]
        if lat:
            keys = [lambda gs=gs: ckb_s[:, gs]] + keys
            vals = [lambda gs=gs: cvb_s[:, gs]] + vals
        _attend_blocks(seq, [(q2_s, 2 * Q_BLOCK, keys, vals)], emit, s_scr)


def _gqa(q, k, v, qn, kn, cache, tables, *, seq):
    t = q.shape[0]
    n_seq = t // seq
    lat = cache is not None
    nq = GQA_HEADS * GQA_HEAD_DIM
    nk = GQA_KV_HEADS * LANE
    tok = lambda w: pl.BlockSpec((seq, w), lambda b: (b, 0))
    full = lambda a: pl.BlockSpec(a.shape, lambda b: (0,) * a.ndim)
    args = [q, k, v, qn, kn]
    in_specs = [tok(nq), tok(nk), tok(nk), full(qn), full(kn)]
    past = 0
    if lat:
        past = cache[0].shape[1]
        in_specs += [pl.BlockSpec((1, past, nk), lambda b: (b, 0, 0))] * 2 + [full(tables[0]), full(tables[1])]
        args += [cache[0], cache[1], tables[0], tables[1]]
    out_shape = [jax.ShapeDtypeStruct((t, nq), BF16)]
    out_specs = [tok(nq)]
    if not lat:
        out_shape += [jax.ShapeDtypeStruct((t, nk), F32)]
        out_specs += [pl.BlockSpec((seq, nk), lambda b: (b, 0))]
    scratch = [pltpu.VMEM((seq, nq), F32),
               pltpu.VMEM((seq, nk), BF16),
               pltpu.VMEM((seq, nk), BF16),
               pltpu.VMEM((2 * seq, LANE), BF16)]
    if lat:
        scratch += [pltpu.VMEM((past, nk), BF16), pltpu.VMEM((past, nk), BF16)]
        scratch += [pltpu.VMEM((2 * Q_BLOCK, past + seq), F32)] * 2

    return pl.pallas_call(
        functools.partial(_gqa_kernel, seq=seq, lat=lat),
        grid=(n_seq,),
        in_specs=in_specs,
        out_specs=out_specs,
        out_shape=out_shape,
        scratch_shapes=scratch,
        compiler_params=_cparams(1),
        name="gqa_lat" if lat else "gqa_ctx",
    )(*args)


def _route(sel, sc):
    gs = []
    for g in range(N_GROUPS):
        r = sel[g * GROUP_SIZE:(g + 1) * GROUP_SIZE]
        best = None
        for i in range(GROUP_SIZE):
            for j in range(i + 1, GROUP_SIZE):
                s = r[i] + r[j]
                best = s if best is None else jnp.maximum(best, s)
        gs.append(best)
    cur = gs[0]
    grp = jnp.zeros_like(cur, dtype=jnp.int32)
    for g in range(1, N_GROUPS):
        upd = gs[g] > cur
        grp = jnp.where(upd, g, grp)
        cur = jnp.where(upd, gs[g], cur)

    def pick(rows, i):
        v = rows[i]
        for g in range(1, N_GROUPS):
            v = jnp.where(grp == g, rows[g * GROUP_SIZE + i], v)
        return v

    cand = [pick(sel, i) for i in range(GROUP_SIZE)]
    csc = [pick(sc, i) for i in range(GROUP_SIZE)]
    v1, i1, w1 = cand[0], jnp.zeros_like(grp), csc[0]
    for i in range(1, GROUP_SIZE):
        upd = cand[i] > v1
        v1 = jnp.where(upd, cand[i], v1)
        i1 = jnp.where(upd, i, i1)
        w1 = jnp.where(upd, csc[i], w1)
    v2 = i2 = w2 = None
    for i in range(GROUP_SIZE):
        ok = i1 != i
        if v2 is None:
            v2 = jnp.where(ok, cand[i], -jnp.inf)
            i2 = jnp.zeros_like(grp)
            w2 = csc[i]
        else:
            upd = ok & (cand[i] > v2)
            v2 = jnp.where(upd, cand[i], v2)
            i2 = jnp.where(upd, i, i2)
            w2 = jnp.where(upd, csc[i], w2)
    tot = w1 + w2
    local = [jnp.where(i1 == i, w1 / tot, 0.0) + jnp.where(i2 == i, w2 / tot, 0.0)
             for i in range(GROUP_SIZE)]
    onehot = [jnp.where(grp == g, 1.0, 0.0) for g in range(N_GROUPS)]
    return local, onehot


def _outproj_kernel(*refs, n_y):
    x_ref, mod_ref = refs[0], refs[1]
    y_refs = refs[2:2 + n_y]
    w_ref, g_ref, wr_ref, rb_ref, x1_ref, h2_ref, gate_ref, cnt_ref = refs[2 + n_y:]
    m = mod_ref[0]
    off = 0
    out = None
    for y_ref in y_refs:
        k = y_ref.shape[1]
        part = _dot(y_ref[...], w_ref[off:off + k, :])
        out = part if out is None else out + part
        off += k
    x1 = x_ref[...] + m[:, 2 * D:3 * D] * out
    x1_ref[...] = x1
    h2 = _norm_mod(x1, g_ref[...], m[:, 3 * D:4 * D], m[:, 4 * D:5 * D])
    h2_hi = h2.astype(BF16)
    h2_ref[...] = h2_hi
    h2_mid = (h2 - h2_hi.astype(F32)).astype(BF16)
    parts = _dot(h2_hi, wr_ref[...]) + _dot(h2_mid, wr_ref[...])
    logits = (parts + pltpu.roll(parts, LANE - N_EXPERTS, axis=1)
              + pltpu.roll(parts, LANE - 2 * N_EXPERTS, axis=1))
    sc_t = _sigmoid(logits.T[0:N_EXPERTS, :])
    sel_t = sc_t + rb_ref[...]
    local, onehot = _route([sel_t[e:e + 1] for e in range(N_EXPERTS)],
                           [sc_t[e:e + 1] for e in range(N_EXPERTS)])
    t = x1.shape[0]
    rows_t = (local + [jnp.zeros((GROUP_LANE - GROUP_SIZE, t), F32)] + onehot
              + [jnp.zeros((LANE - GROUP_LANE - N_GROUPS, t), F32)])
    routed = jnp.concatenate(rows_t, axis=0).T
    gate_ref[...] = routed
    cnt = jnp.sum(routed, axis=0, keepdims=True).astype(jnp.int32)
    cnt_ref[0] = jnp.broadcast_to(cnt, (8, LANE))


def _outproj(x, mod, ys, w, g, wr, rb):
    t = x.shape[0]
    full = lambda a: pl.BlockSpec(a.shape, lambda i: (0,) * a.ndim)
    return pl.pallas_call(
        functools.partial(_outproj_kernel, n_y=len(ys)),
        grid=(t // TOKEN_TILE,),
        in_specs=[pl.BlockSpec((TOKEN_TILE, D), lambda i: (i, 0)),
                  _mod_spec(mod, t, TOKEN_TILE)]
                 + [pl.BlockSpec((TOKEN_TILE, y.shape[1]), lambda i: (i, 0)) for y in ys]
                 + [full(w), full(g), full(wr), full(rb)],
        out_specs=[pl.BlockSpec((TOKEN_TILE, D), lambda i: (i, 0)),
                   pl.BlockSpec((TOKEN_TILE, D), lambda i: (i, 0)),
                   pl.BlockSpec((TOKEN_TILE, LANE), lambda i: (i, 0)),
                   pl.BlockSpec((1, 8, LANE), lambda i: (i, 0, 0))],
        out_shape=[jax.ShapeDtypeStruct((t, D), F32),
                   jax.ShapeDtypeStruct((t, D), BF16),
                   jax.ShapeDtypeStruct((t, LANE), F32),
                   jax.ShapeDtypeStruct((t // TOKEN_TILE, 8, LANE), jnp.int32)],
        compiler_params=_cparams(1),
        name="outproj_router",
    )(x, mod, *ys, w, g, wr, rb)


def _pad_to(n, shift):
    return ((n + (1 << shift) - 1) >> shift) << shift


SEG_SHIFT = SEG_ALIGN.bit_length() - 1
TILE_SHIFT = EXPERT_TILE.bit_length() - 1


def _segment_copies(plan_ref, i, nt, start, make):
    total = 0
    for g in range(N_GROUPS):
        n = plan_ref[i * N_GROUPS + g]
        lo = plan_ref[(nt + i) * N_GROUPS + g]
        base = plan_ref[(2 * nt + i) * N_GROUPS + g]
        chunks = (n + SEG_ALIGN - 1) >> SEG_SHIFT

        def body(c, carry, lo=lo, base=base):
            for cp in make(pl.multiple_of(lo + c * SEG_ALIGN, SEG_ALIGN),
                           pl.multiple_of(base + c * SEG_ALIGN, SEG_ALIGN)):
                if start:
                    cp.start()
                else:
                    cp.wait()
            return carry

        lax.fori_loop(0, chunks, body, 0)
        total = total + chunks
    return total


def _dispatch_kernel(cnt_ref, h_ref, r_ref, xs_hbm, rs_hbm, dest_ref, plan_ref,
                     xs_l, rs_l, zx, zr, fill_s, sem, *, nt, n_rows):
    i = pl.program_id(0)
    t = TOKEN_TILE
    k_tiles = n_rows // EXPERT_TILE

    @pl.when(i == 0)
    def _():
        zx[...] = jnp.zeros_like(zx)
        zr[...] = jnp.zeros_like(zr)
        tots = []
        for g in range(N_GROUPS):
            tots.append(lax.fori_loop(
                0, nt, lambda tt, acc, g=g: acc + _pad_to(cnt_ref[tt * N_GROUPS + g], SEG_SHIFT), jnp.int32(0)))
        starts = []
        cur = jnp.int32(0)
        for g in range(N_GROUPS):
            starts.append(cur)
            fill_s[2 * g] = cur + tots[g]
            cur = cur + _pad_to(tots[g], TILE_SHIFT)
            fill_s[2 * g + 1] = cur
        fill_s[2 * N_GROUPS] = cur
        fill_s[2 * N_GROUPS + 1] = n_rows

        def per_tile(tt, run):
            lo = jnp.int32(0)
            new = []
            for g in range(N_GROUPS):
                n = cnt_ref[tt * N_GROUPS + g]
                plan_ref[tt * N_GROUPS + g] = n
                plan_ref[(nt + tt) * N_GROUPS + g] = lo
                plan_ref[(2 * nt + tt) * N_GROUPS + g] = run[g]
                seg = _pad_to(n, SEG_SHIFT)
                lo = lo + seg
                new.append(run[g] + seg)
            return tuple(new)

        lax.fori_loop(0, nt, per_tile, tuple(starts))
        for k in range(k_tiles):
            row = k * EXPERT_TILE
            grp = 0
            for g in range(1, N_GROUPS):
                grp = grp + (row >= starts[g]).astype(jnp.int32)
            plan_ref[3 * nt * N_GROUPS + k] = grp
        plan_ref[3 * nt * N_GROUPS + k_tiles] = cur >> TILE_SHIFT

    routed = r_ref[...]
    lane = _iota((t, LANE), 1)
    is_grp = (lane >= GROUP_LANE) & (lane < GROUP_LANE + N_GROUPS)
    onehot = jnp.where(is_grp, routed, 0.0)
    tril = (_iota((t, t), 0) >= _iota((t, t), 1)).astype(BF16)
    rank = _dot(tril, onehot.astype(BF16))
    seg_lo = jnp.zeros((1, LANE), F32)
    lane1 = _iota((1, LANE), 1)
    for g in range(N_GROUPS):
        lo = plan_ref[(nt + i) * N_GROUPS + g]
        seg_lo = jnp.where(lane1 == GROUP_LANE + g, (lo - 1).astype(F32), seg_lo)
    val = jnp.where(onehot > 0.0, rank + seg_lo, 0.0)
    dest = _dot_x2(val, jnp.ones((LANE, LANE), BF16))
    dest_ref[...] = dest
    dest_row = dest.T[0:1, :].astype(jnp.int32)
    perm = jnp.where(_iota((DISPATCH_ROWS, t), 0) == dest_row, 1.0, 0.0).astype(BF16)
    xs_l[...] = _dot(perm, h_ref[...]).astype(BF16)
    rs_l[...] = _dot_3x(perm, routed)

    def to_buffer(lo, base):
        return (pltpu.make_async_copy(xs_l.at[pl.ds(lo, SEG_ALIGN)], xs_hbm.at[pl.ds(base, SEG_ALIGN)], sem.at[0]),
                pltpu.make_async_copy(rs_l.at[pl.ds(lo, SEG_ALIGN)], rs_hbm.at[pl.ds(base, SEG_ALIGN)], sem.at[1]))

    total = _segment_copies(plan_ref, i, nt, True, to_buffer)

    def wait_one(c, carry):
        for cp in to_buffer(0, 0):
            cp.wait()
        return carry

    lax.fori_loop(0, total, wait_one, 0)

    @pl.when(i == nt - 1)
    def _():
        def zero(base):
            return (pltpu.make_async_copy(zx, xs_hbm.at[pl.ds(base, SEG_ALIGN)], sem.at[0]),
                    pltpu.make_async_copy(zr, rs_hbm.at[pl.ds(base, SEG_ALIGN)], sem.at[1]))

        n_fill = 0
        for r in range(N_GROUPS + 1):
            lo = fill_s[2 * r]
            chunks = (fill_s[2 * r + 1] - lo) >> SEG_SHIFT

            def body(c, carry, lo=lo):
                for cp in zero(pl.multiple_of(lo + c * SEG_ALIGN, SEG_ALIGN)):
                    cp.start()
                return carry

            lax.fori_loop(0, chunks, body, 0)
            n_fill = n_fill + chunks

        def wait_fill(c, carry):
            for cp in zero(0):
                cp.wait()
            return carry

        lax.fori_loop(0, n_fill, wait_fill, 0)


def _moe_rows(n_tokens):
    nt = n_tokens // TOKEN_TILE
    worst = n_tokens + nt * N_GROUPS * (SEG_ALIGN - 1) + N_GROUPS * (EXPERT_TILE - SEG_ALIGN)
    return _pad_to(worst, TILE_SHIFT)


def _dispatch(counts, h2, routed):
    t = h2.shape[0]
    nt = t // TOKEN_TILE
    n_rows = _moe_rows(t)
    plan_len = 3 * nt * N_GROUPS + n_rows // EXPERT_TILE + 1
    grid_spec = pltpu.PrefetchScalarGridSpec(
        num_scalar_prefetch=1,
        grid=(nt,),
        in_specs=[pl.BlockSpec((TOKEN_TILE, D), lambda i, c: (i, 0)),
                  pl.BlockSpec((TOKEN_TILE, LANE), lambda i, c: (i, 0))],
        out_specs=[pl.BlockSpec(memory_space=pl.ANY),
                   pl.BlockSpec(memory_space=pl.ANY),
                   pl.BlockSpec((TOKEN_TILE, LANE), lambda i, c: (i, 0)),
                   pl.BlockSpec(memory_space=pltpu.SMEM)],
        scratch_shapes=[pltpu.VMEM((DISPATCH_ROWS, D), BF16),
                        pltpu.VMEM((DISPATCH_ROWS, LANE), F32),
                        pltpu.VMEM((SEG_ALIGN, D), BF16),
                        pltpu.VMEM((SEG_ALIGN, LANE), F32),
                        pltpu.SMEM((2 * N_GROUPS + 2,), jnp.int32),
                        pltpu.SemaphoreType.DMA((2,))])
    return pl.pallas_call(
        functools.partial(_dispatch_kernel, nt=nt, n_rows=n_rows),
        grid_spec=grid_spec,
        out_shape=[jax.ShapeDtypeStruct((n_rows, D), BF16),
                   jax.ShapeDtypeStruct((n_rows, LANE), F32),
                   jax.ShapeDtypeStruct((t, LANE), F32),
                   jax.ShapeDtypeStruct((plan_len,), jnp.int32)],
        compiler_params=_cparams(1),
        name="moe_dispatch",
    )(counts, h2, routed)


def _experts_kernel(plan_ref, xs_ref, rs_ref, wg_ref, wu_ref, wd_ref, y_ref, *, used_at):
    k = pl.program_id(0)
    used = plan_ref[used_at]

    @pl.when(k < used)
    def _():
        rows = xs_ref[...]
        gates = rs_ref[...]
        acc = None
        for e in range(GROUP_SIZE):
            a = _dot(rows, wg_ref[e])
            u = _dot(rows, wu_ref[e])
            act = (_silu(a) * u * gates[:, e:e + 1]).astype(BF16)
            part = _dot(act, wd_ref[e])
            acc = part if acc is None else acc + part
        y_ref[...] = acc.astype(y_ref.dtype)

    @pl.when(k >= used)
    def _():
        y_ref[...] = jnp.zeros_like(y_ref)


def _experts(plan, xs, rs, wg, wu, wd, layer, nt):
    n_rows = xs.shape[0]
    k_tiles = n_rows // EXPERT_TILE
    at = 3 * nt * N_GROUPS
    wspec = lambda a, b: pl.BlockSpec((None, GROUP_SIZE, a, b), lambda k, p: (layer, p[at + k], 0, 0))
    grid_spec = pltpu.PrefetchScalarGridSpec(
        num_scalar_prefetch=1,
        grid=(k_tiles,),
        in_specs=[pl.BlockSpec((EXPERT_TILE, D), lambda k, p: (k, 0)),
                  pl.BlockSpec((EXPERT_TILE, LANE), lambda k, p: (k, 0)),
                  wspec(D, D_EXPERT), wspec(D, D_EXPERT), wspec(D_EXPERT, D)],
        out_specs=pl.BlockSpec((EXPERT_TILE, D), lambda k, p: (k, 0)))
    return pl.pallas_call(
        functools.partial(_experts_kernel, used_at=at + k_tiles),
        grid_spec=grid_spec,
        out_shape=jax.ShapeDtypeStruct((n_rows, D), BF16),
        compiler_params=_cparams(1),
        name="moe_experts",
    )(plan, xs, rs, wg, wu, wd)


def _combine_kernel(plan_ref, x_ref, mod_ref, dest_ref, ys_hbm, o_ref, y_l, sem, *, nt):
    i = pl.program_id(0)

    @pl.when(i == 0)
    def _():
        y_l[...] = jnp.zeros_like(y_l)

    def from_buffer(lo, base):
        return (pltpu.make_async_copy(ys_hbm.at[pl.ds(base, SEG_ALIGN)], y_l.at[pl.ds(lo, SEG_ALIGN)], sem.at[0]),)

    total = _segment_copies(plan_ref, i, nt, True, from_buffer)

    def wait_one(c, carry):
        from_buffer(0, 0)[0].wait()
        return carry

    lax.fori_loop(0, total, wait_one, 0)
    dest = dest_ref[...].astype(jnp.int32)
    lane = _iota(dest.shape, 1)
    back = jnp.concatenate(
        [jnp.where(dest == lane + j * LANE, 1.0, 0.0).astype(BF16) for j in range(DISPATCH_ROWS // LANE)], axis=1)
    y = _dot(back, y_l[...])
    o_ref[...] = x_ref[...] + mod_ref[0][:, 5 * D:6 * D] * y


def _combine(plan, x1, mod, dest, ys):
    t = x1.shape[0]
    nt = t // TOKEN_TILE
    per_row = t // mod.shape[0]
    grid_spec = pltpu.PrefetchScalarGridSpec(
        num_scalar_prefetch=1,
        grid=(nt,),
        in_specs=[pl.BlockSpec((TOKEN_TILE, D), lambda i, p: (i, 0)),
                  pl.BlockSpec((1, 1, 6 * D), lambda i, p: ((i * TOKEN_TILE) // per_row, 0, 0)),
                  pl.BlockSpec((TOKEN_TILE, LANE), lambda i, p: (i, 0)),
                  pl.BlockSpec(memory_space=pl.ANY)],
        out_specs=pl.BlockSpec((TOKEN_TILE, D), lambda i, p: (i, 0)),
        scratch_shapes=[pltpu.VMEM((DISPATCH_ROWS, D), BF16),
                        pltpu.SemaphoreType.DMA((1,))])
    return pl.pallas_call(
        functools.partial(_combine_kernel, nt=nt),
        grid_spec=grid_spec,
        out_shape=jax.ShapeDtypeStruct((t, D), F32),
        compiler_params=_cparams(1),
        name="moe_combine",
    )(plan, x1, mod, dest, ys)


def _moe(x1, mod, h2, routed, cnt, wg, wu, wd, layer):
    nt = x1.shape[0] // TOKEN_TILE
    counts = cnt[:, 0, GROUP_LANE:GROUP_LANE + N_GROUPS].reshape(nt * N_GROUPS)
    xs, rs, dest, plan = _dispatch(counts, h2, routed)
    ys = _experts(plan, xs, rs, wg, wu, wd, layer, nt)
    return _combine(plan, x1, mod, dest, ys)


def _rope_angles(pos, half):
    inv = jnp.power(ROPE_THETA, -jnp.arange(half, dtype=F32) / half)
    ang = pos.astype(F32)[:, None] * inv[None, :]
    return jnp.cos(ang), jnp.sin(ang)


def _rope_tables(seq, rot, lead, lane_tiles):
    rows = seq // GRID_W
    row = jnp.repeat(jnp.arange(rows), GRID_W)
    col = jnp.tile(jnp.arange(GRID_W), rows)
    cr, sr = _rope_angles(row, rot // 4)
    cc, sc = _rope_angles(col, rot // 4)
    cos = jnp.concatenate([cr, cr, cc, cc], axis=1)
    sin = jnp.concatenate([-sr, sr, -sc, sc], axis=1)
    width = LANE // lane_tiles
    pad = width - lead - rot
    cos = jnp.concatenate([jnp.ones((seq, lead), F32), cos, jnp.ones((seq, pad), F32)], axis=1)
    sin = jnp.concatenate([jnp.zeros((seq, lead), F32), sin, jnp.zeros((seq, pad), F32)], axis=1)
    return jnp.tile(cos, (1, lane_tiles)), jnp.tile(sin, (1, lane_tiles))


def _pad_lanes(a, width):
    return jnp.pad(a, [(0, 0)] * (a.ndim - 1) + [(0, width - a.shape[-1])])


def _even_weights(w_in, w_q_b, w_kv_b):
    z = w_in[:, 0:1024]
    xbc = w_in[:, 1024:2304]
    dt = w_in[:, 2304:2320]
    qa = w_in[:, 2320:2576]
    ckv = w_in[:, 2576:2704]
    kpe = w_in[:, 2704:2736]
    zeros = lambda n: jnp.zeros((D, n), w_in.dtype)
    w = jnp.concatenate([z, xbc, qa, ckv, zeros(MLA_NOPE), kpe, zeros(LANE - MLA_QK), dt,
                         zeros(LANE - SSD_HEADS)], axis=1).astype(BF16)
    wq = _pad_lanes(w_q_b.reshape(MLA_Q_LORA, MLA_HEADS, MLA_QK), LANE).reshape(MLA_Q_LORA, MLA_HEADS * LANE)
    kvb = w_kv_b.reshape(MLA_KV_LORA, MLA_HEADS, MLA_NOPE + MLA_V)
    wk = _pad_lanes(kvb[:, :, :MLA_NOPE], LANE).reshape(MLA_KV_LORA, MLA_HEADS * LANE)
    wv = kvb[:, :, MLA_NOPE:].reshape(MLA_KV_LORA, MLA_HEADS * MLA_V)
    return w, wq.astype(BF16), wk.astype(BF16), wv.astype(BF16)


def _dup_heads(a):
    s = a.shape[:-1]
    a = a.reshape(s + (GQA_KV_HEADS, 1, GQA_HEAD_DIM))
    return jnp.broadcast_to(a, s + (GQA_KV_HEADS, 2, GQA_HEAD_DIM)).reshape(s + (GQA_KV_HEADS * LANE,))


def _undup_heads(a):
    s = a.shape[:-1]
    return a.reshape(s + (GQA_KV_HEADS, 2, GQA_HEAD_DIM))[..., 0, :]


def _state_to_kernel(h0):
    b = h0.shape[0]
    t = jnp.transpose(h0, (0, 3, 1, 2)).reshape(b, SSD_STATE, SSD_INNER)
    half = SSD_INNER // SSD_GROUPS
    col = jnp.arange(SSD_INNER) // half
    parts = [jnp.where(col == g, t, 0.0) for g in range(SSD_GROUPS)]
    return jnp.concatenate(parts, axis=1)


def _state_from_kernel(st):
    return st.reshape(st.shape[0], SSD_HEADS, SSD_HEADDIM, SSD_STATE)


def kernel(x_prompt, x_sample, c, state_ssd_fwd, state_ssd_bwd, cache_mla_ckv, cache_mla_kpe, cache_gqa_k, cache_gqa_v, c_ctx, g_mix, g_ffn, w_ada, b_ada, w_router, router_bias, w_exp_gate, w_exp_up, w_exp_down, w_in_even, ssd_conv_w, ssd_conv_b, ssd_a_log_fwd, ssd_a_log_bwd, ssd_dt_bias_fwd, ssd_dt_bias_bwd, ssd_d, ssd_norm, mla_q_a_norm, mla_w_q_b, mla_kv_a_norm, mla_w_kv_b, mla_q_norm, mla_k_norm, w_out_even, w_in_odd, gqa_q_norm, gqa_k_norm, w_out_odd):
    nb, ls, _ = x_prompt.shape
    db, dl, _ = x_sample.shape
    depth = w_ada.shape[0]
    n_ctx = nb * ls

    xs = [x_prompt.reshape(n_ctx, D), x_sample.reshape(db * dl, D)]
    seqs = (ls, dl)
    rows = 16
    cond = jnp.concatenate([c_ctx[None, :], c, jnp.zeros((rows - 1 - db, D), F32)], axis=0)
    mod_all = _ada_mod(cond, w_ada, b_ada)

    wr = _pad_lanes(jnp.concatenate(_split3(w_router), axis=1), LANE)
    rb = router_bias.reshape(N_EXPERTS, 1)
    wg_all = w_exp_gate.astype(BF16)
    wu_all = w_exp_up.astype(BF16)
    wd_all = w_exp_down.astype(BF16)

    outs = {}
    for i in range(depth):
        j = i // 2
        mods = [mod_all[i, 0:1].reshape(1, 1, 6 * D), mod_all[i, 1:1 + db].reshape(db, 1, 6 * D)]
        g1 = g_mix[i].reshape(1, D)
        g2 = g_ffn[i].reshape(1, D)
        if i % 2 == 0:
            w, wq, wk, wv = _even_weights(w_in_even[j], mla_w_q_b[j], mla_w_kv_b[j])
            cw = jnp.pad(ssd_conv_w[j], ((0, 5), (0, 0)))
            cb = ssd_conv_b[j].reshape(1, SSD_CONV_DIM)
            hp = _pad_lanes(jnp.stack([ssd_a_log_fwd[j], ssd_a_log_bwd[j], ssd_dt_bias_fwd[j],
                                       ssd_dt_bias_bwd[j], ssd_d[j]]), LANE)
            hp = jnp.pad(hp, ((0, 3), (0, 0)))
            nw = ssd_norm[j].reshape(1, SSD_INNER)
            qan = mla_q_a_norm[j].reshape(1, MLA_Q_LORA)
            kvn = mla_kv_a_norm[j].reshape(1, MLA_KV_LORA)
            qn = _pad_lanes(mla_q_norm[j].reshape(1, MLA_QK), LANE)
            kn = _pad_lanes(mla_k_norm[j].reshape(1, MLA_QK), LANE)
            h0 = (_state_to_kernel(state_ssd_fwd[:, j]), _state_to_kernel(state_ssd_bwd[:, j]))
            ckpe = jnp.pad(cache_mla_kpe[:, j], ((0, 0), (0, 0), (MLA_NOPE, LANE - MLA_QK)))
            tables = _rope_tables(dl, MLA_ROPE, MLA_NOPE, 1)
            w_out = w_out_even[j].astype(BF16)
            ys = []
            for s in range(2):
                z, xbc, qa, kv, dt = _inproj(xs[s], mods[s], g1, w,
                                             (SSD_INNER, SSD_CONV_DIM, MLA_Q_LORA, 2 * LANE, LANE))
                if s == 0:
                    y, sf, sb = _ssd(z, xbc, dt, cw, cb, hp, nw, None, seq=seqs[s])
                    o, ckv_new, kpe_new = _mla(qa, kv, qan, kvn, wq, wk, wv, qn, kn, None, None, seq=seqs[s])
                    outs.setdefault("ssd_f", []).append(_state_from_kernel(sf))
                    outs.setdefault("ssd_b", []).append(_state_from_kernel(sb))
                    outs.setdefault("ckv", []).append(ckv_new.reshape(nb, ls, MLA_KV_LORA))
                    outs.setdefault("kpe", []).append(kpe_new[:, MLA_NOPE:MLA_QK].reshape(nb, ls, MLA_ROPE))
                else:
                    y, = _ssd(z, xbc, dt, cw, cb, hp, nw, h0, seq=seqs[s])
                    o, = _mla(qa, kv, qan, kvn, wq, wk, wv, qn, kn, (cache_mla_ckv[:, j], ckpe), tables,
                              seq=seqs[s])
                ys.append((y, o))
        else:
            wi = w_in_odd[j]
            nq = GQA_HEADS * GQA_HEAD_DIM
            nkv = GQA_KV_HEADS * GQA_HEAD_DIM
            nk = GQA_KV_HEADS * LANE
            w = jnp.concatenate([wi[:, :nq], _dup_heads(wi[:, nq:nq + nkv]), _dup_heads(wi[:, nq + nkv:])],
                                axis=1).astype(BF16)
            qn = jnp.tile(gqa_q_norm[j].reshape(1, GQA_HEAD_DIM), (1, nq // GQA_HEAD_DIM))
            kn = jnp.tile(gqa_k_norm[j].reshape(1, GQA_HEAD_DIM), (1, nk // GQA_HEAD_DIM))
            cache = (_dup_heads(cache_gqa_k[:, j].reshape(db, -1, nkv)),
                     _dup_heads(cache_gqa_v[:, j].reshape(db, -1, nkv)))
            tables = _rope_tables(dl, GQA_HEAD_DIM, 0, 2)
            w_out = w_out_odd[j].astype(BF16)
            ys = []
            for s in range(2):
                q, k, v = _inproj(xs[s], mods[s], g1, w, (nq, nk, nk))
                if s == 0:
                    o, k_new = _gqa(q, k, v, qn, kn, None, None, seq=seqs[s])
                    outs.setdefault("gk", []).append(
                        _undup_heads(k_new).reshape(nb, ls, GQA_KV_HEADS, GQA_HEAD_DIM))
                    outs.setdefault("gv", []).append(
                        _undup_heads(v).reshape(nb, ls, GQA_KV_HEADS, GQA_HEAD_DIM))
                else:
                    o, = _gqa(q, k, v, qn, kn, cache, tables, seq=seqs[s])
                ys.append((o,))
        for s in range(2):
            x1, h2, routed, cnt = _outproj(xs[s], mods[s], ys[s], w_out, g2, wr, rb)
            xs[s] = _moe(x1, mods[s], h2, routed, cnt, wg_all, wu_all, wd_all, i)

    stack = lambda key: jnp.stack(outs[key], axis=1)
    return (xs[0].reshape(nb, ls, D), xs[1].reshape(db, dl, D),
            stack("ssd_f"), stack("ssd_b"), stack("ckv"), stack("kpe"), stack("gk"), stack("gv"))
```

```python
import functools
import math

import jax
import jax.numpy as jnp
from jax import lax
from jax.experimental import pallas as pl
from jax.experimental.pallas import tpu as pltpu

F32 = jnp.float32
BF16 = jnp.bfloat16

D = 1024
EPS = 1e-6
GRID_W = 64
ROPE_THETA = 10000.0
SSD_HEADS = 16
SSD_HEADDIM = 64
SSD_INNER = 1024
SSD_STATE = 64
SSD_GROUPS = 2
SSD_CONV_DIM = 1280
SSD_CHUNK = 128
MLA_HEADS = 8
MLA_Q_LORA = 256
MLA_KV_LORA = 128
MLA_NOPE = 64
MLA_ROPE = 32
MLA_V = 64
MLA_QK = 96
GQA_HEADS = 16
GQA_KV_HEADS = 4
GQA_HEAD_DIM = 64
N_EXPERTS = 16
N_GROUPS = 4
GROUP_SIZE = 4
D_EXPERT = 256

LANE = 128
Q_BLOCK = 128
LOG2E = 1.4426950408889634
TOKEN_TILE = 512
EXPERT_TILE = 512
SEG_ALIGN = 16
GROUP_LANE = 16
DISPATCH_ROWS = TOKEN_TILE + LANE
ADA_TILE = 1536
VMEM_LIMIT = 56 * 1024 * 1024


def _cparams(n_axes):
    return pltpu.CompilerParams(dimension_semantics=("arbitrary",) * n_axes,
                                vmem_limit_bytes=VMEM_LIMIT)


def _dot(a, b):
    return jnp.dot(a, b, preferred_element_type=F32)


def _dot_nt(a, b):
    return lax.dot_general(a, b, (((1,), (1,)), ((), ())), preferred_element_type=F32)


def _split3(x):
    hi = x.astype(BF16)
    r1 = x - hi.astype(F32)
    mid = r1.astype(BF16)
    lo = (r1 - mid.astype(F32)).astype(BF16)
    return hi, mid, lo


def _dot_x3(x, e):
    hi, mid, lo = _split3(x)
    return _dot(hi, e) + _dot(mid, e) + _dot(lo, e)


def _dot_x2(x, e):
    hi = x.astype(BF16)
    mid = (x - hi.astype(F32)).astype(BF16)
    return _dot(hi, e) + _dot(mid, e)


def _dot_3x(e, x):
    hi, mid, lo = _split3(x)
    return _dot(e, hi) + _dot(e, mid) + _dot(e, lo)


def _dot_f32(a, b):
    a0, a1, a2 = _split3(a)
    b0, b1, b2 = _split3(b)
    return (_dot(a0, b0) + (_dot(a0, b1) + _dot(a1, b0))
            + (_dot(a1, b1) + _dot(a0, b2) + _dot(a2, b0)))


def _sigmoid(x):
    return 1.0 / (1.0 + jnp.exp(-x))


def _silu(x):
    return x * _sigmoid(x)


def _iota(shape, axis):
    return lax.broadcasted_iota(jnp.int32, shape, axis)


def _ada_kernel(c_ref, w_ref, b_ref, o_ref):
    a = _silu(c_ref[...]).astype(BF16)
    o_ref[0] = _dot(a, w_ref[0].astype(BF16)) + b_ref[0]


def _ada_mod(cond, w_ada, b_ada):
    depth, _, n = w_ada.shape
    rows = cond.shape[0]
    return pl.pallas_call(
        _ada_kernel,
        grid=(depth, n // ADA_TILE),
        in_specs=[pl.BlockSpec((rows, D), lambda l, j: (0, 0)),
                  pl.BlockSpec((1, D, ADA_TILE), lambda l, j: (l, 0, j)),
                  pl.BlockSpec((1, 1, ADA_TILE), lambda l, j: (l, 0, j))],
        out_specs=pl.BlockSpec((1, rows, ADA_TILE), lambda l, j: (l, 0, j)),
        out_shape=jax.ShapeDtypeStruct((depth, rows, n), F32),
        compiler_params=_cparams(2),
        name="ada_mod",
    )(cond, w_ada, b_ada.reshape(depth, 1, n))


def _mod_spec(mod, n_tokens, tile, n_axes=1):
    per_row = n_tokens // mod.shape[0]
    if n_axes == 1:
        return pl.BlockSpec((1, 1, 6 * D), lambda i: ((i * tile) // per_row, 0, 0))
    return pl.BlockSpec((1, 1, 6 * D), lambda i, e: ((i * tile) // per_row, 0, 0))


def _norm_mod(x, g, shift, scale):
    ms = jnp.mean(x * x, axis=-1, keepdims=True)
    y = x * lax.rsqrt(ms + EPS) * g
    return y * (1.0 + scale) + shift


def _inproj_kernel(x_ref, mod_ref, g_ref, w_ref, *o_refs, widths):
    m = mod_ref[0]
    h = _norm_mod(x_ref[...], g_ref[...], m[:, 0:D], m[:, D:2 * D]).astype(BF16)
    off = 0
    for o_ref, wd in zip(o_refs, widths):
        o_ref[...] = _dot(h, w_ref[:, off:off + wd]).astype(o_ref.dtype)
        off += wd


def _inproj(x, mod, g, w, widths):
    t = x.shape[0]
    n = w.shape[1]
    return pl.pallas_call(
        functools.partial(_inproj_kernel, widths=widths),
        grid=(t // TOKEN_TILE,),
        in_specs=[pl.BlockSpec((TOKEN_TILE, D), lambda i: (i, 0)),
                  _mod_spec(mod, t, TOKEN_TILE),
                  pl.BlockSpec((1, D), lambda i: (0, 0)),
                  pl.BlockSpec((D, n), lambda i: (0, 0))],
        out_specs=[pl.BlockSpec((TOKEN_TILE, wd), lambda i: (i, 0)) for wd in widths],
        out_shape=[jax.ShapeDtypeStruct((t, wd), F32) for wd in widths],
        compiler_params=_cparams(1),
        name="inproj",
    )(x, mod, g, w)


def _ssd_kernel(*refs, seq, has_h0, want_state):
    it = iter(refs)
    z_ref, xbc_ref, dt_ref, cw_ref, cb_ref, hp_ref, nw_ref = [next(it) for _ in range(7)]
    h0_refs = (next(it), next(it)) if has_h0 else None
    y_ref = next(it)
    s_refs = (next(it), next(it)) if want_state else None
    xs_ref, bc_ref, yacc_ref, st_ref = [next(it) for _ in range(4)]

    q = SSD_CHUNK
    nc = seq // q
    gn = SSD_GROUPS * SSD_STATE

    cw = cw_ref[...]
    cb = cb_ref[...]
    rid = _iota((q, SSD_CONV_DIM), 0)

    def conv_body(c, carry):
        r0 = pl.multiple_of(c * q, q)
        xc = xbc_ref[pl.ds(r0, q), :]
        prev = xbc_ref[pl.ds(jnp.maximum(r0 - 1, 0), 1), :]
        nxt = xbc_ref[pl.ds(jnp.minimum(r0 + q, seq - 1), 1), :]
        prev = jnp.where(c > 0, prev, 0.0)
        nxt = jnp.where(c < nc - 1, nxt, 0.0)
        xp = jnp.where(rid == 0, prev, pltpu.roll(xc, 1, axis=0))
        xn = jnp.where(rid == q - 1, nxt, pltpu.roll(xc, q - 1, axis=0))
        v = _silu(xp * cw[0:1] + xc * cw[1:2] + xn * cw[2:3] + cb)
        xs_ref[pl.ds(r0, q), :] = v[:, :SSD_INNER]
        bc_ref[pl.ds(r0, q), :] = v[:, SSD_INNER:]
        return carry

    lax.fori_loop(0, nc, conv_body, 0)

    hp = hp_ref[...]
    expand = (_iota((LANE, SSD_INNER), 1) // SSD_HEADDIM == _iota((LANE, SSD_INNER), 0)).astype(BF16)
    blockmask = (_iota((gn, SSD_INNER), 0) // SSD_STATE
                 == _iota((gn, SSD_INNER), 1) // (SSD_INNER // SSD_GROUPS))
    ri = _iota((q, q), 0)
    ci = _iota((q, q), 1)
    lane = _iota((q, LANE), 1)
    lo_half = lane < SSD_STATE

    def scan_pass(fwd):
        a = -jnp.exp(hp[0:1] if fwd else hp[1:2])
        bias = hp[2:3] if fwd else hp[3:4]
        causal = (ri >= ci) if fwd else (ri <= ci)
        tri = causal.astype(BF16)
        if has_h0:
            st_ref[...] = h0_refs[0 if fwd else 1][0]
        else:
            st_ref[...] = jnp.zeros((gn, SSD_INNER), F32)

        def body(k, carry):
            c = k if fwd else nc - 1 - k
            r0 = pl.multiple_of(c * q, q)
            xs = xs_ref[pl.ds(r0, q), :]
            bc = bc_ref[pl.ds(r0, q), :]
            bm = bc[:, :gn]
            cm = bc[:, gn:]
            xdt_raw = dt_ref[pl.ds(r0, q), :] + bias
            dt = jnp.maximum(xdt_raw, 0.0) + jnp.log1p(jnp.exp(-jnp.abs(xdt_raw)))
            cum = _dot_3x(tri, dt * a)
            cum_t = cum.T
            last = cum[q - 1:q] if fwd else cum[0:1]
            stacked = jnp.concatenate([dt, jnp.exp(cum), jnp.exp(last - cum)], axis=0)
            ex = _dot_x2(stacked, expand)
            dt_e, ecum_e, dend_e = ex[0:q], ex[q:2 * q], ex[2 * q:3 * q]
            cdec_e = ecum_e[q - 1:q] if fwd else ecum_e[0:1]
            xdt = xs * dt_e
            st = st_ref[...]
            cmb = cm.astype(BF16)
            bmb = bm.astype(BF16)
            y = _dot(cmb, st.astype(BF16)) * ecum_e
            cb0 = _dot_nt(jnp.where(lo_half, cm, 0.0).astype(BF16), bmb)
            cb1 = _dot_nt(jnp.where(lo_half, 0.0, cm).astype(BF16), bmb)
            pairs = []
            for j in range(SSD_HEADS // 2):
                cbg = cb0 if j < SSD_HEADS // 4 else cb1
                xp = xdt[:, j * LANE:(j + 1) * LANE]
                scs = []
                for half in range(2):
                    h = 2 * j + half
                    seg = cum[:, h:h + 1] - cum_t[h:h + 1, :]
                    dec = jnp.exp(jnp.where(causal, seg, -jnp.inf))
                    scs.append((cbg * dec).astype(BF16))
                rhs = jnp.concatenate([jnp.where(lo_half, xp, 0.0), jnp.where(lo_half, 0.0, xp)], axis=0)
                pairs.append(_dot(jnp.concatenate(scs, axis=1), rhs.astype(BF16)))
            y = y + jnp.concatenate(pairs, axis=1)
            if fwd:
                yacc_ref[pl.ds(r0, q), :] = y
            else:
                yacc_ref[pl.ds(r0, q), :] += y
            new = _dot(bm.T.astype(BF16), (xdt * dend_e).astype(BF16))
            st_ref[...] = jnp.where(blockmask, st * cdec_e + new, 0.0)
            return carry

        lax.fori_loop(0, nc, body, 0)
        if want_state:
            for j in range(SSD_INNER // LANE):
                g = (j * LANE) // (SSD_INNER // SSD_GROUPS)
                t = st_ref[:, j * LANE:(j + 1) * LANE].T
                s_refs[0 if fwd else 1][0, j * LANE:(j + 1) * LANE, :] = t[:, g * SSD_STATE:(g + 1) * SSD_STATE]

    scan_pass(True)
    scan_pass(False)

    d_e = _dot_x3(jnp.broadcast_to(hp[4:5], (8, LANE)), expand)[0:1]
    nw = nw_ref[...]

    def out_body(c, carry):
        r0 = pl.multiple_of(c * q, q)
        y = yacc_ref[pl.ds(r0, q), :] + xs_ref[pl.ds(r0, q), :] * d_e
        y = y * _silu(z_ref[pl.ds(r0, q), :])
        ms = jnp.mean(y * y, axis=-1, keepdims=True)
        y_ref[pl.ds(r0, q), :] = (y * lax.rsqrt(ms + EPS) * nw).astype(y_ref.dtype)
        return carry

    lax.fori_loop(0, nc, out_body, 0)


def _ssd(z, xbc, dt, cw, cb, hp, nw, h0, *, seq):
    t = z.shape[0]
    n_seq = t // seq
    has_h0 = h0 is not None
    want_state = not has_h0
    gn = SSD_GROUPS * SSD_STATE
    tok = lambda w: pl.BlockSpec((seq, w), lambda b: (b, 0))
    full = lambda a: pl.BlockSpec(a.shape, lambda b: (0,) * a.ndim)
    in_specs = [tok(SSD_INNER), tok(SSD_CONV_DIM), tok(LANE), full(cw), full(cb), full(hp), full(nw)]
    args = [z, xbc, dt, cw, cb, hp, nw]
    if has_h0:
        in_specs += [pl.BlockSpec((1, gn, SSD_INNER), lambda b: (b, 0, 0))] * 2
        args += list(h0)
    out_shape = [jax.ShapeDtypeStruct((t, SSD_INNER), BF16)]
    out_specs = [tok(SSD_INNER)]
    if want_state:
        out_shape += [jax.ShapeDtypeStruct((n_seq, SSD_INNER, SSD_STATE), F32)] * 2
        out_specs += [pl.BlockSpec((1, SSD_INNER, SSD_STATE), lambda b: (b, 0, 0))] * 2

    return pl.pallas_call(
        functools.partial(_ssd_kernel, seq=seq, has_h0=has_h0, want_state=want_state),
        grid=(n_seq,),
        in_specs=in_specs,
        out_specs=out_specs,
        out_shape=out_shape,
        scratch_shapes=[pltpu.VMEM((seq, SSD_INNER), F32),
                        pltpu.VMEM((seq, 2 * gn), F32),
                        pltpu.VMEM((seq, SSD_INNER), F32),
                        pltpu.VMEM((gn, SSD_INNER), F32)],
        compiler_params=_cparams(1),
        name="ssd_lat" if has_h0 else "ssd_ctx",
    )(*args)


def _rope(x, cos, sin, first_half, dist):
    partner = jnp.where(first_half, pltpu.roll(x, LANE - dist, axis=1), pltpu.roll(x, dist, axis=1))
    return x * cos + partner * sin


def _softmax_pv(ss, vals):
    m = ss[0].max(axis=-1, keepdims=True)
    for s in ss[1:]:
        m = jnp.maximum(m, s.max(axis=-1, keepdims=True))
    ps = [jnp.exp2(s - m) for s in ss]
    den = ps[0].sum(axis=-1, keepdims=True)
    for p in ps[1:]:
        den = den + p.sum(axis=-1, keepdims=True)
    o = _dot(ps[0].astype(BF16), vals[0]())
    for p, v in zip(ps[1:], vals[1:]):
        o = o + _dot(p.astype(BF16), v())
    return o / den


def _attend_blocks(seq, units, emit, s_scr):
    nblk = seq // Q_BLOCK

    def scores(u, blk, static):
        q_ref, rows, keys, _ = units[u]
        r = blk * rows if static else pl.multiple_of(blk * rows, rows)
        qb = q_ref[pl.ds(r, rows), :]
        return [_dot_nt(qb, k()) for k in keys]

    if s_scr is None:
        for i in range(nblk):
            emit(i * Q_BLOCK, [_softmax_pv(scores(u, i, True), units[u][3]) for u in range(len(units))])
        return

    def put(slot, blk):
        for u in range(len(units)):
            off = 0
            for s in scores(u, blk, False):
                s_scr[slot][u][:, off:off + s.shape[1]] = s
                off += s.shape[1]

    def take(slot, blk):
        outs = []
        for u, (_, _, keys, vals) in enumerate(units):
            off = 0
            ss = []
            for k in keys:
                n = k().shape[0]
                ss.append(s_scr[slot][u][:, off:off + n])
                off += n
            outs.append(_softmax_pv(ss, vals))
        emit(pl.multiple_of(blk * Q_BLOCK, Q_BLOCK), outs)

    put(0, 0)

    def body(ii, carry):
        b0 = 2 * ii
        put(1, b0 + 1)
        take(0, b0)
        put(0, jnp.where(b0 + 2 >= nblk, 0, b0 + 2))
        take(1, b0 + 1)
        return carry

    lax.fori_loop(0, nblk // 2, body, 0)


def _mla_kernel(*refs, seq, lat):
    it = iter(refs)
    qa_ref, kv_ref, qan_ref, kvn_ref, wq_ref, wk_ref, wv_ref, qn_ref, kn_ref = [next(it) for _ in range(9)]
    if lat:
        cckv_ref, ckpe_ref, cos_ref, sin_ref = [next(it) for _ in range(4)]
    o_ref = next(it)
    if not lat:
        ckv_out, kpe_out = next(it), next(it)
    q_s, k_s, v_s, qh0_s, qh1_s, kh0_s, kh1_s = [next(it) for _ in range(7)]
    s_scr = None
    if lat:
        kc_s, vc_s, kch0_s, kch1_s = [next(it) for _ in range(4)]
        s_scr = [[next(it), next(it)], [next(it), next(it)]]

    qscale = MLA_QK ** -0.5 * LOG2E
    kv = kv_ref[...]
    kpe = kv[:, LANE:]
    ckv_raw = kv[:, :LANE]
    ckv = ckv_raw * lax.rsqrt(jnp.mean(ckv_raw * ckv_raw, axis=-1, keepdims=True) + EPS) * kvn_ref[...]
    qa = qa_ref[...]
    qa = qa * lax.rsqrt(jnp.mean(qa * qa, axis=-1, keepdims=True) + EPS) * qan_ref[...]
    if not lat:
        ckv_out[...] = ckv
        kpe_out[...] = kpe
    ckvb = ckv.astype(BF16)
    q_s[...] = _dot(qa.astype(BF16), wq_ref[...])
    k_s[...] = _dot(ckvb, wk_ref[...])
    v_s[...] = _dot(ckvb, wv_ref[...]).astype(BF16)
    if lat:
        ccb = cckv_ref[0].astype(BF16)
        kc_s[...] = _dot(ccb, wk_ref[...])
        vc_s[...] = _dot(ccb, wv_ref[...]).astype(BF16)
        ckpe = ckpe_ref[0]
        cos = cos_ref[...]
        sin = sin_ref[...]
        lane = _iota((seq, LANE), 1)
        first_half = (lane % (MLA_ROPE // 2)) < (MLA_ROPE // 4)
    qn = qn_ref[...]
    kn = kn_ref[...]
    lo_lanes = _iota((Q_BLOCK, LANE), 1) < MLA_V

    def head_norm(x, g):
        return x * lax.rsqrt(jnp.sum(x * x, axis=-1, keepdims=True) * (1.0 / MLA_QK) + EPS) * g

    for j in range(MLA_HEADS // 2):
        pair = slice(j * LANE, (j + 1) * LANE)
        for half, (qh_s, kh_s) in enumerate(((qh0_s, kh0_s), (qh1_s, kh1_s))):
            sl = slice((2 * j + half) * LANE, (2 * j + half + 1) * LANE)
            qh = head_norm(q_s[:, sl], qn)
            kh = head_norm(k_s[:, sl] + kpe, kn)
            if lat:
                qh = _rope(qh, cos, sin, first_half, MLA_ROPE // 4)
                kh = _rope(kh, cos, sin, first_half, MLA_ROPE // 4)
                (kch0_s, kch1_s)[half][...] = head_norm(kc_s[:, sl] + ckpe, kn).astype(BF16)
            qh_s[...] = (qh * qscale).astype(BF16)
            kh_s[...] = kh.astype(BF16)

        def emit(r0, outs, pair=pair):
            o_ref[pl.ds(r0, Q_BLOCK), pair] = jnp.where(lo_lanes, outs[0], outs[1]).astype(o_ref.dtype)

        units = []
        for half, (qh_s, kh_s) in enumerate(((qh0_s, kh0_s), (qh1_s, kh1_s))):
            keys = [lambda r=kh_s: r[...]]
            vals = [lambda pair=pair: v_s[:, pair]]
            if lat:
                keys = [lambda r=(kch0_s, kch1_s)[half]: r[...]] + keys
                vals = [lambda pair=pair: vc_s[:, pair]] + vals
            units.append((qh_s, Q_BLOCK, keys, vals))
        _attend_blocks(seq, units, emit, s_scr)


def _mla(qa, kv, qan, kvn, wq, wk, wv, qn, kn, cache, tables, *, seq):
    t = qa.shape[0]
    n_seq = t // seq
    lat = cache is not None
    hv = MLA_HEADS * MLA_V
    tok = lambda w: pl.BlockSpec((seq, w), lambda b: (b, 0))
    full = lambda a: pl.BlockSpec(a.shape, lambda b: (0,) * a.ndim)
    args = [qa, kv, qan, kvn, wq, wk, wv, qn, kn]
    in_specs = [tok(MLA_Q_LORA), tok(2 * LANE)] + [full(a) for a in args[2:]]
    past = 0
    if lat:
        past = cache[0].shape[1]
        in_specs += [pl.BlockSpec((1, past, LANE), lambda b: (b, 0, 0))] * 2 + [full(tables[0]), full(tables[1])]
        args += [cache[0], cache[1], tables[0], tables[1]]
    out_shape = [jax.ShapeDtypeStruct((t, hv), BF16)]
    out_specs = [tok(hv)]
    if not lat:
        out_shape += [jax.ShapeDtypeStruct((t, LANE), F32)] * 2
        out_specs += [pl.BlockSpec((seq, LANE), lambda b: (b, 0))] * 2
    scratch = [pltpu.VMEM((seq, MLA_HEADS * LANE), F32),
               pltpu.VMEM((seq, MLA_HEADS * LANE), F32),
               pltpu.VMEM((seq, hv), BF16)] + [pltpu.VMEM((seq, LANE), BF16)] * 4
    if lat:
        scratch += [pltpu.VMEM((past, MLA_HEADS * LANE), F32),
                    pltpu.VMEM((past, hv), BF16),
                    pltpu.VMEM((past, LANE), BF16),
                    pltpu.VMEM((past, LANE), BF16)]
        scratch += [pltpu.VMEM((Q_BLOCK, past + seq), F32)] * 4

    return pl.pallas_call(
        functools.partial(_mla_kernel, seq=seq, lat=lat),
        grid=(n_seq,),
        in_specs=in_specs,
        out_specs=out_specs,
        out_shape=out_shape,
        scratch_shapes=scratch,
        compiler_params=_cparams(1),
        name="mla_lat" if lat else "mla_ctx",
    )(*args)


def _gqa_kernel(*refs, seq, lat):
    it = iter(refs)
    q_ref, k_ref, v_ref, qn_ref, kn_ref = [next(it) for _ in range(5)]
    if lat:
        ck_ref, cv_ref, cos_ref, sin_ref = [next(it) for _ in range(4)]
    o_ref = next(it)
    if not lat:
        kn_out = next(it)
    qn_s, kn_s, vb_s, q2_s = [next(it) for _ in range(4)]
    s_scr = None
    if lat:
        ckb_s, cvb_s = next(it), next(it)
        s_scr = [[next(it)], [next(it)]]

    qscale = GQA_HEAD_DIM ** -0.5 * LOG2E
    hd = GQA_HEAD_DIM
    nq = GQA_HEADS * hd
    nk = GQA_KV_HEADS * LANE

    def half_norm(x, g, width):
        n_half = width // hd
        ind = (_iota((width, LANE), 0) // hd == _iota((width, LANE), 1)).astype(BF16)
        ind_t = (_iota((LANE, width), 1) // hd == _iota((LANE, width), 0)).astype(BF16)
        del n_half
        ss = _dot_x3(x * x, ind)
        r = lax.rsqrt(ss * (1.0 / hd) + EPS)
        return x * _dot_x3(r, ind_t) * g

    qn = half_norm(q_ref[...], qn_ref[...], nq)
    kn = half_norm(k_ref[...], kn_ref[...], nk)
    if not lat:
        kn_out[...] = kn
    if lat:
        cos = cos_ref[...]
        sin = sin_ref[...]
        lane = _iota((seq, LANE), 1)
        first_half = (lane % (hd // 2)) < (hd // 4)
        for j in range(nq // LANE):
            sl = slice(j * LANE, (j + 1) * LANE)
            qn_s[:, sl] = _rope(qn[:, sl], cos, sin, first_half, hd // 4)
        for j in range(nk // LANE):
            sl = slice(j * LANE, (j + 1) * LANE)
            kn_s[:, sl] = _rope(kn[:, sl], cos, sin, first_half, hd // 4).astype(BF16)
        ckb_s[...] = ck_ref[0].astype(BF16)
        cvb_s[...] = cv_ref[0].astype(BF16)
    else:
        qn_s[...] = qn
        kn_s[...] = kn.astype(BF16)
    vb_s[...] = v_ref[...].astype(BF16)
    lo_q = _iota((Q_BLOCK, LANE), 1) < hd

    for j in range(GQA_HEADS // 2):
        pair = slice(j * LANE, (j + 1) * LANE)
        g = (2 * j) // (GQA_HEADS // GQA_KV_HEADS)
        gs = slice(g * LANE, (g + 1) * LANE)
        for i in range(seq // Q_BLOCK):
            qp = qn_s[i * Q_BLOCK:(i + 1) * Q_BLOCK, pair] * qscale
            q2_s[2 * i * Q_BLOCK:(2 * i + 1) * Q_BLOCK, :] = jnp.where(lo_q, qp, 0.0).astype(BF16)
            q2_s[(2 * i + 1) * Q_BLOCK:(2 * i + 2) * Q_BLOCK, :] = jnp.where(lo_q, 0.0, qp).astype(BF16)

        def emit(r0, outs, pair=pair):
            o = outs[0]
            o_ref[pl.ds(r0, Q_BLOCK), pair] = jnp.where(lo_q, o[:Q_BLOCK], o[Q_BLOCK:]).astype(o_ref.dtype)

        keys = [lambda gs=gs: kn_s[:, gs]]
        vals = [lambda gs=gs: vb_s[:, gs]]
        if lat:
            keys = [lambda gs=gs: ckb_s[:, gs]] + keys
            vals = [lambda gs=gs: cvb_s[:, gs]] + vals
        _attend_blocks(seq, [(q2_s, 2 * Q_BLOCK, keys, vals)], emit, s_scr)


def _gqa(q, k, v, qn, kn, cache, tables, *, seq):
    t = q.shape[0]
    n_seq = t // seq
    lat = cache is not None
    nq = GQA_HEADS * GQA_HEAD_DIM
    nk = GQA_KV_HEADS * LANE
    tok = lambda w: pl.BlockSpec((seq, w), lambda b: (b, 0))
    full = lambda a: pl.BlockSpec(a.shape, lambda b: (0,) * a.ndim)
    args = [q, k, v, qn, kn]
    in_specs = [tok(nq), tok(nk), tok(nk), full(qn), full(kn)]
    past = 0
    if lat:
        past = cache[0].shape[1]
        in_specs += [pl.BlockSpec((1, past, nk), lambda b: (b, 0, 0))] * 2 + [full(tables[0]), full(tables[1])]
        args += [cache[0], cache[1], tables[0], tables[1]]
    out_shape = [jax.ShapeDtypeStruct((t, nq), BF16)]
    out_specs = [tok(nq)]
    if not lat:
        out_shape += [jax.ShapeDtypeStruct((t, nk), F32)]
        out_specs += [pl.BlockSpec((seq, nk), lambda b: (b, 0))]
    scratch = [pltpu.VMEM((seq, nq), F32),
               pltpu.VMEM((seq, nk), BF16),
               pltpu.VMEM((seq, nk), BF16),
               pltpu.VMEM((2 * seq, LANE), BF16)]
    if lat:
        scratch += [pltpu.VMEM((past, nk), BF16), pltpu.VMEM((past, nk), BF16)]
        scratch += [pltpu.VMEM((2 * Q_BLOCK, past + seq), F32)] * 2

    return pl.pallas_call(
        functools.partial(_gqa_kernel, seq=seq, lat=lat),
        grid=(n_seq,),
        in_specs=in_specs,
        out_specs=out_specs,
        out_shape=out_shape,
        scratch_shapes=scratch,
        compiler_params=_cparams(1),
        name="gqa_lat" if lat else "gqa_ctx",
    )(*args)


def _route(sel, sc):
    gs = []
    for g in range(N_GROUPS):
        r = sel[g * GROUP_SIZE:(g + 1) * GROUP_SIZE]
        best = None
        for i in range(GROUP_SIZE):
            for j in range(i + 1, GROUP_SIZE):
                s = r[i] + r[j]
                best = s if best is None else jnp.maximum(best, s)
        gs.append(best)
    cur = gs[0]
    grp = jnp.zeros_like(cur, dtype=jnp.int32)
    for g in range(1, N_GROUPS):
        upd = gs[g] > cur
        grp = jnp.where(upd, g, grp)
        cur = jnp.where(upd, gs[g], cur)

    def pick(rows, i):
        v = rows[i]
        for g in range(1, N_GROUPS):
            v = jnp.where(grp == g, rows[g * GROUP_SIZE + i], v)
        return v

    cand = [pick(sel, i) for i in range(GROUP_SIZE)]
    csc = [pick(sc, i) for i in range(GROUP_SIZE)]
    v1, i1, w1 = cand[0], jnp.zeros_like(grp), csc[0]
    for i in range(1, GROUP_SIZE):
        upd = cand[i] > v1
        v1 = jnp.where(upd, cand[i], v1)
        i1 = jnp.where(upd, i, i1)
        w1 = jnp.where(upd, csc[i], w1)
    v2 = i2 = w2 = None
    for i in range(GROUP_SIZE):
        ok = i1 != i
        if v2 is None:
            v2 = jnp.where(ok, cand[i], -jnp.inf)
            i2 = jnp.zeros_like(grp)
            w2 = csc[i]
        else:
            upd = ok & (cand[i] > v2)
            v2 = jnp.where(upd, cand[i], v2)
            i2 = jnp.where(upd, i, i2)
            w2 = jnp.where(upd, csc[i], w2)
    tot = w1 + w2
    local = [jnp.where(i1 == i, w1 / tot, 0.0) + jnp.where(i2 == i, w2 / tot, 0.0)
             for i in range(GROUP_SIZE)]
    onehot = [jnp.where(grp == g, 1.0, 0.0) for g in range(N_GROUPS)]
    return local, onehot


def _outproj_kernel(*refs, n_y):
    x_ref, mod_ref = refs[0], refs[1]
    y_refs = refs[2:2 + n_y]
    w_ref, g_ref, wr_ref, rb_ref, x1_ref, h2_ref, gate_ref, cnt_ref = refs[2 + n_y:]
    m = mod_ref[0]
    off = 0
    out = None
    for y_ref in y_refs:
        k = y_ref.shape[1]
        part = _dot(y_ref[...], w_ref[off:off + k, :])
        out = part if out is None else out + part
        off += k
    x1 = x_ref[...] + m[:, 2 * D:3 * D] * out
    x1_ref[...] = x1
    h2 = _norm_mod(x1, g_ref[...], m[:, 3 * D:4 * D], m[:, 4 * D:5 * D])
    h2_hi = h2.astype(BF16)
    h2_ref[...] = h2_hi
    h2_mid = (h2 - h2_hi.astype(F32)).astype(BF16)
    parts = _dot(h2_hi, wr_ref[...]) + _dot(h2_mid, wr_ref[...])
    logits = (parts + pltpu.roll(parts, LANE - N_EXPERTS, axis=1)
              + pltpu.roll(parts, LANE - 2 * N_EXPERTS, axis=1))
    sc_t = _sigmoid(logits.T[0:N_EXPERTS, :])
    sel_t = sc_t + rb_ref[...]
    local, onehot = _route([sel_t[e:e + 1] for e in range(N_EXPERTS)],
                           [sc_t[e:e + 1] for e in range(N_EXPERTS)])
    t = x1.shape[0]
    rows_t = (local + [jnp.zeros((GROUP_LANE - GROUP_SIZE, t), F32)] + onehot
              + [jnp.zeros((LANE - GROUP_LANE - N_GROUPS, t), F32)])
    routed = jnp.concatenate(rows_t, axis=0).T
    gate_ref[...] = routed
    cnt = jnp.sum(routed, axis=0, keepdims=True).astype(jnp.int32)
    cnt_ref[0] = jnp.broadcast_to(cnt, (8, LANE))


def _outproj(x, mod, ys, w, g, wr, rb):
    t = x.shape[0]
    full = lambda a: pl.BlockSpec(a.shape, lambda i: (0,) * a.ndim)
    return pl.pallas_call(
        functools.partial(_outproj_kernel, n_y=len(ys)),
        grid=(t // TOKEN_TILE,),
        in_specs=[pl.BlockSpec((TOKEN_TILE, D), lambda i: (i, 0)),
                  _mod_spec(mod, t, TOKEN_TILE)]
                 + [pl.BlockSpec((TOKEN_TILE, y.shape[1]), lambda i: (i, 0)) for y in ys]
                 + [full(w), full(g), full(wr), full(rb)],
        out_specs=[pl.BlockSpec((TOKEN_TILE, D), lambda i: (i, 0)),
                   pl.BlockSpec((TOKEN_TILE, D), lambda i: (i, 0)),
                   pl.BlockSpec((TOKEN_TILE, LANE), lambda i: (i, 0)),
                   pl.BlockSpec((1, 8, LANE), lambda i: (i, 0, 0))],
        out_shape=[jax.ShapeDtypeStruct((t, D), F32),
                   jax.ShapeDtypeStruct((t, D), BF16),
                   jax.ShapeDtypeStruct((t, LANE), F32),
                   jax.ShapeDtypeStruct((t // TOKEN_TILE, 8, LANE), jnp.int32)],
        compiler_params=_cparams(1),
        name="outproj_router",
    )(x, mod, *ys, w, g, wr, rb)


def _pad_to(n, shift):
    return ((n + (1 << shift) - 1) >> shift) << shift


SEG_SHIFT = SEG_ALIGN.bit_length() - 1
TILE_SHIFT = EXPERT_TILE.bit_length() - 1


def _start_segments(plan_ref, i, nt, make):
    for g in range(N_GROUPS):
        n = plan_ref[i * N_GROUPS + g]
        lo = plan_ref[(nt + i) * N_GROUPS + g]
        base = plan_ref[(2 * nt + i) * N_GROUPS + g]

        def body(c, carry, lo=lo, base=base):
            for cp in make(pl.multiple_of(lo + c * SEG_ALIGN, SEG_ALIGN),
                           pl.multiple_of(base + c * SEG_ALIGN, SEG_ALIGN)):
                cp.start()
            return carry

        lax.fori_loop(0, (n + SEG_ALIGN - 1) >> SEG_SHIFT, body, 0)


def _wait_segments(plan_ref, i, make):
    total = jnp.int32(0)
    for g in range(N_GROUPS):
        total = total + ((plan_ref[i * N_GROUPS + g] + SEG_ALIGN - 1) >> SEG_SHIFT)

    def body(c, carry):
        for cp in make(0, 0):
            cp.wait()
        return carry

    lax.fori_loop(0, total, body, 0)


def _dispatch_kernel(cnt_ref, h_ref, r_ref, xs_hbm, rs_hbm, dest_ref, plan_ref,
                     xs_l, rs_l, zx, zr, fill_s, sem, *, nt, n_rows):
    i = pl.program_id(0)
    t = TOKEN_TILE
    k_tiles = n_rows // EXPERT_TILE

    @pl.when(i == 0)
    def _():
        zx[...] = jnp.zeros_like(zx)
        zr[...] = jnp.zeros_like(zr)
        tots = []
        for g in range(N_GROUPS):
            tots.append(lax.fori_loop(
                0, nt, lambda tt, acc, g=g: acc + _pad_to(cnt_ref[tt * N_GROUPS + g], SEG_SHIFT), jnp.int32(0)))
        starts = []
        cur = jnp.int32(0)
        for g in range(N_GROUPS):
            starts.append(cur)
            fill_s[2 * g] = cur + tots[g]
            cur = cur + _pad_to(tots[g], TILE_SHIFT)
            fill_s[2 * g + 1] = cur
        fill_s[2 * N_GROUPS] = cur
        fill_s[2 * N_GROUPS + 1] = n_rows

        def per_tile(tt, run):
            lo = jnp.int32(0)
            new = []
            for g in range(N_GROUPS):
                n = cnt_ref[tt * N_GROUPS + g]
                plan_ref[tt * N_GROUPS + g] = n
                plan_ref[(nt + tt) * N_GROUPS + g] = lo
                plan_ref[(2 * nt + tt) * N_GROUPS + g] = run[g]
                seg = _pad_to(n, SEG_SHIFT)
                lo = lo + seg
                new.append(run[g] + seg)
            return tuple(new)

        lax.fori_loop(0, nt, per_tile, tuple(starts))
        for k in range(k_tiles):
            row = k * EXPERT_TILE
            grp = 0
            for g in range(1, N_GROUPS):
                grp = grp + (row >= starts[g]).astype(jnp.int32)
            plan_ref[3 * nt * N_GROUPS + k] = grp
        plan_ref[3 * nt * N_GROUPS + k_tiles] = cur >> TILE_SHIFT

    routed = r_ref[...]
    lane = _iota((t, LANE), 1)
    is_grp = (lane >= GROUP_LANE) & (lane < GROUP_LANE + N_GROUPS)
    onehot = jnp.where(is_grp, routed, 0.0)
    tril = (_iota((t, t), 0) >= _iota((t, t), 1)).astype(BF16)
    rank = _dot(tril, onehot.astype(BF16))
    seg_lo = jnp.zeros((1, LANE), F32)
    lane1 = _iota((1, LANE), 1)
    for g in range(N_GROUPS):
        lo = plan_ref[(nt + i) * N_GROUPS + g]
        seg_lo = jnp.where(lane1 == GROUP_LANE + g, (lo - 1).astype(F32), seg_lo)
    val = jnp.where(onehot > 0.0, rank + seg_lo, 0.0)
    dest = _dot_x2(val, jnp.ones((LANE, LANE), BF16))
    dest_ref[...] = dest
    dest_row = dest.T[0:1, :].astype(jnp.int32)
    perm = jnp.where(_iota((DISPATCH_ROWS, t), 0) == dest_row, 1.0, 0.0).astype(BF16)
    slot = i % 2
    xs_l[slot] = _dot(perm, h_ref[...]).astype(BF16)
    rs_l[slot] = _dot_3x(perm, routed)

    def to_buffer(slot):
        def make(lo, base):
            return (pltpu.make_async_copy(xs_l.at[slot, pl.ds(lo, SEG_ALIGN)],
                                          xs_hbm.at[pl.ds(base, SEG_ALIGN)], sem.at[slot, 0]),
                    pltpu.make_async_copy(rs_l.at[slot, pl.ds(lo, SEG_ALIGN)],
                                          rs_hbm.at[pl.ds(base, SEG_ALIGN)], sem.at[slot, 1]))
        return make

    _start_segments(plan_ref, i, nt, to_buffer(slot))

    @pl.when(i > 0)
    def _():
        _wait_segments(plan_ref, i - 1, to_buffer(1 - slot))

    @pl.when(i == nt - 1)
    def _():
        _wait_segments(plan_ref, i, to_buffer(slot))
        def zero(base):
            return (pltpu.make_async_copy(zx, xs_hbm.at[pl.ds(base, SEG_ALIGN)], sem.at[0, 0]),
                    pltpu.make_async_copy(zr, rs_hbm.at[pl.ds(base, SEG_ALIGN)], sem.at[0, 1]))

        n_fill = jnp.int32(0)
        for r in range(N_GROUPS + 1):
            lo = fill_s[2 * r]
            chunks = (fill_s[2 * r + 1] - lo) >> SEG_SHIFT

            def body(c, carry, lo=lo):
                for cp in zero(pl.multiple_of(lo + c * SEG_ALIGN, SEG_ALIGN)):
                    cp.start()
                return carry

            lax.fori_loop(0, chunks, body, 0)
            n_fill = n_fill + chunks

        def wait_fill(c, carry):
            for cp in zero(0):
                cp.wait()
            return carry

        lax.fori_loop(0, n_fill, wait_fill, 0)


def _moe_rows(n_tokens):
    nt = n_tokens // TOKEN_TILE
    worst = n_tokens + nt * N_GROUPS * (SEG_ALIGN - 1) + N_GROUPS * (EXPERT_TILE - SEG_ALIGN)
    return _pad_to(worst, TILE_SHIFT)


def _dispatch(counts, h2, routed):
    t = h2.shape[0]
    nt = t // TOKEN_TILE
    n_rows = _moe_rows(t)
    plan_len = 3 * nt * N_GROUPS + n_rows // EXPERT_TILE + 1
    grid_spec = pltpu.PrefetchScalarGridSpec(
        num_scalar_prefetch=1,
        grid=(nt,),
        in_specs=[pl.BlockSpec((TOKEN_TILE, D), lambda i, c: (i, 0)),
                  pl.BlockSpec((TOKEN_TILE, LANE), lambda i, c: (i, 0))],
        out_specs=[pl.BlockSpec(memory_space=pl.ANY),
                   pl.BlockSpec(memory_space=pl.ANY),
                   pl.BlockSpec((TOKEN_TILE, LANE), lambda i, c: (i, 0)),
                   pl.BlockSpec(memory_space=pltpu.SMEM)],
        scratch_shapes=[pltpu.VMEM((2, DISPATCH_ROWS, D), BF16),
                        pltpu.VMEM((2, DISPATCH_ROWS, LANE), F32),
                        pltpu.VMEM((SEG_ALIGN, D), BF16),
                        pltpu.VMEM((SEG_ALIGN, LANE), F32),
                        pltpu.SMEM((2 * N_GROUPS + 2,), jnp.int32),
                        pltpu.SemaphoreType.DMA((2, 2))])
    return pl.pallas_call(
        functools.partial(_dispatch_kernel, nt=nt, n_rows=n_rows),
        grid_spec=grid_spec,
        out_shape=[jax.ShapeDtypeStruct((n_rows, D), BF16),
                   jax.ShapeDtypeStruct((n_rows, LANE), F32),
                   jax.ShapeDtypeStruct((t, LANE), F32),
                   jax.ShapeDtypeStruct((plan_len,), jnp.int32)],
        compiler_params=_cparams(1),
        name="moe_dispatch",
    )(counts, h2, routed)


def _experts_kernel(plan_ref, xs_ref, rs_ref, wg_ref, wu_ref, wd_ref, y_ref, *, used_at):
    k = pl.program_id(0)
    used = plan_ref[used_at]

    @pl.when(k < used)
    def _():
        rows = xs_ref[...]
        gates = rs_ref[...]
        acc = None
        for e in range(GROUP_SIZE):
            a = _dot(rows, wg_ref[e])
            u = _dot(rows, wu_ref[e])
            act = (_silu(a) * u * gates[:, e:e + 1]).astype(BF16)
            part = _dot(act, wd_ref[e])
            acc = part if acc is None else acc + part
        y_ref[...] = acc.astype(y_ref.dtype)

    @pl.when(k >= used)
    def _():
        y_ref[...] = jnp.zeros_like(y_ref)


def _experts(plan, xs, rs, wg, wu, wd, layer, nt):
    n_rows = xs.shape[0]
    k_tiles = n_rows // EXPERT_TILE
    at = 3 * nt * N_GROUPS
    wspec = lambda a, b: pl.BlockSpec((None, GROUP_SIZE, a, b), lambda k, p: (layer, p[at + k], 0, 0))
    grid_spec = pltpu.PrefetchScalarGridSpec(
        num_scalar_prefetch=1,
        grid=(k_tiles,),
        in_specs=[pl.BlockSpec((EXPERT_TILE, D), lambda k, p: (k, 0)),
                  pl.BlockSpec((EXPERT_TILE, LANE), lambda k, p: (k, 0)),
                  wspec(D, D_EXPERT), wspec(D, D_EXPERT), wspec(D_EXPERT, D)],
        out_specs=pl.BlockSpec((EXPERT_TILE, D), lambda k, p: (k, 0)))
    return pl.pallas_call(
        functools.partial(_experts_kernel, used_at=at + k_tiles),
        grid_spec=grid_spec,
        out_shape=jax.ShapeDtypeStruct((n_rows, D), BF16),
        compiler_params=_cparams(1),
        name="moe_experts",
    )(plan, xs, rs, wg, wu, wd)


def _combine_kernel(plan_ref, x_ref, mod_ref, dest_ref, ys_hbm, o_ref, y_l, sem, *, nt):
    i = pl.program_id(0)

    slot = i % 2

    def from_buffer(slot):
        def make(lo, base):
            return (pltpu.make_async_copy(ys_hbm.at[pl.ds(base, SEG_ALIGN)],
                                          y_l.at[slot, pl.ds(lo, SEG_ALIGN)], sem.at[slot]),)
        return make

    @pl.when(i == 0)
    def _():
        y_l[...] = jnp.zeros_like(y_l)
        _start_segments(plan_ref, i, nt, from_buffer(slot))

    @pl.when(i + 1 < nt)
    def _():
        _start_segments(plan_ref, i + 1, nt, from_buffer(1 - slot))

    _wait_segments(plan_ref, i, from_buffer(slot))
    dest = dest_ref[...].astype(jnp.int32)
    lane = _iota(dest.shape, 1)
    back = jnp.concatenate(
        [jnp.where(dest == lane + j * LANE, 1.0, 0.0).astype(BF16) for j in range(DISPATCH_ROWS // LANE)], axis=1)
    y = _dot(back, y_l[slot])
    o_ref[...] = x_ref[...] + mod_ref[0][:, 5 * D:6 * D] * y


def _combine(plan, x1, mod, dest, ys):
    t = x1.shape[0]
    nt = t // TOKEN_TILE
    per_row = t // mod.shape[0]
    grid_spec = pltpu.PrefetchScalarGridSpec(
        num_scalar_prefetch=1,
        grid=(nt,),
        in_specs=[pl.BlockSpec((TOKEN_TILE, D), lambda i, p: (i, 0)),
                  pl.BlockSpec((1, 1, 6 * D), lambda i, p: ((i * TOKEN_TILE) // per_row, 0, 0)),
                  pl.BlockSpec((TOKEN_TILE, LANE), lambda i, p: (i, 0)),
                  pl.BlockSpec(memory_space=pl.ANY)],
        out_specs=pl.BlockSpec((TOKEN_TILE, D), lambda i, p: (i, 0)),
        scratch_shapes=[pltpu.VMEM((2, DISPATCH_ROWS, D), BF16),
                        pltpu.SemaphoreType.DMA((2,))])
    return pl.pallas_call(
        functools.partial(_combine_kernel, nt=nt),
        grid_spec=grid_spec,
        out_shape=jax.ShapeDtypeStruct((t, D), F32),
        compiler_params=_cparams(1),
        name="moe_combine",
    )(plan, x1, mod, dest, ys)


def _moe(x1, mod, h2, routed, cnt, wg, wu, wd, layer):
    nt = x1.shape[0] // TOKEN_TILE
    counts = cnt[:, 0, GROUP_LANE:GROUP_LANE + N_GROUPS].reshape(nt * N_GROUPS)
    xs, rs, dest, plan = _dispatch(counts, h2, routed)
    ys = _experts(plan, xs, rs, wg, wu, wd, layer, nt)
    return _combine(plan, x1, mod, dest, ys)


def _rope_angles(pos, half):
    inv = jnp.power(ROPE_THETA, -jnp.arange(half, dtype=F32) / half)
    ang = pos.astype(F32)[:, None] * inv[None, :]
    return jnp.cos(ang), jnp.sin(ang)


def _rope_tables(seq, rot, lead, lane_tiles):
    rows = seq // GRID_W
    row = jnp.repeat(jnp.arange(rows), GRID_W)
    col = jnp.tile(jnp.arange(GRID_W), rows)
    cr, sr = _rope_angles(row, rot // 4)
    cc, sc = _rope_angles(col, rot // 4)
    cos = jnp.concatenate([cr, cr, cc, cc], axis=1)
    sin = jnp.concatenate([-sr, sr, -sc, sc], axis=1)
    width = LANE // lane_tiles
    pad = width - lead - rot
    cos = jnp.concatenate([jnp.ones((seq, lead), F32), cos, jnp.ones((seq, pad), F32)], axis=1)
    sin = jnp.concatenate([jnp.zeros((seq, lead), F32), sin, jnp.zeros((seq, pad), F32)], axis=1)
    return jnp.tile(cos, (1, lane_tiles)), jnp.tile(sin, (1, lane_tiles))


def _pad_lanes(a, width):
    return jnp.pad(a, [(0, 0)] * (a.ndim - 1) + [(0, width - a.shape[-1])])


def _even_weights(w_in, w_q_b, w_kv_b):
    z = w_in[:, 0:1024]
    xbc = w_in[:, 1024:2304]
    dt = w_in[:, 2304:2320]
    qa = w_in[:, 2320:2576]
    ckv = w_in[:, 2576:2704]
    kpe = w_in[:, 2704:2736]
    zeros = lambda n: jnp.zeros((D, n), w_in.dtype)
    w = jnp.concatenate([z, xbc, qa, ckv, zeros(MLA_NOPE), kpe, zeros(LANE - MLA_QK), dt,
                         zeros(LANE - SSD_HEADS)], axis=1).astype(BF16)
    wq = _pad_lanes(w_q_b.reshape(MLA_Q_LORA, MLA_HEADS, MLA_QK), LANE).reshape(MLA_Q_LORA, MLA_HEADS * LANE)
    kvb = w_kv_b.reshape(MLA_KV_LORA, MLA_HEADS, MLA_NOPE + MLA_V)
    wk = _pad_lanes(kvb[:, :, :MLA_NOPE], LANE).reshape(MLA_KV_LORA, MLA_HEADS * LANE)
    wv = kvb[:, :, MLA_NOPE:].reshape(MLA_KV_LORA, MLA_HEADS * MLA_V)
    return w, wq.astype(BF16), wk.astype(BF16), wv.astype(BF16)


def _dup_heads(a):
    s = a.shape[:-1]
    a = a.reshape(s + (GQA_KV_HEADS, 1, GQA_HEAD_DIM))
    return jnp.broadcast_to(a, s + (GQA_KV_HEADS, 2, GQA_HEAD_DIM)).reshape(s + (GQA_KV_HEADS * LANE,))


def _undup_heads(a):
    s = a.shape[:-1]
    return a.reshape(s + (GQA_KV_HEADS, 2, GQA_HEAD_DIM))[..., 0, :]


def _state_to_kernel(h0):
    b = h0.shape[0]
    t = jnp.transpose(h0, (0, 3, 1, 2)).reshape(b, SSD_STATE, SSD_INNER)
    half = SSD_INNER // SSD_GROUPS
    col = jnp.arange(SSD_INNER) // half
    parts = [jnp.where(col == g, t, 0.0) for g in range(SSD_GROUPS)]
    return jnp.concatenate(parts, axis=1)


def _state_from_kernel(st):
    return st.reshape(st.shape[0], SSD_HEADS, SSD_HEADDIM, SSD_STATE)


def kernel(x_prompt, x_sample, c, state_ssd_fwd, state_ssd_bwd, cache_mla_ckv, cache_mla_kpe, cache_gqa_k, cache_gqa_v, c_ctx, g_mix, g_ffn, w_ada, b_ada, w_router, router_bias, w_exp_gate, w_exp_up, w_exp_down, w_in_even, ssd_conv_w, ssd_conv_b, ssd_a_log_fwd, ssd_a_log_bwd, ssd_dt_bias_fwd, ssd_dt_bias_bwd, ssd_d, ssd_norm, mla_q_a_norm, mla_w_q_b, mla_kv_a_norm, mla_w_kv_b, mla_q_norm, mla_k_norm, w_out_even, w_in_odd, gqa_q_norm, gqa_k_norm, w_out_odd):
    nb, ls, _ = x_prompt.shape
    db, dl, _ = x_sample.shape
    depth = w_ada.shape[0]
    n_ctx = nb * ls

    xs = [x_prompt.reshape(n_ctx, D), x_sample.reshape(db * dl, D)]
    seqs = (ls, dl)
    rows = 16
    cond = jnp.concatenate([c_ctx[None, :], c, jnp.zeros((rows - 1 - db, D), F32)], axis=0)
    mod_all = _ada_mod(cond, w_ada, b_ada)

    wr = _pad_lanes(jnp.concatenate(_split3(w_router), axis=1), LANE)
    rb = router_bias.reshape(N_EXPERTS, 1)
    wg_all = w_exp_gate.astype(BF16)
    wu_all = w_exp_up.astype(BF16)
    wd_all = w_exp_down.astype(BF16)

    outs = {}
    for i in range(depth):
        j = i // 2
        mods = [mod_all[i, 0:1].reshape(1, 1, 6 * D), mod_all[i, 1:1 + db].reshape(db, 1, 6 * D)]
        g1 = g_mix[i].reshape(1, D)
        g2 = g_ffn[i].reshape(1, D)
        if i % 2 == 0:
            w, wq, wk, wv = _even_weights(w_in_even[j], mla_w_q_b[j], mla_w_kv_b[j])
            cw = jnp.pad(ssd_conv_w[j], ((0, 5), (0, 0)))
            cb = ssd_conv_b[j].reshape(1, SSD_CONV_DIM)
            hp = _pad_lanes(jnp.stack([ssd_a_log_fwd[j], ssd_a_log_bwd[j], ssd_dt_bias_fwd[j],
                                       ssd_dt_bias_bwd[j], ssd_d[j]]), LANE)
            hp = jnp.pad(hp, ((0, 3), (0, 0)))
            nw = ssd_norm[j].reshape(1, SSD_INNER)
            qan = mla_q_a_norm[j].reshape(1, MLA_Q_LORA)
            kvn = mla_kv_a_norm[j].reshape(1, MLA_KV_LORA)
            qn = _pad_lanes(mla_q_norm[j].reshape(1, MLA_QK), LANE)
            kn = _pad_lanes(mla_k_norm[j].reshape(1, MLA_QK), LANE)
            h0 = (_state_to_kernel(state_ssd_fwd[:, j]), _state_to_kernel(state_ssd_bwd[:, j]))
            ckpe = jnp.pad(cache_mla_kpe[:, j], ((0, 0), (0, 0), (MLA_NOPE, LANE - MLA_QK)))
            tables = _rope_tables(dl, MLA_ROPE, MLA_NOPE, 1)
            w_out = w_out_even[j].astype(BF16)
            ys = []
            for s in range(2):
                z, xbc, qa, kv, dt = _inproj(xs[s], mods[s], g1, w,
                                             (SSD_INNER, SSD_CONV_DIM, MLA_Q_LORA, 2 * LANE, LANE))
                if s == 0:
                    y, sf, sb = _ssd(z, xbc, dt, cw, cb, hp, nw, None, seq=seqs[s])
                    o, ckv_new, kpe_new = _mla(qa, kv, qan, kvn, wq, wk, wv, qn, kn, None, None, seq=seqs[s])
                    outs.setdefault("ssd_f", []).append(_state_from_kernel(sf))
                    outs.setdefault("ssd_b", []).append(_state_from_kernel(sb))
                    outs.setdefault("ckv", []).append(ckv_new.reshape(nb, ls, MLA_KV_LORA))
                    outs.setdefault("kpe", []).append(kpe_new[:, MLA_NOPE:MLA_QK].reshape(nb, ls, MLA_ROPE))
                else:
                    y, = _ssd(z, xbc, dt, cw, cb, hp, nw, h0, seq=seqs[s])
                    o, = _mla(qa, kv, qan, kvn, wq, wk, wv, qn, kn, (cache_mla_ckv[:, j], ckpe), tables,
                              seq=seqs[s])
                ys.append((y, o))
        else:
            wi = w_in_odd[j]
            nq = GQA_HEADS * GQA_HEAD_DIM
            nkv = GQA_KV_HEADS * GQA_HEAD_DIM
            nk = GQA_KV_HEADS * LANE
            w = jnp.concatenate([wi[:, :nq], _dup_heads(wi[:, nq:nq + nkv]), _dup_heads(wi[:, nq + nkv:])],
                                axis=1).astype(BF16)
            qn = jnp.tile(gqa_q_norm[j].reshape(1, GQA_HEAD_DIM), (1, nq // GQA_HEAD_DIM))
            kn = jnp.tile(gqa_k_norm[j].reshape(1, GQA_HEAD_DIM), (1, nk // GQA_HEAD_DIM))
            cache = (_dup_heads(cache_gqa_k[:, j].reshape(db, -1, nkv)),
                     _dup_heads(cache_gqa_v[:, j].reshape(db, -1, nkv)))
            tables = _rope_tables(dl, GQA_HEAD_DIM, 0, 2)
            w_out = w_out_odd[j].astype(BF16)
            ys = []
            for s in range(2):
                q, k, v = _inproj(xs[s], mods[s], g1, w, (nq, nk, nk))
                if s == 0:
                    o, k_new = _gqa(q, k, v, qn, kn, None, None, seq=seqs[s])
                    outs.setdefault("gk", []).append(
                        _undup_heads(k_new).reshape(nb, ls, GQA_KV_HEADS, GQA_HEAD_DIM))
                    outs.setdefault("gv", []).append(
                        _undup_heads(v).reshape(nb, ls, GQA_KV_HEADS, GQA_HEAD_DIM))
                else:
                    o, = _gqa(q, k, v, qn, kn, cache, tables, seq=seqs[s])
                ys.append((o,))
        for s in range(2):
            x1, h2, routed, cnt = _outproj(xs[s], mods[s], ys[s], w_out, g2, wr, rb)
            xs[s] = _moe(x1, mods[s], h2, routed, cnt, wg_all, wu_all, wd_all, i)

    stack = lambda key: jnp.stack(outs[key], axis=1)
    return (xs[0].reshape(nb, ls, D), xs[1].reshape(db, dl, D),
            stack("ssd_f"), stack("ssd_b"), stack("ckv"), stack("kpe"), stack("gk"), stack("gv"))
```

```python
import functools
import math

import jax
import jax.numpy as jnp
from jax import lax
from jax.experimental import pallas as pl
from jax.experimental.pallas import tpu as pltpu

F32 = jnp.float32
BF16 = jnp.bfloat16

D = 1024
EPS = 1e-6
GRID_W = 64
ROPE_THETA = 10000.0
SSD_HEADS = 16
SSD_HEADDIM = 64
SSD_INNER = 1024
SSD_STATE = 64
SSD_GROUPS = 2
SSD_CONV_DIM = 1280
SSD_CHUNK = 128
MLA_HEADS = 8
MLA_Q_LORA = 256
MLA_KV_LORA = 128
MLA_NOPE = 64
MLA_ROPE = 32
MLA_V = 64
MLA_QK = 96
GQA_HEADS = 16
GQA_KV_HEADS = 4
GQA_HEAD_DIM = 64
N_EXPERTS = 16
N_GROUPS = 4
GROUP_SIZE = 4
D_EXPERT = 256

LANE = 128
Q_BLOCK = 128
LOG2E = 1.4426950408889634
TOKEN_TILE = 512
EXPERT_TILE = 512
SEG_ALIGN = 16
GROUP_LANE = 16
DISPATCH_ROWS = TOKEN_TILE + LANE
ADA_TILE = 1536
VMEM_LIMIT = 56 * 1024 * 1024


def _cparams(n_axes):
    return pltpu.CompilerParams(dimension_semantics=("arbitrary",) * n_axes,
                                vmem_limit_bytes=VMEM_LIMIT)


def _dot(a, b):
    return jnp.dot(a, b, preferred_element_type=F32)


def _dot_nt(a, b):
    return lax.dot_general(a, b, (((1,), (1,)), ((), ())), preferred_element_type=F32)


def _split3(x):
    hi = x.astype(BF16)
    r1 = x - hi.astype(F32)
    mid = r1.astype(BF16)
    lo = (r1 - mid.astype(F32)).astype(BF16)
    return hi, mid, lo


def _dot_x3(x, e):
    hi, mid, lo = _split3(x)
    return _dot(hi, e) + _dot(mid, e) + _dot(lo, e)


def _dot_x2(x, e):
    hi = x.astype(BF16)
    mid = (x - hi.astype(F32)).astype(BF16)
    return _dot(hi, e) + _dot(mid, e)


def _dot_3x(e, x):
    hi, mid, lo = _split3(x)
    return _dot(e, hi) + _dot(e, mid) + _dot(e, lo)


def _dot_f32(a, b):
    a0, a1, a2 = _split3(a)
    b0, b1, b2 = _split3(b)
    return (_dot(a0, b0) + (_dot(a0, b1) + _dot(a1, b0))
            + (_dot(a1, b1) + _dot(a0, b2) + _dot(a2, b0)))


def _sigmoid(x):
    return 1.0 / (1.0 + jnp.exp(-x))


def _silu(x):
    return x * _sigmoid(x)


def _iota(shape, axis):
    return lax.broadcasted_iota(jnp.int32, shape, axis)


def _ada_kernel(c_ref, w_ref, b_ref, o_ref):
    a = _silu(c_ref[...]).astype(BF16)
    o_ref[0] = _dot(a, w_ref[0].astype(BF16)) + b_ref[0]


def _ada_mod(cond, w_ada, b_ada):
    depth, _, n = w_ada.shape
    rows = cond.shape[0]
    return pl.pallas_call(
        _ada_kernel,
        grid=(depth, n // ADA_TILE),
        in_specs=[pl.BlockSpec((rows, D), lambda l, j: (0, 0)),
                  pl.BlockSpec((1, D, ADA_TILE), lambda l, j: (l, 0, j)),
                  pl.BlockSpec((1, 1, ADA_TILE), lambda l, j: (l, 0, j))],
        out_specs=pl.BlockSpec((1, rows, ADA_TILE), lambda l, j: (l, 0, j)),
        out_shape=jax.ShapeDtypeStruct((depth, rows, n), F32),
        compiler_params=_cparams(2),
        name="ada_mod",
    )(cond, w_ada, b_ada.reshape(depth, 1, n))


def _mod_spec(mod, n_tokens, tile, n_axes=1):
    per_row = n_tokens // mod.shape[0]
    if n_axes == 1:
        return pl.BlockSpec((1, 1, 6 * D), lambda i: ((i * tile) // per_row, 0, 0))
    return pl.BlockSpec((1, 1, 6 * D), lambda i, e: ((i * tile) // per_row, 0, 0))


def _norm_mod(x, g, shift, scale):
    ms = jnp.mean(x * x, axis=-1, keepdims=True)
    y = x * lax.rsqrt(ms + EPS) * g
    return y * (1.0 + scale) + shift


def _inproj_kernel(x_ref, mod_ref, g_ref, w_ref, *o_refs, widths):
    m = mod_ref[0]
    h = _norm_mod(x_ref[...], g_ref[...], m[:, 0:D], m[:, D:2 * D]).astype(BF16)
    off = 0
    for o_ref, wd in zip(o_refs, widths):
        o_ref[...] = _dot(h, w_ref[:, off:off + wd]).astype(o_ref.dtype)
        off += wd


def _inproj(x, mod, g, w, widths):
    t = x.shape[0]
    n = w.shape[1]
    return pl.pallas_call(
        functools.partial(_inproj_kernel, widths=widths),
        grid=(t // TOKEN_TILE,),
        in_specs=[pl.BlockSpec((TOKEN_TILE, D), lambda i: (i, 0)),
                  _mod_spec(mod, t, TOKEN_TILE),
                  pl.BlockSpec((1, D), lambda i: (0, 0)),
                  pl.BlockSpec((D, n), lambda i: (0, 0))],
        out_specs=[pl.BlockSpec((TOKEN_TILE, wd), lambda i: (i, 0)) for wd in widths],
        out_shape=[jax.ShapeDtypeStruct((t, wd), F32) for wd in widths],
        compiler_params=_cparams(1),
        name="inproj",
    )(x, mod, g, w)


def _ssd_kernel(*refs, seq, has_h0, want_state):
    it = iter(refs)
    z_ref, xbc_ref, dt_ref, cw_ref, cb_ref, hp_ref, nw_ref = [next(it) for _ in range(7)]
    h0_refs = (next(it), next(it)) if has_h0 else None
    y_ref = next(it)
    s_refs = (next(it), next(it)) if want_state else None
    xs_ref, bc_ref, yacc_ref, st_ref = [next(it) for _ in range(4)]

    q = SSD_CHUNK
    nc = seq // q
    unrolled = nc <= 2
    gn = SSD_GROUPS * SSD_STATE

    cw = cw_ref[...]
    cb = cb_ref[...]
    rid = _iota((q, SSD_CONV_DIM), 0)

    def rows(c):
        return pl.ds(c * q, q) if isinstance(c, int) else pl.ds(pl.multiple_of(c * q, q), q)

    def conv_chunk(c):
        xc = xbc_ref[rows(c), :]
        if isinstance(c, int):
            prev = xbc_ref[c * q - 1:c * q, :] if c > 0 else jnp.zeros((1, SSD_CONV_DIM), F32)
            nxt = xbc_ref[(c + 1) * q:(c + 1) * q + 1, :] if c < nc - 1 else jnp.zeros((1, SSD_CONV_DIM), F32)
        else:
            prev = xbc_ref[pl.ds(jnp.maximum(c * q - 1, 0), 1), :]
            nxt = xbc_ref[pl.ds(jnp.minimum(c * q + q, seq - 1), 1), :]
            prev = jnp.where(c > 0, prev, 0.0)
            nxt = jnp.where(c < nc - 1, nxt, 0.0)
        xp = jnp.where(rid == 0, prev, pltpu.roll(xc, 1, axis=0))
        xn = jnp.where(rid == q - 1, nxt, pltpu.roll(xc, q - 1, axis=0))
        v = _silu(xp * cw[0:1] + xc * cw[1:2] + xn * cw[2:3] + cb)
        xs_ref[rows(c), :] = v[:, :SSD_INNER]
        bc_ref[rows(c), :] = v[:, SSD_INNER:]

    hp = hp_ref[...]
    expand = (_iota((LANE, SSD_INNER), 1) // SSD_HEADDIM == _iota((LANE, SSD_INNER), 0)).astype(BF16)
    blockmask = (_iota((gn, SSD_INNER), 0) // SSD_STATE
                 == _iota((gn, SSD_INNER), 1) // (SSD_INNER // SSD_GROUPS))
    ri = _iota((q, q), 0)
    ci = _iota((q, q), 1)
    lane = _iota((q, LANE), 1)
    lo_half = lane < SSD_STATE
    d_e = _dot_x3(jnp.broadcast_to(hp[4:5], (8, LANE)), expand)[0:1]
    nw = nw_ref[...]

    def scan_chunk(c, fwd, a, bias, causal, tri):
        xs = xs_ref[rows(c), :]
        bc = bc_ref[rows(c), :]
        bm = bc[:, :gn]
        cm = bc[:, gn:]
        raw = dt_ref[rows(c), :] + bias
        dt = jnp.maximum(raw, 0.0) + jnp.log1p(jnp.exp(-jnp.abs(raw)))
        cum = _dot_3x(tri, dt * a)
        cum_t = cum.T
        dt_t = dt.T
        last = cum[q - 1:q] if fwd else cum[0:1]
        ecum = jnp.exp(cum)
        ex = _dot_x2(jnp.concatenate([ecum, dt * jnp.exp(last - cum)], axis=0), expand)
        ecum_e, w_e = ex[0:q], ex[q:2 * q]
        cdec_e = ecum_e[q - 1:q] if fwd else ecum_e[0:1]
        st = st_ref[...]
        cmb = cm.astype(BF16)
        bmb = bm.astype(BF16)
        y = _dot(cmb, st.astype(BF16)) * ecum_e
        cb0 = _dot_nt(jnp.where(lo_half, cm, 0.0).astype(BF16), bmb)
        cb1 = _dot_nt(jnp.where(lo_half, 0.0, cm).astype(BF16), bmb)
        pairs = []
        for j in range(SSD_HEADS // 2):
            cbg = cb0 if j < SSD_HEADS // 4 else cb1
            xp = xs[:, j * LANE:(j + 1) * LANE]
            scs = []
            for half in range(2):
                h = 2 * j + half
                seg = cum[:, h:h + 1] - cum_t[h:h + 1, :]
                dec = jnp.exp(jnp.where(causal, seg, -jnp.inf))
                scs.append((cbg * dec * dt_t[h:h + 1, :]).astype(BF16))
            rhs = jnp.concatenate([jnp.where(lo_half, xp, 0.0), jnp.where(lo_half, 0.0, xp)], axis=0)
            pairs.append(_dot(jnp.concatenate(scs, axis=1), rhs.astype(BF16)))
        y = y + jnp.concatenate(pairs, axis=1)
        new = _dot(bm.T.astype(BF16), (xs * w_e).astype(BF16))
        st_ref[...] = jnp.where(blockmask, st * cdec_e + new, 0.0)
        if fwd:
            yacc_ref[rows(c), :] = y
        else:
            y = y + yacc_ref[rows(c), :] + xs * d_e
            y = y * _silu(z_ref[rows(c), :])
            ms = jnp.mean(y * y, axis=-1, keepdims=True)
            y_ref[rows(c), :] = (y * lax.rsqrt(ms + EPS) * nw).astype(y_ref.dtype)

    def scan_pass(fwd):
        a = -jnp.exp(hp[0:1] if fwd else hp[1:2])
        bias = hp[2:3] if fwd else hp[3:4]
        causal = (ri >= ci) if fwd else (ri <= ci)
        tri = causal.astype(BF16)
        if has_h0:
            st_ref[...] = h0_refs[0 if fwd else 1][0]
        else:
            st_ref[...] = jnp.zeros((gn, SSD_INNER), F32)
        if unrolled:
            for k in range(nc):
                scan_chunk(k if fwd else nc - 1 - k, fwd, a, bias, causal, tri)
        else:
            def body(k, carry):
                c = k if fwd else nc - 1 - k
                scan_chunk(c, fwd, a, bias, causal, tri)
                if fwd:
                    conv_chunk(jnp.minimum(c + 1, nc - 1))
                return carry

            lax.fori_loop(0, nc, body, 0)
        if want_state:
            for j in range(SSD_INNER // LANE):
                g = (j * LANE) // (SSD_INNER // SSD_GROUPS)
                t = st_ref[:, j * LANE:(j + 1) * LANE].T
                s_refs[0 if fwd else 1][0, j * LANE:(j + 1) * LANE, :] = t[:, g * SSD_STATE:(g + 1) * SSD_STATE]

    if unrolled:
        for c in range(nc):
            conv_chunk(c)
    else:
        conv_chunk(0)
    scan_pass(True)
    scan_pass(False)


def _ssd(z, xbc, dt, cw, cb, hp, nw, h0, *, seq):
    t = z.shape[0]
    n_seq = t // seq
    has_h0 = h0 is not None
    want_state = not has_h0
    gn = SSD_GROUPS * SSD_STATE
    tok = lambda w: pl.BlockSpec((seq, w), lambda b: (b, 0))
    full = lambda a: pl.BlockSpec(a.shape, lambda b: (0,) * a.ndim)
    in_specs = [tok(SSD_INNER), tok(SSD_CONV_DIM), tok(LANE), full(cw), full(cb), full(hp), full(nw)]
    args = [z, xbc, dt, cw, cb, hp, nw]
    if has_h0:
        in_specs += [pl.BlockSpec((1, gn, SSD_INNER), lambda b: (b, 0, 0))] * 2
        args += list(h0)
    out_shape = [jax.ShapeDtypeStruct((t, SSD_INNER), BF16)]
    out_specs = [tok(SSD_INNER)]
    if want_state:
        out_shape += [jax.ShapeDtypeStruct((n_seq, SSD_INNER, SSD_STATE), F32)] * 2
        out_specs += [pl.BlockSpec((1, SSD_INNER, SSD_STATE), lambda b: (b, 0, 0))] * 2

    return pl.pallas_call(
        functools.partial(_ssd_kernel, seq=seq, has_h0=has_h0, want_state=want_state),
        grid=(n_seq,),
        in_specs=in_specs,
        out_specs=out_specs,
        out_shape=out_shape,
        scratch_shapes=[pltpu.VMEM((seq, SSD_INNER), F32),
                        pltpu.VMEM((seq, 2 * gn), F32),
                        pltpu.VMEM((seq, SSD_INNER), F32),
                        pltpu.VMEM((gn, SSD_INNER), F32)],
        compiler_params=_cparams(1),
        name="ssd_lat" if has_h0 else "ssd_ctx",
    )(*args)


def _rope(x, cos, sin, first_half, dist):
    partner = jnp.where(first_half, pltpu.roll(x, LANE - dist, axis=1), pltpu.roll(x, dist, axis=1))
    return x * cos + partner * sin


def _softmax_pv(ss, vals):
    m = ss[0].max(axis=-1, keepdims=True)
    for s in ss[1:]:
        m = jnp.maximum(m, s.max(axis=-1, keepdims=True))
    ps = [jnp.exp2(s - m) for s in ss]
    den = ps[0].sum(axis=-1, keepdims=True)
    for p in ps[1:]:
        den = den + p.sum(axis=-1, keepdims=True)
    o = _dot(ps[0].astype(BF16), vals[0]())
    for p, v in zip(ps[1:], vals[1:]):
        o = o + _dot(p.astype(BF16), v())
    return o / den


def _attend_blocks(seq, units, emit, s_scr):
    nblk = seq // Q_BLOCK

    def scores(u, blk, static):
        q_ref, rows, keys, _ = units[u]
        r = blk * rows if static else pl.multiple_of(blk * rows, rows)
        qb = q_ref[pl.ds(r, rows), :]
        return [_dot_nt(qb, k()) for k in keys]

    if s_scr is None:
        for i in range(nblk):
            emit(i * Q_BLOCK, [_softmax_pv(scores(u, i, True), units[u][3]) for u in range(len(units))])
        return

    def put(slot, blk):
        for u in range(len(units)):
            off = 0
            for s in scores(u, blk, False):
                s_scr[slot][u][:, off:off + s.shape[1]] = s
                off += s.shape[1]

    def take(slot, blk):
        outs = []
        for u, (_, _, keys, vals) in enumerate(units):
            off = 0
            ss = []
            for k in keys:
                n = k().shape[0]
                ss.append(s_scr[slot][u][:, off:off + n])
                off += n
            outs.append(_softmax_pv(ss, vals))
        emit(pl.multiple_of(blk * Q_BLOCK, Q_BLOCK), outs)

    put(0, 0)

    def body(ii, carry):
        b0 = 2 * ii
        put(1, b0 + 1)
        take(0, b0)
        put(0, jnp.where(b0 + 2 >= nblk, 0, b0 + 2))
        take(1, b0 + 1)
        return carry

    lax.fori_loop(0, nblk // 2, body, 0)


def _mla_kernel(*refs, seq, lat):
    it = iter(refs)
    qa_ref, kv_ref, qan_ref, kvn_ref, wq_ref, wk_ref, wv_ref, qn_ref, kn_ref = [next(it) for _ in range(9)]
    if lat:
        cckv_ref, ckpe_ref, cos_ref, sin_ref = [next(it) for _ in range(4)]
    o_ref = next(it)
    if not lat:
        ckv_out, kpe_out = next(it), next(it)
    q_s, k_s, v_s, qh0_s, qh1_s, kh0_s, kh1_s = [next(it) for _ in range(7)]
    s_scr = None
    if lat:
        kc_s, vc_s, kch0_s, kch1_s = [next(it) for _ in range(4)]
        s_scr = [[next(it), next(it)], [next(it), next(it)]]

    qscale = MLA_QK ** -0.5 * LOG2E
    kv = kv_ref[...]
    kpe = kv[:, LANE:]
    ckv_raw = kv[:, :LANE]
    ckv = ckv_raw * lax.rsqrt(jnp.mean(ckv_raw * ckv_raw, axis=-1, keepdims=True) + EPS) * kvn_ref[...]
    qa = qa_ref[...]
    qa = qa * lax.rsqrt(jnp.mean(qa * qa, axis=-1, keepdims=True) + EPS) * qan_ref[...]
    if not lat:
        ckv_out[...] = ckv
        kpe_out[...] = kpe
    ckvb = ckv.astype(BF16)
    q_s[...] = _dot(qa.astype(BF16), wq_ref[...])
    k_s[...] = _dot(ckvb, wk_ref[...])
    v_s[...] = _dot(ckvb, wv_ref[...]).astype(BF16)
    if lat:
        ccb = cckv_ref[0].astype(BF16)
        kc_s[...] = _dot(ccb, wk_ref[...])
        vc_s[...] = _dot(ccb, wv_ref[...]).astype(BF16)
        ckpe = ckpe_ref[0]
        cos = cos_ref[...]
        sin = sin_ref[...]
        lane = _iota((seq, LANE), 1)
        first_half = (lane % (MLA_ROPE // 2)) < (MLA_ROPE // 4)
    qn = qn_ref[...]
    kn = kn_ref[...]
    lo_lanes = _iota((Q_BLOCK, LANE), 1) < MLA_V

    def head_norm(x, g):
        return x * lax.rsqrt(jnp.sum(x * x, axis=-1, keepdims=True) * (1.0 / MLA_QK) + EPS) * g

    for j in range(MLA_HEADS // 2):
        pair = slice(j * LANE, (j + 1) * LANE)
        for half, (qh_s, kh_s) in enumerate(((qh0_s, kh0_s), (qh1_s, kh1_s))):
            sl = slice((2 * j + half) * LANE, (2 * j + half + 1) * LANE)
            qh = head_norm(q_s[:, sl], qn)
            kh = head_norm(k_s[:, sl] + kpe, kn)
            if lat:
                qh = _rope(qh, cos, sin, first_half, MLA_ROPE // 4)
                kh = _rope(kh, cos, sin, first_half, MLA_ROPE // 4)
                (kch0_s, kch1_s)[half][...] = head_norm(kc_s[:, sl] + ckpe, kn).astype(BF16)
            qh_s[...] = (qh * qscale).astype(BF16)
            kh_s[...] = kh.astype(BF16)

        def emit(r0, outs, pair=pair):
            o_ref[pl.ds(r0, Q_BLOCK), pair] = jnp.where(lo_lanes, outs[0], outs[1]).astype(o_ref.dtype)

        units = []
        for half, (qh_s, kh_s) in enumerate(((qh0_s, kh0_s), (qh1_s, kh1_s))):
            keys = [lambda r=kh_s: r[...]]
            vals = [lambda pair=pair: v_s[:, pair]]
            if lat:
                keys = [lambda r=(kch0_s, kch1_s)[half]: r[...]] + keys
                vals = [lambda pair=pair: vc_s[:, pair]] + vals
            units.append((qh_s, Q_BLOCK, keys, vals))
        _attend_blocks(seq, units, emit, s_scr)


def _mla(qa, kv, qan, kvn, wq, wk, wv, qn, kn, cache, tables, *, seq):
    t = qa.shape[0]
    n_seq = t // seq
    lat = cache is not None
    hv = MLA_HEADS * MLA_V
    tok = lambda w: pl.BlockSpec((seq, w), lambda b: (b, 0))
    full = lambda a: pl.BlockSpec(a.shape, lambda b: (0,) * a.ndim)
    args = [qa, kv, qan, kvn, wq, wk, wv, qn, kn]
    in_specs = [tok(MLA_Q_LORA), tok(2 * LANE)] + [full(a) for a in args[2:]]
    past = 0
    if lat:
        past = cache[0].shape[1]
        in_specs += [pl.BlockSpec((1, past, LANE), lambda b: (b, 0, 0))] * 2 + [full(tables[0]), full(tables[1])]
        args += [cache[0], cache[1], tables[0], tables[1]]
    out_shape = [jax.ShapeDtypeStruct((t, hv), BF16)]
    out_specs = [tok(hv)]
    if not lat:
        out_shape += [jax.ShapeDtypeStruct((t, LANE), F32)] * 2
        out_specs += [pl.BlockSpec((seq, LANE), lambda b: (b, 0))] * 2
    scratch = [pltpu.VMEM((seq, MLA_HEADS * LANE), F32),
               pltpu.VMEM((seq, MLA_HEADS * LANE), F32),
               pltpu.VMEM((seq, hv), BF16)] + [pltpu.VMEM((seq, LANE), BF16)] * 4
    if lat:
        scratch += [pltpu.VMEM((past, MLA_HEADS * LANE), F32),
                    pltpu.VMEM((past, hv), BF16),
                    pltpu.VMEM((past, LANE), BF16),
                    pltpu.VMEM((past, LANE), BF16)]
        scratch += [pltpu.VMEM((Q_BLOCK, past + seq), F32)] * 4

    return pl.pallas_call(
        functools.partial(_mla_kernel, seq=seq, lat=lat),
        grid=(n_seq,),
        in_specs=in_specs,
        out_specs=out_specs,
        out_shape=out_shape,
        scratch_shapes=scratch,
        compiler_params=_cparams(1),
        name="mla_lat" if lat else "mla_ctx",
    )(*args)


def _gqa_kernel(*refs, seq, lat):
    it = iter(refs)
    q_ref, k_ref, v_ref, qn_ref, kn_ref = [next(it) for _ in range(5)]
    if lat:
        ck_ref, cv_ref, cos_ref, sin_ref = [next(it) for _ in range(4)]
    o_ref = next(it)
    if not lat:
        kn_out = next(it)
    qn_s, kn_s, vb_s, q2_s = [next(it) for _ in range(4)]
    s_scr = None
    if lat:
        ckb_s, cvb_s = next(it), next(it)
        s_scr = [[next(it)], [next(it)]]

    qscale = GQA_HEAD_DIM ** -0.5 * LOG2E
    hd = GQA_HEAD_DIM
    nq = GQA_HEADS * hd
    nk = GQA_KV_HEADS * LANE

    def half_norm(x, g, width):
        n_half = width // hd
        ind = (_iota((width, LANE), 0) // hd == _iota((width, LANE), 1)).astype(BF16)
        ind_t = (_iota((LANE, width), 1) // hd == _iota((LANE, width), 0)).astype(BF16)
        del n_half
        ss = _dot_x3(x * x, ind)
        r = lax.rsqrt(ss * (1.0 / hd) + EPS)
        return x * _dot_x3(r, ind_t) * g

    qn = half_norm(q_ref[...], qn_ref[...], nq)
    kn = half_norm(k_ref[...], kn_ref[...], nk)
    if not lat:
        kn_out[...] = kn
    if lat:
        cos = cos_ref[...]
        sin = sin_ref[...]
        lane = _iota((seq, LANE), 1)
        first_half = (lane % (hd // 2)) < (hd // 4)
        for j in range(nq // LANE):
            sl = slice(j * LANE, (j + 1) * LANE)
            qn_s[:, sl] = _rope(qn[:, sl], cos, sin, first_half, hd // 4)
        for j in range(nk // LANE):
            sl = slice(j * LANE, (j + 1) * LANE)
            kn_s[:, sl] = _rope(kn[:, sl], cos, sin, first_half, hd // 4).astype(BF16)
        ckb_s[...] = ck_ref[0].astype(BF16)
        cvb_s[...] = cv_ref[0].astype(BF16)
    else:
        qn_s[...] = qn
        kn_s[...] = kn.astype(BF16)
    vb_s[...] = v_ref[...].astype(BF16)
    lo_q = _iota((Q_BLOCK, LANE), 1) < hd

    for j in range(GQA_HEADS // 2):
        pair = slice(j * LANE, (j + 1) * LANE)
        g = (2 * j) // (GQA_HEADS // GQA_KV_HEADS)
        gs = slice(g * LANE, (g + 1) * LANE)
        for i in range(seq // Q_BLOCK):
            qp = qn_s[i * Q_BLOCK:(i + 1) * Q_BLOCK, pair] * qscale
            q2_s[2 * i * Q_BLOCK:(2 * i + 1) * Q_BLOCK, :] = jnp.where(lo_q, qp, 0.0).astype(BF16)
            q2_s[(2 * i + 1) * Q_BLOCK:(2 * i + 2) * Q_BLOCK, :] = jnp.where(lo_q, 0.0, qp).astype(BF16)

        def emit(r0, outs, pair=pair):
            o = outs[0]
            o_ref[pl.ds(r0, Q_BLOCK), pair] = jnp.where(lo_q, o[:Q_BLOCK], o[Q_BLOCK:]).astype(o_ref.dtype)

        keys = [lambda gs=gs: kn_s[:, gs]]
        vals = [lambda gs=gs: vb_s[:, gs]]
        if lat:
            keys = [lambda gs=gs: ckb_s[:, gs]] + keys
            vals = [lambda gs=gs: cvb_s[:, gs]] + vals
        _attend_blocks(seq, [(q2_s, 2 * Q_BLOCK, keys, vals)], emit, s_scr)


def _gqa(q, k, v, qn, kn, cache, tables, *, seq):
    t = q.shape[0]
    n_seq = t // seq
    lat = cache is not None
    nq = GQA_HEADS * GQA_HEAD_DIM
    nk = GQA_KV_HEADS * LANE
    tok = lambda w: pl.BlockSpec((seq, w), lambda b: (b, 0))
    full = lambda a: pl.BlockSpec(a.shape, lambda b: (0,) * a.ndim)
    args = [q, k, v, qn, kn]
    in_specs = [tok(nq), tok(nk), tok(nk), full(qn), full(kn)]
    past = 0
    if lat:
        past = cache[0].shape[1]
        in_specs += [pl.BlockSpec((1, past, nk), lambda b: (b, 0, 0))] * 2 + [full(tables[0]), full(tables[1])]
        args += [cache[0], cache[1], tables[0], tables[1]]
    out_shape = [jax.ShapeDtypeStruct((t, nq), BF16)]
    out_specs = [tok(nq)]
    if not lat:
        out_shape += [jax.ShapeDtypeStruct((t, nk), F32)]
        out_specs += [pl.BlockSpec((seq, nk), lambda b: (b, 0))]
    scratch = [pltpu.VMEM((seq, nq), F32),
               pltpu.VMEM((seq, nk), BF16),
               pltpu.VMEM((seq, nk), BF16),
               pltpu.VMEM((2 * seq, LANE), BF16)]
    if lat:
        scratch += [pltpu.VMEM((past, nk), BF16), pltpu.VMEM((past, nk), BF16)]
        scratch += [pltpu.VMEM((2 * Q_BLOCK, past + seq), F32)] * 2

    return pl.pallas_call(
        functools.partial(_gqa_kernel, seq=seq, lat=lat),
        grid=(n_seq,),
        in_specs=in_specs,
        out_specs=out_specs,
        out_shape=out_shape,
        scratch_shapes=scratch,
        compiler_params=_cparams(1),
        name="gqa_lat" if lat else "gqa_ctx",
    )(*args)


def _route(sel, sc):
    gs = []
    for g in range(N_GROUPS):
        r = sel[g * GROUP_SIZE:(g + 1) * GROUP_SIZE]
        best = None
        for i in range(GROUP_SIZE):
            for j in range(i + 1, GROUP_SIZE):
                s = r[i] + r[j]
                best = s if best is None else jnp.maximum(best, s)
        gs.append(best)
    cur = gs[0]
    grp = jnp.zeros_like(cur, dtype=jnp.int32)
    for g in range(1, N_GROUPS):
        upd = gs[g] > cur
        grp = jnp.where(upd, g, grp)
        cur = jnp.where(upd, gs[g], cur)

    def pick(rows, i):
        v = rows[i]
        for g in range(1, N_GROUPS):
            v = jnp.where(grp == g, rows[g * GROUP_SIZE + i], v)
        return v

    cand = [pick(sel, i) for i in range(GROUP_SIZE)]
    csc = [pick(sc, i) for i in range(GROUP_SIZE)]
    v1, i1, w1 = cand[0], jnp.zeros_like(grp), csc[0]
    for i in range(1, GROUP_SIZE):
        upd = cand[i] > v1
        v1 = jnp.where(upd, cand[i], v1)
        i1 = jnp.where(upd, i, i1)
        w1 = jnp.where(upd, csc[i], w1)
    v2 = i2 = w2 = None
    for i in range(GROUP_SIZE):
        ok = i1 != i
        if v2 is None:
            v2 = jnp.where(ok, cand[i], -jnp.inf)
            i2 = jnp.zeros_like(grp)
            w2 = csc[i]
        else:
            upd = ok & (cand[i] > v2)
            v2 = jnp.where(upd, cand[i], v2)
            i2 = jnp.where(upd, i, i2)
            w2 = jnp.where(upd, csc[i], w2)
    tot = w1 + w2
    local = [jnp.where(i1 == i, w1 / tot, 0.0) + jnp.where(i2 == i, w2 / tot, 0.0)
             for i in range(GROUP_SIZE)]
    onehot = [jnp.where(grp == g, 1.0, 0.0) for g in range(N_GROUPS)]
    return local, onehot


def _outproj_kernel(*refs, n_y):
    x_ref, mod_ref = refs[0], refs[1]
    y_refs = refs[2:2 + n_y]
    w_ref, g_ref, wr_ref, rb_ref, x1_ref, h2_ref, gate_ref, cnt_ref = refs[2 + n_y:]
    m = mod_ref[0]
    off = 0
    out = None
    for y_ref in y_refs:
        k = y_ref.shape[1]
        part = _dot(y_ref[...], w_ref[off:off + k, :])
        out = part if out is None else out + part
        off += k
    x1 = x_ref[...] + m[:, 2 * D:3 * D] * out
    x1_ref[...] = x1
    h2 = _norm_mod(x1, g_ref[...], m[:, 3 * D:4 * D], m[:, 4 * D:5 * D])
    h2_hi = h2.astype(BF16)
    h2_ref[...] = h2_hi
    h2_mid = (h2 - h2_hi.astype(F32)).astype(BF16)
    parts = _dot(h2_hi, wr_ref[...]) + _dot(h2_mid, wr_ref[...])
    logits = (parts + pltpu.roll(parts, LANE - N_EXPERTS, axis=1)
              + pltpu.roll(parts, LANE - 2 * N_EXPERTS, axis=1))
    sc_t = _sigmoid(logits.T[0:N_EXPERTS, :])
    sel_t = sc_t + rb_ref[...]
    local, onehot = _route([sel_t[e:e + 1] for e in range(N_EXPERTS)],
                           [sc_t[e:e + 1] for e in range(N_EXPERTS)])
    t = x1.shape[0]
    rows_t = (local + [jnp.zeros((GROUP_LANE - GROUP_SIZE, t), F32)] + onehot
              + [jnp.zeros((LANE - GROUP_LANE - N_GROUPS, t), F32)])
    routed = jnp.concatenate(rows_t, axis=0).T
    gate_ref[...] = routed
    cnt = jnp.sum(routed, axis=0, keepdims=True).astype(jnp.int32)
    cnt_ref[0] = jnp.broadcast_to(cnt, (8, LANE))


def _outproj(x, mod, ys, w, g, wr, rb):
    t = x.shape[0]
    full = lambda a: pl.BlockSpec(a.shape, lambda i: (0,) * a.ndim)
    return pl.pallas_call(
        functools.partial(_outproj_kernel, n_y=len(ys)),
        grid=(t // TOKEN_TILE,),
        in_specs=[pl.BlockSpec((TOKEN_TILE, D), lambda i: (i, 0)),
                  _mod_spec(mod, t, TOKEN_TILE)]
                 + [pl.BlockSpec((TOKEN_TILE, y.shape[1]), lambda i: (i, 0)) for y in ys]
                 + [full(w), full(g), full(wr), full(rb)],
        out_specs=[pl.BlockSpec((TOKEN_TILE, D), lambda i: (i, 0)),
                   pl.BlockSpec((TOKEN_TILE, D), lambda i: (i, 0)),
                   pl.BlockSpec((TOKEN_TILE, LANE), lambda i: (i, 0)),
                   pl.BlockSpec((1, 8, LANE), lambda i: (i, 0, 0))],
        out_shape=[jax.ShapeDtypeStruct((t, D), F32),
                   jax.ShapeDtypeStruct((t, D), BF16),
                   jax.ShapeDtypeStruct((t, LANE), F32),
                   jax.ShapeDtypeStruct((t // TOKEN_TILE, 8, LANE), jnp.int32)],
        compiler_params=_cparams(1),
        name="outproj_router",
    )(x, mod, *ys, w, g, wr, rb)


def _pad_to(n, shift):
    return ((n + (1 << shift) - 1) >> shift) << shift


SEG_SHIFT = SEG_ALIGN.bit_length() - 1
TILE_SHIFT = EXPERT_TILE.bit_length() - 1


def _start_segments(plan_ref, i, nt, make):
    for g in range(N_GROUPS):
        n = plan_ref[i * N_GROUPS + g]
        lo = plan_ref[(nt + i) * N_GROUPS + g]
        base = plan_ref[(2 * nt + i) * N_GROUPS + g]

        def body(c, carry, lo=lo, base=base):
            for cp in make(pl.multiple_of(lo + c * SEG_ALIGN, SEG_ALIGN),
                           pl.multiple_of(base + c * SEG_ALIGN, SEG_ALIGN)):
                cp.start()
            return carry

        lax.fori_loop(0, (n + SEG_ALIGN - 1) >> SEG_SHIFT, body, 0)


def _wait_segments(plan_ref, i, make):
    total = jnp.int32(0)
    for g in range(N_GROUPS):
        total = total + ((plan_ref[i * N_GROUPS + g] + SEG_ALIGN - 1) >> SEG_SHIFT)

    def body(c, carry):
        for cp in make(0, 0):
            cp.wait()
        return carry

    lax.fori_loop(0, total, body, 0)


def _dispatch_kernel(cnt_ref, h_ref, r_ref, xs_hbm, rs_hbm, dest_ref, plan_ref,
                     xs_l, rs_l, zx, zr, fill_s, sem, *, nt, n_rows):
    i = pl.program_id(0)
    t = TOKEN_TILE
    k_tiles = n_rows // EXPERT_TILE

    @pl.when(i == 0)
    def _():
        zx[...] = jnp.zeros_like(zx)
        zr[...] = jnp.zeros_like(zr)
        tots = []
        for g in range(N_GROUPS):
            tots.append(lax.fori_loop(
                0, nt, lambda tt, acc, g=g: acc + _pad_to(cnt_ref[tt * N_GROUPS + g], SEG_SHIFT), jnp.int32(0)))
        starts = []
        cur = jnp.int32(0)
        for g in range(N_GROUPS):
            starts.append(cur)
            fill_s[2 * g] = cur + tots[g]
            cur = cur + _pad_to(tots[g], TILE_SHIFT)
            fill_s[2 * g + 1] = cur
        fill_s[2 * N_GROUPS] = cur
        fill_s[2 * N_GROUPS + 1] = n_rows

        def per_tile(tt, run):
            lo = jnp.int32(0)
            new = []
            for g in range(N_GROUPS):
                n = cnt_ref[tt * N_GROUPS + g]
                plan_ref[tt * N_GROUPS + g] = n
                plan_ref[(nt + tt) * N_GROUPS + g] = lo
                plan_ref[(2 * nt + tt) * N_GROUPS + g] = run[g]
                seg = _pad_to(n, SEG_SHIFT)
                lo = lo + seg
                new.append(run[g] + seg)
            return tuple(new)

        lax.fori_loop(0, nt, per_tile, tuple(starts))
        for k in range(k_tiles):
            row = k * EXPERT_TILE
            grp = 0
            for g in range(1, N_GROUPS):
                grp = grp + (row >= starts[g]).astype(jnp.int32)
            plan_ref[3 * nt * N_GROUPS + k] = grp
        plan_ref[3 * nt * N_GROUPS + k_tiles] = cur >> TILE_SHIFT

    routed = r_ref[...]
    lane = _iota((t, LANE), 1)
    is_grp = (lane >= GROUP_LANE) & (lane < GROUP_LANE + N_GROUPS)
    onehot = jnp.where(is_grp, routed, 0.0)
    tril = (_iota((t, t), 0) >= _iota((t, t), 1)).astype(BF16)
    rank = _dot(tril, onehot.astype(BF16))
    seg_lo = jnp.zeros((1, LANE), F32)
    lane1 = _iota((1, LANE), 1)
    for g in range(N_GROUPS):
        lo = plan_ref[(nt + i) * N_GROUPS + g]
        seg_lo = jnp.where(lane1 == GROUP_LANE + g, (lo - 1).astype(F32), seg_lo)
    val = jnp.where(onehot > 0.0, rank + seg_lo, 0.0)
    dest = _dot_x2(val, jnp.ones((LANE, LANE), BF16))
    dest_ref[...] = dest
    dest_row = dest.T[0:1, :].astype(jnp.int32)
    perm = jnp.where(_iota((DISPATCH_ROWS, t), 0) == dest_row, 1.0, 0.0).astype(BF16)
    slot = i % 2
    xs_l[slot] = _dot(perm, h_ref[...]).astype(BF16)
    rs_l[slot] = _dot_3x(perm, routed)

    def to_buffer(slot):
        def make(lo, base):
            return (pltpu.make_async_copy(xs_l.at[slot, pl.ds(lo, SEG_ALIGN)],
                                          xs_hbm.at[pl.ds(base, SEG_ALIGN)], sem.at[slot, 0]),
                    pltpu.make_async_copy(rs_l.at[slot, pl.ds(lo, SEG_ALIGN)],
                                          rs_hbm.at[pl.ds(base, SEG_ALIGN)], sem.at[slot, 1]))
        return make

    _start_segments(plan_ref, i, nt, to_buffer(slot))

    @pl.when(i > 0)
    def _():
        _wait_segments(plan_ref, i - 1, to_buffer(1 - slot))

    @pl.when(i == nt - 1)
    def _():
        _wait_segments(plan_ref, i, to_buffer(slot))
        def zero(base):
            return (pltpu.make_async_copy(zx, xs_hbm.at[pl.ds(base, SEG_ALIGN)], sem.at[0, 0]),
                    pltpu.make_async_copy(zr, rs_hbm.at[pl.ds(base, SEG_ALIGN)], sem.at[0, 1]))

        n_fill = jnp.int32(0)
        for r in range(N_GROUPS + 1):
            lo = fill_s[2 * r]
            chunks = (fill_s[2 * r + 1] - lo) >> SEG_SHIFT

            def body(c, carry, lo=lo):
                for cp in zero(pl.multiple_of(lo + c * SEG_ALIGN, SEG_ALIGN)):
                    cp.start()
                return carry

            lax.fori_loop(0, chunks, body, 0)
            n_fill = n_fill + chunks

        def wait_fill(c, carry):
            for cp in zero(0):
                cp.wait()
            return carry

        lax.fori_loop(0, n_fill, wait_fill, 0)


def _moe_rows(n_tokens):
    nt = n_tokens // TOKEN_TILE
    worst = n_tokens + nt * N_GROUPS * (SEG_ALIGN - 1) + N_GROUPS * (EXPERT_TILE - SEG_ALIGN)
    return _pad_to(worst, TILE_SHIFT)


def _dispatch(counts, h2, routed):
    t = h2.shape[0]
    nt = t // TOKEN_TILE
    n_rows = _moe_rows(t)
    plan_len = 3 * nt * N_GROUPS + n_rows // EXPERT_TILE + 1
    grid_spec = pltpu.PrefetchScalarGridSpec(
        num_scalar_prefetch=1,
        grid=(nt,),
        in_specs=[pl.BlockSpec((TOKEN_TILE, D), lambda i, c: (i, 0)),
                  pl.BlockSpec((TOKEN_TILE, LANE), lambda i, c: (i, 0))],
        out_specs=[pl.BlockSpec(memory_space=pl.ANY),
                   pl.BlockSpec(memory_space=pl.ANY),
                   pl.BlockSpec((TOKEN_TILE, LANE), lambda i, c: (i, 0)),
                   pl.BlockSpec(memory_space=pltpu.SMEM)],
        scratch_shapes=[pltpu.VMEM((2, DISPATCH_ROWS, D), BF16),
                        pltpu.VMEM((2, DISPATCH_ROWS, LANE), F32),
                        pltpu.VMEM((SEG_ALIGN, D), BF16),
                        pltpu.VMEM((SEG_ALIGN, LANE), F32),
                        pltpu.SMEM((2 * N_GROUPS + 2,), jnp.int32),
                        pltpu.SemaphoreType.DMA((2, 2))])
    return pl.pallas_call(
        functools.partial(_dispatch_kernel, nt=nt, n_rows=n_rows),
        grid_spec=grid_spec,
        out_shape=[jax.ShapeDtypeStruct((n_rows, D), BF16),
                   jax.ShapeDtypeStruct((n_rows, LANE), F32),
                   jax.ShapeDtypeStruct((t, LANE), F32),
                   jax.ShapeDtypeStruct((plan_len,), jnp.int32)],
        compiler_params=_cparams(1),
        name="moe_dispatch",
    )(counts, h2, routed)


def _experts_kernel(plan_ref, xs_ref, rs_ref, wg_ref, wu_ref, wd_ref, y_ref, *, used_at):
    k = pl.program_id(0)
    used = plan_ref[used_at]

    @pl.when(k < used)
    def _():
        rows = xs_ref[...]
        gates = rs_ref[...]
        acc = None
        for e in range(GROUP_SIZE):
            a = _dot(rows, wg_ref[e])
            u = _dot(rows, wu_ref[e])
            act = (_silu(a) * u * gates[:, e:e + 1]).astype(BF16)
            part = _dot(act, wd_ref[e])
            acc = part if acc is None else acc + part
        y_ref[...] = acc.astype(y_ref.dtype)

    @pl.when(k >= used)
    def _():
        y_ref[...] = jnp.zeros_like(y_ref)


def _experts(plan, xs, rs, wg, wu, wd, layer, nt):
    n_rows = xs.shape[0]
    k_tiles = n_rows // EXPERT_TILE
    at = 3 * nt * N_GROUPS
    wspec = lambda a, b: pl.BlockSpec((None, GROUP_SIZE, a, b), lambda k, p: (layer, p[at + k], 0, 0))
    grid_spec = pltpu.PrefetchScalarGridSpec(
        num_scalar_prefetch=1,
        grid=(k_tiles,),
        in_specs=[pl.BlockSpec((EXPERT_TILE, D), lambda k, p: (k, 0)),
                  pl.BlockSpec((EXPERT_TILE, LANE), lambda k, p: (k, 0)),
                  wspec(D, D_EXPERT), wspec(D, D_EXPERT), wspec(D_EXPERT, D)],
        out_specs=pl.BlockSpec((EXPERT_TILE, D), lambda k, p: (k, 0)))
    return pl.pallas_call(
        functools.partial(_experts_kernel, used_at=at + k_tiles),
        grid_spec=grid_spec,
        out_shape=jax.ShapeDtypeStruct((n_rows, D), BF16),
        compiler_params=_cparams(1),
        name="moe_experts",
    )(plan, xs, rs, wg, wu, wd)


def _combine_kernel(plan_ref, x_ref, mod_ref, dest_ref, ys_hbm, o_ref, y_l, sem, *, nt):
    i = pl.program_id(0)

    slot = i % 2

    def from_buffer(slot):
        def make(lo, base):
            return (pltpu.make_async_copy(ys_hbm.at[pl.ds(base, SEG_ALIGN)],
                                          y_l.at[slot, pl.ds(lo, SEG_ALIGN)], sem.at[slot]),)
        return make

    @pl.when(i == 0)
    def _():
        y_l[...] = jnp.zeros_like(y_l)
        _start_segments(plan_ref, i, nt, from_buffer(slot))

    @pl.when(i + 1 < nt)
    def _():
        _start_segments(plan_ref, i + 1, nt, from_buffer(1 - slot))

    _wait_segments(plan_ref, i, from_buffer(slot))
    dest = dest_ref[...].astype(jnp.int32)
    lane = _iota(dest.shape, 1)
    back = jnp.concatenate(
        [jnp.where(dest == lane + j * LANE, 1.0, 0.0).astype(BF16) for j in range(DISPATCH_ROWS // LANE)], axis=1)
    y = _dot(back, y_l[slot])
    o_ref[...] = x_ref[...] + mod_ref[0][:, 5 * D:6 * D] * y


def _combine(plan, x1, mod, dest, ys):
    t = x1.shape[0]
    nt = t // TOKEN_TILE
    per_row = t // mod.shape[0]
    grid_spec = pltpu.PrefetchScalarGridSpec(
        num_scalar_prefetch=1,
        grid=(nt,),
        in_specs=[pl.BlockSpec((TOKEN_TILE, D), lambda i, p: (i, 0)),
                  pl.BlockSpec((1, 1, 6 * D), lambda i, p: ((i * TOKEN_TILE) // per_row, 0, 0)),
                  pl.BlockSpec((TOKEN_TILE, LANE), lambda i, p: (i, 0)),
                  pl.BlockSpec(memory_space=pl.ANY)],
        out_specs=pl.BlockSpec((TOKEN_TILE, D), lambda i, p: (i, 0)),
        scratch_shapes=[pltpu.VMEM((2, DISPATCH_ROWS, D), BF16),
                        pltpu.SemaphoreType.DMA((2,))])
    return pl.pallas_call(
        functools.partial(_combine_kernel, nt=nt),
        grid_spec=grid_spec,
        out_shape=jax.ShapeDtypeStruct((t, D), F32),
        compiler_params=_cparams(1),
        name="moe_combine",
    )(plan, x1, mod, dest, ys)


def _moe(x1, mod, h2, routed, cnt, wg, wu, wd, layer):
    nt = x1.shape[0] // TOKEN_TILE
    counts = cnt[:, 0, GROUP_LANE:GROUP_LANE + N_GROUPS].reshape(nt * N_GROUPS)
    xs, rs, dest, plan = _dispatch(counts, h2, routed)
    ys = _experts(plan, xs, rs, wg, wu, wd, layer, nt)
    return _combine(plan, x1, mod, dest, ys)


def _rope_angles(pos, half):
    inv = jnp.power(ROPE_THETA, -jnp.arange(half, dtype=F32) / half)
    ang = pos.astype(F32)[:, None] * inv[None, :]
    return jnp.cos(ang), jnp.sin(ang)


def _rope_tables(seq, rot, lead, lane_tiles):
    rows = seq // GRID_W
    row = jnp.repeat(jnp.arange(rows), GRID_W)
    col = jnp.tile(jnp.arange(GRID_W), rows)
    cr, sr = _rope_angles(row, rot // 4)
    cc, sc = _rope_angles(col, rot // 4)
    cos = jnp.concatenate([cr, cr, cc, cc], axis=1)
    sin = jnp.concatenate([-sr, sr, -sc, sc], axis=1)
    width = LANE // lane_tiles
    pad = width - lead - rot
    cos = jnp.concatenate([jnp.ones((seq, lead), F32), cos, jnp.ones((seq, pad), F32)], axis=1)
    sin = jnp.concatenate([jnp.zeros((seq, lead), F32), sin, jnp.zeros((seq, pad), F32)], axis=1)
    return jnp.tile(cos, (1, lane_tiles)), jnp.tile(sin, (1, lane_tiles))


def _pad_lanes(a, width):
    return jnp.pad(a, [(0, 0)] * (a.ndim - 1) + [(0, width - a.shape[-1])])


def _even_weights(w_in, w_q_b, w_kv_b):
    z = w_in[:, 0:1024]
    xbc = w_in[:, 1024:2304]
    dt = w_in[:, 2304:2320]
    qa = w_in[:, 2320:2576]
    ckv = w_in[:, 2576:2704]
    kpe = w_in[:, 2704:2736]
    zeros = lambda n: jnp.zeros((D, n), w_in.dtype)
    w = jnp.concatenate([z, xbc, qa, ckv, zeros(MLA_NOPE), kpe, zeros(LANE - MLA_QK), dt,
                         zeros(LANE - SSD_HEADS)], axis=1).astype(BF16)
    wq = _pad_lanes(w_q_b.reshape(MLA_Q_LORA, MLA_HEADS, MLA_QK), LANE).reshape(MLA_Q_LORA, MLA_HEADS * LANE)
    kvb = w_kv_b.reshape(MLA_KV_LORA, MLA_HEADS, MLA_NOPE + MLA_V)
    wk = _pad_lanes(kvb[:, :, :MLA_NOPE], LANE).reshape(MLA_KV_LORA, MLA_HEADS * LANE)
    wv = kvb[:, :, MLA_NOPE:].reshape(MLA_KV_LORA, MLA_HEADS * MLA_V)
    return w, wq.astype(BF16), wk.astype(BF16), wv.astype(BF16)


def _dup_heads(a):
    s = a.shape[:-1]
    a = a.reshape(s + (GQA_KV_HEADS, 1, GQA_HEAD_DIM))
    return jnp.broadcast_to(a, s + (GQA_KV_HEADS, 2, GQA_HEAD_DIM)).reshape(s + (GQA_KV_HEADS * LANE,))


def _undup_heads(a):
    s = a.shape[:-1]
    return a.reshape(s + (GQA_KV_HEADS, 2, GQA_HEAD_DIM))[..., 0, :]


def _state_to_kernel(h0):
    b = h0.shape[0]
    t = jnp.transpose(h0, (0, 3, 1, 2)).reshape(b, SSD_STATE, SSD_INNER)
    half = SSD_INNER // SSD_GROUPS
    col = jnp.arange(SSD_INNER) // half
    parts = [jnp.where(col == g, t, 0.0) for g in range(SSD_GROUPS)]
    return jnp.concatenate(parts, axis=1)


def _state_from_kernel(st):
    return st.reshape(st.shape[0], SSD_HEADS, SSD_HEADDIM, SSD_STATE)


def kernel(x_prompt, x_sample, c, state_ssd_fwd, state_ssd_bwd, cache_mla_ckv, cache_mla_kpe, cache_gqa_k, cache_gqa_v, c_ctx, g_mix, g_ffn, w_ada, b_ada, w_router, router_bias, w_exp_gate, w_exp_up, w_exp_down, w_in_even, ssd_conv_w, ssd_conv_b, ssd_a_log_fwd, ssd_a_log_bwd, ssd_dt_bias_fwd, ssd_dt_bias_bwd, ssd_d, ssd_norm, mla_q_a_norm, mla_w_q_b, mla_kv_a_norm, mla_w_kv_b, mla_q_norm, mla_k_norm, w_out_even, w_in_odd, gqa_q_norm, gqa_k_norm, w_out_odd):
    nb, ls, _ = x_prompt.shape
    db, dl, _ = x_sample.shape
    depth = w_ada.shape[0]
    n_ctx = nb * ls

    xs = [x_prompt.reshape(n_ctx, D), x_sample.reshape(db * dl, D)]
    seqs = (ls, dl)
    rows = 16
    cond = jnp.concatenate([c_ctx[None, :], c, jnp.zeros((rows - 1 - db, D), F32)], axis=0)
    mod_all = _ada_mod(cond, w_ada, b_ada)

    wr = _pad_lanes(jnp.concatenate(_split3(w_router), axis=1), LANE)
    rb = router_bias.reshape(N_EXPERTS, 1)
    wg_all = w_exp_gate.astype(BF16)
    wu_all = w_exp_up.astype(BF16)
    wd_all = w_exp_down.astype(BF16)

    outs = {}
    for i in range(depth):
        j = i // 2
        mods = [mod_all[i, 0:1].reshape(1, 1, 6 * D), mod_all[i, 1:1 + db].reshape(db, 1, 6 * D)]
        g1 = g_mix[i].reshape(1, D)
        g2 = g_ffn[i].reshape(1, D)
        if i % 2 == 0:
            w, wq, wk, wv = _even_weights(w_in_even[j], mla_w_q_b[j], mla_w_kv_b[j])
            cw = jnp.pad(ssd_conv_w[j], ((0, 5), (0, 0)))
            cb = ssd_conv_b[j].reshape(1, SSD_CONV_DIM)
            hp = _pad_lanes(jnp.stack([ssd_a_log_fwd[j], ssd_a_log_bwd[j], ssd_dt_bias_fwd[j],
                                       ssd_dt_bias_bwd[j], ssd_d[j]]), LANE)
            hp = jnp.pad(hp, ((0, 3), (0, 0)))
            nw = ssd_norm[j].reshape(1, SSD_INNER)
            qan = mla_q_a_norm[j].reshape(1, MLA_Q_LORA)
            kvn = mla_kv_a_norm[j].reshape(1, MLA_KV_LORA)
            qn = _pad_lanes(mla_q_norm[j].reshape(1, MLA_QK), LANE)
            kn = _pad_lanes(mla_k_norm[j].reshape(1, MLA_QK), LANE)
            h0 = (_state_to_kernel(state_ssd_fwd[:, j]), _state_to_kernel(state_ssd_bwd[:, j]))
            ckpe = jnp.pad(cache_mla_kpe[:, j], ((0, 0), (0, 0), (MLA_NOPE, LANE - MLA_QK)))
            tables = _rope_tables(dl, MLA_ROPE, MLA_NOPE, 1)
            w_out = w_out_even[j].astype(BF16)
            ys = []
            for s in range(2):
                z, xbc, qa, kv, dt = _inproj(xs[s], mods[s], g1, w,
                                             (SSD_INNER, SSD_CONV_DIM, MLA_Q_LORA, 2 * LANE, LANE))
                if s == 0:
                    y, sf, sb = _ssd(z, xbc, dt, cw, cb, hp, nw, None, seq=seqs[s])
                    o, ckv_new, kpe_new = _mla(qa, kv, qan, kvn, wq, wk, wv, qn, kn, None, None, seq=seqs[s])
                    outs.setdefault("ssd_f", []).append(_state_from_kernel(sf))
                    outs.setdefault("ssd_b", []).append(_state_from_kernel(sb))
                    outs.setdefault("ckv", []).append(ckv_new.reshape(nb, ls, MLA_KV_LORA))
                    outs.setdefault("kpe", []).append(kpe_new[:, MLA_NOPE:MLA_QK].reshape(nb, ls, MLA_ROPE))
                else:
                    y, = _ssd(z, xbc, dt, cw, cb, hp, nw, h0, seq=seqs[s])
                    o, = _mla(qa, kv, qan, kvn, wq, wk, wv, qn, kn, (cache_mla_ckv[:, j], ckpe), tables,
                              seq=seqs[s])
                ys.append((y, o))
        else:
            wi = w_in_odd[j]
            nq = GQA_HEADS * GQA_HEAD_DIM
            nkv = GQA_KV_HEADS * GQA_HEAD_DIM
            nk = GQA_KV_HEADS * LANE
            w = jnp.concatenate([wi[:, :nq], _dup_heads(wi[:, nq:nq + nkv]), _dup_heads(wi[:, nq + nkv:])],
                                axis=1).astype(BF16)
            qn = jnp.tile(gqa_q_norm[j].reshape(1, GQA_HEAD_DIM), (1, nq // GQA_HEAD_DIM))
            kn = jnp.tile(gqa_k_norm[j].reshape(1, GQA_HEAD_DIM), (1, nk // GQA_HEAD_DIM))
            cache = (_dup_heads(cache_gqa_k[:, j].reshape(db, -1, nkv)),
                     _dup_heads(cache_gqa_v[:, j].reshape(db, -1, nkv)))
            tables = _rope_tables(dl, GQA_HEAD_DIM, 0, 2)
            w_out = w_out_odd[j].astype(BF16)
            ys = []
            for s in range(2):
                q, k, v = _inproj(xs[s], mods[s], g1, w, (nq, nk, nk))
                if s == 0:
                    o, k_new = _gqa(q, k, v, qn, kn, None, None, seq=seqs[s])
                    outs.setdefault("gk", []).append(
                        _undup_heads(k_new).reshape(nb, ls, GQA_KV_HEADS, GQA_HEAD_DIM))
                    outs.setdefault("gv", []).append(
                        _undup_heads(v).reshape(nb, ls, GQA_KV_HEADS, GQA_HEAD_DIM))
                else:
                    o, = _gqa(q, k, v, qn, kn, cache, tables, seq=seqs[s])
                ys.append((o,))
        for s in range(2):
            x1, h2, routed, cnt = _outproj(xs[s], mods[s], ys[s], w_out, g2, wr, rb)
            xs[s] = _moe(x1, mods[s], h2, routed, cnt, wg_all, wu_all, wd_all, i)

    stack = lambda key: jnp.stack(outs[key], axis=1)
    return (xs[0].reshape(nb, ls, D), xs[1].reshape(db, dl, D),
            stack("ssd_f"), stack("ssd_b"), stack("ckv"), stack("kpe"), stack("gk"), stack("gv"))
```

```python
import functools
import math

import jax
import jax.numpy as jnp
from jax import lax
from jax.experimental import pallas as pl
from jax.experimental.pallas import tpu as pltpu

F32 = jnp.float32
BF16 = jnp.bfloat16

D = 1024
EPS = 1e-6
GRID_W = 64
ROPE_THETA = 10000.0
SSD_HEADS = 16
SSD_HEADDIM = 64
SSD_INNER = 1024
SSD_STATE = 64
SSD_GROUPS = 2
SSD_CONV_DIM = 1280
SSD_CHUNK = 128
MLA_HEADS = 8
MLA_Q_LORA = 256
MLA_KV_LORA = 128
MLA_NOPE = 64
MLA_ROPE = 32
MLA_V = 64
MLA_QK = 96
GQA_HEADS = 16
GQA_KV_HEADS = 4
GQA_HEAD_DIM = 64
N_EXPERTS = 16
N_GROUPS = 4
GROUP_SIZE = 4
D_EXPERT = 256

LANE = 128
Q_BLOCK = 128
LOG2E = 1.4426950408889634
TOKEN_TILE = 512
EXPERT_TILE = 512
SEG_ALIGN = 16
GROUP_LANE = 16
DISPATCH_ROWS = TOKEN_TILE + LANE
ADA_TILE = 1536
VMEM_LIMIT = 56 * 1024 * 1024


def _cparams(n_axes):
    return pltpu.CompilerParams(dimension_semantics=("arbitrary",) * n_axes,
                                vmem_limit_bytes=VMEM_LIMIT)


def _dot(a, b):
    return jnp.dot(a, b, preferred_element_type=F32)


def _dot_nt(a, b):
    return lax.dot_general(a, b, (((1,), (1,)), ((), ())), preferred_element_type=F32)


def _split3(x):
    hi = x.astype(BF16)
    r1 = x - hi.astype(F32)
    mid = r1.astype(BF16)
    lo = (r1 - mid.astype(F32)).astype(BF16)
    return hi, mid, lo


def _dot_x3(x, e):
    hi, mid, lo = _split3(x)
    return _dot(hi, e) + _dot(mid, e) + _dot(lo, e)


def _dot_x2(x, e):
    hi = x.astype(BF16)
    mid = (x - hi.astype(F32)).astype(BF16)
    return _dot(hi, e) + _dot(mid, e)


def _dot_3x(e, x):
    hi, mid, lo = _split3(x)
    return _dot(e, hi) + _dot(e, mid) + _dot(e, lo)


def _dot_f32(a, b):
    a0, a1, a2 = _split3(a)
    b0, b1, b2 = _split3(b)
    return (_dot(a0, b0) + (_dot(a0, b1) + _dot(a1, b0))
            + (_dot(a1, b1) + _dot(a0, b2) + _dot(a2, b0)))


def _sigmoid(x):
    return 1.0 / (1.0 + jnp.exp(-x))


def _silu(x):
    return x * _sigmoid(x)


def _iota(shape, axis):
    return lax.broadcasted_iota(jnp.int32, shape, axis)


def _ada_kernel(c_ref, w_ref, b_ref, o_ref):
    a = _silu(c_ref[...]).astype(BF16)
    o_ref[0] = _dot(a, w_ref[0].astype(BF16)) + b_ref[0]


def _ada_mod(cond, w_ada, b_ada):
    depth, _, n = w_ada.shape
    rows = cond.shape[0]
    return pl.pallas_call(
        _ada_kernel,
        grid=(depth, n // ADA_TILE),
        in_specs=[pl.BlockSpec((rows, D), lambda l, j: (0, 0)),
                  pl.BlockSpec((1, D, ADA_TILE), lambda l, j: (l, 0, j)),
                  pl.BlockSpec((1, 1, ADA_TILE), lambda l, j: (l, 0, j))],
        out_specs=pl.BlockSpec((1, rows, ADA_TILE), lambda l, j: (l, 0, j)),
        out_shape=jax.ShapeDtypeStruct((depth, rows, n), F32),
        compiler_params=_cparams(2),
        name="ada_mod",
    )(cond, w_ada, b_ada.reshape(depth, 1, n))


def _mod_spec(mod, n_tokens, tile, n_axes=1):
    per_row = n_tokens // mod.shape[0]
    if n_axes == 1:
        return pl.BlockSpec((1, 1, 6 * D), lambda i: ((i * tile) // per_row, 0, 0))
    return pl.BlockSpec((1, 1, 6 * D), lambda i, e: ((i * tile) // per_row, 0, 0))


def _norm_mod(x, g, shift, scale):
    ms = jnp.mean(x * x, axis=-1, keepdims=True)
    y = x * lax.rsqrt(ms + EPS) * g
    return y * (1.0 + scale) + shift


def _inproj_kernel(x_ref, mod_ref, g_ref, w_ref, *o_refs, widths):
    m = mod_ref[0]
    h = _norm_mod(x_ref[...], g_ref[...], m[:, 0:D], m[:, D:2 * D]).astype(BF16)
    off = 0
    for o_ref, wd in zip(o_refs, widths):
        o_ref[...] = _dot(h, w_ref[:, off:off + wd]).astype(o_ref.dtype)
        off += wd


def _inproj(x, mod, g, w, widths):
    t = x.shape[0]
    n = w.shape[1]
    return pl.pallas_call(
        functools.partial(_inproj_kernel, widths=widths),
        grid=(t // TOKEN_TILE,),
        in_specs=[pl.BlockSpec((TOKEN_TILE, D), lambda i: (i, 0)),
                  _mod_spec(mod, t, TOKEN_TILE),
                  pl.BlockSpec((1, D), lambda i: (0, 0)),
                  pl.BlockSpec((D, n), lambda i: (0, 0))],
        out_specs=[pl.BlockSpec((TOKEN_TILE, wd), lambda i: (i, 0)) for wd in widths],
        out_shape=[jax.ShapeDtypeStruct((t, wd), F32) for wd in widths],
        compiler_params=_cparams(1),
        name="inproj",
    )(x, mod, g, w)


def _ssd_kernel(*refs, seq, has_h0, want_state):
    it = iter(refs)
    z_ref, xbc_ref, dt_ref, cw_ref, cb_ref, hp_ref, nw_ref = [next(it) for _ in range(7)]
    h0_refs = (next(it), next(it)) if has_h0 else None
    y_ref = next(it)
    s_refs = (next(it), next(it)) if want_state else None
    xs_ref, bc_ref, yacc_ref, st_ref = [next(it) for _ in range(4)]

    q = SSD_CHUNK
    nc = seq // q
    unrolled = nc <= 2
    gn = SSD_GROUPS * SSD_STATE

    cw = cw_ref[...]
    cb = cb_ref[...]
    rid = _iota((q, SSD_CONV_DIM), 0)

    def rows(c):
        return pl.ds(c * q, q) if isinstance(c, int) else pl.ds(pl.multiple_of(c * q, q), q)

    def conv_chunk(c):
        xc = xbc_ref[rows(c), :]
        if isinstance(c, int):
            prev = xbc_ref[c * q - 1:c * q, :] if c > 0 else jnp.zeros((1, SSD_CONV_DIM), F32)
            nxt = xbc_ref[(c + 1) * q:(c + 1) * q + 1, :] if c < nc - 1 else jnp.zeros((1, SSD_CONV_DIM), F32)
        else:
            prev = xbc_ref[pl.ds(jnp.maximum(c * q - 1, 0), 1), :]
            nxt = xbc_ref[pl.ds(jnp.minimum(c * q + q, seq - 1), 1), :]
            prev = jnp.where(c > 0, prev, 0.0)
            nxt = jnp.where(c < nc - 1, nxt, 0.0)
        xp = jnp.where(rid == 0, prev, pltpu.roll(xc, 1, axis=0))
        xn = jnp.where(rid == q - 1, nxt, pltpu.roll(xc, q - 1, axis=0))
        v = _silu(xp * cw[0:1] + xc * cw[1:2] + xn * cw[2:3] + cb)
        xs_ref[rows(c), :] = v[:, :SSD_INNER]
        bc_ref[rows(c), :] = v[:, SSD_INNER:]

    hp = hp_ref[...]
    expand = (_iota((LANE, SSD_INNER), 1) // SSD_HEADDIM == _iota((LANE, SSD_INNER), 0)).astype(BF16)
    blockmask = (_iota((gn, SSD_INNER), 0) // SSD_STATE
                 == _iota((gn, SSD_INNER), 1) // (SSD_INNER // SSD_GROUPS))
    ri = _iota((q, q), 0)
    ci = _iota((q, q), 1)
    lane = _iota((q, LANE), 1)
    lo_half = lane < SSD_STATE
    d_e = _dot_x3(jnp.broadcast_to(hp[4:5], (8, LANE)), expand)[0:1]
    nw = nw_ref[...]

    def scan_chunk(c, fwd, a, bias, causal, tri):
        xs = xs_ref[rows(c), :]
        bc = bc_ref[rows(c), :]
        bm = bc[:, :gn]
        cm = bc[:, gn:]
        raw = dt_ref[rows(c), :] + bias
        dt = jnp.maximum(raw, 0.0) + jnp.log1p(jnp.exp(-jnp.abs(raw)))
        cum = _dot_3x(tri, dt * a)
        cum_t = cum.T
        dt_t = dt.T
        last = cum[q - 1:q] if fwd else cum[0:1]
        ecum = jnp.exp(cum)
        ex = _dot_x2(jnp.concatenate([ecum, dt * jnp.exp(last - cum)], axis=0), expand)
        ecum_e, w_e = ex[0:q], ex[q:2 * q]
        cdec_e = ecum_e[q - 1:q] if fwd else ecum_e[0:1]
        st = st_ref[...]
        cmb = cm.astype(BF16)
        bmb = bm.astype(BF16)
        y = _dot(cmb, st.astype(BF16)) * ecum_e
        cb0 = _dot_nt(jnp.where(lo_half, cm, 0.0).astype(BF16), bmb)
        cb1 = _dot_nt(jnp.where(lo_half, 0.0, cm).astype(BF16), bmb)
        pairs = []
        for j in range(SSD_HEADS // 2):
            cbg = cb0 if j < SSD_HEADS // 4 else cb1
            xp = xs[:, j * LANE:(j + 1) * LANE]
            scs = []
            for half in range(2):
                h = 2 * j + half
                seg = cum[:, h:h + 1] - cum_t[h:h + 1, :]
                dec = jnp.exp(jnp.where(causal, seg, -jnp.inf))
                scs.append((cbg * dec * dt_t[h:h + 1, :]).astype(BF16))
            rhs = jnp.concatenate([jnp.where(lo_half, xp, 0.0), jnp.where(lo_half, 0.0, xp)], axis=0)
            pairs.append(_dot(jnp.concatenate(scs, axis=1), rhs.astype(BF16)))
        y = y + jnp.concatenate(pairs, axis=1)
        new = _dot(bm.T.astype(BF16), (xs * w_e).astype(BF16))
        st_ref[...] = jnp.where(blockmask, st * cdec_e + new, 0.0)
        if fwd:
            yacc_ref[rows(c), :] = y
        else:
            y = y + yacc_ref[rows(c), :] + xs * d_e
            y = y * _silu(z_ref[rows(c), :])
            ms = jnp.mean(y * y, axis=-1, keepdims=True)
            y_ref[rows(c), :] = (y * lax.rsqrt(ms + EPS) * nw).astype(y_ref.dtype)

    def scan_pass(fwd):
        a = -jnp.exp(hp[0:1] if fwd else hp[1:2])
        bias = hp[2:3] if fwd else hp[3:4]
        causal = (ri >= ci) if fwd else (ri <= ci)
        tri = causal.astype(BF16)
        if has_h0:
            st_ref[...] = h0_refs[0 if fwd else 1][0]
        else:
            st_ref[...] = jnp.zeros((gn, SSD_INNER), F32)
        if unrolled:
            for k in range(nc):
                scan_chunk(k if fwd else nc - 1 - k, fwd, a, bias, causal, tri)
        else:
            def body(k, carry):
                c = k if fwd else nc - 1 - k
                scan_chunk(c, fwd, a, bias, causal, tri)
                if fwd:
                    conv_chunk(jnp.minimum(c + 1, nc - 1))
                return carry

            lax.fori_loop(0, nc, body, 0)
        if want_state:
            for j in range(SSD_INNER // LANE):
                g = (j * LANE) // (SSD_INNER // SSD_GROUPS)
                t = st_ref[:, j * LANE:(j + 1) * LANE].T
                s_refs[0 if fwd else 1][0, j * LANE:(j + 1) * LANE, :] = t[:, g * SSD_STATE:(g + 1) * SSD_STATE]

    if unrolled:
        for c in range(nc):
            conv_chunk(c)
    else:
        conv_chunk(0)
    scan_pass(True)
    scan_pass(False)


def _ssd(z, xbc, dt, cw, cb, hp, nw, h0, *, seq):
    t = z.shape[0]
    n_seq = t // seq
    has_h0 = h0 is not None
    want_state = not has_h0
    gn = SSD_GROUPS * SSD_STATE
    tok = lambda w: pl.BlockSpec((seq, w), lambda b: (b, 0))
    full = lambda a: pl.BlockSpec(a.shape, lambda b: (0,) * a.ndim)
    in_specs = [tok(SSD_INNER), tok(SSD_CONV_DIM), tok(LANE), full(cw), full(cb), full(hp), full(nw)]
    args = [z, xbc, dt, cw, cb, hp, nw]
    if has_h0:
        in_specs += [pl.BlockSpec((1, gn, SSD_INNER), lambda b: (b, 0, 0))] * 2
        args += list(h0)
    out_shape = [jax.ShapeDtypeStruct((t, SSD_INNER), BF16)]
    out_specs = [tok(SSD_INNER)]
    if want_state:
        out_shape += [jax.ShapeDtypeStruct((n_seq, SSD_INNER, SSD_STATE), F32)] * 2
        out_specs += [pl.BlockSpec((1, SSD_INNER, SSD_STATE), lambda b: (b, 0, 0))] * 2

    return pl.pallas_call(
        functools.partial(_ssd_kernel, seq=seq, has_h0=has_h0, want_state=want_state),
        grid=(n_seq,),
        in_specs=in_specs,
        out_specs=out_specs,
        out_shape=out_shape,
        scratch_shapes=[pltpu.VMEM((seq, SSD_INNER), F32),
                        pltpu.VMEM((seq, 2 * gn), F32),
                        pltpu.VMEM((seq, SSD_INNER), F32),
                        pltpu.VMEM((gn, SSD_INNER), F32)],
        compiler_params=_cparams(1),
        name="ssd_lat" if has_h0 else "ssd_ctx",
    )(*args)


def _softmax_pv(ss, vals):
    m = ss[0].max(axis=-1, keepdims=True)
    for s in ss[1:]:
        m = jnp.maximum(m, s.max(axis=-1, keepdims=True))
    ps = [jnp.exp2(s - m) for s in ss]
    den = ps[0].sum(axis=-1, keepdims=True)
    for p in ps[1:]:
        den = den + p.sum(axis=-1, keepdims=True)
    o = _dot(ps[0].astype(BF16), vals[0]())
    for p, v in zip(ps[1:], vals[1:]):
        o = o + _dot(p.astype(BF16), v())
    return o / den


def _attend_blocks(seq, units, emit, s_scr):
    nblk = seq // Q_BLOCK

    def scores(u, blk, static):
        q_ref, rows, keys, _ = units[u]
        r = blk * rows if static else pl.multiple_of(blk * rows, rows)
        qb = q_ref[pl.ds(r, rows), :]
        return [_dot_nt(qb, k()) for k in keys]

    if s_scr is None:
        for i in range(nblk):
            emit(i * Q_BLOCK, [_softmax_pv(scores(u, i, True), units[u][3]) for u in range(len(units))])
        return

    def put(slot, blk):
        for u in range(len(units)):
            off = 0
            for s in scores(u, blk, False):
                s_scr[slot][u][:, off:off + s.shape[1]] = s
                off += s.shape[1]

    def take(slot, blk):
        outs = []
        for u, (_, _, keys, vals) in enumerate(units):
            off = 0
            ss = []
            for k in keys:
                n = k().shape[0]
                ss.append(s_scr[slot][u][:, off:off + n])
                off += n
            outs.append(_softmax_pv(ss, vals))
        emit(pl.multiple_of(blk * Q_BLOCK, Q_BLOCK), outs)

    put(0, 0)

    def body(ii, carry):
        b0 = 2 * ii
        put(1, b0 + 1)
        take(0, b0)
        put(0, jnp.where(b0 + 2 >= nblk, 0, b0 + 2))
        take(1, b0 + 1)
        return carry

    lax.fori_loop(0, nblk // 2, body, 0)


def _mla_kernel(*refs, seq, lat):
    it = iter(refs)
    qa_ref, kv_ref, qan_ref, kvn_ref, wq_ref, wk_ref, wv_ref, qn_ref, kn_ref = [next(it) for _ in range(9)]
    if lat:
        cckv_ref, ckpe_ref, cos_ref, sin_ref = [next(it) for _ in range(4)]
    o_ref = next(it)
    if not lat:
        ckv_out, kpe_out = next(it), next(it)
    q_s, k_s, v_s, qh0_s, qh1_s, kh0_s, kh1_s = [next(it) for _ in range(7)]
    s_scr = None
    if lat:
        kc_s, vc_s, kch0_s, kch1_s = [next(it) for _ in range(4)]
        s_scr = [[next(it), next(it)], [next(it), next(it)]]

    kv = kv_ref[...]
    kpe = kv[:, LANE:]
    ckv_raw = kv[:, :LANE]
    ckv = ckv_raw * lax.rsqrt(jnp.mean(ckv_raw * ckv_raw, axis=-1, keepdims=True) + EPS) * kvn_ref[...]
    qa = qa_ref[...]
    qa = qa * lax.rsqrt(jnp.mean(qa * qa, axis=-1, keepdims=True) + EPS) * qan_ref[...]
    if not lat:
        ckv_out[...] = ckv
        kpe_out[...] = kpe
    ckvb = ckv.astype(BF16)
    q_s[...] = _dot(qa.astype(BF16), wq_ref[...])
    k_s[...] = _dot(ckvb, wk_ref[...])
    v_s[...] = _dot(ckvb, wv_ref[...]).astype(BF16)
    if lat:
        ccb = cckv_ref[0].astype(BF16)
        kc_s[...] = _dot(ccb, wk_ref[...])
        vc_s[...] = _dot(ccb, wv_ref[...]).astype(BF16)
        ckpe = ckpe_ref[0]
        cos = cos_ref[...]
        sin = sin_ref[...]
        dist = MLA_ROPE // 4
        li = _iota((2 * LANE, 2 * LANE), 0)
        lj = _iota((2 * LANE, 2 * LANE), 1)
        swap = (li == jnp.where((lj % (2 * dist)) < dist, lj + dist, lj - dist)).astype(BF16)
    qn = qn_ref[...]
    kn = kn_ref[...]
    lo_lanes = _iota((Q_BLOCK, LANE), 1) < MLA_V

    def head_norm(x, g):
        return x * lax.rsqrt(jnp.sum(x * x, axis=-1, keepdims=True) * (1.0 / MLA_QK) + EPS) * g

    def rotate(x2):
        return x2 * cos + _dot_x2(x2, swap) * sin

    for j in range(MLA_HEADS // 2):
        pair = slice(j * LANE, (j + 1) * LANE)
        sls = [slice((2 * j + half) * LANE, (2 * j + half + 1) * LANE) for half in range(2)]
        q2 = jnp.concatenate([head_norm(q_s[:, sl], qn) for sl in sls], axis=1)
        k2 = jnp.concatenate([head_norm(k_s[:, sl] + kpe, kn) for sl in sls], axis=1)
        if lat:
            q2 = rotate(q2)
            k2 = rotate(k2)
            for half, kch_s in enumerate((kch0_s, kch1_s)):
                kch_s[...] = head_norm(kc_s[:, sls[half]] + ckpe, kn).astype(BF16)
        for half, (qh_s, kh_s) in enumerate(((qh0_s, kh0_s), (qh1_s, kh1_s))):
            qh_s[...] = q2[:, half * LANE:(half + 1) * LANE].astype(BF16)
            kh_s[...] = k2[:, half * LANE:(half + 1) * LANE].astype(BF16)

        def emit(r0, outs, pair=pair):
            o_ref[pl.ds(r0, Q_BLOCK), pair] = jnp.where(lo_lanes, outs[0], outs[1]).astype(o_ref.dtype)

        units = []
        for half, (qh_s, kh_s) in enumerate(((qh0_s, kh0_s), (qh1_s, kh1_s))):
            keys = [lambda r=kh_s: r[...]]
            vals = [lambda pair=pair: v_s[:, pair]]
            if lat:
                keys = [lambda r=(kch0_s, kch1_s)[half]: r[...]] + keys
                vals = [lambda pair=pair: vc_s[:, pair]] + vals
            units.append((qh_s, Q_BLOCK, keys, vals))
        _attend_blocks(seq, units, emit, s_scr)


def _mla(qa, kv, qan, kvn, wq, wk, wv, qn, kn, cache, tables, *, seq):
    t = qa.shape[0]
    n_seq = t // seq
    lat = cache is not None
    hv = MLA_HEADS * MLA_V
    tok = lambda w: pl.BlockSpec((seq, w), lambda b: (b, 0))
    full = lambda a: pl.BlockSpec(a.shape, lambda b: (0,) * a.ndim)
    args = [qa, kv, qan, kvn, wq, wk, wv, qn, kn]
    in_specs = [tok(MLA_Q_LORA), tok(2 * LANE)] + [full(a) for a in args[2:]]
    past = 0
    if lat:
        past = cache[0].shape[1]
        in_specs += [pl.BlockSpec((1, past, LANE), lambda b: (b, 0, 0))] * 2 + [full(tables[0]), full(tables[1])]
        args += [cache[0], cache[1], tables[0], tables[1]]
    out_shape = [jax.ShapeDtypeStruct((t, hv), BF16)]
    out_specs = [tok(hv)]
    if not lat:
        out_shape += [jax.ShapeDtypeStruct((t, LANE), F32)] * 2
        out_specs += [pl.BlockSpec((seq, LANE), lambda b: (b, 0))] * 2
    scratch = [pltpu.VMEM((seq, MLA_HEADS * LANE), F32),
               pltpu.VMEM((seq, MLA_HEADS * LANE), F32),
               pltpu.VMEM((seq, hv), BF16)] + [pltpu.VMEM((seq, LANE), BF16)] * 4
    if lat:
        scratch += [pltpu.VMEM((past, MLA_HEADS * LANE), F32),
                    pltpu.VMEM((past, hv), BF16),
                    pltpu.VMEM((past, LANE), BF16),
                    pltpu.VMEM((past, LANE), BF16)]
        scratch += [pltpu.VMEM((Q_BLOCK, past + seq), F32)] * 4

    return pl.pallas_call(
        functools.partial(_mla_kernel, seq=seq, lat=lat),
        grid=(n_seq,),
        in_specs=in_specs,
        out_specs=out_specs,
        out_shape=out_shape,
        scratch_shapes=scratch,
        compiler_params=_cparams(1),
        name="mla_lat" if lat else "mla_ctx",
    )(*args)


def _gqa_kernel(*refs, seq, lat):
    it = iter(refs)
    q_ref, k_ref, v_ref, qn_ref, kn_ref = [next(it) for _ in range(5)]
    if lat:
        ck_ref, cv_ref, cos_ref, sin_ref = [next(it) for _ in range(4)]
    o_ref = next(it)
    if not lat:
        kn_out = next(it)
    kn_s, vb_s, q2_s = [next(it) for _ in range(3)]
    s_scr = None
    if lat:
        ckb_s, cvb_s = next(it), next(it)
        s_scr = [[next(it)], [next(it)]]

    hd = GQA_HEAD_DIM
    nq = GQA_HEADS * hd
    nk = GQA_KV_HEADS * LANE
    nblk = seq // Q_BLOCK
    wide = 2 * LANE
    li = _iota((wide, wide), 0)
    lj = _iota((wide, wide), 1)
    same_head = (li // hd == lj // hd).astype(BF16)
    if lat:
        dist = hd // 4
        swap = (li == jnp.where((lj % (2 * dist)) < dist, lj + dist, lj - dist)).astype(BF16)
        cos = cos_ref[...]
        sin = sin_ref[...]
    lo_q = _iota((Q_BLOCK, LANE), 1) < hd

    def prepared(x_ref, g_ref, b):
        sl = slice(b * wide, (b + 1) * wide)
        x = x_ref[:, sl]
        ss = _dot_x2(x * x, same_head)
        xn = x * lax.rsqrt(ss * (1.0 / hd) + EPS) * g_ref[:, sl]
        if not lat:
            return xn, xn
        return xn, xn * cos + _dot_x2(xn, swap) * sin

    for b in range(nk // wide):
        sl = slice(b * wide, (b + 1) * wide)
        kn, kr = prepared(k_ref, kn_ref, b)
        if not lat:
            kn_out[:, sl] = kn
        kn_s[:, sl] = kr.astype(BF16)
    if lat:
        ckb_s[...] = ck_ref[0].astype(BF16)
        cvb_s[...] = cv_ref[0].astype(BF16)
    vb_s[...] = v_ref[...].astype(BF16)
    for b in range(nq // wide):
        _, qr = prepared(q_ref, qn_ref, b)
        for half in range(wide // LANE):
            p = b * (wide // LANE) + half
            for i in range(nblk):
                blk = qr[i * Q_BLOCK:(i + 1) * Q_BLOCK, half * LANE:(half + 1) * LANE]
                q2_s[p, 2 * i * Q_BLOCK:(2 * i + 1) * Q_BLOCK, :] = jnp.where(lo_q, blk, 0.0).astype(BF16)
                q2_s[p, (2 * i + 1) * Q_BLOCK:(2 * i + 2) * Q_BLOCK, :] = jnp.where(lo_q, 0.0, blk).astype(BF16)

    for j in range(GQA_HEADS // 2):
        pair = slice(j * LANE, (j + 1) * LANE)
        g = (2 * j) // (GQA_HEADS // GQA_KV_HEADS)
        gs = slice(g * LANE, (g + 1) * LANE)

        def emit(r0, outs, pair=pair):
            o = outs[0]
            o_ref[pl.ds(r0, Q_BLOCK), pair] = jnp.where(lo_q, o[:Q_BLOCK], o[Q_BLOCK:]).astype(o_ref.dtype)

        keys = [lambda gs=gs: kn_s[:, gs]]
        vals = [lambda gs=gs: vb_s[:, gs]]
        if lat:
            keys = [lambda gs=gs: ckb_s[:, gs]] + keys
            vals = [lambda gs=gs: cvb_s[:, gs]] + vals
        _attend_blocks(seq, [(q2_s.at[j], 2 * Q_BLOCK, keys, vals)], emit, s_scr)


def _gqa(q, k, v, qn, kn, cache, tables, *, seq):
    t = q.shape[0]
    n_seq = t // seq
    lat = cache is not None
    nq = GQA_HEADS * GQA_HEAD_DIM
    nk = GQA_KV_HEADS * LANE
    tok = lambda w: pl.BlockSpec((seq, w), lambda b: (b, 0))
    full = lambda a: pl.BlockSpec(a.shape, lambda b: (0,) * a.ndim)
    args = [q, k, v, qn, kn]
    in_specs = [tok(nq), tok(nk), tok(nk), full(qn), full(kn)]
    past = 0
    if lat:
        past = cache[0].shape[1]
        in_specs += [pl.BlockSpec((1, past, nk), lambda b: (b, 0, 0))] * 2 + [full(tables[0]), full(tables[1])]
        args += [cache[0], cache[1], tables[0], tables[1]]
    out_shape = [jax.ShapeDtypeStruct((t, nq), BF16)]
    out_specs = [tok(nq)]
    if not lat:
        out_shape += [jax.ShapeDtypeStruct((t, nk), F32)]
        out_specs += [pl.BlockSpec((seq, nk), lambda b: (b, 0))]
    scratch = [pltpu.VMEM((seq, nk), BF16),
               pltpu.VMEM((seq, nk), BF16),
               pltpu.VMEM((GQA_HEADS // 2, 2 * seq, LANE), BF16)]
    if lat:
        scratch += [pltpu.VMEM((past, nk), BF16), pltpu.VMEM((past, nk), BF16)]
        scratch += [pltpu.VMEM((2 * Q_BLOCK, past + seq), F32)] * 2

    return pl.pallas_call(
        functools.partial(_gqa_kernel, seq=seq, lat=lat),
        grid=(n_seq,),
        in_specs=in_specs,
        out_specs=out_specs,
        out_shape=out_shape,
        scratch_shapes=scratch,
        compiler_params=_cparams(1),
        name="gqa_lat" if lat else "gqa_ctx",
    )(*args)


def _route(sel, sc):
    gs = []
    for g in range(N_GROUPS):
        r = sel[g * GROUP_SIZE:(g + 1) * GROUP_SIZE]
        best = None
        for i in range(GROUP_SIZE):
            for j in range(i + 1, GROUP_SIZE):
                s = r[i] + r[j]
                best = s if best is None else jnp.maximum(best, s)
        gs.append(best)
    cur = gs[0]
    grp = jnp.zeros_like(cur, dtype=jnp.int32)
    for g in range(1, N_GROUPS):
        upd = gs[g] > cur
        grp = jnp.where(upd, g, grp)
        cur = jnp.where(upd, gs[g], cur)

    def pick(rows, i):
        v = rows[i]
        for g in range(1, N_GROUPS):
            v = jnp.where(grp == g, rows[g * GROUP_SIZE + i], v)
        return v

    cand = [pick(sel, i) for i in range(GROUP_SIZE)]
    csc = [pick(sc, i) for i in range(GROUP_SIZE)]
    v1, i1, w1 = cand[0], jnp.zeros_like(grp), csc[0]
    for i in range(1, GROUP_SIZE):
        upd = cand[i] > v1
        v1 = jnp.where(upd, cand[i], v1)
        i1 = jnp.where(upd, i, i1)
        w1 = jnp.where(upd, csc[i], w1)
    v2 = i2 = w2 = None
    for i in range(GROUP_SIZE):
        ok = i1 != i
        if v2 is None:
            v2 = jnp.where(ok, cand[i], -jnp.inf)
            i2 = jnp.zeros_like(grp)
            w2 = csc[i]
        else:
            upd = ok & (cand[i] > v2)
            v2 = jnp.where(upd, cand[i], v2)
            i2 = jnp.where(upd, i, i2)
            w2 = jnp.where(upd, csc[i], w2)
    tot = w1 + w2
    local = [jnp.where(i1 == i, w1 / tot, 0.0) + jnp.where(i2 == i, w2 / tot, 0.0)
             for i in range(GROUP_SIZE)]
    onehot = [jnp.where(grp == g, 1.0, 0.0) for g in range(N_GROUPS)]
    return local, onehot


def _outproj_kernel(*refs, n_y):
    x_ref, mod_ref = refs[0], refs[1]
    y_refs = refs[2:2 + n_y]
    w_ref, g_ref, wr_ref, rb_ref, x1_ref, h2_ref, gate_ref, cnt_ref = refs[2 + n_y:]
    m = mod_ref[0]
    off = 0
    out = None
    for y_ref in y_refs:
        k = y_ref.shape[1]
        part = _dot(y_ref[...], w_ref[off:off + k, :])
        out = part if out is None else out + part
        off += k
    x1 = x_ref[...] + m[:, 2 * D:3 * D] * out
    x1_ref[...] = x1
    h2 = _norm_mod(x1, g_ref[...], m[:, 3 * D:4 * D], m[:, 4 * D:5 * D])
    h2_hi = h2.astype(BF16)
    h2_ref[...] = h2_hi
    h2_mid = (h2 - h2_hi.astype(F32)).astype(BF16)
    parts = _dot(h2_hi, wr_ref[...]) + _dot(h2_mid, wr_ref[...])
    logits = (parts + pltpu.roll(parts, LANE - N_EXPERTS, axis=1)
              + pltpu.roll(parts, LANE - 2 * N_EXPERTS, axis=1))
    sc_t = _sigmoid(logits.T[0:N_EXPERTS, :])
    sel_t = sc_t + rb_ref[...]
    local, onehot = _route([sel_t[e:e + 1] for e in range(N_EXPERTS)],
                           [sc_t[e:e + 1] for e in range(N_EXPERTS)])
    t = x1.shape[0]
    rows_t = (local + [jnp.zeros((GROUP_LANE - GROUP_SIZE, t), F32)] + onehot
              + [jnp.zeros((LANE - GROUP_LANE - N_GROUPS, t), F32)])
    routed = jnp.concatenate(rows_t, axis=0).T
    gate_ref[...] = routed
    cnt = jnp.sum(routed, axis=0, keepdims=True).astype(jnp.int32)
    cnt_ref[0] = jnp.broadcast_to(cnt, (8, LANE))


def _outproj(x, mod, ys, w, g, wr, rb):
    t = x.shape[0]
    full = lambda a: pl.BlockSpec(a.shape, lambda i: (0,) * a.ndim)
    return pl.pallas_call(
        functools.partial(_outproj_kernel, n_y=len(ys)),
        grid=(t // TOKEN_TILE,),
        in_specs=[pl.BlockSpec((TOKEN_TILE, D), lambda i: (i, 0)),
                  _mod_spec(mod, t, TOKEN_TILE)]
                 + [pl.BlockSpec((TOKEN_TILE, y.shape[1]), lambda i: (i, 0)) for y in ys]
                 + [full(w), full(g), full(wr), full(rb)],
        out_specs=[pl.BlockSpec((TOKEN_TILE, D), lambda i: (i, 0)),
                   pl.BlockSpec((TOKEN_TILE, D), lambda i: (i, 0)),
                   pl.BlockSpec((TOKEN_TILE, LANE), lambda i: (i, 0)),
                   pl.BlockSpec((1, 8, LANE), lambda i: (i, 0, 0))],
        out_shape=[jax.ShapeDtypeStruct((t, D), F32),
                   jax.ShapeDtypeStruct((t, D), BF16),
                   jax.ShapeDtypeStruct((t, LANE), F32),
                   jax.ShapeDtypeStruct((t // TOKEN_TILE, 8, LANE), jnp.int32)],
        compiler_params=_cparams(1),
        name="outproj_router",
    )(x, mod, *ys, w, g, wr, rb)


def _pad_to(n, shift):
    return ((n + (1 << shift) - 1) >> shift) << shift


SEG_SHIFT = SEG_ALIGN.bit_length() - 1
TILE_SHIFT = EXPERT_TILE.bit_length() - 1


def _start_segments(plan_ref, i, nt, make):
    for g in range(N_GROUPS):
        n = plan_ref[i * N_GROUPS + g]
        lo = plan_ref[(nt + i) * N_GROUPS + g]
        base = plan_ref[(2 * nt + i) * N_GROUPS + g]

        def body(c, carry, lo=lo, base=base):
            for cp in make(pl.multiple_of(lo + c * SEG_ALIGN, SEG_ALIGN),
                           pl.multiple_of(base + c * SEG_ALIGN, SEG_ALIGN)):
                cp.start()
            return carry

        lax.fori_loop(0, (n + SEG_ALIGN - 1) >> SEG_SHIFT, body, 0)


def _wait_segments(plan_ref, i, make):
    total = jnp.int32(0)
    for g in range(N_GROUPS):
        total = total + ((plan_ref[i * N_GROUPS + g] + SEG_ALIGN - 1) >> SEG_SHIFT)

    def body(c, carry):
        for cp in make(0, 0):
            cp.wait()
        return carry

    lax.fori_loop(0, total, body, 0)


def _dispatch_kernel(cnt_ref, h_ref, r_ref, xs_hbm, rs_hbm, dest_ref, plan_ref,
                     xs_l, rs_l, zx, zr, fill_s, sem, *, nt, n_rows):
    i = pl.program_id(0)
    t = TOKEN_TILE
    k_tiles = n_rows // EXPERT_TILE

    @pl.when(i == 0)
    def _():
        zx[...] = jnp.zeros_like(zx)
        zr[...] = jnp.zeros_like(zr)
        tots = []
        for g in range(N_GROUPS):
            tots.append(lax.fori_loop(
                0, nt, lambda tt, acc, g=g: acc + _pad_to(cnt_ref[tt * N_GROUPS + g], SEG_SHIFT), jnp.int32(0)))
        starts = []
        cur = jnp.int32(0)
        for g in range(N_GROUPS):
            starts.append(cur)
            fill_s[2 * g] = cur + tots[g]
            cur = cur + _pad_to(tots[g], TILE_SHIFT)
            fill_s[2 * g + 1] = cur
        fill_s[2 * N_GROUPS] = cur
        fill_s[2 * N_GROUPS + 1] = n_rows

        def per_tile(tt, run):
            lo = jnp.int32(0)
            new = []
            for g in range(N_GROUPS):
                n = cnt_ref[tt * N_GROUPS + g]
                plan_ref[tt * N_GROUPS + g] = n
                plan_ref[(nt + tt) * N_GROUPS + g] = lo
                plan_ref[(2 * nt + tt) * N_GROUPS + g] = run[g]
                seg = _pad_to(n, SEG_SHIFT)
                lo = lo + seg
                new.append(run[g] + seg)
            return tuple(new)

        lax.fori_loop(0, nt, per_tile, tuple(starts))
        for k in range(k_tiles):
            row = k * EXPERT_TILE
            grp = 0
            for g in range(1, N_GROUPS):
                grp = grp + (row >= starts[g]).astype(jnp.int32)
            plan_ref[3 * nt * N_GROUPS + k] = grp
        plan_ref[3 * nt * N_GROUPS + k_tiles] = cur >> TILE_SHIFT

    routed = r_ref[...]
    lane = _iota((t, LANE), 1)
    is_grp = (lane >= GROUP_LANE) & (lane < GROUP_LANE + N_GROUPS)
    onehot = jnp.where(is_grp, routed, 0.0)
    tril = (_iota((t, t), 0) >= _iota((t, t), 1)).astype(BF16)
    rank = _dot(tril, onehot.astype(BF16))
    seg_lo = jnp.zeros((1, LANE), F32)
    lane1 = _iota((1, LANE), 1)
    for g in range(N_GROUPS):
        lo = plan_ref[(nt + i) * N_GROUPS + g]
        seg_lo = jnp.where(lane1 == GROUP_LANE + g, (lo - 1).astype(F32), seg_lo)
    val = jnp.where(onehot > 0.0, rank + seg_lo, 0.0)
    dest = _dot_x2(val, jnp.ones((LANE, LANE), BF16))
    dest_ref[...] = dest
    dest_row = dest.T[0:1, :].astype(jnp.int32)
    perm = jnp.where(_iota((DISPATCH_ROWS, t), 0) == dest_row, 1.0, 0.0).astype(BF16)
    slot = i % 2
    xs_l[slot] = _dot(perm, h_ref[...]).astype(BF16)
    rs_l[slot] = _dot_3x(perm, routed)

    def to_buffer(slot):
        def make(lo, base):
            return (pltpu.make_async_copy(xs_l.at[slot, pl.ds(lo, SEG_ALIGN)],
                                          xs_hbm.at[pl.ds(base, SEG_ALIGN)], sem.at[slot, 0]),
                    pltpu.make_async_copy(rs_l.at[slot, pl.ds(lo, SEG_ALIGN)],
                                          rs_hbm.at[pl.ds(base, SEG_ALIGN)], sem.at[slot, 1]))
        return make

    _start_segments(plan_ref, i, nt, to_buffer(slot))

    @pl.when(i > 0)
    def _():
        _wait_segments(plan_ref, i - 1, to_buffer(1 - slot))

    @pl.when(i == nt - 1)
    def _():
        _wait_segments(plan_ref, i, to_buffer(slot))
        def zero(base):
            return (pltpu.make_async_copy(zx, xs_hbm.at[pl.ds(base, SEG_ALIGN)], sem.at[0, 0]),
                    pltpu.make_async_copy(zr, rs_hbm.at[pl.ds(base, SEG_ALIGN)], sem.at[0, 1]))

        n_fill = jnp.int32(0)
        for r in range(N_GROUPS + 1):
            lo = fill_s[2 * r]
            chunks = (fill_s[2 * r + 1] - lo) >> SEG_SHIFT

            def body(c, carry, lo=lo):
                for cp in zero(pl.multiple_of(lo + c * SEG_ALIGN, SEG_ALIGN)):
                    cp.start()
                return carry

            lax.fori_loop(0, chunks, body, 0)
            n_fill = n_fill + chunks

        def wait_fill(c, carry):
            for cp in zero(0):
                cp.wait()
            return carry

        lax.fori_loop(0, n_fill, wait_fill, 0)


def _moe_rows(n_tokens):
    nt = n_tokens // TOKEN_TILE
    worst = n_tokens + nt * N_GROUPS * (SEG_ALIGN - 1) + N_GROUPS * (EXPERT_TILE - SEG_ALIGN)
    return _pad_to(worst, TILE_SHIFT)


def _dispatch(counts, h2, routed):
    t = h2.shape[0]
    nt = t // TOKEN_TILE
    n_rows = _moe_rows(t)
    plan_len = 3 * nt * N_GROUPS + n_rows // EXPERT_TILE + 1
    grid_spec = pltpu.PrefetchScalarGridSpec(
        num_scalar_prefetch=1,
        grid=(nt,),
        in_specs=[pl.BlockSpec((TOKEN_TILE, D), lambda i, c: (i, 0)),
                  pl.BlockSpec((TOKEN_TILE, LANE), lambda i, c: (i, 0))],
        out_specs=[pl.BlockSpec(memory_space=pl.ANY),
                   pl.BlockSpec(memory_space=pl.ANY),
                   pl.BlockSpec((TOKEN_TILE, LANE), lambda i, c: (i, 0)),
                   pl.BlockSpec(memory_space=pltpu.SMEM)],
        scratch_shapes=[pltpu.VMEM((2, DISPATCH_ROWS, D), BF16),
                        pltpu.VMEM((2, DISPATCH_ROWS, LANE), F32),
                        pltpu.VMEM((SEG_ALIGN, D), BF16),
                        pltpu.VMEM((SEG_ALIGN, LANE), F32),
                        pltpu.SMEM((2 * N_GROUPS + 2,), jnp.int32),
                        pltpu.SemaphoreType.DMA((2, 2))])
    return pl.pallas_call(
        functools.partial(_dispatch_kernel, nt=nt, n_rows=n_rows),
        grid_spec=grid_spec,
        out_shape=[jax.ShapeDtypeStruct((n_rows, D), BF16),
                   jax.ShapeDtypeStruct((n_rows, LANE), F32),
                   jax.ShapeDtypeStruct((t, LANE), F32),
                   jax.ShapeDtypeStruct((plan_len,), jnp.int32)],
        compiler_params=_cparams(1),
        name="moe_dispatch",
    )(counts, h2, routed)


def _experts_kernel(plan_ref, xs_ref, rs_ref, wg_ref, wu_ref, wd_ref, y_ref, *, used_at):
    k = pl.program_id(0)
    used = plan_ref[used_at]

    @pl.when(k < used)
    def _():
        rows = xs_ref[...]
        gates = rs_ref[...]
        acc = None
        for e in range(GROUP_SIZE):
            a = _dot(rows, wg_ref[e])
            u = _dot(rows, wu_ref[e])
            act = (_silu(a) * u * gates[:, e:e + 1]).astype(BF16)
            part = _dot(act, wd_ref[e])
            acc = part if acc is None else acc + part
        y_ref[...] = acc.astype(y_ref.dtype)

    @pl.when(k >= used)
    def _():
        y_ref[...] = jnp.zeros_like(y_ref)


def _experts(plan, xs, rs, wg, wu, wd, layer, nt):
    n_rows = xs.shape[0]
    k_tiles = n_rows // EXPERT_TILE
    at = 3 * nt * N_GROUPS
    wspec = lambda a, b: pl.BlockSpec((None, GROUP_SIZE, a, b), lambda k, p: (layer, p[at + k], 0, 0))
    grid_spec = pltpu.PrefetchScalarGridSpec(
        num_scalar_prefetch=1,
        grid=(k_tiles,),
        in_specs=[pl.BlockSpec((EXPERT_TILE, D), lambda k, p: (k, 0)),
                  pl.BlockSpec((EXPERT_TILE, LANE), lambda k, p: (k, 0)),
                  wspec(D, D_EXPERT), wspec(D, D_EXPERT), wspec(D_EXPERT, D)],
        out_specs=pl.BlockSpec((EXPERT_TILE, D), lambda k, p: (k, 0)))
    return pl.pallas_call(
        functools.partial(_experts_kernel, used_at=at + k_tiles),
        grid_spec=grid_spec,
        out_shape=jax.ShapeDtypeStruct((n_rows, D), BF16),
        compiler_params=_cparams(1),
        name="moe_experts",
    )(plan, xs, rs, wg, wu, wd)


def _combine_kernel(plan_ref, x_ref, mod_ref, dest_ref, ys_hbm, o_ref, y_l, sem, *, nt):
    i = pl.program_id(0)

    slot = i % 2

    def from_buffer(slot):
        def make(lo, base):
            return (pltpu.make_async_copy(ys_hbm.at[pl.ds(base, SEG_ALIGN)],
                                          y_l.at[slot, pl.ds(lo, SEG_ALIGN)], sem.at[slot]),)
        return make

    @pl.when(i == 0)
    def _():
        y_l[...] = jnp.zeros_like(y_l)
        _start_segments(plan_ref, i, nt, from_buffer(slot))

    @pl.when(i + 1 < nt)
    def _():
        _start_segments(plan_ref, i + 1, nt, from_buffer(1 - slot))

    _wait_segments(plan_ref, i, from_buffer(slot))
    dest = dest_ref[...].astype(jnp.int32)
    lane = _iota(dest.shape, 1)
    back = jnp.concatenate(
        [jnp.where(dest == lane + j * LANE, 1.0, 0.0).astype(BF16) for j in range(DISPATCH_ROWS // LANE)], axis=1)
    y = _dot(back, y_l[slot])
    o_ref[...] = x_ref[...] + mod_ref[0][:, 5 * D:6 * D] * y


def _combine(plan, x1, mod, dest, ys):
    t = x1.shape[0]
    nt = t // TOKEN_TILE
    per_row = t // mod.shape[0]
    grid_spec = pltpu.PrefetchScalarGridSpec(
        num_scalar_prefetch=1,
        grid=(nt,),
        in_specs=[pl.BlockSpec((TOKEN_TILE, D), lambda i, p: (i, 0)),
                  pl.BlockSpec((1, 1, 6 * D), lambda i, p: ((i * TOKEN_TILE) // per_row, 0, 0)),
                  pl.BlockSpec((TOKEN_TILE, LANE), lambda i, p: (i, 0)),
                  pl.BlockSpec(memory_space=pl.ANY)],
        out_specs=pl.BlockSpec((TOKEN_TILE, D), lambda i, p: (i, 0)),
        scratch_shapes=[pltpu.VMEM((2, DISPATCH_ROWS, D), BF16),
                        pltpu.SemaphoreType.DMA((2,))])
    return pl.pallas_call(
        functools.partial(_combine_kernel, nt=nt),
        grid_spec=grid_spec,
        out_shape=jax.ShapeDtypeStruct((t, D), F32),
        compiler_params=_cparams(1),
        name="moe_combine",
    )(plan, x1, mod, dest, ys)


def _moe(x1, mod, h2, routed, cnt, wg, wu, wd, layer):
    nt = x1.shape[0] // TOKEN_TILE
    counts = cnt[:, 0, GROUP_LANE:GROUP_LANE + N_GROUPS].reshape(nt * N_GROUPS)
    xs, rs, dest, plan = _dispatch(counts, h2, routed)
    ys = _experts(plan, xs, rs, wg, wu, wd, layer, nt)
    return _combine(plan, x1, mod, dest, ys)


def _rope_angles(pos, half):
    inv = jnp.power(ROPE_THETA, -jnp.arange(half, dtype=F32) / half)
    ang = pos.astype(F32)[:, None] * inv[None, :]
    return jnp.cos(ang), jnp.sin(ang)


def _rope_tables(seq, rot, lead, lane_tiles):
    rows = seq // GRID_W
    row = jnp.repeat(jnp.arange(rows), GRID_W)
    col = jnp.tile(jnp.arange(GRID_W), rows)
    cr, sr = _rope_angles(row, rot // 4)
    cc, sc = _rope_angles(col, rot // 4)
    cos = jnp.concatenate([cr, cr, cc, cc], axis=1)
    sin = jnp.concatenate([-sr, sr, -sc, sc], axis=1)
    width = LANE // lane_tiles
    pad = width - lead - rot
    cos = jnp.concatenate([jnp.ones((seq, lead), F32), cos, jnp.ones((seq, pad), F32)], axis=1)
    sin = jnp.concatenate([jnp.zeros((seq, lead), F32), sin, jnp.zeros((seq, pad), F32)], axis=1)
    return jnp.tile(cos, (1, lane_tiles)), jnp.tile(sin, (1, lane_tiles))


def _pad_lanes(a, width):
    return jnp.pad(a, [(0, 0)] * (a.ndim - 1) + [(0, width - a.shape[-1])])


def _even_weights(w_in, w_q_b, w_kv_b):
    z = w_in[:, 0:1024]
    xbc = w_in[:, 1024:2304]
    dt = w_in[:, 2304:2320]
    qa = w_in[:, 2320:2576]
    ckv = w_in[:, 2576:2704]
    kpe = w_in[:, 2704:2736]
    zeros = lambda n: jnp.zeros((D, n), w_in.dtype)
    w = jnp.concatenate([z, xbc, qa, ckv, zeros(MLA_NOPE), kpe, zeros(LANE - MLA_QK), dt,
                         zeros(LANE - SSD_HEADS)], axis=1).astype(BF16)
    wq = _pad_lanes(w_q_b.reshape(MLA_Q_LORA, MLA_HEADS, MLA_QK), LANE).reshape(MLA_Q_LORA, MLA_HEADS * LANE)
    kvb = w_kv_b.reshape(MLA_KV_LORA, MLA_HEADS, MLA_NOPE + MLA_V)
    wk = _pad_lanes(kvb[:, :, :MLA_NOPE], LANE).reshape(MLA_KV_LORA, MLA_HEADS * LANE)
    wv = kvb[:, :, MLA_NOPE:].reshape(MLA_KV_LORA, MLA_HEADS * MLA_V)
    return w, wq.astype(BF16), wk.astype(BF16), wv.astype(BF16)


def _dup_heads(a):
    s = a.shape[:-1]
    a = a.reshape(s + (GQA_KV_HEADS, 1, GQA_HEAD_DIM))
    return jnp.broadcast_to(a, s + (GQA_KV_HEADS, 2, GQA_HEAD_DIM)).reshape(s + (GQA_KV_HEADS * LANE,))


def _undup_heads(a):
    s = a.shape[:-1]
    return a.reshape(s + (GQA_KV_HEADS, 2, GQA_HEAD_DIM))[..., 0, :]


def _state_to_kernel(h0):
    b = h0.shape[0]
    t = jnp.transpose(h0, (0, 3, 1, 2)).reshape(b, SSD_STATE, SSD_INNER)
    half = SSD_INNER // SSD_GROUPS
    col = jnp.arange(SSD_INNER) // half
    parts = [jnp.where(col == g, t, 0.0) for g in range(SSD_GROUPS)]
    return jnp.concatenate(parts, axis=1)


def _state_from_kernel(st):
    return st.reshape(st.shape[0], SSD_HEADS, SSD_HEADDIM, SSD_STATE)


def kernel(x_prompt, x_sample, c, state_ssd_fwd, state_ssd_bwd, cache_mla_ckv, cache_mla_kpe, cache_gqa_k, cache_gqa_v, c_ctx, g_mix, g_ffn, w_ada, b_ada, w_router, router_bias, w_exp_gate, w_exp_up, w_exp_down, w_in_even, ssd_conv_w, ssd_conv_b, ssd_a_log_fwd, ssd_a_log_bwd, ssd_dt_bias_fwd, ssd_dt_bias_bwd, ssd_d, ssd_norm, mla_q_a_norm, mla_w_q_b, mla_kv_a_norm, mla_w_kv_b, mla_q_norm, mla_k_norm, w_out_even, w_in_odd, gqa_q_norm, gqa_k_norm, w_out_odd):
    nb, ls, _ = x_prompt.shape
    db, dl, _ = x_sample.shape
    depth = w_ada.shape[0]
    n_ctx = nb * ls

    xs = [x_prompt.reshape(n_ctx, D), x_sample.reshape(db * dl, D)]
    seqs = (ls, dl)
    rows = 16
    cond = jnp.concatenate([c_ctx[None, :], c, jnp.zeros((rows - 1 - db, D), F32)], axis=0)
    mod_all = _ada_mod(cond, w_ada, b_ada)

    wr = _pad_lanes(jnp.concatenate(_split3(w_router), axis=1), LANE)
    rb = router_bias.reshape(N_EXPERTS, 1)
    wg_all = w_exp_gate.astype(BF16)
    wu_all = w_exp_up.astype(BF16)
    wd_all = w_exp_down.astype(BF16)

    outs = {}
    for i in range(depth):
        j = i // 2
        mods = [mod_all[i, 0:1].reshape(1, 1, 6 * D), mod_all[i, 1:1 + db].reshape(db, 1, 6 * D)]
        g1 = g_mix[i].reshape(1, D)
        g2 = g_ffn[i].reshape(1, D)
        if i % 2 == 0:
            w, wq, wk, wv = _even_weights(w_in_even[j], mla_w_q_b[j], mla_w_kv_b[j])
            cw = jnp.pad(ssd_conv_w[j], ((0, 5), (0, 0)))
            cb = ssd_conv_b[j].reshape(1, SSD_CONV_DIM)
            hp = _pad_lanes(jnp.stack([ssd_a_log_fwd[j], ssd_a_log_bwd[j], ssd_dt_bias_fwd[j],
                                       ssd_dt_bias_bwd[j], ssd_d[j]]), LANE)
            hp = jnp.pad(hp, ((0, 3), (0, 0)))
            nw = ssd_norm[j].reshape(1, SSD_INNER)
            qan = mla_q_a_norm[j].reshape(1, MLA_Q_LORA)
            kvn = mla_kv_a_norm[j].reshape(1, MLA_KV_LORA)
            qn = _pad_lanes(mla_q_norm[j].reshape(1, MLA_QK) * (MLA_QK ** -0.5 * LOG2E), LANE)
            kn = _pad_lanes(mla_k_norm[j].reshape(1, MLA_QK), LANE)
            h0 = (_state_to_kernel(state_ssd_fwd[:, j]), _state_to_kernel(state_ssd_bwd[:, j]))
            ckpe = jnp.pad(cache_mla_kpe[:, j], ((0, 0), (0, 0), (MLA_NOPE, LANE - MLA_QK)))
            tables = tuple(jnp.tile(tb, (1, 2)) for tb in _rope_tables(dl, MLA_ROPE, MLA_NOPE, 1))
            w_out = w_out_even[j].astype(BF16)
            ys = []
            for s in range(2):
                z, xbc, qa, kv, dt = _inproj(xs[s], mods[s], g1, w,
                                             (SSD_INNER, SSD_CONV_DIM, MLA_Q_LORA, 2 * LANE, LANE))
                if s == 0:
                    y, sf, sb = _ssd(z, xbc, dt, cw, cb, hp, nw, None, seq=seqs[s])
                    o, ckv_new, kpe_new = _mla(qa, kv, qan, kvn, wq, wk, wv, qn, kn, None, None, seq=seqs[s])
                    outs.setdefault("ssd_f", []).append(_state_from_kernel(sf))
                    outs.setdefault("ssd_b", []).append(_state_from_kernel(sb))
                    outs.setdefault("ckv", []).append(ckv_new.reshape(nb, ls, MLA_KV_LORA))
                    outs.setdefault("kpe", []).append(kpe_new[:, MLA_NOPE:MLA_QK].reshape(nb, ls, MLA_ROPE))
                else:
                    y, = _ssd(z, xbc, dt, cw, cb, hp, nw, h0, seq=seqs[s])
                    o, = _mla(qa, kv, qan, kvn, wq, wk, wv, qn, kn, (cache_mla_ckv[:, j], ckpe), tables,
                              seq=seqs[s])
                ys.append((y, o))
        else:
            wi = w_in_odd[j]
            nq = GQA_HEADS * GQA_HEAD_DIM
            nkv = GQA_KV_HEADS * GQA_HEAD_DIM
            nk = GQA_KV_HEADS * LANE
            w = jnp.concatenate([wi[:, :nq], _dup_heads(wi[:, nq:nq + nkv]), _dup_heads(wi[:, nq + nkv:])],
                                axis=1).astype(BF16)
            qscale = GQA_HEAD_DIM ** -0.5 * LOG2E
            qn = jnp.tile(gqa_q_norm[j].reshape(1, GQA_HEAD_DIM) * qscale, (1, nq // GQA_HEAD_DIM))
            kn = jnp.tile(gqa_k_norm[j].reshape(1, GQA_HEAD_DIM), (1, nk // GQA_HEAD_DIM))
            cache = (_dup_heads(cache_gqa_k[:, j].reshape(db, -1, nkv)),
                     _dup_heads(cache_gqa_v[:, j].reshape(db, -1, nkv)))
            tables = tuple(jnp.tile(tb, (1, 2)) for tb in _rope_tables(dl, GQA_HEAD_DIM, 0, 2))
            w_out = w_out_odd[j].astype(BF16)
            ys = []
            for s in range(2):
                q, k, v = _inproj(xs[s], mods[s], g1, w, (nq, nk, nk))
                if s == 0:
                    o, k_new = _gqa(q, k, v, qn, kn, None, None, seq=seqs[s])
                    outs.setdefault("gk", []).append(
                        _undup_heads(k_new).reshape(nb, ls, GQA_KV_HEADS, GQA_HEAD_DIM))
                    outs.setdefault("gv", []).append(
                        _undup_heads(v).reshape(nb, ls, GQA_KV_HEADS, GQA_HEAD_DIM))
                else:
                    o, = _gqa(q, k, v, qn, kn, cache, tables, seq=seqs[s])
                ys.append((o,))
        for s in range(2):
            x1, h2, routed, cnt = _outproj(xs[s], mods[s], ys[s], w_out, g2, wr, rb)
            xs[s] = _moe(x1, mods[s], h2, routed, cnt, wg_all, wu_all, wd_all, i)

    stack = lambda key: jnp.stack(outs[key], axis=1)
    return (xs[0].reshape(nb, ls, D), xs[1].reshape(db, dl, D),
            stack("ssd_f"), stack("ssd_b"), stack("ckv"), stack("kpe"), stack("gk"), stack("gv"))
```

```python
import functools
import math

import jax
import jax.numpy as jnp
from jax import lax
from jax.experimental import pallas as pl
from jax.experimental.pallas import tpu as pltpu

F32 = jnp.float32
BF16 = jnp.bfloat16

D = 1024
EPS = 1e-6
GRID_W = 64
ROPE_THETA = 10000.0
SSD_HEADS = 16
SSD_HEADDIM = 64
SSD_INNER = 1024
SSD_STATE = 64
SSD_GROUPS = 2
SSD_CONV_DIM = 1280
SSD_CHUNK = 128
MLA_HEADS = 8
MLA_Q_LORA = 256
MLA_KV_LORA = 128
MLA_NOPE = 64
MLA_ROPE = 32
MLA_V = 64
MLA_QK = 96
GQA_HEADS = 16
GQA_KV_HEADS = 4
GQA_HEAD_DIM = 64
N_EXPERTS = 16
N_GROUPS = 4
GROUP_SIZE = 4
D_EXPERT = 256

LANE = 128
Q_BLOCK = 128
LOG2E = 1.4426950408889634
TOKEN_TILE = 512
EXPERT_TILE = 512
SEG_ALIGN = 16
GROUP_LANE = 16
DISPATCH_ROWS = TOKEN_TILE + LANE
ADA_TILE = 1536
VMEM_LIMIT = 56 * 1024 * 1024


def _cparams(n_axes):
    return pltpu.CompilerParams(dimension_semantics=("arbitrary",) * n_axes,
                                vmem_limit_bytes=VMEM_LIMIT)


def _dot(a, b):
    return jnp.dot(a, b, preferred_element_type=F32)


def _dot_nt(a, b):
    return lax.dot_general(a, b, (((1,), (1,)), ((), ())), preferred_element_type=F32)


def _split3(x):
    hi = x.astype(BF16)
    r1 = x - hi.astype(F32)
    mid = r1.astype(BF16)
    lo = (r1 - mid.astype(F32)).astype(BF16)
    return hi, mid, lo


def _dot_x3(x, e):
    hi, mid, lo = _split3(x)
    return _dot(hi, e) + _dot(mid, e) + _dot(lo, e)


def _dot_x2(x, e):
    hi = x.astype(BF16)
    mid = (x - hi.astype(F32)).astype(BF16)
    return _dot(hi, e) + _dot(mid, e)


def _dot_3x(e, x):
    hi, mid, lo = _split3(x)
    return _dot(e, hi) + _dot(e, mid) + _dot(e, lo)


def _dot_f32(a, b):
    a0, a1, a2 = _split3(a)
    b0, b1, b2 = _split3(b)
    return (_dot(a0, b0) + (_dot(a0, b1) + _dot(a1, b0))
            + (_dot(a1, b1) + _dot(a0, b2) + _dot(a2, b0)))


def _sigmoid(x):
    return 1.0 / (1.0 + jnp.exp(-x))


def _silu(x):
    return x * _sigmoid(x)


def _iota(shape, axis):
    return lax.broadcasted_iota(jnp.int32, shape, axis)


def _ada_kernel(c_ref, w_ref, b_ref, o_ref):
    a = _silu(c_ref[...]).astype(BF16)
    o_ref[0] = _dot(a, w_ref[0].astype(BF16)) + b_ref[0]


def _ada_mod(cond, w_ada, b_ada):
    depth, _, n = w_ada.shape
    rows = cond.shape[0]
    return pl.pallas_call(
        _ada_kernel,
        grid=(depth, n // ADA_TILE),
        in_specs=[pl.BlockSpec((rows, D), lambda l, j: (0, 0)),
                  pl.BlockSpec((1, D, ADA_TILE), lambda l, j: (l, 0, j)),
                  pl.BlockSpec((1, 1, ADA_TILE), lambda l, j: (l, 0, j))],
        out_specs=pl.BlockSpec((1, rows, ADA_TILE), lambda l, j: (l, 0, j)),
        out_shape=jax.ShapeDtypeStruct((depth, rows, n), F32),
        compiler_params=_cparams(2),
        name="ada_mod",
    )(cond, w_ada, b_ada.reshape(depth, 1, n))


def _mod_spec(mod, n_tokens, tile, n_axes=1):
    per_row = n_tokens // mod.shape[0]
    if n_axes == 1:
        return pl.BlockSpec((1, 1, 6 * D), lambda i: ((i * tile) // per_row, 0, 0))
    return pl.BlockSpec((1, 1, 6 * D), lambda i, e: ((i * tile) // per_row, 0, 0))


def _norm_mod(x, g, shift, scale):
    ms = jnp.mean(x * x, axis=-1, keepdims=True)
    y = x * lax.rsqrt(ms + EPS) * g
    return y * (1.0 + scale) + shift


def _inproj_kernel(x_ref, mod_ref, g_ref, w_ref, *refs, widths, pieces):
    o_refs, wb = refs[:-1], refs[-1]

    @pl.when(pl.program_id(0) == 0)
    def _():
        wb[...] = jnp.zeros_like(wb)
        for src, width, dst in pieces:
            wb[:, dst:dst + width] = w_ref[:, src:src + width].astype(BF16)

    m = mod_ref[0]
    h = _norm_mod(x_ref[...], g_ref[...], m[:, 0:D], m[:, D:2 * D]).astype(BF16)
    off = 0
    for o_ref, wd in zip(o_refs, widths):
        o_ref[...] = _dot(h, wb[:, off:off + wd]).astype(o_ref.dtype)
        off += wd


def _inproj(x, mod, g, w, widths, pieces):
    t = x.shape[0]
    return pl.pallas_call(
        functools.partial(_inproj_kernel, widths=widths, pieces=pieces),
        grid=(t // TOKEN_TILE,),
        in_specs=[pl.BlockSpec((TOKEN_TILE, D), lambda i: (i, 0)),
                  _mod_spec(mod, t, TOKEN_TILE),
                  pl.BlockSpec((1, D), lambda i: (0, 0)),
                  pl.BlockSpec(w.shape, lambda i: (0, 0))],
        out_specs=[pl.BlockSpec((TOKEN_TILE, wd), lambda i: (i, 0)) for wd in widths],
        out_shape=[jax.ShapeDtypeStruct((t, wd), F32) for wd in widths],
        scratch_shapes=[pltpu.VMEM((D, sum(widths)), BF16)],
        compiler_params=_cparams(1),
        name="inproj",
    )(x, mod, g, w)


EVEN_WIDTHS = (SSD_INNER, SSD_CONV_DIM, MLA_Q_LORA, 2 * LANE, LANE)
_XBC_END = SSD_INNER + SSD_CONV_DIM
_QA0 = _XBC_END + SSD_HEADS
_KV0 = _QA0 + MLA_Q_LORA
EVEN_PIECES = ((0, _XBC_END, 0),
               (_QA0, MLA_Q_LORA, _XBC_END),
               (_KV0, MLA_KV_LORA, _XBC_END + MLA_Q_LORA),
               (_KV0 + MLA_KV_LORA, MLA_ROPE, _XBC_END + MLA_Q_LORA + LANE + MLA_NOPE),
               (_XBC_END, SSD_HEADS, _XBC_END + MLA_Q_LORA + 2 * LANE))
_NQ = GQA_HEADS * GQA_HEAD_DIM
_NKV = GQA_KV_HEADS * GQA_HEAD_DIM
ODD_WIDTHS = (_NQ, GQA_KV_HEADS * LANE, GQA_KV_HEADS * LANE)
ODD_PIECES = ((0, _NQ, 0),) + tuple(
    (_NQ + part * _NKV + h * GQA_HEAD_DIM, GQA_HEAD_DIM,
     _NQ + part * GQA_KV_HEADS * LANE + h * LANE + rep * GQA_HEAD_DIM)
    for part in range(2) for h in range(GQA_KV_HEADS) for rep in range(2))


def _ssd_kernel(*refs, seq, has_h0, want_state):
    it = iter(refs)
    z_ref, xbc_ref, dt_ref, cw_ref, cb_ref, hp_ref, nw_ref = [next(it) for _ in range(7)]
    h0_refs = (next(it), next(it)) if has_h0 else None
    y_ref = next(it)
    s_refs = (next(it), next(it)) if want_state else None
    xs_ref, bc_ref, yacc_ref, st_ref = [next(it) for _ in range(4)]

    q = SSD_CHUNK
    nc = seq // q
    unrolled = nc <= 2
    gn = SSD_GROUPS * SSD_STATE

    cw = cw_ref[...]
    cb = cb_ref[...]
    rid = _iota((q, SSD_CONV_DIM), 0)

    def rows(c):
        return pl.ds(c * q, q) if isinstance(c, int) else pl.ds(pl.multiple_of(c * q, q), q)

    def conv_chunk(c):
        xc = xbc_ref[rows(c), :]
        if isinstance(c, int):
            prev = xbc_ref[c * q - 1:c * q, :] if c > 0 else jnp.zeros((1, SSD_CONV_DIM), F32)
            nxt = xbc_ref[(c + 1) * q:(c + 1) * q + 1, :] if c < nc - 1 else jnp.zeros((1, SSD_CONV_DIM), F32)
        else:
            prev = xbc_ref[pl.ds(jnp.maximum(c * q - 1, 0), 1), :]
            nxt = xbc_ref[pl.ds(jnp.minimum(c * q + q, seq - 1), 1), :]
            prev = jnp.where(c > 0, prev, 0.0)
            nxt = jnp.where(c < nc - 1, nxt, 0.0)
        xp = jnp.where(rid == 0, prev, pltpu.roll(xc, 1, axis=0))
        xn = jnp.where(rid == q - 1, nxt, pltpu.roll(xc, q - 1, axis=0))
        v = _silu(xp * cw[0:1] + xc * cw[1:2] + xn * cw[2:3] + cb)
        xs_ref[rows(c), :] = v[:, :SSD_INNER]
        bc_ref[rows(c), :] = v[:, SSD_INNER:]

    hp = hp_ref[...]
    expand = (_iota((LANE, SSD_INNER), 1) // SSD_HEADDIM == _iota((LANE, SSD_INNER), 0)).astype(BF16)
    blockmask = (_iota((gn, SSD_INNER), 0) // SSD_STATE
                 == _iota((gn, SSD_INNER), 1) // (SSD_INNER // SSD_GROUPS))
    ri = _iota((q, q), 0)
    ci = _iota((q, q), 1)
    lane = _iota((q, LANE), 1)
    lo_half = lane < SSD_STATE
    d_e = _dot_x3(jnp.broadcast_to(hp[4:5], (8, LANE)), expand)[0:1]
    nw = nw_ref[...]

    def scan_chunk(c, fwd, a, bias, causal, tri):
        xs = xs_ref[rows(c), :]
        bc = bc_ref[rows(c), :]
        bm = bc[:, :gn]
        cm = bc[:, gn:]
        raw = dt_ref[rows(c), :] + bias
        dt = jnp.maximum(raw, 0.0) + jnp.log1p(jnp.exp(-jnp.abs(raw)))
        cum = _dot_3x(tri, dt * a)
        cum_t = cum.T
        dt_t = dt.T
        last = cum[q - 1:q] if fwd else cum[0:1]
        ecum = jnp.exp(cum)
        ex = _dot_x2(jnp.concatenate([ecum, dt * jnp.exp(last - cum)], axis=0), expand)
        ecum_e, w_e = ex[0:q], ex[q:2 * q]
        cdec_e = ecum_e[q - 1:q] if fwd else ecum_e[0:1]
        st = st_ref[...]
        cmb = cm.astype(BF16)
        bmb = bm.astype(BF16)
        y = _dot(cmb, st.astype(BF16)) * ecum_e
        cb0 = _dot_nt(jnp.where(lo_half, cm, 0.0).astype(BF16), bmb)
        cb1 = _dot_nt(jnp.where(lo_half, 0.0, cm).astype(BF16), bmb)
        pairs = []
        for j in range(SSD_HEADS // 2):
            cbg = cb0 if j < SSD_HEADS // 4 else cb1
            xp = xs[:, j * LANE:(j + 1) * LANE]
            scs = []
            for half in range(2):
                h = 2 * j + half
                seg = cum[:, h:h + 1] - cum_t[h:h + 1, :]
                dec = jnp.exp(jnp.where(causal, seg, -jnp.inf))
                scs.append((cbg * dec * dt_t[h:h + 1, :]).astype(BF16))
            rhs = jnp.concatenate([jnp.where(lo_half, xp, 0.0), jnp.where(lo_half, 0.0, xp)], axis=0)
            pairs.append(_dot(jnp.concatenate(scs, axis=1), rhs.astype(BF16)))
        y = y + jnp.concatenate(pairs, axis=1)
        new = _dot(bm.T.astype(BF16), (xs * w_e).astype(BF16))
        st_ref[...] = jnp.where(blockmask, st * cdec_e + new, 0.0)
        if fwd:
            yacc_ref[rows(c), :] = y
        else:
            y = y + yacc_ref[rows(c), :] + xs * d_e
            y = y * _silu(z_ref[rows(c), :])
            ms = jnp.mean(y * y, axis=-1, keepdims=True)
            y_ref[rows(c), :] = (y * lax.rsqrt(ms + EPS) * nw).astype(y_ref.dtype)

    def scan_pass(fwd):
        a = -jnp.exp(hp[0:1] if fwd else hp[1:2])
        bias = hp[2:3] if fwd else hp[3:4]
        causal = (ri >= ci) if fwd else (ri <= ci)
        tri = causal.astype(BF16)
        if has_h0:
            st_ref[...] = h0_refs[0 if fwd else 1][0]
        else:
            st_ref[...] = jnp.zeros((gn, SSD_INNER), F32)
        if unrolled:
            for k in range(nc):
                scan_chunk(k if fwd else nc - 1 - k, fwd, a, bias, causal, tri)
        else:
            def body(k, carry):
                c = k if fwd else nc - 1 - k
                scan_chunk(c, fwd, a, bias, causal, tri)
                if fwd:
                    conv_chunk(jnp.minimum(c + 1, nc - 1))
                return carry

            lax.fori_loop(0, nc, body, 0)
        if want_state:
            for j in range(SSD_INNER // LANE):
                g = (j * LANE) // (SSD_INNER // SSD_GROUPS)
                t = st_ref[:, j * LANE:(j + 1) * LANE].T
                s_refs[0 if fwd else 1][0, j * LANE:(j + 1) * LANE, :] = t[:, g * SSD_STATE:(g + 1) * SSD_STATE]

    if unrolled:
        for c in range(nc):
            conv_chunk(c)
    else:
        conv_chunk(0)
    scan_pass(True)
    scan_pass(False)


def _ssd(z, xbc, dt, cw, cb, hp, nw, h0, *, seq):
    t = z.shape[0]
    n_seq = t // seq
    has_h0 = h0 is not None
    want_state = not has_h0
    gn = SSD_GROUPS * SSD_STATE
    tok = lambda w: pl.BlockSpec((seq, w), lambda b: (b, 0))
    full = lambda a: pl.BlockSpec(a.shape, lambda b: (0,) * a.ndim)
    in_specs = [tok(SSD_INNER), tok(SSD_CONV_DIM), tok(LANE), full(cw), full(cb), full(hp), full(nw)]
    args = [z, xbc, dt, cw, cb, hp, nw]
    if has_h0:
        in_specs += [pl.BlockSpec((1, gn, SSD_INNER), lambda b: (b, 0, 0))] * 2
        args += list(h0)
    out_shape = [jax.ShapeDtypeStruct((t, SSD_INNER), BF16)]
    out_specs = [tok(SSD_INNER)]
    if want_state:
        out_shape += [jax.ShapeDtypeStruct((n_seq, SSD_INNER, SSD_STATE), F32)] * 2
        out_specs += [pl.BlockSpec((1, SSD_INNER, SSD_STATE), lambda b: (b, 0, 0))] * 2

    return pl.pallas_call(
        functools.partial(_ssd_kernel, seq=seq, has_h0=has_h0, want_state=want_state),
        grid=(n_seq,),
        in_specs=in_specs,
        out_specs=out_specs,
        out_shape=out_shape,
        scratch_shapes=[pltpu.VMEM((seq, SSD_INNER), F32),
                        pltpu.VMEM((seq, 2 * gn), F32),
                        pltpu.VMEM((seq, SSD_INNER), F32),
                        pltpu.VMEM((gn, SSD_INNER), F32)],
        compiler_params=_cparams(1),
        name="ssd_lat" if has_h0 else "ssd_ctx",
    )(*args)


def _softmax_pv(ss, vals):
    m = ss[0].max(axis=-1, keepdims=True)
    for s in ss[1:]:
        m = jnp.maximum(m, s.max(axis=-1, keepdims=True))
    ps = [jnp.exp2(s - m) for s in ss]
    den = ps[0].sum(axis=-1, keepdims=True)
    for p in ps[1:]:
        den = den + p.sum(axis=-1, keepdims=True)
    o = _dot(ps[0].astype(BF16), vals[0]())
    for p, v in zip(ps[1:], vals[1:]):
        o = o + _dot(p.astype(BF16), v())
    return o / den


def _attend_blocks(seq, units, emit, s_scr):
    nblk = seq // Q_BLOCK

    def scores(u, blk, static):
        q_ref, rows, keys, _ = units[u]
        r = blk * rows if static else pl.multiple_of(blk * rows, rows)
        qb = q_ref[pl.ds(r, rows), :]
        return [_dot_nt(qb, k()) for k in keys]

    if s_scr is None:
        for i in range(nblk):
            emit(i * Q_BLOCK, [_softmax_pv(scores(u, i, True), units[u][3]) for u in range(len(units))])
        return

    def put(slot, blk):
        for u in range(len(units)):
            off = 0
            for s in scores(u, blk, False):
                s_scr[slot][u][:, off:off + s.shape[1]] = s
                off += s.shape[1]

    def take(slot, blk):
        outs = []
        for u, (_, _, keys, vals) in enumerate(units):
            off = 0
            ss = []
            for k in keys:
                n = k().shape[0]
                ss.append(s_scr[slot][u][:, off:off + n])
                off += n
            outs.append(_softmax_pv(ss, vals))
        emit(pl.multiple_of(blk * Q_BLOCK, Q_BLOCK), outs)

    put(0, 0)

    def body(ii, carry):
        b0 = 2 * ii
        put(1, b0 + 1)
        take(0, b0)
        put(0, jnp.where(b0 + 2 >= nblk, 0, b0 + 2))
        take(1, b0 + 1)
        return carry

    lax.fori_loop(0, nblk // 2, body, 0)


def _mla_kernel(*refs, seq, lat):
    it = iter(refs)
    qa_ref, kv_ref, qan_ref, kvn_ref, wq_ref, wk_ref, wv_ref, qn_ref, kn_ref = [next(it) for _ in range(9)]
    if lat:
        cckv_ref, ckpe_ref, cos_ref, sin_ref = [next(it) for _ in range(4)]
    o_ref = next(it)
    if not lat:
        ckv_out, kpe_out = next(it), next(it)
    q_s, k_s, v_s, qh0_s, qh1_s, kh0_s, kh1_s = [next(it) for _ in range(7)]
    s_scr = None
    if lat:
        kc_s, vc_s, kch0_s, kch1_s = [next(it) for _ in range(4)]
        s_scr = [[next(it), next(it)], [next(it), next(it)]]

    kv = kv_ref[...]
    kpe = kv[:, LANE:]
    ckv_raw = kv[:, :LANE]
    ckv = ckv_raw * lax.rsqrt(jnp.mean(ckv_raw * ckv_raw, axis=-1, keepdims=True) + EPS) * kvn_ref[...]
    qa = qa_ref[...]
    qa = qa * lax.rsqrt(jnp.mean(qa * qa, axis=-1, keepdims=True) + EPS) * qan_ref[...]
    if not lat:
        ckv_out[...] = ckv
        kpe_out[...] = kpe
    ckvb = ckv.astype(BF16)
    q_s[...] = _dot(qa.astype(BF16), wq_ref[...])
    k_s[...] = _dot(ckvb, wk_ref[...])
    v_s[...] = _dot(ckvb, wv_ref[...]).astype(BF16)
    if lat:
        ccb = cckv_ref[0].astype(BF16)
        kc_s[...] = _dot(ccb, wk_ref[...])
        vc_s[...] = _dot(ccb, wv_ref[...]).astype(BF16)
        ckpe = ckpe_ref[0]
        cos = cos_ref[...]
        sin = sin_ref[...]
        dist = MLA_ROPE // 4
        li = _iota((2 * LANE, 2 * LANE), 0)
        lj = _iota((2 * LANE, 2 * LANE), 1)
        swap = (li == jnp.where((lj % (2 * dist)) < dist, lj + dist, lj - dist)).astype(BF16)
    qn = qn_ref[...]
    kn = kn_ref[...]
    lo_lanes = _iota((Q_BLOCK, LANE), 1) < MLA_V

    def head_norm(x, g):
        return x * lax.rsqrt(jnp.sum(x * x, axis=-1, keepdims=True) * (1.0 / MLA_QK) + EPS) * g

    def rotate(x2):
        return x2 * cos + _dot_x2(x2, swap) * sin

    for j in range(MLA_HEADS // 2):
        pair = slice(j * LANE, (j + 1) * LANE)
        sls = [slice((2 * j + half) * LANE, (2 * j + half + 1) * LANE) for half in range(2)]
        q2 = jnp.concatenate([head_norm(q_s[:, sl], qn) for sl in sls], axis=1)
        k2 = jnp.concatenate([head_norm(k_s[:, sl] + kpe, kn) for sl in sls], axis=1)
        if lat:
            q2 = rotate(q2)
            k2 = rotate(k2)
            for half, kch_s in enumerate((kch0_s, kch1_s)):
                kch_s[...] = head_norm(kc_s[:, sls[half]] + ckpe, kn).astype(BF16)
        for half, (qh_s, kh_s) in enumerate(((qh0_s, kh0_s), (qh1_s, kh1_s))):
            qh_s[...] = q2[:, half * LANE:(half + 1) * LANE].astype(BF16)
            kh_s[...] = k2[:, half * LANE:(half + 1) * LANE].astype(BF16)

        def emit(r0, outs, pair=pair):
            o_ref[pl.ds(r0, Q_BLOCK), pair] = jnp.where(lo_lanes, outs[0], outs[1]).astype(o_ref.dtype)

        units = []
        for half, (qh_s, kh_s) in enumerate(((qh0_s, kh0_s), (qh1_s, kh1_s))):
            keys = [lambda r=kh_s: r[...]]
            vals = [lambda pair=pair: v_s[:, pair]]
            if lat:
                keys = [lambda r=(kch0_s, kch1_s)[half]: r[...]] + keys
                vals = [lambda pair=pair: vc_s[:, pair]] + vals
            units.append((qh_s, Q_BLOCK, keys, vals))
        _attend_blocks(seq, units, emit, s_scr)


def _mla(qa, kv, qan, kvn, wq, wk, wv, qn, kn, cache, tables, *, seq):
    t = qa.shape[0]
    n_seq = t // seq
    lat = cache is not None
    hv = MLA_HEADS * MLA_V
    tok = lambda w: pl.BlockSpec((seq, w), lambda b: (b, 0))
    full = lambda a: pl.BlockSpec(a.shape, lambda b: (0,) * a.ndim)
    args = [qa, kv, qan, kvn, wq, wk, wv, qn, kn]
    in_specs = [tok(MLA_Q_LORA), tok(2 * LANE)] + [full(a) for a in args[2:]]
    past = 0
    if lat:
        past = cache[0].shape[1]
        in_specs += [pl.BlockSpec((1, past, LANE), lambda b: (b, 0, 0))] * 2 + [full(tables[0]), full(tables[1])]
        args += [cache[0], cache[1], tables[0], tables[1]]
    out_shape = [jax.ShapeDtypeStruct((t, hv), BF16)]
    out_specs = [tok(hv)]
    if not lat:
        out_shape += [jax.ShapeDtypeStruct((t, LANE), F32)] * 2
        out_specs += [pl.BlockSpec((seq, LANE), lambda b: (b, 0))] * 2
    scratch = [pltpu.VMEM((seq, MLA_HEADS * LANE), F32),
               pltpu.VMEM((seq, MLA_HEADS * LANE), F32),
               pltpu.VMEM((seq, hv), BF16)] + [pltpu.VMEM((seq, LANE), BF16)] * 4
    if lat:
        scratch += [pltpu.VMEM((past, MLA_HEADS * LANE), F32),
                    pltpu.VMEM((past, hv), BF16),
                    pltpu.VMEM((past, LANE), BF16),
                    pltpu.VMEM((past, LANE), BF16)]
        scratch += [pltpu.VMEM((Q_BLOCK, past + seq), F32)] * 4

    return pl.pallas_call(
        functools.partial(_mla_kernel, seq=seq, lat=lat),
        grid=(n_seq,),
        in_specs=in_specs,
        out_specs=out_specs,
        out_shape=out_shape,
        scratch_shapes=scratch,
        compiler_params=_cparams(1),
        name="mla_lat" if lat else "mla_ctx",
    )(*args)


def _gqa_kernel(*refs, seq, lat):
    it = iter(refs)
    q_ref, k_ref, v_ref, qn_ref, kn_ref = [next(it) for _ in range(5)]
    if lat:
        ck_ref, cv_ref, cos_ref, sin_ref = [next(it) for _ in range(4)]
    o_ref = next(it)
    if not lat:
        kn_out = next(it)
    kn_s, vb_s, q2_s = [next(it) for _ in range(3)]
    s_scr = None
    if lat:
        ckb_s, cvb_s = next(it), next(it)
        s_scr = [[next(it)], [next(it)]]

    hd = GQA_HEAD_DIM
    nq = GQA_HEADS * hd
    nk = GQA_KV_HEADS * LANE
    nblk = seq // Q_BLOCK
    wide = 2 * LANE
    li = _iota((wide, wide), 0)
    lj = _iota((wide, wide), 1)
    same_head = (li // hd == lj // hd).astype(BF16)
    if lat:
        dist = hd // 4
        swap = (li == jnp.where((lj % (2 * dist)) < dist, lj + dist, lj - dist)).astype(BF16)
        cos = cos_ref[...]
        sin = sin_ref[...]
    lo_q = _iota((Q_BLOCK, LANE), 1) < hd

    def prepared(x_ref, g_ref, b):
        sl = slice(b * wide, (b + 1) * wide)
        x = x_ref[:, sl]
        ss = _dot_x2(x * x, same_head)
        xn = x * lax.rsqrt(ss * (1.0 / hd) + EPS) * g_ref[:, sl]
        if not lat:
            return xn, xn
        return xn, xn * cos + _dot_x2(xn, swap) * sin

    for b in range(nk // wide):
        sl = slice(b * wide, (b + 1) * wide)
        kn, kr = prepared(k_ref, kn_ref, b)
        if not lat:
            kn_out[:, sl] = kn
        kn_s[:, sl] = kr.astype(BF16)
    if lat:
        ckb_s[...] = ck_ref[0].astype(BF16)
        cvb_s[...] = cv_ref[0].astype(BF16)
    vb_s[...] = v_ref[...].astype(BF16)
    for b in range(nq // wide):
        _, qr = prepared(q_ref, qn_ref, b)
        for half in range(wide // LANE):
            p = b * (wide // LANE) + half
            for i in range(nblk):
                blk = qr[i * Q_BLOCK:(i + 1) * Q_BLOCK, half * LANE:(half + 1) * LANE]
                q2_s[p, 2 * i * Q_BLOCK:(2 * i + 1) * Q_BLOCK, :] = jnp.where(lo_q, blk, 0.0).astype(BF16)
                q2_s[p, (2 * i + 1) * Q_BLOCK:(2 * i + 2) * Q_BLOCK, :] = jnp.where(lo_q, 0.0, blk).astype(BF16)

    for j in range(GQA_HEADS // 2):
        pair = slice(j * LANE, (j + 1) * LANE)
        g = (2 * j) // (GQA_HEADS // GQA_KV_HEADS)
        gs = slice(g * LANE, (g + 1) * LANE)

        def emit(r0, outs, pair=pair):
            o = outs[0]
            o_ref[pl.ds(r0, Q_BLOCK), pair] = jnp.where(lo_q, o[:Q_BLOCK], o[Q_BLOCK:]).astype(o_ref.dtype)

        keys = [lambda gs=gs: kn_s[:, gs]]
        vals = [lambda gs=gs: vb_s[:, gs]]
        if lat:
            keys = [lambda gs=gs: ckb_s[:, gs]] + keys
            vals = [lambda gs=gs: cvb_s[:, gs]] + vals
        _attend_blocks(seq, [(q2_s.at[j], 2 * Q_BLOCK, keys, vals)], emit, s_scr)


def _gqa(q, k, v, qn, kn, cache, tables, *, seq):
    t = q.shape[0]
    n_seq = t // seq
    lat = cache is not None
    nq = GQA_HEADS * GQA_HEAD_DIM
    nk = GQA_KV_HEADS * LANE
    tok = lambda w: pl.BlockSpec((seq, w), lambda b: (b, 0))
    full = lambda a: pl.BlockSpec(a.shape, lambda b: (0,) * a.ndim)
    args = [q, k, v, qn, kn]
    in_specs = [tok(nq), tok(nk), tok(nk), full(qn), full(kn)]
    past = 0
    if lat:
        past = cache[0].shape[1]
        in_specs += [pl.BlockSpec((1, past, nk), lambda b: (b, 0, 0))] * 2 + [full(tables[0]), full(tables[1])]
        args += [cache[0], cache[1], tables[0], tables[1]]
    out_shape = [jax.ShapeDtypeStruct((t, nq), BF16)]
    out_specs = [tok(nq)]
    if not lat:
        out_shape += [jax.ShapeDtypeStruct((t, nk), F32)]
        out_specs += [pl.BlockSpec((seq, nk), lambda b: (b, 0))]
    scratch = [pltpu.VMEM((seq, nk), BF16),
               pltpu.VMEM((seq, nk), BF16),
               pltpu.VMEM((GQA_HEADS // 2, 2 * seq, LANE), BF16)]
    if lat:
        scratch += [pltpu.VMEM((past, nk), BF16), pltpu.VMEM((past, nk), BF16)]
        scratch += [pltpu.VMEM((2 * Q_BLOCK, past + seq), F32)] * 2

    return pl.pallas_call(
        functools.partial(_gqa_kernel, seq=seq, lat=lat),
        grid=(n_seq,),
        in_specs=in_specs,
        out_specs=out_specs,
        out_shape=out_shape,
        scratch_shapes=scratch,
        compiler_params=_cparams(1),
        name="gqa_lat" if lat else "gqa_ctx",
    )(*args)


def _route(sel, sc):
    gs = []
    for g in range(N_GROUPS):
        r = sel[g * GROUP_SIZE:(g + 1) * GROUP_SIZE]
        best = None
        for i in range(GROUP_SIZE):
            for j in range(i + 1, GROUP_SIZE):
                s = r[i] + r[j]
                best = s if best is None else jnp.maximum(best, s)
        gs.append(best)
    cur = gs[0]
    grp = jnp.zeros_like(cur, dtype=jnp.int32)
    for g in range(1, N_GROUPS):
        upd = gs[g] > cur
        grp = jnp.where(upd, g, grp)
        cur = jnp.where(upd, gs[g], cur)

    def pick(rows, i):
        v = rows[i]
        for g in range(1, N_GROUPS):
            v = jnp.where(grp == g, rows[g * GROUP_SIZE + i], v)
        return v

    cand = [pick(sel, i) for i in range(GROUP_SIZE)]
    csc = [pick(sc, i) for i in range(GROUP_SIZE)]
    v1, i1, w1 = cand[0], jnp.zeros_like(grp), csc[0]
    for i in range(1, GROUP_SIZE):
        upd = cand[i] > v1
        v1 = jnp.where(upd, cand[i], v1)
        i1 = jnp.where(upd, i, i1)
        w1 = jnp.where(upd, csc[i], w1)
    v2 = i2 = w2 = None
    for i in range(GROUP_SIZE):
        ok = i1 != i
        if v2 is None:
            v2 = jnp.where(ok, cand[i], -jnp.inf)
            i2 = jnp.zeros_like(grp)
            w2 = csc[i]
        else:
            upd = ok & (cand[i] > v2)
            v2 = jnp.where(upd, cand[i], v2)
            i2 = jnp.where(upd, i, i2)
            w2 = jnp.where(upd, csc[i], w2)
    tot = w1 + w2
    local = [jnp.where(i1 == i, w1 / tot, 0.0) + jnp.where(i2 == i, w2 / tot, 0.0)
             for i in range(GROUP_SIZE)]
    onehot = [jnp.where(grp == g, 1.0, 0.0) for g in range(N_GROUPS)]
    return local, onehot


def _outproj_kernel(*refs, n_y):
    x_ref, mod_ref = refs[0], refs[1]
    y_refs = refs[2:2 + n_y]
    w_ref, g_ref, wr_ref, rb_ref, x1_ref, h2_ref, gate_ref, cnt_ref = refs[2 + n_y:]
    m = mod_ref[0]
    off = 0
    out = None
    for y_ref in y_refs:
        k = y_ref.shape[1]
        part = _dot(y_ref[...], w_ref[off:off + k, :])
        out = part if out is None else out + part
        off += k
    x1 = x_ref[...] + m[:, 2 * D:3 * D] * out
    x1_ref[...] = x1
    h2 = _norm_mod(x1, g_ref[...], m[:, 3 * D:4 * D], m[:, 4 * D:5 * D])
    h2_hi = h2.astype(BF16)
    h2_ref[...] = h2_hi
    h2_mid = (h2 - h2_hi.astype(F32)).astype(BF16)
    parts = _dot(h2_hi, wr_ref[...]) + _dot(h2_mid, wr_ref[...])
    logits = (parts + pltpu.roll(parts, LANE - N_EXPERTS, axis=1)
              + pltpu.roll(parts, LANE - 2 * N_EXPERTS, axis=1))
    sc_t = _sigmoid(logits.T[0:N_EXPERTS, :])
    sel_t = sc_t + rb_ref[...]
    local, onehot = _route([sel_t[e:e + 1] for e in range(N_EXPERTS)],
                           [sc_t[e:e + 1] for e in range(N_EXPERTS)])
    t = x1.shape[0]
    rows_t = (local + [jnp.zeros((GROUP_LANE - GROUP_SIZE, t), F32)] + onehot
              + [jnp.zeros((LANE - GROUP_LANE - N_GROUPS, t), F32)])
    routed = jnp.concatenate(rows_t, axis=0).T
    gate_ref[...] = routed
    cnt = jnp.sum(routed, axis=0, keepdims=True).astype(jnp.int32)
    cnt_ref[0] = jnp.broadcast_to(cnt, (8, LANE))


def _outproj(x, mod, ys, w, g, wr, rb):
    t = x.shape[0]
    full = lambda a: pl.BlockSpec(a.shape, lambda i: (0,) * a.ndim)
    return pl.pallas_call(
        functools.partial(_outproj_kernel, n_y=len(ys)),
        grid=(t // TOKEN_TILE,),
        in_specs=[pl.BlockSpec((TOKEN_TILE, D), lambda i: (i, 0)),
                  _mod_spec(mod, t, TOKEN_TILE)]
                 + [pl.BlockSpec((TOKEN_TILE, y.shape[1]), lambda i: (i, 0)) for y in ys]
                 + [full(w), full(g), full(wr), full(rb)],
        out_specs=[pl.BlockSpec((TOKEN_TILE, D), lambda i: (i, 0)),
                   pl.BlockSpec((TOKEN_TILE, D), lambda i: (i, 0)),
                   pl.BlockSpec((TOKEN_TILE, LANE), lambda i: (i, 0)),
                   pl.BlockSpec((1, 8, LANE), lambda i: (i, 0, 0))],
        out_shape=[jax.ShapeDtypeStruct((t, D), F32),
                   jax.ShapeDtypeStruct((t, D), BF16),
                   jax.ShapeDtypeStruct((t, LANE), F32),
                   jax.ShapeDtypeStruct((t // TOKEN_TILE, 8, LANE), jnp.int32)],
        compiler_params=_cparams(1),
        name="outproj_router",
    )(x, mod, *ys, w, g, wr, rb)


def _pad_to(n, shift):
    return ((n + (1 << shift) - 1) >> shift) << shift


SEG_SHIFT = SEG_ALIGN.bit_length() - 1
TILE_SHIFT = EXPERT_TILE.bit_length() - 1


def _start_segments(plan_ref, i, nt, make):
    for g in range(N_GROUPS):
        n = plan_ref[i * N_GROUPS + g]
        lo = plan_ref[(nt + i) * N_GROUPS + g]
        base = plan_ref[(2 * nt + i) * N_GROUPS + g]

        def body(c, carry, lo=lo, base=base):
            for cp in make(pl.multiple_of(lo + c * SEG_ALIGN, SEG_ALIGN),
                           pl.multiple_of(base + c * SEG_ALIGN, SEG_ALIGN)):
                cp.start()
            return carry

        lax.fori_loop(0, (n + SEG_ALIGN - 1) >> SEG_SHIFT, body, 0)


def _wait_segments(plan_ref, i, make):
    total = jnp.int32(0)
    for g in range(N_GROUPS):
        total = total + ((plan_ref[i * N_GROUPS + g] + SEG_ALIGN - 1) >> SEG_SHIFT)

    def body(c, carry):
        for cp in make(0, 0):
            cp.wait()
        return carry

    lax.fori_loop(0, total, body, 0)


def _dispatch_kernel(cnt_ref, h_ref, r_ref, xs_hbm, rs_hbm, dest_ref, plan_ref,
                     xs_l, rs_l, zx, zr, fill_s, sem, *, nt, n_rows):
    i = pl.program_id(0)
    t = TOKEN_TILE
    k_tiles = n_rows // EXPERT_TILE

    @pl.when(i == 0)
    def _():
        zx[...] = jnp.zeros_like(zx)
        zr[...] = jnp.zeros_like(zr)
        tots = []
        for g in range(N_GROUPS):
            tots.append(lax.fori_loop(
                0, nt, lambda tt, acc, g=g: acc + _pad_to(cnt_ref[tt * N_GROUPS + g], SEG_SHIFT), jnp.int32(0)))
        starts = []
        cur = jnp.int32(0)
        for g in range(N_GROUPS):
            starts.append(cur)
            fill_s[2 * g] = cur + tots[g]
            cur = cur + _pad_to(tots[g], TILE_SHIFT)
            fill_s[2 * g + 1] = cur
        fill_s[2 * N_GROUPS] = cur
        fill_s[2 * N_GROUPS + 1] = n_rows

        def per_tile(tt, run):
            lo = jnp.int32(0)
            new = []
            for g in range(N_GROUPS):
                n = cnt_ref[tt * N_GROUPS + g]
                plan_ref[tt * N_GROUPS + g] = n
                plan_ref[(nt + tt) * N_GROUPS + g] = lo
                plan_ref[(2 * nt + tt) * N_GROUPS + g] = run[g]
                seg = _pad_to(n, SEG_SHIFT)
                lo = lo + seg
                new.append(run[g] + seg)
            return tuple(new)

        lax.fori_loop(0, nt, per_tile, tuple(starts))
        for k in range(k_tiles):
            row = k * EXPERT_TILE
            grp = 0
            for g in range(1, N_GROUPS):
                grp = grp + (row >= starts[g]).astype(jnp.int32)
            plan_ref[3 * nt * N_GROUPS + k] = grp
        plan_ref[3 * nt * N_GROUPS + k_tiles] = cur >> TILE_SHIFT

    routed = r_ref[...]
    lane = _iota((t, LANE), 1)
    is_grp = (lane >= GROUP_LANE) & (lane < GROUP_LANE + N_GROUPS)
    onehot = jnp.where(is_grp, routed, 0.0)
    tril = (_iota((t, t), 0) >= _iota((t, t), 1)).astype(BF16)
    rank = _dot(tril, onehot.astype(BF16))
    seg_lo = jnp.zeros((1, LANE), F32)
    lane1 = _iota((1, LANE), 1)
    for g in range(N_GROUPS):
        lo = plan_ref[(nt + i) * N_GROUPS + g]
        seg_lo = jnp.where(lane1 == GROUP_LANE + g, (lo - 1).astype(F32), seg_lo)
    val = jnp.where(onehot > 0.0, rank + seg_lo, 0.0)
    dest = _dot_x2(val, jnp.ones((LANE, LANE), BF16))
    dest_ref[...] = dest
    dest_row = dest.T[0:1, :].astype(jnp.int32)
    perm = jnp.where(_iota((DISPATCH_ROWS, t), 0) == dest_row, 1.0, 0.0).astype(BF16)
    slot = i % 2
    xs_l[slot] = _dot(perm, h_ref[...]).astype(BF16)
    rs_l[slot] = _dot_3x(perm, routed)

    def to_buffer(slot):
        def make(lo, base):
            return (pltpu.make_async_copy(xs_l.at[slot, pl.ds(lo, SEG_ALIGN)],
                                          xs_hbm.at[pl.ds(base, SEG_ALIGN)], sem.at[slot, 0]),
                    pltpu.make_async_copy(rs_l.at[slot, pl.ds(lo, SEG_ALIGN)],
                                          rs_hbm.at[pl.ds(base, SEG_ALIGN)], sem.at[slot, 1]))
        return make

    _start_segments(plan_ref, i, nt, to_buffer(slot))

    @pl.when(i > 0)
    def _():
        _wait_segments(plan_ref, i - 1, to_buffer(1 - slot))

    @pl.when(i == nt - 1)
    def _():
        _wait_segments(plan_ref, i, to_buffer(slot))
        def zero(base):
            return (pltpu.make_async_copy(zx, xs_hbm.at[pl.ds(base, SEG_ALIGN)], sem.at[0, 0]),
                    pltpu.make_async_copy(zr, rs_hbm.at[pl.ds(base, SEG_ALIGN)], sem.at[0, 1]))

        n_fill = jnp.int32(0)
        for r in range(N_GROUPS + 1):
            lo = fill_s[2 * r]
            chunks = (fill_s[2 * r + 1] - lo) >> SEG_SHIFT

            def body(c, carry, lo=lo):
                for cp in zero(pl.multiple_of(lo + c * SEG_ALIGN, SEG_ALIGN)):
                    cp.start()
                return carry

            lax.fori_loop(0, chunks, body, 0)
            n_fill = n_fill + chunks

        def wait_fill(c, carry):
            for cp in zero(0):
                cp.wait()
            return carry

        lax.fori_loop(0, n_fill, wait_fill, 0)


def _moe_rows(n_tokens):
    nt = n_tokens // TOKEN_TILE
    worst = n_tokens + nt * N_GROUPS * (SEG_ALIGN - 1) + N_GROUPS * (EXPERT_TILE - SEG_ALIGN)
    return _pad_to(worst, TILE_SHIFT)


def _dispatch(counts, h2, routed):
    t = h2.shape[0]
    nt = t // TOKEN_TILE
    n_rows = _moe_rows(t)
    plan_len = 3 * nt * N_GROUPS + n_rows // EXPERT_TILE + 1
    grid_spec = pltpu.PrefetchScalarGridSpec(
        num_scalar_prefetch=1,
        grid=(nt,),
        in_specs=[pl.BlockSpec((TOKEN_TILE, D), lambda i, c: (i, 0)),
                  pl.BlockSpec((TOKEN_TILE, LANE), lambda i, c: (i, 0))],
        out_specs=[pl.BlockSpec(memory_space=pl.ANY),
                   pl.BlockSpec(memory_space=pl.ANY),
                   pl.BlockSpec((TOKEN_TILE, LANE), lambda i, c: (i, 0)),
                   pl.BlockSpec(memory_space=pltpu.SMEM)],
        scratch_shapes=[pltpu.VMEM((2, DISPATCH_ROWS, D), BF16),
                        pltpu.VMEM((2, DISPATCH_ROWS, LANE), F32),
                        pltpu.VMEM((SEG_ALIGN, D), BF16),
                        pltpu.VMEM((SEG_ALIGN, LANE), F32),
                        pltpu.SMEM((2 * N_GROUPS + 2,), jnp.int32),
                        pltpu.SemaphoreType.DMA((2, 2))])
    return pl.pallas_call(
        functools.partial(_dispatch_kernel, nt=nt, n_rows=n_rows),
        grid_spec=grid_spec,
        out_shape=[jax.ShapeDtypeStruct((n_rows, D), BF16),
                   jax.ShapeDtypeStruct((n_rows, LANE), F32),
                   jax.ShapeDtypeStruct((t, LANE), F32),
                   jax.ShapeDtypeStruct((plan_len,), jnp.int32)],
        compiler_params=_cparams(1),
        name="moe_dispatch",
    )(counts, h2, routed)


def _experts_kernel(plan_ref, xs_ref, rs_ref, wg_ref, wu_ref, wd_ref, y_ref, wg_b, wu_b, wd_b,
                    *, group_at, used_at):
    k = pl.program_id(0)
    used = plan_ref[used_at]
    group = plan_ref[group_at + k]
    prev = plan_ref[group_at + jnp.maximum(k - 1, 0)]

    @pl.when((k == 0) | (group != prev))
    def _():
        wg_b[...] = wg_ref[...].astype(BF16)
        wu_b[...] = wu_ref[...].astype(BF16)
        wd_b[...] = wd_ref[...].astype(BF16)

    @pl.when(k < used)
    def _():
        rows = xs_ref[...]
        gates = rs_ref[...]
        acc = None
        for e in range(GROUP_SIZE):
            a = _dot(rows, wg_b[e])
            u = _dot(rows, wu_b[e])
            act = (_silu(a) * u * gates[:, e:e + 1]).astype(BF16)
            part = _dot(act, wd_b[e])
            acc = part if acc is None else acc + part
        y_ref[...] = acc.astype(y_ref.dtype)

    @pl.when(k >= used)
    def _():
        y_ref[...] = jnp.zeros_like(y_ref)


def _experts(plan, xs, rs, wg, wu, wd, layer, nt):
    n_rows = xs.shape[0]
    k_tiles = n_rows // EXPERT_TILE
    at = 3 * nt * N_GROUPS
    wspec = lambda a, b: pl.BlockSpec((None, GROUP_SIZE, a, b), lambda k, p: (layer, p[at + k], 0, 0))
    grid_spec = pltpu.PrefetchScalarGridSpec(
        num_scalar_prefetch=1,
        grid=(k_tiles,),
        in_specs=[pl.BlockSpec((EXPERT_TILE, D), lambda k, p: (k, 0)),
                  pl.BlockSpec((EXPERT_TILE, LANE), lambda k, p: (k, 0)),
                  wspec(D, D_EXPERT), wspec(D, D_EXPERT), wspec(D_EXPERT, D)],
        out_specs=pl.BlockSpec((EXPERT_TILE, D), lambda k, p: (k, 0)),
        scratch_shapes=[pltpu.VMEM((GROUP_SIZE, D, D_EXPERT), BF16),
                        pltpu.VMEM((GROUP_SIZE, D, D_EXPERT), BF16),
                        pltpu.VMEM((GROUP_SIZE, D_EXPERT, D), BF16)])
    return pl.pallas_call(
        functools.partial(_experts_kernel, group_at=at, used_at=at + k_tiles),
        grid_spec=grid_spec,
        out_shape=jax.ShapeDtypeStruct((n_rows, D), BF16),
        compiler_params=_cparams(1),
        name="moe_experts",
    )(plan, xs, rs, wg, wu, wd)


def _combine_kernel(plan_ref, x_ref, mod_ref, dest_ref, ys_hbm, o_ref, y_l, sem, *, nt):
    i = pl.program_id(0)

    slot = i % 2

    def from_buffer(slot):
        def make(lo, base):
            return (pltpu.make_async_copy(ys_hbm.at[pl.ds(base, SEG_ALIGN)],
                                          y_l.at[slot, pl.ds(lo, SEG_ALIGN)], sem.at[slot]),)
        return make

    @pl.when(i == 0)
    def _():
        y_l[...] = jnp.zeros_like(y_l)
        _start_segments(plan_ref, i, nt, from_buffer(slot))

    @pl.when(i + 1 < nt)
    def _():
        _start_segments(plan_ref, i + 1, nt, from_buffer(1 - slot))

    _wait_segments(plan_ref, i, from_buffer(slot))
    dest = dest_ref[...].astype(jnp.int32)
    lane = _iota(dest.shape, 1)
    back = jnp.concatenate(
        [jnp.where(dest == lane + j * LANE, 1.0, 0.0).astype(BF16) for j in range(DISPATCH_ROWS // LANE)], axis=1)
    y = _dot(back, y_l[slot])
    o_ref[...] = x_ref[...] + mod_ref[0][:, 5 * D:6 * D] * y


def _combine(plan, x1, mod, dest, ys):
    t = x1.shape[0]
    nt = t // TOKEN_TILE
    per_row = t // mod.shape[0]
    grid_spec = pltpu.PrefetchScalarGridSpec(
        num_scalar_prefetch=1,
        grid=(nt,),
        in_specs=[pl.BlockSpec((TOKEN_TILE, D), lambda i, p: (i, 0)),
                  pl.BlockSpec((1, 1, 6 * D), lambda i, p: ((i * TOKEN_TILE) // per_row, 0, 0)),
                  pl.BlockSpec((TOKEN_TILE, LANE), lambda i, p: (i, 0)),
                  pl.BlockSpec(memory_space=pl.ANY)],
        out_specs=pl.BlockSpec((TOKEN_TILE, D), lambda i, p: (i, 0)),
        scratch_shapes=[pltpu.VMEM((2, DISPATCH_ROWS, D), BF16),
                        pltpu.SemaphoreType.DMA((2,))])
    return pl.pallas_call(
        functools.partial(_combine_kernel, nt=nt),
        grid_spec=grid_spec,
        out_shape=jax.ShapeDtypeStruct((t, D), F32),
        compiler_params=_cparams(1),
        name="moe_combine",
    )(plan, x1, mod, dest, ys)


def _moe(x1, mod, h2, routed, cnt, wg, wu, wd, layer):
    nt = x1.shape[0] // TOKEN_TILE
    counts = cnt[:, 0, GROUP_LANE:GROUP_LANE + N_GROUPS].reshape(nt * N_GROUPS)
    xs, rs, dest, plan = _dispatch(counts, h2, routed)
    ys = _experts(plan, xs, rs, wg, wu, wd, layer, nt)
    return _combine(plan, x1, mod, dest, ys)


def _rope_angles(pos, half):
    inv = jnp.power(ROPE_THETA, -jnp.arange(half, dtype=F32) / half)
    ang = pos.astype(F32)[:, None] * inv[None, :]
    return jnp.cos(ang), jnp.sin(ang)


def _rope_tables(seq, rot, lead, lane_tiles):
    rows = seq // GRID_W
    row = jnp.repeat(jnp.arange(rows), GRID_W)
    col = jnp.tile(jnp.arange(GRID_W), rows)
    cr, sr = _rope_angles(row, rot // 4)
    cc, sc = _rope_angles(col, rot // 4)
    cos = jnp.concatenate([cr, cr, cc, cc], axis=1)
    sin = jnp.concatenate([-sr, sr, -sc, sc], axis=1)
    width = LANE // lane_tiles
    pad = width - lead - rot
    cos = jnp.concatenate([jnp.ones((seq, lead), F32), cos, jnp.ones((seq, pad), F32)], axis=1)
    sin = jnp.concatenate([jnp.zeros((seq, lead), F32), sin, jnp.zeros((seq, pad), F32)], axis=1)
    return jnp.tile(cos, (1, lane_tiles)), jnp.tile(sin, (1, lane_tiles))


def _pad_lanes(a, width):
    return jnp.pad(a, [(0, 0)] * (a.ndim - 1) + [(0, width - a.shape[-1])])


def _mla_weights(w_q_b, w_kv_b):
    wq = _pad_lanes(w_q_b.reshape(MLA_Q_LORA, MLA_HEADS, MLA_QK), LANE).reshape(MLA_Q_LORA, MLA_HEADS * LANE)
    kvb = w_kv_b.reshape(MLA_KV_LORA, MLA_HEADS, MLA_NOPE + MLA_V)
    wk = _pad_lanes(kvb[:, :, :MLA_NOPE], LANE).reshape(MLA_KV_LORA, MLA_HEADS * LANE)
    wv = kvb[:, :, MLA_NOPE:].reshape(MLA_KV_LORA, MLA_HEADS * MLA_V)
    return wq.astype(BF16), wk.astype(BF16), wv.astype(BF16)


def _dup_heads(a):
    s = a.shape[:-1]
    a = a.reshape(s + (GQA_KV_HEADS, 1, GQA_HEAD_DIM))
    return jnp.broadcast_to(a, s + (GQA_KV_HEADS, 2, GQA_HEAD_DIM)).reshape(s + (GQA_KV_HEADS * LANE,))


def _undup_heads(a):
    s = a.shape[:-1]
    return a.reshape(s + (GQA_KV_HEADS, 2, GQA_HEAD_DIM))[..., 0, :]


def _state_to_kernel(h0):
    b = h0.shape[0]
    t = jnp.transpose(h0, (0, 3, 1, 2)).reshape(b, SSD_STATE, SSD_INNER)
    half = SSD_INNER // SSD_GROUPS
    col = jnp.arange(SSD_INNER) // half
    parts = [jnp.where(col == g, t, 0.0) for g in range(SSD_GROUPS)]
    return jnp.concatenate(parts, axis=1)


def _state_from_kernel(st):
    return st.reshape(st.shape[0], SSD_HEADS, SSD_HEADDIM, SSD_STATE)


def kernel(x_prompt, x_sample, c, state_ssd_fwd, state_ssd_bwd, cache_mla_ckv, cache_mla_kpe, cache_gqa_k, cache_gqa_v, c_ctx, g_mix, g_ffn, w_ada, b_ada, w_router, router_bias, w_exp_gate, w_exp_up, w_exp_down, w_in_even, ssd_conv_w, ssd_conv_b, ssd_a_log_fwd, ssd_a_log_bwd, ssd_dt_bias_fwd, ssd_dt_bias_bwd, ssd_d, ssd_norm, mla_q_a_norm, mla_w_q_b, mla_kv_a_norm, mla_w_kv_b, mla_q_norm, mla_k_norm, w_out_even, w_in_odd, gqa_q_norm, gqa_k_norm, w_out_odd):
    nb, ls, _ = x_prompt.shape
    db, dl, _ = x_sample.shape
    depth = w_ada.shape[0]
    n_ctx = nb * ls

    xs = [x_prompt.reshape(n_ctx, D), x_sample.reshape(db * dl, D)]
    seqs = (ls, dl)
    rows = 16
    cond = jnp.concatenate([c_ctx[None, :], c, jnp.zeros((rows - 1 - db, D), F32)], axis=0)
    mod_all = _ada_mod(cond, w_ada, b_ada)

    wr = _pad_lanes(jnp.concatenate(_split3(w_router), axis=1), LANE)
    rb = router_bias.reshape(N_EXPERTS, 1)
    wg_all, wu_all, wd_all = w_exp_gate, w_exp_up, w_exp_down

    outs = {}
    for i in range(depth):
        j = i // 2
        mods = [mod_all[i, 0:1].reshape(1, 1, 6 * D), mod_all[i, 1:1 + db].reshape(db, 1, 6 * D)]
        g1 = g_mix[i].reshape(1, D)
        g2 = g_ffn[i].reshape(1, D)
        if i % 2 == 0:
            wq, wk, wv = _mla_weights(mla_w_q_b[j], mla_w_kv_b[j])
            cw = jnp.pad(ssd_conv_w[j], ((0, 5), (0, 0)))
            cb = ssd_conv_b[j].reshape(1, SSD_CONV_DIM)
            hp = _pad_lanes(jnp.stack([ssd_a_log_fwd[j], ssd_a_log_bwd[j], ssd_dt_bias_fwd[j],
                                       ssd_dt_bias_bwd[j], ssd_d[j]]), LANE)
            hp = jnp.pad(hp, ((0, 3), (0, 0)))
            nw = ssd_norm[j].reshape(1, SSD_INNER)
            qan = mla_q_a_norm[j].reshape(1, MLA_Q_LORA)
            kvn = mla_kv_a_norm[j].reshape(1, MLA_KV_LORA)
            qn = _pad_lanes(mla_q_norm[j].reshape(1, MLA_QK) * (MLA_QK ** -0.5 * LOG2E), LANE)
            kn = _pad_lanes(mla_k_norm[j].reshape(1, MLA_QK), LANE)
            h0 = (_state_to_kernel(state_ssd_fwd[:, j]), _state_to_kernel(state_ssd_bwd[:, j]))
            ckpe = jnp.pad(cache_mla_kpe[:, j], ((0, 0), (0, 0), (MLA_NOPE, LANE - MLA_QK)))
            tables = tuple(jnp.tile(tb, (1, 2)) for tb in _rope_tables(dl, MLA_ROPE, MLA_NOPE, 1))
            w_out = w_out_even[j].astype(BF16)
            ys = []
            for s in range(2):
                z, xbc, qa, kv, dt = _inproj(xs[s], mods[s], g1, w_in_even[j], EVEN_WIDTHS, EVEN_PIECES)
                if s == 0:
                    y, sf, sb = _ssd(z, xbc, dt, cw, cb, hp, nw, None, seq=seqs[s])
                    o, ckv_new, kpe_new = _mla(qa, kv, qan, kvn, wq, wk, wv, qn, kn, None, None, seq=seqs[s])
                    outs.setdefault("ssd_f", []).append(_state_from_kernel(sf))
                    outs.setdefault("ssd_b", []).append(_state_from_kernel(sb))
                    outs.setdefault("ckv", []).append(ckv_new.reshape(nb, ls, MLA_KV_LORA))
                    outs.setdefault("kpe", []).append(kpe_new[:, MLA_NOPE:MLA_QK].reshape(nb, ls, MLA_ROPE))
                else:
                    y, = _ssd(z, xbc, dt, cw, cb, hp, nw, h0, seq=seqs[s])
                    o, = _mla(qa, kv, qan, kvn, wq, wk, wv, qn, kn, (cache_mla_ckv[:, j], ckpe), tables,
                              seq=seqs[s])
                ys.append((y, o))
        else:
            nq = GQA_HEADS * GQA_HEAD_DIM
            nkv = GQA_KV_HEADS * GQA_HEAD_DIM
            nk = GQA_KV_HEADS * LANE
            qscale = GQA_HEAD_DIM ** -0.5 * LOG2E
            qn = jnp.tile(gqa_q_norm[j].reshape(1, GQA_HEAD_DIM) * qscale, (1, nq // GQA_HEAD_DIM))
            kn = jnp.tile(gqa_k_norm[j].reshape(1, GQA_HEAD_DIM), (1, nk // GQA_HEAD_DIM))
            cache = (_dup_heads(cache_gqa_k[:, j].reshape(db, -1, nkv)),
                     _dup_heads(cache_gqa_v[:, j].reshape(db, -1, nkv)))
            tables = tuple(jnp.tile(tb, (1, 2)) for tb in _rope_tables(dl, GQA_HEAD_DIM, 0, 2))
            w_out = w_out_odd[j].astype(BF16)
            ys = []
            for s in range(2):
                q, k, v = _inproj(xs[s], mods[s], g1, w_in_odd[j], ODD_WIDTHS, ODD_PIECES)
                if s == 0:
                    o, k_new = _gqa(q, k, v, qn, kn, None, None, seq=seqs[s])
                    outs.setdefault("gk", []).append(
                        _undup_heads(k_new).reshape(nb, ls, GQA_KV_HEADS, GQA_HEAD_DIM))
                    outs.setdefault("gv", []).append(
                        _undup_heads(v).reshape(nb, ls, GQA_KV_HEADS, GQA_HEAD_DIM))
                else:
                    o, = _gqa(q, k, v, qn, kn, cache, tables, seq=seqs[s])
                ys.append((o,))
        for s in range(2):
            x1, h2, routed, cnt = _outproj(xs[s], mods[s], ys[s], w_out, g2, wr, rb)
            xs[s] = _moe(x1, mods[s], h2, routed, cnt, wg_all, wu_all, wd_all, i)

    stack = lambda key: jnp.stack(outs[key], axis=1)
    return (xs[0].reshape(nb, ls, D), xs[1].reshape(db, dl, D),
            stack("ssd_f"), stack("ssd_b"), stack("ckv"), stack("kpe"), stack("gk"), stack("gv"))
```

```python
import functools
import math

import jax
import jax.numpy as jnp
from jax import lax
from jax.experimental import pallas as pl
from jax.experimental.pallas import tpu as pltpu

F32 = jnp.float32
BF16 = jnp.bfloat16

D = 1024
EPS = 1e-6
GRID_W = 64
ROPE_THETA = 10000.0
SSD_HEADS = 16
SSD_HEADDIM = 64
SSD_INNER = 1024
SSD_STATE = 64
SSD_GROUPS = 2
SSD_CONV_DIM = 1280
SSD_CHUNK = 128
MLA_HEADS = 8
MLA_Q_LORA = 256
MLA_KV_LORA = 128
MLA_NOPE = 64
MLA_ROPE = 32
MLA_V = 64
MLA_QK = 96
GQA_HEADS = 16
GQA_KV_HEADS = 4
GQA_HEAD_DIM = 64
N_EXPERTS = 16
N_GROUPS = 4
GROUP_SIZE = 4
D_EXPERT = 256

LANE = 128
Q_BLOCK = 128
LAT_Q_ROWS = 256
LOG2E = 1.4426950408889634
TOKEN_TILE = 512
EXPERT_TILE = 512
SEG_ALIGN = 16
GROUP_LANE = 16
DISPATCH_ROWS = TOKEN_TILE + LANE
ADA_TILE = 1536
VMEM_LIMIT = 56 * 1024 * 1024


def _cparams(n_axes):
    return pltpu.CompilerParams(dimension_semantics=("arbitrary",) * n_axes,
                                vmem_limit_bytes=VMEM_LIMIT)


def _dot(a, b):
    return jnp.dot(a, b, preferred_element_type=F32)


def _dot_nt(a, b):
    return lax.dot_general(a, b, (((1,), (1,)), ((), ())), preferred_element_type=F32)


def _split3(x):
    hi = x.astype(BF16)
    r1 = x - hi.astype(F32)
    mid = r1.astype(BF16)
    lo = (r1 - mid.astype(F32)).astype(BF16)
    return hi, mid, lo


def _dot_x3(x, e):
    hi, mid, lo = _split3(x)
    return _dot(hi, e) + _dot(mid, e) + _dot(lo, e)


def _dot_x2(x, e):
    hi = x.astype(BF16)
    mid = (x - hi.astype(F32)).astype(BF16)
    return _dot(hi, e) + _dot(mid, e)


def _dot_3x(e, x):
    hi, mid, lo = _split3(x)
    return _dot(e, hi) + _dot(e, mid) + _dot(e, lo)


def _dot_f32(a, b):
    a0, a1, a2 = _split3(a)
    b0, b1, b2 = _split3(b)
    return (_dot(a0, b0) + (_dot(a0, b1) + _dot(a1, b0))
            + (_dot(a1, b1) + _dot(a0, b2) + _dot(a2, b0)))


def _sigmoid(x):
    return 1.0 / (1.0 + jnp.exp(-x))


def _silu(x):
    return x * _sigmoid(x)


def _iota(shape, axis):
    return lax.broadcasted_iota(jnp.int32, shape, axis)


def _ada_kernel(c_ref, w_ref, b_ref, o_ref):
    a = _silu(c_ref[...]).astype(BF16)
    o_ref[0] = _dot(a, w_ref[0].astype(BF16)) + b_ref[0]


def _ada_mod(cond, w_ada, b_ada):
    depth, _, n = w_ada.shape
    rows = cond.shape[0]
    return pl.pallas_call(
        _ada_kernel,
        grid=(depth, n // ADA_TILE),
        in_specs=[pl.BlockSpec((rows, D), lambda l, j: (0, 0)),
                  pl.BlockSpec((1, D, ADA_TILE), lambda l, j: (l, 0, j)),
                  pl.BlockSpec((1, 1, ADA_TILE), lambda l, j: (l, 0, j))],
        out_specs=pl.BlockSpec((1, rows, ADA_TILE), lambda l, j: (l, 0, j)),
        out_shape=jax.ShapeDtypeStruct((depth, rows, n), F32),
        compiler_params=_cparams(2),
        name="ada_mod",
    )(cond, w_ada, b_ada.reshape(depth, 1, n))


def _mod_spec(mod, n_tokens, tile, n_axes=1):
    per_row = n_tokens // mod.shape[0]
    if n_axes == 1:
        return pl.BlockSpec((1, 1, 6 * D), lambda i: ((i * tile) // per_row, 0, 0))
    return pl.BlockSpec((1, 1, 6 * D), lambda i, e: ((i * tile) // per_row, 0, 0))


def _norm_mod(x, g, shift, scale):
    ms = jnp.mean(x * x, axis=-1, keepdims=True)
    y = x * lax.rsqrt(ms + EPS) * g
    return y * (1.0 + scale) + shift


def _inproj_kernel(x_ref, mod_ref, g_ref, w_ref, *refs, widths, pieces):
    o_refs, wb = refs[:-1], refs[-1]

    @pl.when(pl.program_id(0) == 0)
    def _():
        wb[...] = jnp.zeros_like(wb)
        for src, width, dst in pieces:
            wb[:, dst:dst + width] = w_ref[:, src:src + width].astype(BF16)

    m = mod_ref[0]
    h = _norm_mod(x_ref[...], g_ref[...], m[:, 0:D], m[:, D:2 * D]).astype(BF16)
    off = 0
    for o_ref, wd in zip(o_refs, widths):
        o_ref[...] = _dot(h, wb[:, off:off + wd]).astype(o_ref.dtype)
        off += wd


def _inproj(x, mod, g, w, widths, pieces):
    t = x.shape[0]
    return pl.pallas_call(
        functools.partial(_inproj_kernel, widths=widths, pieces=pieces),
        grid=(t // TOKEN_TILE,),
        in_specs=[pl.BlockSpec((TOKEN_TILE, D), lambda i: (i, 0)),
                  _mod_spec(mod, t, TOKEN_TILE),
                  pl.BlockSpec((1, D), lambda i: (0, 0)),
                  pl.BlockSpec(w.shape, lambda i: (0, 0))],
        out_specs=[pl.BlockSpec((TOKEN_TILE, wd), lambda i: (i, 0)) for wd in widths],
        out_shape=[jax.ShapeDtypeStruct((t, wd), F32) for wd in widths],
        scratch_shapes=[pltpu.VMEM((D, sum(widths)), BF16)],
        compiler_params=_cparams(1),
        name="inproj",
    )(x, mod, g, w)


EVEN_WIDTHS = (SSD_INNER, SSD_CONV_DIM, MLA_Q_LORA, 2 * LANE, LANE)
_XBC_END = SSD_INNER + SSD_CONV_DIM
_QA0 = _XBC_END + SSD_HEADS
_KV0 = _QA0 + MLA_Q_LORA
EVEN_PIECES = ((0, _XBC_END, 0),
               (_QA0, MLA_Q_LORA, _XBC_END),
               (_KV0, MLA_KV_LORA, _XBC_END + MLA_Q_LORA),
               (_KV0 + MLA_KV_LORA, MLA_ROPE, _XBC_END + MLA_Q_LORA + LANE + MLA_NOPE),
               (_XBC_END, SSD_HEADS, _XBC_END + MLA_Q_LORA + 2 * LANE))
_NQ = GQA_HEADS * GQA_HEAD_DIM
_NKV = GQA_KV_HEADS * GQA_HEAD_DIM
ODD_WIDTHS = (_NQ, GQA_KV_HEADS * LANE, GQA_KV_HEADS * LANE)
ODD_PIECES = ((0, _NQ, 0),) + tuple(
    (_NQ + part * _NKV + h * GQA_HEAD_DIM, GQA_HEAD_DIM,
     _NQ + part * GQA_KV_HEADS * LANE + h * LANE + rep * GQA_HEAD_DIM)
    for part in range(2) for h in range(GQA_KV_HEADS) for rep in range(2))


def _ssd_kernel(*refs, seq, has_h0, want_state):
    it = iter(refs)
    z_ref, xbc_ref, dt_ref, cw_ref, cb_ref, hp_ref, nw_ref = [next(it) for _ in range(7)]
    h0_refs = (next(it), next(it)) if has_h0 else None
    y_ref = next(it)
    s_refs = (next(it), next(it)) if want_state else None
    xs_ref, bc_ref, yacc_ref, st_ref = [next(it) for _ in range(4)]

    q = SSD_CHUNK
    nc = seq // q
    unrolled = nc <= 2
    gn = SSD_GROUPS * SSD_STATE

    cw = cw_ref[...]
    cb = cb_ref[...]
    rid = _iota((q, SSD_CONV_DIM), 0)

    def rows(c):
        return pl.ds(c * q, q) if isinstance(c, int) else pl.ds(pl.multiple_of(c * q, q), q)

    def conv_chunk(c):
        xc = xbc_ref[rows(c), :]
        if isinstance(c, int):
            prev = xbc_ref[c * q - 1:c * q, :] if c > 0 else jnp.zeros((1, SSD_CONV_DIM), F32)
            nxt = xbc_ref[(c + 1) * q:(c + 1) * q + 1, :] if c < nc - 1 else jnp.zeros((1, SSD_CONV_DIM), F32)
        else:
            prev = xbc_ref[pl.ds(jnp.maximum(c * q - 1, 0), 1), :]
            nxt = xbc_ref[pl.ds(jnp.minimum(c * q + q, seq - 1), 1), :]
            prev = jnp.where(c > 0, prev, 0.0)
            nxt = jnp.where(c < nc - 1, nxt, 0.0)
        xp = jnp.where(rid == 0, prev, pltpu.roll(xc, 1, axis=0))
        xn = jnp.where(rid == q - 1, nxt, pltpu.roll(xc, q - 1, axis=0))
        v = _silu(xp * cw[0:1] + xc * cw[1:2] + xn * cw[2:3] + cb)
        xs_ref[rows(c), :] = v[:, :SSD_INNER]
        bc_ref[rows(c), :] = v[:, SSD_INNER:]

    hp = hp_ref[...]
    expand = (_iota((LANE, SSD_INNER), 1) // SSD_HEADDIM == _iota((LANE, SSD_INNER), 0)).astype(BF16)
    blockmask = (_iota((gn, SSD_INNER), 0) // SSD_STATE
                 == _iota((gn, SSD_INNER), 1) // (SSD_INNER // SSD_GROUPS))
    ri = _iota((q, q), 0)
    ci = _iota((q, q), 1)
    lane = _iota((q, LANE), 1)
    lo_half = lane < SSD_STATE
    d_e = _dot_x3(jnp.broadcast_to(hp[4:5], (8, LANE)), expand)[0:1]
    nw = nw_ref[...]

    def scan_chunk(c, fwd, a, bias, causal, tri):
        xs = xs_ref[rows(c), :]
        bc = bc_ref[rows(c), :]
        bm = bc[:, :gn]
        cm = bc[:, gn:]
        raw = dt_ref[rows(c), :] + bias
        dt = jnp.maximum(raw, 0.0) + jnp.log1p(jnp.exp(-jnp.abs(raw)))
        cum = _dot_3x(tri, dt * a)
        cum_t = cum.T
        dt_t = dt.T
        last = cum[q - 1:q] if fwd else cum[0:1]
        ecum = jnp.exp(cum)
        ex = _dot_x2(jnp.concatenate([ecum, dt * jnp.exp(last - cum)], axis=0), expand)
        ecum_e, w_e = ex[0:q], ex[q:2 * q]
        cdec_e = ecum_e[q - 1:q] if fwd else ecum_e[0:1]
        st = st_ref[...]
        cmb = cm.astype(BF16)
        bmb = bm.astype(BF16)
        y = _dot(cmb, st.astype(BF16)) * ecum_e
        cb0 = _dot_nt(jnp.where(lo_half, cm, 0.0).astype(BF16), bmb)
        cb1 = _dot_nt(jnp.where(lo_half, 0.0, cm).astype(BF16), bmb)
        pairs = []
        for j in range(SSD_HEADS // 2):
            cbg = cb0 if j < SSD_HEADS // 4 else cb1
            xp = xs[:, j * LANE:(j + 1) * LANE]
            scs = []
            for half in range(2):
                h = 2 * j + half
                seg = cum[:, h:h + 1] - cum_t[h:h + 1, :]
                dec = jnp.exp(jnp.where(causal, seg, -jnp.inf))
                scs.append((cbg * dec * dt_t[h:h + 1, :]).astype(BF16))
            rhs = jnp.concatenate([jnp.where(lo_half, xp, 0.0), jnp.where(lo_half, 0.0, xp)], axis=0)
            pairs.append(_dot(jnp.concatenate(scs, axis=1), rhs.astype(BF16)))
        y = y + jnp.concatenate(pairs, axis=1)
        new = _dot(bm.T.astype(BF16), (xs * w_e).astype(BF16))
        st_ref[...] = jnp.where(blockmask, st * cdec_e + new, 0.0)
        if fwd:
            yacc_ref[rows(c), :] = y
        else:
            y = y + yacc_ref[rows(c), :] + xs * d_e
            y = y * _silu(z_ref[rows(c), :])
            ms = jnp.mean(y * y, axis=-1, keepdims=True)
            y_ref[rows(c), :] = (y * lax.rsqrt(ms + EPS) * nw).astype(y_ref.dtype)

    def scan_pass(fwd):
        a = -jnp.exp(hp[0:1] if fwd else hp[1:2])
        bias = hp[2:3] if fwd else hp[3:4]
        causal = (ri >= ci) if fwd else (ri <= ci)
        tri = causal.astype(BF16)
        if has_h0:
            st_ref[...] = h0_refs[0 if fwd else 1][0]
        else:
            st_ref[...] = jnp.zeros((gn, SSD_INNER), F32)
        if unrolled:
            for k in range(nc):
                scan_chunk(k if fwd else nc - 1 - k, fwd, a, bias, causal, tri)
        else:
            def body(k, carry):
                c = k if fwd else nc - 1 - k
                scan_chunk(c, fwd, a, bias, causal, tri)
                if fwd:
                    conv_chunk(jnp.minimum(c + 1, nc - 1))
                return carry

            lax.fori_loop(0, nc, body, 0)
        if want_state:
            for j in range(SSD_INNER // LANE):
                g = (j * LANE) // (SSD_INNER // SSD_GROUPS)
                t = st_ref[:, j * LANE:(j + 1) * LANE].T
                s_refs[0 if fwd else 1][0, j * LANE:(j + 1) * LANE, :] = t[:, g * SSD_STATE:(g + 1) * SSD_STATE]

    if unrolled:
        for c in range(nc):
            conv_chunk(c)
    else:
        conv_chunk(0)
    scan_pass(True)
    scan_pass(False)


def _ssd(z, xbc, dt, cw, cb, hp, nw, h0, *, seq):
    t = z.shape[0]
    n_seq = t // seq
    has_h0 = h0 is not None
    want_state = not has_h0
    gn = SSD_GROUPS * SSD_STATE
    tok = lambda w: pl.BlockSpec((seq, w), lambda b: (b, 0))
    full = lambda a: pl.BlockSpec(a.shape, lambda b: (0,) * a.ndim)
    in_specs = [tok(SSD_INNER), tok(SSD_CONV_DIM), tok(LANE), full(cw), full(cb), full(hp), full(nw)]
    args = [z, xbc, dt, cw, cb, hp, nw]
    if has_h0:
        in_specs += [pl.BlockSpec((1, gn, SSD_INNER), lambda b: (b, 0, 0))] * 2
        args += list(h0)
    out_shape = [jax.ShapeDtypeStruct((t, SSD_INNER), BF16)]
    out_specs = [tok(SSD_INNER)]
    if want_state:
        out_shape += [jax.ShapeDtypeStruct((n_seq, SSD_INNER, SSD_STATE), F32)] * 2
        out_specs += [pl.BlockSpec((1, SSD_INNER, SSD_STATE), lambda b: (b, 0, 0))] * 2

    return pl.pallas_call(
        functools.partial(_ssd_kernel, seq=seq, has_h0=has_h0, want_state=want_state),
        grid=(n_seq,),
        in_specs=in_specs,
        out_specs=out_specs,
        out_shape=out_shape,
        scratch_shapes=[pltpu.VMEM((seq, SSD_INNER), F32),
                        pltpu.VMEM((seq, 2 * gn), F32),
                        pltpu.VMEM((seq, SSD_INNER), F32),
                        pltpu.VMEM((gn, SSD_INNER), F32)],
        compiler_params=_cparams(1),
        name="ssd_lat" if has_h0 else "ssd_ctx",
    )(*args)


def _softmax_pv(ss, vals):
    m = ss[0].max(axis=-1, keepdims=True)
    for s in ss[1:]:
        m = jnp.maximum(m, s.max(axis=-1, keepdims=True))
    ps = [jnp.exp2(s - m) for s in ss]
    den = ps[0].sum(axis=-1, keepdims=True)
    for p in ps[1:]:
        den = den + p.sum(axis=-1, keepdims=True)
    o = _dot(ps[0].astype(BF16), vals[0]())
    for p, v in zip(ps[1:], vals[1:]):
        o = o + _dot(p.astype(BF16), v())
    return o / den


def _attend_blocks(seq, units, emit, s_scr, qrows):
    nblk = seq // qrows

    def scores(u, blk, static):
        q_ref, rows, keys, _ = units[u]
        r = blk * rows if static else pl.multiple_of(blk * rows, rows)
        qb = q_ref[pl.ds(r, rows), :]
        return [_dot_nt(qb, k()) for k in keys]

    if s_scr is None:
        for i in range(nblk):
            emit(i * qrows, [_softmax_pv(scores(u, i, True), units[u][3]) for u in range(len(units))])
        return

    def put(slot, blk):
        for u in range(len(units)):
            off = 0
            for s in scores(u, blk, isinstance(blk, int)):
                s_scr[slot][u][:, off:off + s.shape[1]] = s
                off += s.shape[1]

    def take(slot, blk):
        outs = []
        for u, (_, _, keys, vals) in enumerate(units):
            off = 0
            ss = []
            for k in keys:
                n = k().shape[0]
                ss.append(s_scr[slot][u][:, off:off + n])
                off += n
            outs.append(_softmax_pv(ss, vals))
        emit(blk * qrows if isinstance(blk, int) else pl.multiple_of(blk * qrows, qrows), outs)

    put(0, 0)

    def body(ii, carry):
        b0 = 2 * ii
        put(1, b0 + 1)
        take(0, b0)

        def more():
            put(0, b0 + 2)
            take(1, b0 + 1)

        def last():
            take(1, b0 + 1)

        lax.cond(b0 + 2 < nblk, more, last)
        return carry

    lax.fori_loop(0, nblk // 2, body, 0)


def _mla_kernel(*refs, seq, lat):
    it = iter(refs)
    qa_ref, kv_ref, qan_ref, kvn_ref, wq_ref, wk_ref, wv_ref, qn_ref, kn_ref = [next(it) for _ in range(9)]
    if lat:
        cckv_ref, ckpe_ref, cos_ref, sin_ref = [next(it) for _ in range(4)]
    o_ref = next(it)
    if not lat:
        ckv_out, kpe_out = next(it), next(it)
    q_s, k_s, v_s, qh0_s, qh1_s, kh0_s, kh1_s = [next(it) for _ in range(7)]
    s_scr = None
    if lat:
        kc_s, vc_s, kch0_s, kch1_s = [next(it) for _ in range(4)]
        s_scr = [[next(it), next(it)], [next(it), next(it)]]

    kv = kv_ref[...]
    kpe = kv[:, LANE:]
    ckv_raw = kv[:, :LANE]
    ckv = ckv_raw * lax.rsqrt(jnp.mean(ckv_raw * ckv_raw, axis=-1, keepdims=True) + EPS) * kvn_ref[...]
    qa = qa_ref[...]
    qa = qa * lax.rsqrt(jnp.mean(qa * qa, axis=-1, keepdims=True) + EPS) * qan_ref[...]
    if not lat:
        ckv_out[...] = ckv
        kpe_out[...] = kpe
    ckvb = ckv.astype(BF16)
    q_s[...] = _dot(qa.astype(BF16), wq_ref[...])
    k_s[...] = _dot(ckvb, wk_ref[...])
    v_s[...] = _dot(ckvb, wv_ref[...]).astype(BF16)
    if lat:
        ccb = cckv_ref[0].astype(BF16)
        kc_s[...] = _dot(ccb, wk_ref[...])
        vc_s[...] = _dot(ccb, wv_ref[...]).astype(BF16)
        ckpe = ckpe_ref[0]
        cos = cos_ref[...]
        sin = sin_ref[...]
        dist = MLA_ROPE // 4
        li = _iota((2 * LANE, 2 * LANE), 0)
        lj = _iota((2 * LANE, 2 * LANE), 1)
        swap = (li == jnp.where((lj % (2 * dist)) < dist, lj + dist, lj - dist)).astype(BF16)
    qn = qn_ref[...]
    kn = kn_ref[...]
    qrows = LAT_Q_ROWS if lat else Q_BLOCK
    lo_lanes = _iota((qrows, LANE), 1) < MLA_V

    def head_norm(x, g):
        return x * lax.rsqrt(jnp.sum(x * x, axis=-1, keepdims=True) * (1.0 / MLA_QK) + EPS) * g

    def rotate(x2):
        return x2 * cos + _dot_x2(x2, swap) * sin

    for j in range(MLA_HEADS // 2):
        pair = slice(j * LANE, (j + 1) * LANE)
        sls = [slice((2 * j + half) * LANE, (2 * j + half + 1) * LANE) for half in range(2)]
        q2 = jnp.concatenate([head_norm(q_s[:, sl], qn) for sl in sls], axis=1)
        k2 = jnp.concatenate([head_norm(k_s[:, sl] + kpe, kn) for sl in sls], axis=1)
        if lat:
            q2 = rotate(q2)
            k2 = rotate(k2)
            for half, kch_s in enumerate((kch0_s, kch1_s)):
                kch_s[...] = head_norm(kc_s[:, sls[half]] + ckpe, kn).astype(BF16)
        for half, (qh_s, kh_s) in enumerate(((qh0_s, kh0_s), (qh1_s, kh1_s))):
            qh_s[...] = q2[:, half * LANE:(half + 1) * LANE].astype(BF16)
            kh_s[...] = k2[:, half * LANE:(half + 1) * LANE].astype(BF16)

        def emit(r0, outs, pair=pair):
            o_ref[pl.ds(r0, qrows), pair] = jnp.where(lo_lanes, outs[0], outs[1]).astype(o_ref.dtype)

        units = []
        for half, (qh_s, kh_s) in enumerate(((qh0_s, kh0_s), (qh1_s, kh1_s))):
            keys = [lambda r=kh_s: r[...]]
            vals = [lambda pair=pair: v_s[:, pair]]
            if lat:
                keys = [lambda r=(kch0_s, kch1_s)[half]: r[...]] + keys
                vals = [lambda pair=pair: vc_s[:, pair]] + vals
            units.append((qh_s, qrows, keys, vals))
        _attend_blocks(seq, units, emit, s_scr, qrows)


def _mla(qa, kv, qan, kvn, wq, wk, wv, qn, kn, cache, tables, *, seq):
    t = qa.shape[0]
    n_seq = t // seq
    lat = cache is not None
    hv = MLA_HEADS * MLA_V
    tok = lambda w: pl.BlockSpec((seq, w), lambda b: (b, 0))
    full = lambda a: pl.BlockSpec(a.shape, lambda b: (0,) * a.ndim)
    args = [qa, kv, qan, kvn, wq, wk, wv, qn, kn]
    in_specs = [tok(MLA_Q_LORA), tok(2 * LANE)] + [full(a) for a in args[2:]]
    past = 0
    if lat:
        past = cache[0].shape[1]
        in_specs += [pl.BlockSpec((1, past, LANE), lambda b: (b, 0, 0))] * 2 + [full(tables[0]), full(tables[1])]
        args += [cache[0], cache[1], tables[0], tables[1]]
    out_shape = [jax.ShapeDtypeStruct((t, hv), BF16)]
    out_specs = [tok(hv)]
    if not lat:
        out_shape += [jax.ShapeDtypeStruct((t, LANE), F32)] * 2
        out_specs += [pl.BlockSpec((seq, LANE), lambda b: (b, 0))] * 2
    scratch = [pltpu.VMEM((seq, MLA_HEADS * LANE), F32),
               pltpu.VMEM((seq, MLA_HEADS * LANE), F32),
               pltpu.VMEM((seq, hv), BF16)] + [pltpu.VMEM((seq, LANE), BF16)] * 4
    if lat:
        scratch += [pltpu.VMEM((past, MLA_HEADS * LANE), F32),
                    pltpu.VMEM((past, hv), BF16),
                    pltpu.VMEM((past, LANE), BF16),
                    pltpu.VMEM((past, LANE), BF16)]
        scratch += [pltpu.VMEM((LAT_Q_ROWS, past + seq), F32)] * 4

    return pl.pallas_call(
        functools.partial(_mla_kernel, seq=seq, lat=lat),
        grid=(n_seq,),
        in_specs=in_specs,
        out_specs=out_specs,
        out_shape=out_shape,
        scratch_shapes=scratch,
        compiler_params=_cparams(1),
        name="mla_lat" if lat else "mla_ctx",
    )(*args)


def _gqa_kernel(*refs, seq, lat):
    it = iter(refs)
    q_ref, k_ref, v_ref, qn_ref, kn_ref = [next(it) for _ in range(5)]
    if lat:
        ck_ref, cv_ref, cos_ref, sin_ref = [next(it) for _ in range(4)]
    o_ref = next(it)
    if not lat:
        kn_out = next(it)
    kn_s, vb_s, q2_s = [next(it) for _ in range(3)]
    s_scr = None
    if lat:
        ckb_s, cvb_s = next(it), next(it)
        s_scr = [[next(it)], [next(it)]]

    hd = GQA_HEAD_DIM
    nq = GQA_HEADS * hd
    nk = GQA_KV_HEADS * LANE
    qrows = LAT_Q_ROWS if lat else Q_BLOCK
    nblk = seq // qrows
    wide = 2 * LANE
    li = _iota((wide, wide), 0)
    lj = _iota((wide, wide), 1)
    same_head = (li // hd == lj // hd).astype(BF16)
    if lat:
        dist = hd // 4
        swap = (li == jnp.where((lj % (2 * dist)) < dist, lj + dist, lj - dist)).astype(BF16)
        cos = cos_ref[...]
        sin = sin_ref[...]
    lo_q = _iota((qrows, LANE), 1) < hd

    def prepared(x_ref, g_ref, b):
        sl = slice(b * wide, (b + 1) * wide)
        x = x_ref[:, sl]
        ss = _dot_x2(x * x, same_head)
        xn = x * lax.rsqrt(ss * (1.0 / hd) + EPS) * g_ref[:, sl]
        if not lat:
            return xn, xn
        return xn, xn * cos + _dot_x2(xn, swap) * sin

    for b in range(nk // wide):
        sl = slice(b * wide, (b + 1) * wide)
        kn, kr = prepared(k_ref, kn_ref, b)
        if not lat:
            kn_out[:, sl] = kn
        kn_s[:, sl] = kr.astype(BF16)
    if lat:
        ckb_s[...] = ck_ref[0].astype(BF16)
        cvb_s[...] = cv_ref[0].astype(BF16)
    vb_s[...] = v_ref[...].astype(BF16)
    for b in range(nq // wide):
        _, qr = prepared(q_ref, qn_ref, b)
        for half in range(wide // LANE):
            p = b * (wide // LANE) + half
            for i in range(nblk):
                blk = qr[i * qrows:(i + 1) * qrows, half * LANE:(half + 1) * LANE]
                q2_s[p, 2 * i * qrows:(2 * i + 1) * qrows, :] = jnp.where(lo_q, blk, 0.0).astype(BF16)
                q2_s[p, (2 * i + 1) * qrows:(2 * i + 2) * qrows, :] = jnp.where(lo_q, 0.0, blk).astype(BF16)

    for j in range(GQA_HEADS // 2):
        pair = slice(j * LANE, (j + 1) * LANE)
        g = (2 * j) // (GQA_HEADS // GQA_KV_HEADS)
        gs = slice(g * LANE, (g + 1) * LANE)

        def emit(r0, outs, pair=pair):
            o = outs[0]
            o_ref[pl.ds(r0, qrows), pair] = jnp.where(lo_q, o[:qrows], o[qrows:]).astype(o_ref.dtype)

        keys = [lambda gs=gs: kn_s[:, gs]]
        vals = [lambda gs=gs: vb_s[:, gs]]
        if lat:
            keys = [lambda gs=gs: ckb_s[:, gs]] + keys
            vals = [lambda gs=gs: cvb_s[:, gs]] + vals
        _attend_blocks(seq, [(q2_s.at[j], 2 * qrows, keys, vals)], emit, s_scr, qrows)


def _gqa(q, k, v, qn, kn, cache, tables, *, seq):
    t = q.shape[0]
    n_seq = t // seq
    lat = cache is not None
    nq = GQA_HEADS * GQA_HEAD_DIM
    nk = GQA_KV_HEADS * LANE
    tok = lambda w: pl.BlockSpec((seq, w), lambda b: (b, 0))
    full = lambda a: pl.BlockSpec(a.shape, lambda b: (0,) * a.ndim)
    args = [q, k, v, qn, kn]
    in_specs = [tok(nq), tok(nk), tok(nk), full(qn), full(kn)]
    past = 0
    if lat:
        past = cache[0].shape[1]
        in_specs += [pl.BlockSpec((1, past, nk), lambda b: (b, 0, 0))] * 2 + [full(tables[0]), full(tables[1])]
        args += [cache[0], cache[1], tables[0], tables[1]]
    out_shape = [jax.ShapeDtypeStruct((t, nq), BF16)]
    out_specs = [tok(nq)]
    if not lat:
        out_shape += [jax.ShapeDtypeStruct((t, nk), F32)]
        out_specs += [pl.BlockSpec((seq, nk), lambda b: (b, 0))]
    scratch = [pltpu.VMEM((seq, nk), BF16),
               pltpu.VMEM((seq, nk), BF16),
               pltpu.VMEM((GQA_HEADS // 2, 2 * seq, LANE), BF16)]
    if lat:
        scratch += [pltpu.VMEM((past, nk), BF16), pltpu.VMEM((past, nk), BF16)]
        scratch += [pltpu.VMEM((2 * LAT_Q_ROWS, past + seq), F32)] * 2

    return pl.pallas_call(
        functools.partial(_gqa_kernel, seq=seq, lat=lat),
        grid=(n_seq,),
        in_specs=in_specs,
        out_specs=out_specs,
        out_shape=out_shape,
        scratch_shapes=scratch,
        compiler_params=_cparams(1),
        name="gqa_lat" if lat else "gqa_ctx",
    )(*args)


def _route(sel, sc):
    gs = []
    for g in range(N_GROUPS):
        r = sel[g * GROUP_SIZE:(g + 1) * GROUP_SIZE]
        best = None
        for i in range(GROUP_SIZE):
            for j in range(i + 1, GROUP_SIZE):
                s = r[i] + r[j]
                best = s if best is None else jnp.maximum(best, s)
        gs.append(best)
    cur = gs[0]
    grp = jnp.zeros_like(cur, dtype=jnp.int32)
    for g in range(1, N_GROUPS):
        upd = gs[g] > cur
        grp = jnp.where(upd, g, grp)
        cur = jnp.where(upd, gs[g], cur)

    def pick(rows, i):
        v = rows[i]
        for g in range(1, N_GROUPS):
            v = jnp.where(grp == g, rows[g * GROUP_SIZE + i], v)
        return v

    cand = [pick(sel, i) for i in range(GROUP_SIZE)]
    csc = [pick(sc, i) for i in range(GROUP_SIZE)]
    v1, i1, w1 = cand[0], jnp.zeros_like(grp), csc[0]
    for i in range(1, GROUP_SIZE):
        upd = cand[i] > v1
        v1 = jnp.where(upd, cand[i], v1)
        i1 = jnp.where(upd, i, i1)
        w1 = jnp.where(upd, csc[i], w1)
    v2 = i2 = w2 = None
    for i in range(GROUP_SIZE):
        ok = i1 != i
        if v2 is None:
            v2 = jnp.where(ok, cand[i], -jnp.inf)
            i2 = jnp.zeros_like(grp)
            w2 = csc[i]
        else:
            upd = ok & (cand[i] > v2)
            v2 = jnp.where(upd, cand[i], v2)
            i2 = jnp.where(upd, i, i2)
            w2 = jnp.where(upd, csc[i], w2)
    tot = w1 + w2
    local = [jnp.where(i1 == i, w1 / tot, 0.0) + jnp.where(i2 == i, w2 / tot, 0.0)
             for i in range(GROUP_SIZE)]
    onehot = [jnp.where(grp == g, 1.0, 0.0) for g in range(N_GROUPS)]
    return local, onehot


def _outproj_kernel(*refs, n_y):
    x_ref, mod_ref = refs[0], refs[1]
    y_refs = refs[2:2 + n_y]
    w_ref, g_ref, wr_ref, rb_ref, x1_ref, h2_ref, gate_ref, cnt_ref = refs[2 + n_y:]
    m = mod_ref[0]
    off = 0
    out = None
    for y_ref in y_refs:
        k = y_ref.shape[1]
        part = _dot(y_ref[...], w_ref[off:off + k, :])
        out = part if out is None else out + part
        off += k
    x1 = x_ref[...] + m[:, 2 * D:3 * D] * out
    x1_ref[...] = x1
    h2 = _norm_mod(x1, g_ref[...], m[:, 3 * D:4 * D], m[:, 4 * D:5 * D])
    h2_hi = h2.astype(BF16)
    h2_ref[...] = h2_hi
    h2_mid = (h2 - h2_hi.astype(F32)).astype(BF16)
    half = h2.shape[0] // 2
    parts = jnp.concatenate([_dot(h2_hi[r], wr_ref[...]) + _dot(h2_mid[r], wr_ref[...])
                             for r in (slice(0, half), slice(half, None))], axis=0)
    logits = (parts + pltpu.roll(parts, LANE - N_EXPERTS, axis=1)
              + pltpu.roll(parts, LANE - 2 * N_EXPERTS, axis=1))
    sc_t = _sigmoid(logits.T[0:N_EXPERTS, :])
    sel_t = sc_t + rb_ref[...]
    local, onehot = _route([sel_t[e:e + 1] for e in range(N_EXPERTS)],
                           [sc_t[e:e + 1] for e in range(N_EXPERTS)])
    t = x1.shape[0]
    rows_t = (local + [jnp.zeros((GROUP_LANE - GROUP_SIZE, t), F32)] + onehot
              + [jnp.zeros((LANE - GROUP_LANE - N_GROUPS, t), F32)])
    routed = jnp.concatenate(rows_t, axis=0).T
    gate_ref[...] = routed
    cnt = jnp.sum(routed, axis=0, keepdims=True).astype(jnp.int32)
    cnt_ref[0] = jnp.broadcast_to(cnt, (8, LANE))


def _outproj(x, mod, ys, w, g, wr, rb):
    t = x.shape[0]
    full = lambda a: pl.BlockSpec(a.shape, lambda i: (0,) * a.ndim)
    return pl.pallas_call(
        functools.partial(_outproj_kernel, n_y=len(ys)),
        grid=(t // TOKEN_TILE,),
        in_specs=[pl.BlockSpec((TOKEN_TILE, D), lambda i: (i, 0)),
                  _mod_spec(mod, t, TOKEN_TILE)]
                 + [pl.BlockSpec((TOKEN_TILE, y.shape[1]), lambda i: (i, 0)) for y in ys]
                 + [full(w), full(g), full(wr), full(rb)],
        out_specs=[pl.BlockSpec((TOKEN_TILE, D), lambda i: (i, 0)),
                   pl.BlockSpec((TOKEN_TILE, D), lambda i: (i, 0)),
                   pl.BlockSpec((TOKEN_TILE, LANE), lambda i: (i, 0)),
                   pl.BlockSpec((1, 8, LANE), lambda i: (i, 0, 0))],
        out_shape=[jax.ShapeDtypeStruct((t, D), F32),
                   jax.ShapeDtypeStruct((t, D), BF16),
                   jax.ShapeDtypeStruct((t, LANE), F32),
                   jax.ShapeDtypeStruct((t // TOKEN_TILE, 8, LANE), jnp.int32)],
        compiler_params=_cparams(1),
        name="outproj_router",
    )(x, mod, *ys, w, g, wr, rb)


def _pad_to(n, shift):
    return ((n + (1 << shift) - 1) >> shift) << shift


SEG_SHIFT = SEG_ALIGN.bit_length() - 1
TILE_SHIFT = EXPERT_TILE.bit_length() - 1


def _start_segments(plan_ref, i, nt, make):
    for g in range(N_GROUPS):
        n = plan_ref[i * N_GROUPS + g]
        lo = plan_ref[(nt + i) * N_GROUPS + g]
        base = plan_ref[(2 * nt + i) * N_GROUPS + g]

        def body(c, carry, lo=lo, base=base):
            for cp in make(pl.multiple_of(lo + c * SEG_ALIGN, SEG_ALIGN),
                           pl.multiple_of(base + c * SEG_ALIGN, SEG_ALIGN)):
                cp.start()
            return carry

        lax.fori_loop(0, (n + SEG_ALIGN - 1) >> SEG_SHIFT, body, 0)


def _wait_segments(plan_ref, i, make):
    total = jnp.int32(0)
    for g in range(N_GROUPS):
        total = total + ((plan_ref[i * N_GROUPS + g] + SEG_ALIGN - 1) >> SEG_SHIFT)

    def body(c, carry):
        for cp in make(0, 0):
            cp.wait()
        return carry

    lax.fori_loop(0, total, body, 0)


def _dispatch_kernel(cnt_ref, h_ref, r_ref, xs_hbm, rs_hbm, dest_ref, plan_ref,
                     xs_l, rs_l, zx, zr, fill_s, sem, *, nt, n_rows):
    i = pl.program_id(0)
    t = TOKEN_TILE
    k_tiles = n_rows // EXPERT_TILE

    @pl.when(i == 0)
    def _():
        zx[...] = jnp.zeros_like(zx)
        zr[...] = jnp.zeros_like(zr)
        tots = []
        for g in range(N_GROUPS):
            tots.append(lax.fori_loop(
                0, nt, lambda tt, acc, g=g: acc + _pad_to(cnt_ref[tt * N_GROUPS + g], SEG_SHIFT), jnp.int32(0)))
        starts = []
        cur = jnp.int32(0)
        for g in range(N_GROUPS):
            starts.append(cur)
            fill_s[2 * g] = cur + tots[g]
            cur = cur + _pad_to(tots[g], TILE_SHIFT)
            fill_s[2 * g + 1] = cur
        fill_s[2 * N_GROUPS] = cur
        fill_s[2 * N_GROUPS + 1] = n_rows

        def per_tile(tt, run):
            lo = jnp.int32(0)
            new = []
            for g in range(N_GROUPS):
                n = cnt_ref[tt * N_GROUPS + g]
                plan_ref[tt * N_GROUPS + g] = n
                plan_ref[(nt + tt) * N_GROUPS + g] = lo
                plan_ref[(2 * nt + tt) * N_GROUPS + g] = run[g]
                seg = _pad_to(n, SEG_SHIFT)
                lo = lo + seg
                new.append(run[g] + seg)
            return tuple(new)

        lax.fori_loop(0, nt, per_tile, tuple(starts))
        for k in range(k_tiles):
            row = k * EXPERT_TILE
            grp = 0
            for g in range(1, N_GROUPS):
                grp = grp + (row >= starts[g]).astype(jnp.int32)
            plan_ref[3 * nt * N_GROUPS + k] = grp
        plan_ref[3 * nt * N_GROUPS + k_tiles] = cur >> TILE_SHIFT

    routed = r_ref[...]
    lane = _iota((t, LANE), 1)
    is_grp = (lane >= GROUP_LANE) & (lane < GROUP_LANE + N_GROUPS)
    onehot = jnp.where(is_grp, routed, 0.0)
    tril = (_iota((t, t), 0) >= _iota((t, t), 1)).astype(BF16)
    rank = _dot(tril, onehot.astype(BF16))
    seg_lo = jnp.zeros((1, LANE), F32)
    lane1 = _iota((1, LANE), 1)
    for g in range(N_GROUPS):
        lo = plan_ref[(nt + i) * N_GROUPS + g]
        seg_lo = jnp.where(lane1 == GROUP_LANE + g, (lo - 1).astype(F32), seg_lo)
    val = jnp.where(onehot > 0.0, rank + seg_lo, 0.0)
    dest = _dot_x2(val, jnp.ones((LANE, LANE), BF16))
    dest_ref[...] = dest
    dest_row = dest.T[0:1, :].astype(jnp.int32)
    perm = jnp.where(_iota((DISPATCH_ROWS, t), 0) == dest_row, 1.0, 0.0).astype(BF16)
    slot = i % 2
    xs_l[slot] = _dot(perm, h_ref[...]).astype(BF16)
    rs_l[slot] = _dot_3x(perm, routed)

    def to_buffer(slot):
        def make(lo, base):
            return (pltpu.make_async_copy(xs_l.at[slot, pl.ds(lo, SEG_ALIGN)],
                                          xs_hbm.at[pl.ds(base, SEG_ALIGN)], sem.at[slot, 0]),
                    pltpu.make_async_copy(rs_l.at[slot, pl.ds(lo, SEG_ALIGN)],
                                          rs_hbm.at[pl.ds(base, SEG_ALIGN)], sem.at[slot, 1]))
        return make

    _start_segments(plan_ref, i, nt, to_buffer(slot))

    @pl.when(i > 0)
    def _():
        _wait_segments(plan_ref, i - 1, to_buffer(1 - slot))

    @pl.when(i == nt - 1)
    def _():
        _wait_segments(plan_ref, i, to_buffer(slot))
        def zero(base):
            return (pltpu.make_async_copy(zx, xs_hbm.at[pl.ds(base, SEG_ALIGN)], sem.at[0, 0]),
                    pltpu.make_async_copy(zr, rs_hbm.at[pl.ds(base, SEG_ALIGN)], sem.at[0, 1]))

        n_fill = jnp.int32(0)
        for r in range(N_GROUPS + 1):
            lo = fill_s[2 * r]
            chunks = (fill_s[2 * r + 1] - lo) >> SEG_SHIFT

            def body(c, carry, lo=lo):
                for cp in zero(pl.multiple_of(lo + c * SEG_ALIGN, SEG_ALIGN)):
                    cp.start()
                return carry

            lax.fori_loop(0, chunks, body, 0)
            n_fill = n_fill + chunks

        def wait_fill(c, carry):
            for cp in zero(0):
                cp.wait()
            return carry

        lax.fori_loop(0, n_fill, wait_fill, 0)


def _moe_rows(n_tokens):
    nt = n_tokens // TOKEN_TILE
    worst = n_tokens + nt * N_GROUPS * (SEG_ALIGN - 1) + N_GROUPS * (EXPERT_TILE - SEG_ALIGN)
    return _pad_to(worst, TILE_SHIFT)


def _dispatch(counts, h2, routed):
    t = h2.shape[0]
    nt = t // TOKEN_TILE
    n_rows = _moe_rows(t)
    plan_len = 3 * nt * N_GROUPS + n_rows // EXPERT_TILE + 1
    grid_spec = pltpu.PrefetchScalarGridSpec(
        num_scalar_prefetch=1,
        grid=(nt,),
        in_specs=[pl.BlockSpec((TOKEN_TILE, D), lambda i, c: (i, 0)),
                  pl.BlockSpec((TOKEN_TILE, LANE), lambda i, c: (i, 0))],
        out_specs=[pl.BlockSpec(memory_space=pl.ANY),
                   pl.BlockSpec(memory_space=pl.ANY),
                   pl.BlockSpec((TOKEN_TILE, LANE), lambda i, c: (i, 0)),
                   pl.BlockSpec(memory_space=pltpu.SMEM)],
        scratch_shapes=[pltpu.VMEM((2, DISPATCH_ROWS, D), BF16),
                        pltpu.VMEM((2, DISPATCH_ROWS, LANE), F32),
                        pltpu.VMEM((SEG_ALIGN, D), BF16),
                        pltpu.VMEM((SEG_ALIGN, LANE), F32),
                        pltpu.SMEM((2 * N_GROUPS + 2,), jnp.int32),
                        pltpu.SemaphoreType.DMA((2, 2))])
    return pl.pallas_call(
        functools.partial(_dispatch_kernel, nt=nt, n_rows=n_rows),
        grid_spec=grid_spec,
        out_shape=[jax.ShapeDtypeStruct((n_rows, D), BF16),
                   jax.ShapeDtypeStruct((n_rows, LANE), F32),
                   jax.ShapeDtypeStruct((t, LANE), F32),
                   jax.ShapeDtypeStruct((plan_len,), jnp.int32)],
        compiler_params=_cparams(1),
        name="moe_dispatch",
    )(counts, h2, routed)


def _experts_kernel(plan_ref, xs_ref, rs_ref, wg_ref, wu_ref, wd_ref, y_ref, wg_b, wu_b, wd_b,
                    *, group_at, used_at):
    k = pl.program_id(0)
    used = plan_ref[used_at]
    group = plan_ref[group_at + k]
    prev = plan_ref[group_at + jnp.maximum(k - 1, 0)]

    @pl.when((k == 0) | (group != prev))
    def _():
        wg_b[...] = wg_ref[...].astype(BF16)
        wu_b[...] = wu_ref[...].astype(BF16)
        wd_b[...] = wd_ref[...].astype(BF16)

    @pl.when(k < used)
    def _():
        rows = xs_ref[...]
        gates = rs_ref[...]
        acc = None
        for e in range(GROUP_SIZE):
            a = _dot(rows, wg_b[e])
            u = _dot(rows, wu_b[e])
            act = (_silu(a) * u * gates[:, e:e + 1]).astype(BF16)
            part = _dot(act, wd_b[e])
            acc = part if acc is None else acc + part
        y_ref[...] = acc.astype(y_ref.dtype)

    @pl.when(k >= used)
    def _():
        y_ref[...] = jnp.zeros_like(y_ref)


def _experts(plan, xs, rs, wg, wu, wd, layer, nt):
    n_rows = xs.shape[0]
    k_tiles = n_rows // EXPERT_TILE
    at = 3 * nt * N_GROUPS
    wspec = lambda a, b: pl.BlockSpec((None, GROUP_SIZE, a, b), lambda k, p: (layer, p[at + k], 0, 0))
    grid_spec = pltpu.PrefetchScalarGridSpec(
        num_scalar_prefetch=1,
        grid=(k_tiles,),
        in_specs=[pl.BlockSpec((EXPERT_TILE, D), lambda k, p: (k, 0)),
                  pl.BlockSpec((EXPERT_TILE, LANE), lambda k, p: (k, 0)),
                  wspec(D, D_EXPERT), wspec(D, D_EXPERT), wspec(D_EXPERT, D)],
        out_specs=pl.BlockSpec((EXPERT_TILE, D), lambda k, p: (k, 0)),
        scratch_shapes=[pltpu.VMEM((GROUP_SIZE, D, D_EXPERT), BF16),
                        pltpu.VMEM((GROUP_SIZE, D, D_EXPERT), BF16),
                        pltpu.VMEM((GROUP_SIZE, D_EXPERT, D), BF16)])
    return pl.pallas_call(
        functools.partial(_experts_kernel, group_at=at, used_at=at + k_tiles),
        grid_spec=grid_spec,
        out_shape=jax.ShapeDtypeStruct((n_rows, D), BF16),
        compiler_params=_cparams(1),
        name="moe_experts",
    )(plan, xs, rs, wg, wu, wd)


def _combine_kernel(plan_ref, x_ref, mod_ref, dest_ref, ys_hbm, o_ref, y_l, sem, *, nt):
    i = pl.program_id(0)

    slot = i % 2

    def from_buffer(slot):
        def make(lo, base):
            return (pltpu.make_async_copy(ys_hbm.at[pl.ds(base, SEG_ALIGN)],
                                          y_l.at[slot, pl.ds(lo, SEG_ALIGN)], sem.at[slot]),)
        return make

    @pl.when(i == 0)
    def _():
        y_l[...] = jnp.zeros_like(y_l)
        _start_segments(plan_ref, i, nt, from_buffer(slot))

    @pl.when(i + 1 < nt)
    def _():
        _start_segments(plan_ref, i + 1, nt, from_buffer(1 - slot))

    _wait_segments(plan_ref, i, from_buffer(slot))
    dest = dest_ref[...].astype(jnp.int32)
    lane = _iota(dest.shape, 1)
    back = jnp.concatenate(
        [jnp.where(dest == lane + j * LANE, 1.0, 0.0).astype(BF16) for j in range(DISPATCH_ROWS // LANE)], axis=1)
    y = _dot(back, y_l[slot])
    o_ref[...] = x_ref[...] + mod_ref[0][:, 5 * D:6 * D] * y


def _combine(plan, x1, mod, dest, ys):
    t = x1.shape[0]
    nt = t // TOKEN_TILE
    per_row = t // mod.shape[0]
    grid_spec = pltpu.PrefetchScalarGridSpec(
        num_scalar_prefetch=1,
        grid=(nt,),
        in_specs=[pl.BlockSpec((TOKEN_TILE, D), lambda i, p: (i, 0)),
                  pl.BlockSpec((1, 1, 6 * D), lambda i, p: ((i * TOKEN_TILE) // per_row, 0, 0)),
                  pl.BlockSpec((TOKEN_TILE, LANE), lambda i, p: (i, 0)),
                  pl.BlockSpec(memory_space=pl.ANY)],
        out_specs=pl.BlockSpec((TOKEN_TILE, D), lambda i, p: (i, 0)),
        scratch_shapes=[pltpu.VMEM((2, DISPATCH_ROWS, D), BF16),
                        pltpu.SemaphoreType.DMA((2,))])
    return pl.pallas_call(
        functools.partial(_combine_kernel, nt=nt),
        grid_spec=grid_spec,
        out_shape=jax.ShapeDtypeStruct((t, D), F32),
        compiler_params=_cparams(1),
        name="moe_combine",
    )(plan, x1, mod, dest, ys)


def _moe(x1, mod, h2, routed, cnt, wg, wu, wd, layer):
    nt = x1.shape[0] // TOKEN_TILE
    counts = cnt[:, 0, GROUP_LANE:GROUP_LANE + N_GROUPS].reshape(nt * N_GROUPS)
    xs, rs, dest, plan = _dispatch(counts, h2, routed)
    ys = _experts(plan, xs, rs, wg, wu, wd, layer, nt)
    return _combine(plan, x1, mod, dest, ys)


def _rope_angles(pos, half):
    inv = jnp.power(ROPE_THETA, -jnp.arange(half, dtype=F32) / half)
    ang = pos.astype(F32)[:, None] * inv[None, :]
    return jnp.cos(ang), jnp.sin(ang)


def _rope_tables(seq, rot, lead, lane_tiles):
    rows = seq // GRID_W
    row = jnp.repeat(jnp.arange(rows), GRID_W)
    col = jnp.tile(jnp.arange(GRID_W), rows)
    cr, sr = _rope_angles(row, rot // 4)
    cc, sc = _rope_angles(col, rot // 4)
    cos = jnp.concatenate([cr, cr, cc, cc], axis=1)
    sin = jnp.concatenate([-sr, sr, -sc, sc], axis=1)
    width = LANE // lane_tiles
    pad = width - lead - rot
    cos = jnp.concatenate([jnp.ones((seq, lead), F32), cos, jnp.ones((seq, pad), F32)], axis=1)
    sin = jnp.concatenate([jnp.zeros((seq, lead), F32), sin, jnp.zeros((seq, pad), F32)], axis=1)
    return jnp.tile(cos, (1, lane_tiles)), jnp.tile(sin, (1, lane_tiles))


def _pad_lanes(a, width):
    return jnp.pad(a, [(0, 0)] * (a.ndim - 1) + [(0, width - a.shape[-1])])


def _mla_weights(w_q_b, w_kv_b):
    wq = _pad_lanes(w_q_b.reshape(MLA_Q_LORA, MLA_HEADS, MLA_QK), LANE).reshape(MLA_Q_LORA, MLA_HEADS * LANE)
    kvb = w_kv_b.reshape(MLA_KV_LORA, MLA_HEADS, MLA_NOPE + MLA_V)
    wk = _pad_lanes(kvb[:, :, :MLA_NOPE], LANE).reshape(MLA_KV_LORA, MLA_HEADS * LANE)
    wv = kvb[:, :, MLA_NOPE:].reshape(MLA_KV_LORA, MLA_HEADS * MLA_V)
    return wq.astype(BF16), wk.astype(BF16), wv.astype(BF16)


def _dup_heads(a):
    s = a.shape[:-1]
    a = a.reshape(s + (GQA_KV_HEADS, 1, GQA_HEAD_DIM))
    return jnp.broadcast_to(a, s + (GQA_KV_HEADS, 2, GQA_HEAD_DIM)).reshape(s + (GQA_KV_HEADS * LANE,))


def _undup_heads(a):
    s = a.shape[:-1]
    return a.reshape(s + (GQA_KV_HEADS, 2, GQA_HEAD_DIM))[..., 0, :]


def _state_to_kernel(h0):
    b = h0.shape[0]
    t = jnp.transpose(h0, (0, 3, 1, 2)).reshape(b, SSD_STATE, SSD_INNER)
    half = SSD_INNER // SSD_GROUPS
    col = jnp.arange(SSD_INNER) // half
    parts = [jnp.where(col == g, t, 0.0) for g in range(SSD_GROUPS)]
    return jnp.concatenate(parts, axis=1)


def _state_from_kernel(st):
    return st.reshape(st.shape[0], SSD_HEADS, SSD_HEADDIM, SSD_STATE)


def kernel(x_prompt, x_sample, c, state_ssd_fwd, state_ssd_bwd, cache_mla_ckv, cache_mla_kpe, cache_gqa_k, cache_gqa_v, c_ctx, g_mix, g_ffn, w_ada, b_ada, w_router, router_bias, w_exp_gate, w_exp_up, w_exp_down, w_in_even, ssd_conv_w, ssd_conv_b, ssd_a_log_fwd, ssd_a_log_bwd, ssd_dt_bias_fwd, ssd_dt_bias_bwd, ssd_d, ssd_norm, mla_q_a_norm, mla_w_q_b, mla_kv_a_norm, mla_w_kv_b, mla_q_norm, mla_k_norm, w_out_even, w_in_odd, gqa_q_norm, gqa_k_norm, w_out_odd):
    nb, ls, _ = x_prompt.shape
    db, dl, _ = x_sample.shape
    depth = w_ada.shape[0]
    n_ctx = nb * ls

    xs = [x_prompt.reshape(n_ctx, D), x_sample.reshape(db * dl, D)]
    seqs = (ls, dl)
    rows = 16
    cond = jnp.concatenate([c_ctx[None, :], c, jnp.zeros((rows - 1 - db, D), F32)], axis=0)
    mod_all = _ada_mod(cond, w_ada, b_ada)

    wr = _pad_lanes(jnp.concatenate(_split3(w_router), axis=1), LANE)
    rb = router_bias.reshape(N_EXPERTS, 1)
    wg_all, wu_all, wd_all = w_exp_gate, w_exp_up, w_exp_down

    outs = {}
    for i in range(depth):
        j = i // 2
        mods = [mod_all[i, 0:1].reshape(1, 1, 6 * D), mod_all[i, 1:1 + db].reshape(db, 1, 6 * D)]
        g1 = g_mix[i].reshape(1, D)
        g2 = g_ffn[i].reshape(1, D)
        if i % 2 == 0:
            wq, wk, wv = _mla_weights(mla_w_q_b[j], mla_w_kv_b[j])
            cw = jnp.pad(ssd_conv_w[j], ((0, 5), (0, 0)))
            cb = ssd_conv_b[j].reshape(1, SSD_CONV_DIM)
            hp = _pad_lanes(jnp.stack([ssd_a_log_fwd[j], ssd_a_log_bwd[j], ssd_dt_bias_fwd[j],
                                       ssd_dt_bias_bwd[j], ssd_d[j]]), LANE)
            hp = jnp.pad(hp, ((0, 3), (0, 0)))
            nw = ssd_norm[j].reshape(1, SSD_INNER)
            qan = mla_q_a_norm[j].reshape(1, MLA_Q_LORA)
            kvn = mla_kv_a_norm[j].reshape(1, MLA_KV_LORA)
            qn = _pad_lanes(mla_q_norm[j].reshape(1, MLA_QK) * (MLA_QK ** -0.5 * LOG2E), LANE)
            kn = _pad_lanes(mla_k_norm[j].reshape(1, MLA_QK), LANE)
            h0 = (_state_to_kernel(state_ssd_fwd[:, j]), _state_to_kernel(state_ssd_bwd[:, j]))
            ckpe = jnp.pad(cache_mla_kpe[:, j], ((0, 0), (0, 0), (MLA_NOPE, LANE - MLA_QK)))
            tables = tuple(jnp.tile(tb, (1, 2)) for tb in _rope_tables(dl, MLA_ROPE, MLA_NOPE, 1))
            w_out = w_out_even[j].astype(BF16)
            ys = []
            for s in range(2):
                z, xbc, qa, kv, dt = _inproj(xs[s], mods[s], g1, w_in_even[j], EVEN_WIDTHS, EVEN_PIECES)
                if s == 0:
                    y, sf, sb = _ssd(z, xbc, dt, cw, cb, hp, nw, None, seq=seqs[s])
                    o, ckv_new, kpe_new = _mla(qa, kv, qan, kvn, wq, wk, wv, qn, kn, None, None, seq=seqs[s])
                    outs.setdefault("ssd_f", []).append(_state_from_kernel(sf))
                    outs.setdefault("ssd_b", []).append(_state_from_kernel(sb))
                    outs.setdefault("ckv", []).append(ckv_new.reshape(nb, ls, MLA_KV_LORA))
                    outs.setdefault("kpe", []).append(kpe_new[:, MLA_NOPE:MLA_QK].reshape(nb, ls, MLA_ROPE))
                else:
                    y, = _ssd(z, xbc, dt, cw, cb, hp, nw, h0, seq=seqs[s])
                    o, = _mla(qa, kv, qan, kvn, wq, wk, wv, qn, kn, (cache_mla_ckv[:, j], ckpe), tables,
                              seq=seqs[s])
                ys.append((y, o))
        else:
            nq = GQA_HEADS * GQA_HEAD_DIM
            nkv = GQA_KV_HEADS * GQA_HEAD_DIM
            nk = GQA_KV_HEADS * LANE
            qscale = GQA_HEAD_DIM ** -0.5 * LOG2E
            qn = jnp.tile(gqa_q_norm[j].reshape(1, GQA_HEAD_DIM) * qscale, (1, nq // GQA_HEAD_DIM))
            kn = jnp.tile(gqa_k_norm[j].reshape(1, GQA_HEAD_DIM), (1, nk // GQA_HEAD_DIM))
            cache = (_dup_heads(cache_gqa_k[:, j].reshape(db, -1, nkv)),
                     _dup_heads(cache_gqa_v[:, j].reshape(db, -1, nkv)))
            tables = tuple(jnp.tile(tb, (1, 2)) for tb in _rope_tables(dl, GQA_HEAD_DIM, 0, 2))
            w_out = w_out_odd[j].astype(BF16)
            ys = []
            for s in range(2):
                q, k, v = _inproj(xs[s], mods[s], g1, w_in_odd[j], ODD_WIDTHS, ODD_PIECES)
                if s == 0:
                    o, k_new = _gqa(q, k, v, qn, kn, None, None, seq=seqs[s])
                    outs.setdefault("gk", []).append(
                        _undup_heads(k_new).reshape(nb, ls, GQA_KV_HEADS, GQA_HEAD_DIM))
                    outs.setdefault("gv", []).append(
                        _undup_heads(v).reshape(nb, ls, GQA_KV_HEADS, GQA_HEAD_DIM))
                else:
                    o, = _gqa(q, k, v, qn, kn, cache, tables, seq=seqs[s])
                ys.append((o,))
        for s in range(2):
            x1, h2, routed, cnt = _outproj(xs[s], mods[s], ys[s], w_out, g2, wr, rb)
            xs[s] = _moe(x1, mods[s], h2, routed, cnt, wg_all, wu_all, wd_all, i)

    stack = lambda key: jnp.stack(outs[key], axis=1)
    return (xs[0].reshape(nb, ls, D), xs[1].reshape(db, dl, D),
            stack("ssd_f"), stack("ssd_b"), stack("ckv"), stack("kpe"), stack("gk"), stack("gv"))
```

```python
import functools
import math

import jax
import jax.numpy as jnp
from jax import lax
from jax.experimental import pallas as pl
from jax.experimental.pallas import tpu as pltpu

F32 = jnp.float32
BF16 = jnp.bfloat16

D = 1024
EPS = 1e-6
GRID_W = 64
ROPE_THETA = 10000.0
SSD_HEADS = 16
SSD_HEADDIM = 64
SSD_INNER = 1024
SSD_STATE = 64
SSD_GROUPS = 2
SSD_CONV_DIM = 1280
SSD_CHUNK = 128
MLA_HEADS = 8
MLA_Q_LORA = 256
MLA_KV_LORA = 128
MLA_NOPE = 64
MLA_ROPE = 32
MLA_V = 64
MLA_QK = 96
GQA_HEADS = 16
GQA_KV_HEADS = 4
GQA_HEAD_DIM = 64
N_EXPERTS = 16
N_GROUPS = 4
GROUP_SIZE = 4
D_EXPERT = 256

LANE = 128
Q_BLOCK = 128
LAT_Q_ROWS = 256
LOG2E = 1.4426950408889634
TOKEN_TILE = 512
EXPERT_TILE = 512
SEG_ALIGN = 16
GROUP_LANE = 16
DISPATCH_ROWS = TOKEN_TILE + LANE
SORTED_WIDTH = D + LANE
ADA_TILE = 1536
VMEM_LIMIT = 56 * 1024 * 1024


def _cparams(n_axes):
    return pltpu.CompilerParams(dimension_semantics=("arbitrary",) * n_axes,
                                vmem_limit_bytes=VMEM_LIMIT)


def _dot(a, b):
    return jnp.dot(a, b, preferred_element_type=F32)


def _dot_nt(a, b):
    return lax.dot_general(a, b, (((1,), (1,)), ((), ())), preferred_element_type=F32)


def _split3(x):
    hi = x.astype(BF16)
    r1 = x - hi.astype(F32)
    mid = r1.astype(BF16)
    lo = (r1 - mid.astype(F32)).astype(BF16)
    return hi, mid, lo


def _dot_x3(x, e):
    hi, mid, lo = _split3(x)
    return _dot(hi, e) + _dot(mid, e) + _dot(lo, e)


def _dot_x2(x, e):
    hi = x.astype(BF16)
    mid = (x - hi.astype(F32)).astype(BF16)
    return _dot(hi, e) + _dot(mid, e)


def _dot_3x(e, x):
    hi, mid, lo = _split3(x)
    return _dot(e, hi) + _dot(e, mid) + _dot(e, lo)


def _dot_f32(a, b):
    a0, a1, a2 = _split3(a)
    b0, b1, b2 = _split3(b)
    return (_dot(a0, b0) + (_dot(a0, b1) + _dot(a1, b0))
            + (_dot(a1, b1) + _dot(a0, b2) + _dot(a2, b0)))


def _sigmoid(x):
    return 1.0 / (1.0 + jnp.exp(-x))


def _silu(x):
    return x * _sigmoid(x)


def _iota(shape, axis):
    return lax.broadcasted_iota(jnp.int32, shape, axis)


def _ada_kernel(c_ref, w_ref, b_ref, o_ref):
    a = _silu(c_ref[...]).astype(BF16)
    o_ref[0] = _dot(a, w_ref[0].astype(BF16)) + b_ref[0]


def _ada_mod(cond, w_ada, b_ada):
    depth, _, n = w_ada.shape
    rows = cond.shape[0]
    return pl.pallas_call(
        _ada_kernel,
        grid=(depth, n // ADA_TILE),
        in_specs=[pl.BlockSpec((rows, D), lambda l, j: (0, 0)),
                  pl.BlockSpec((1, D, ADA_TILE), lambda l, j: (l, 0, j)),
                  pl.BlockSpec((1, 1, ADA_TILE), lambda l, j: (l, 0, j))],
        out_specs=pl.BlockSpec((1, rows, ADA_TILE), lambda l, j: (l, 0, j)),
        out_shape=jax.ShapeDtypeStruct((depth, rows, n), F32),
        compiler_params=_cparams(2),
        name="ada_mod",
    )(cond, w_ada, b_ada.reshape(depth, 1, n))


def _mod_spec(mod, n_tokens, tile, n_axes=1):
    per_row = n_tokens // mod.shape[0]
    if n_axes == 1:
        return pl.BlockSpec((1, 1, 6 * D), lambda i: ((i * tile) // per_row, 0, 0))
    return pl.BlockSpec((1, 1, 6 * D), lambda i, e: ((i * tile) // per_row, 0, 0))


def _norm_mod(x, g, shift, scale):
    ms = jnp.mean(x * x, axis=-1, keepdims=True)
    y = x * lax.rsqrt(ms + EPS) * g
    return y * (1.0 + scale) + shift


def _inproj_kernel(x_ref, mod_ref, g_ref, w_ref, *refs, widths, pieces):
    o_refs, wb = refs[:-1], refs[-1]

    @pl.when(pl.program_id(0) == 0)
    def _():
        wb[...] = jnp.zeros_like(wb)
        for src, width, dst in pieces:
            wb[:, dst:dst + width] = w_ref[:, src:src + width].astype(BF16)

    m = mod_ref[0]
    h = _norm_mod(x_ref[...], g_ref[...], m[:, 0:D], m[:, D:2 * D]).astype(BF16)
    off = 0
    for o_ref, wd in zip(o_refs, widths):
        o_ref[...] = _dot(h, wb[:, off:off + wd]).astype(o_ref.dtype)
        off += wd


def _inproj(x, mod, g, w, widths, pieces):
    t = x.shape[0]
    return pl.pallas_call(
        functools.partial(_inproj_kernel, widths=widths, pieces=pieces),
        grid=(t // TOKEN_TILE,),
        in_specs=[pl.BlockSpec((TOKEN_TILE, D), lambda i: (i, 0)),
                  _mod_spec(mod, t, TOKEN_TILE),
                  pl.BlockSpec((1, D), lambda i: (0, 0)),
                  pl.BlockSpec(w.shape, lambda i: (0, 0))],
        out_specs=[pl.BlockSpec((TOKEN_TILE, wd), lambda i: (i, 0)) for wd in widths],
        out_shape=[jax.ShapeDtypeStruct((t, wd), F32) for wd in widths],
        scratch_shapes=[pltpu.VMEM((D, sum(widths)), BF16)],
        compiler_params=_cparams(1),
        name="inproj",
    )(x, mod, g, w)


EVEN_WIDTHS = (SSD_INNER, SSD_CONV_DIM, MLA_Q_LORA, 2 * LANE, LANE)
_XBC_END = SSD_INNER + SSD_CONV_DIM
_QA0 = _XBC_END + SSD_HEADS
_KV0 = _QA0 + MLA_Q_LORA
EVEN_PIECES = ((0, _XBC_END, 0),
               (_QA0, MLA_Q_LORA, _XBC_END),
               (_KV0, MLA_KV_LORA, _XBC_END + MLA_Q_LORA),
               (_KV0 + MLA_KV_LORA, MLA_ROPE, _XBC_END + MLA_Q_LORA + LANE + MLA_NOPE),
               (_XBC_END, SSD_HEADS, _XBC_END + MLA_Q_LORA + 2 * LANE))
_NQ = GQA_HEADS * GQA_HEAD_DIM
_NKV = GQA_KV_HEADS * GQA_HEAD_DIM
ODD_WIDTHS = (_NQ, GQA_KV_HEADS * LANE, GQA_KV_HEADS * LANE)
ODD_PIECES = ((0, _NQ, 0),) + tuple(
    (_NQ + part * _NKV + h * GQA_HEAD_DIM, GQA_HEAD_DIM,
     _NQ + part * GQA_KV_HEADS * LANE + h * LANE + rep * GQA_HEAD_DIM)
    for part in range(2) for h in range(GQA_KV_HEADS) for rep in range(2))


def _ssd_kernel(*refs, seq, has_h0, want_state):
    it = iter(refs)
    z_ref, xbc_ref, dt_ref, cw_ref, cb_ref, hp_ref, nw_ref = [next(it) for _ in range(7)]
    h0_refs = (next(it), next(it)) if has_h0 else None
    y_ref = next(it)
    s_refs = (next(it), next(it)) if want_state else None
    xs_ref, bc_ref, yacc_ref, st_ref = [next(it) for _ in range(4)]

    q = SSD_CHUNK
    nc = seq // q
    unrolled = nc <= 2
    gn = SSD_GROUPS * SSD_STATE

    cw = cw_ref[...]
    cb = cb_ref[...]
    rid = _iota((q, SSD_CONV_DIM), 0)

    def rows(c):
        return pl.ds(c * q, q) if isinstance(c, int) else pl.ds(pl.multiple_of(c * q, q), q)

    def conv_chunk(c):
        xc = xbc_ref[rows(c), :]
        if isinstance(c, int):
            prev = xbc_ref[c * q - 1:c * q, :] if c > 0 else jnp.zeros((1, SSD_CONV_DIM), F32)
            nxt = xbc_ref[(c + 1) * q:(c + 1) * q + 1, :] if c < nc - 1 else jnp.zeros((1, SSD_CONV_DIM), F32)
        else:
            prev = xbc_ref[pl.ds(jnp.maximum(c * q - 1, 0), 1), :]
            nxt = xbc_ref[pl.ds(jnp.minimum(c * q + q, seq - 1), 1), :]
            prev = jnp.where(c > 0, prev, 0.0)
            nxt = jnp.where(c < nc - 1, nxt, 0.0)
        xp = jnp.where(rid == 0, prev, pltpu.roll(xc, 1, axis=0))
        xn = jnp.where(rid == q - 1, nxt, pltpu.roll(xc, q - 1, axis=0))
        v = _silu(xp * cw[0:1] + xc * cw[1:2] + xn * cw[2:3] + cb)
        xs_ref[rows(c), :] = v[:, :SSD_INNER]
        bc_ref[rows(c), :] = v[:, SSD_INNER:]

    hp = hp_ref[...]
    expand = (_iota((LANE, SSD_INNER), 1) // SSD_HEADDIM == _iota((LANE, SSD_INNER), 0)).astype(BF16)
    blockmask = (_iota((gn, SSD_INNER), 0) // SSD_STATE
                 == _iota((gn, SSD_INNER), 1) // (SSD_INNER // SSD_GROUPS))
    ri = _iota((q, q), 0)
    ci = _iota((q, q), 1)
    lane = _iota((q, LANE), 1)
    lo_half = lane < SSD_STATE
    d_e = _dot_x3(jnp.broadcast_to(hp[4:5], (8, LANE)), expand)[0:1]
    nw = nw_ref[...]

    def scan_chunk(c, fwd, a, bias, causal, tri):
        xs = xs_ref[rows(c), :]
        bc = bc_ref[rows(c), :]
        bm = bc[:, :gn]
        cm = bc[:, gn:]
        raw = dt_ref[rows(c), :] + bias
        dt = jnp.maximum(raw, 0.0) + jnp.log1p(jnp.exp(-jnp.abs(raw)))
        cum = _dot_3x(tri, dt * a)
        cum_t = cum.T
        dt_t = dt.T
        last = cum[q - 1:q] if fwd else cum[0:1]
        ecum = jnp.exp(cum)
        ex = _dot_x2(jnp.concatenate([ecum, dt * jnp.exp(last - cum)], axis=0), expand)
        ecum_e, w_e = ex[0:q], ex[q:2 * q]
        cdec_e = ecum_e[q - 1:q] if fwd else ecum_e[0:1]
        st = st_ref[...]
        cmb = cm.astype(BF16)
        bmb = bm.astype(BF16)
        y = _dot(cmb, st.astype(BF16)) * ecum_e
        cb0 = _dot_nt(jnp.where(lo_half, cm, 0.0).astype(BF16), bmb)
        cb1 = _dot_nt(jnp.where(lo_half, 0.0, cm).astype(BF16), bmb)
        pairs = []
        for j in range(SSD_HEADS // 2):
            cbg = cb0 if j < SSD_HEADS // 4 else cb1
            xp = xs[:, j * LANE:(j + 1) * LANE]
            scs = []
            for half in range(2):
                h = 2 * j + half
                seg = cum[:, h:h + 1] - cum_t[h:h + 1, :]
                dec = jnp.exp(jnp.where(causal, seg, -jnp.inf))
                scs.append((cbg * dec * dt_t[h:h + 1, :]).astype(BF16))
            rhs = jnp.concatenate([jnp.where(lo_half, xp, 0.0), jnp.where(lo_half, 0.0, xp)], axis=0)
            pairs.append(_dot(jnp.concatenate(scs, axis=1), rhs.astype(BF16)))
        y = y + jnp.concatenate(pairs, axis=1)
        new = _dot(bm.T.astype(BF16), (xs * w_e).astype(BF16))
        st_ref[...] = jnp.where(blockmask, st * cdec_e + new, 0.0)
        if fwd:
            yacc_ref[rows(c), :] = y
        else:
            y = y + yacc_ref[rows(c), :] + xs * d_e
            y = y * _silu(z_ref[rows(c), :])
            ms = jnp.mean(y * y, axis=-1, keepdims=True)
            y_ref[rows(c), :] = (y * lax.rsqrt(ms + EPS) * nw).astype(y_ref.dtype)

    def scan_pass(fwd):
        a = -jnp.exp(hp[0:1] if fwd else hp[1:2])
        bias = hp[2:3] if fwd else hp[3:4]
        causal = (ri >= ci) if fwd else (ri <= ci)
        tri = causal.astype(BF16)
        if has_h0:
            st_ref[...] = h0_refs[0 if fwd else 1][0]
        else:
            st_ref[...] = jnp.zeros((gn, SSD_INNER), F32)
        if unrolled:
            for k in range(nc):
                scan_chunk(k if fwd else nc - 1 - k, fwd, a, bias, causal, tri)
        else:
            def body(k, carry):
                c = k if fwd else nc - 1 - k
                scan_chunk(c, fwd, a, bias, causal, tri)
                if fwd:
                    conv_chunk(jnp.minimum(c + 1, nc - 1))
                return carry

            lax.fori_loop(0, nc, body, 0)
        if want_state:
            for j in range(SSD_INNER // LANE):
                g = (j * LANE) // (SSD_INNER // SSD_GROUPS)
                t = st_ref[:, j * LANE:(j + 1) * LANE].T
                s_refs[0 if fwd else 1][0, j * LANE:(j + 1) * LANE, :] = t[:, g * SSD_STATE:(g + 1) * SSD_STATE]

    if unrolled:
        for c in range(nc):
            conv_chunk(c)
    else:
        conv_chunk(0)
    scan_pass(True)
    scan_pass(False)


def _ssd(z, xbc, dt, cw, cb, hp, nw, h0, *, seq):
    t = z.shape[0]
    n_seq = t // seq
    has_h0 = h0 is not None
    want_state = not has_h0
    gn = SSD_GROUPS * SSD_STATE
    tok = lambda w: pl.BlockSpec((seq, w), lambda b: (b, 0))
    full = lambda a: pl.BlockSpec(a.shape, lambda b: (0,) * a.ndim)
    in_specs = [tok(SSD_INNER), tok(SSD_CONV_DIM), tok(LANE), full(cw), full(cb), full(hp), full(nw)]
    args = [z, xbc, dt, cw, cb, hp, nw]
    if has_h0:
        in_specs += [pl.BlockSpec((1, gn, SSD_INNER), lambda b: (b, 0, 0))] * 2
        args += list(h0)
    out_shape = [jax.ShapeDtypeStruct((t, SSD_INNER), BF16)]
    out_specs = [tok(SSD_INNER)]
    if want_state:
        out_shape += [jax.ShapeDtypeStruct((n_seq, SSD_INNER, SSD_STATE), F32)] * 2
        out_specs += [pl.BlockSpec((1, SSD_INNER, SSD_STATE), lambda b: (b, 0, 0))] * 2

    return pl.pallas_call(
        functools.partial(_ssd_kernel, seq=seq, has_h0=has_h0, want_state=want_state),
        grid=(n_seq,),
        in_specs=in_specs,
        out_specs=out_specs,
        out_shape=out_shape,
        scratch_shapes=[pltpu.VMEM((seq, SSD_INNER), F32),
                        pltpu.VMEM((seq, 2 * gn), F32),
                        pltpu.VMEM((seq, SSD_INNER), F32),
                        pltpu.VMEM((gn, SSD_INNER), F32)],
        compiler_params=_cparams(1),
        name="ssd_lat" if has_h0 else "ssd_ctx",
    )(*args)


def _softmax_pv(ss, vals):
    m = ss[0].max(axis=-1, keepdims=True)
    for s in ss[1:]:
        m = jnp.maximum(m, s.max(axis=-1, keepdims=True))
    ps = [jnp.exp2(s - m) for s in ss]
    den = ps[0].sum(axis=-1, keepdims=True)
    for p in ps[1:]:
        den = den + p.sum(axis=-1, keepdims=True)
    o = _dot(ps[0].astype(BF16), vals[0]())
    for p, v in zip(ps[1:], vals[1:]):
        o = o + _dot(p.astype(BF16), v())
    return o / den


def _attend_blocks(seq, units, emit, s_scr, qrows):
    nblk = seq // qrows

    def scores(u, blk, static):
        q_ref, rows, keys, _ = units[u]
        r = blk * rows if static else pl.multiple_of(blk * rows, rows)
        qb = q_ref[pl.ds(r, rows), :]
        return [_dot_nt(qb, k()) for k in keys]

    if s_scr is None:
        for i in range(nblk):
            emit(i * qrows, [_softmax_pv(scores(u, i, True), units[u][3]) for u in range(len(units))])
        return

    def put(slot, blk):
        for u in range(len(units)):
            off = 0
            for s in scores(u, blk, isinstance(blk, int)):
                s_scr[slot][u][:, off:off + s.shape[1]] = s
                off += s.shape[1]

    def take(slot, blk):
        outs = []
        for u, (_, _, keys, vals) in enumerate(units):
            off = 0
            ss = []
            for k in keys:
                n = k().shape[0]
                ss.append(s_scr[slot][u][:, off:off + n])
                off += n
            outs.append(_softmax_pv(ss, vals))
        emit(blk * qrows if isinstance(blk, int) else pl.multiple_of(blk * qrows, qrows), outs)

    put(0, 0)

    def body(ii, carry):
        b0 = 2 * ii
        put(1, b0 + 1)
        take(0, b0)

        def more():
            put(0, b0 + 2)
            take(1, b0 + 1)

        def last():
            take(1, b0 + 1)

        lax.cond(b0 + 2 < nblk, more, last)
        return carry

    lax.fori_loop(0, nblk // 2, body, 0)


def _mla_kernel(*refs, seq, lat):
    it = iter(refs)
    qa_ref, kv_ref, qan_ref, kvn_ref, wq_ref, wk_ref, wv_ref, qn_ref, kn_ref = [next(it) for _ in range(9)]
    if lat:
        cckv_ref, ckpe_ref, cos_ref, sin_ref = [next(it) for _ in range(4)]
    o_ref = next(it)
    if not lat:
        ckv_out, kpe_out = next(it), next(it)
    q_s, k_s, v_s, qh0_s, qh1_s, kh0_s, kh1_s = [next(it) for _ in range(7)]
    s_scr = None
    if lat:
        kc_s, vc_s, kch0_s, kch1_s = [next(it) for _ in range(4)]
        s_scr = [[next(it), next(it)], [next(it), next(it)]]

    kv = kv_ref[...]
    kpe = kv[:, LANE:]
    ckv_raw = kv[:, :LANE]
    ckv = ckv_raw * lax.rsqrt(jnp.mean(ckv_raw * ckv_raw, axis=-1, keepdims=True) + EPS) * kvn_ref[...]
    qa = qa_ref[...]
    qa = qa * lax.rsqrt(jnp.mean(qa * qa, axis=-1, keepdims=True) + EPS) * qan_ref[...]
    if not lat:
        ckv_out[...] = ckv
        kpe_out[...] = kpe
    ckvb = ckv.astype(BF16)
    q_s[...] = _dot(qa.astype(BF16), wq_ref[...])
    k_s[...] = _dot(ckvb, wk_ref[...])
    v_s[...] = _dot(ckvb, wv_ref[...]).astype(BF16)
    if lat:
        ccb = cckv_ref[0].astype(BF16)
        kc_s[...] = _dot(ccb, wk_ref[...])
        vc_s[...] = _dot(ccb, wv_ref[...]).astype(BF16)
        ckpe = ckpe_ref[0]
        cos = cos_ref[...]
        sin = sin_ref[...]
        dist = MLA_ROPE // 4
        li = _iota((2 * LANE, 2 * LANE), 0)
        lj = _iota((2 * LANE, 2 * LANE), 1)
        swap = (li == jnp.where((lj % (2 * dist)) < dist, lj + dist, lj - dist)).astype(BF16)
    qn = qn_ref[...]
    kn = kn_ref[...]
    qrows = LAT_Q_ROWS if lat else Q_BLOCK
    lo_lanes = _iota((qrows, LANE), 1) < MLA_V

    def head_norm(x, g):
        return x * lax.rsqrt(jnp.sum(x * x, axis=-1, keepdims=True) * (1.0 / MLA_QK) + EPS) * g

    def rotate(x2):
        return x2 * cos + _dot_x2(x2, swap) * sin

    for j in range(MLA_HEADS // 2):
        pair = slice(j * LANE, (j + 1) * LANE)
        sls = [slice((2 * j + half) * LANE, (2 * j + half + 1) * LANE) for half in range(2)]
        q2 = jnp.concatenate([head_norm(q_s[:, sl], qn) for sl in sls], axis=1)
        k2 = jnp.concatenate([head_norm(k_s[:, sl] + kpe, kn) for sl in sls], axis=1)
        if lat:
            q2 = rotate(q2)
            k2 = rotate(k2)
            for half, kch_s in enumerate((kch0_s, kch1_s)):
                kch_s[...] = head_norm(kc_s[:, sls[half]] + ckpe, kn).astype(BF16)
        for half, (qh_s, kh_s) in enumerate(((qh0_s, kh0_s), (qh1_s, kh1_s))):
            qh_s[...] = q2[:, half * LANE:(half + 1) * LANE].astype(BF16)
            kh_s[...] = k2[:, half * LANE:(half + 1) * LANE].astype(BF16)

        def emit(r0, outs, pair=pair):
            o_ref[pl.ds(r0, qrows), pair] = jnp.where(lo_lanes, outs[0], outs[1]).astype(o_ref.dtype)

        units = []
        for half, (qh_s, kh_s) in enumerate(((qh0_s, kh0_s), (qh1_s, kh1_s))):
            keys = [lambda r=kh_s: r[...]]
            vals = [lambda pair=pair: v_s[:, pair]]
            if lat:
                keys = [lambda r=(kch0_s, kch1_s)[half]: r[...]] + keys
                vals = [lambda pair=pair: vc_s[:, pair]] + vals
            units.append((qh_s, qrows, keys, vals))
        _attend_blocks(seq, units, emit, s_scr, qrows)


def _mla(qa, kv, qan, kvn, wq, wk, wv, qn, kn, cache, tables, *, seq):
    t = qa.shape[0]
    n_seq = t // seq
    lat = cache is not None
    hv = MLA_HEADS * MLA_V
    tok = lambda w: pl.BlockSpec((seq, w), lambda b: (b, 0))
    full = lambda a: pl.BlockSpec(a.shape, lambda b: (0,) * a.ndim)
    args = [qa, kv, qan, kvn, wq, wk, wv, qn, kn]
    in_specs = [tok(MLA_Q_LORA), tok(2 * LANE)] + [full(a) for a in args[2:]]
    past = 0
    if lat:
        past = cache[0].shape[1]
        in_specs += [pl.BlockSpec((1, past, LANE), lambda b: (b, 0, 0))] * 2 + [full(tables[0]), full(tables[1])]
        args += [cache[0], cache[1], tables[0], tables[1]]
    out_shape = [jax.ShapeDtypeStruct((t, hv), BF16)]
    out_specs = [tok(hv)]
    if not lat:
        out_shape += [jax.ShapeDtypeStruct((t, LANE), F32)] * 2
        out_specs += [pl.BlockSpec((seq, LANE), lambda b: (b, 0))] * 2
    scratch = [pltpu.VMEM((seq, MLA_HEADS * LANE), F32),
               pltpu.VMEM((seq, MLA_HEADS * LANE), F32),
               pltpu.VMEM((seq, hv), BF16)] + [pltpu.VMEM((seq, LANE), BF16)] * 4
    if lat:
        scratch += [pltpu.VMEM((past, MLA_HEADS * LANE), F32),
                    pltpu.VMEM((past, hv), BF16),
                    pltpu.VMEM((past, LANE), BF16),
                    pltpu.VMEM((past, LANE), BF16)]
        scratch += [pltpu.VMEM((LAT_Q_ROWS, past + seq), F32)] * 4

    return pl.pallas_call(
        functools.partial(_mla_kernel, seq=seq, lat=lat),
        grid=(n_seq,),
        in_specs=in_specs,
        out_specs=out_specs,
        out_shape=out_shape,
        scratch_shapes=scratch,
        compiler_params=_cparams(1),
        name="mla_lat" if lat else "mla_ctx",
    )(*args)


def _gqa_kernel(*refs, seq, lat):
    it = iter(refs)
    q_ref, k_ref, v_ref, qn_ref, kn_ref = [next(it) for _ in range(5)]
    if lat:
        ck_ref, cv_ref, cos_ref, sin_ref = [next(it) for _ in range(4)]
    o_ref = next(it)
    if not lat:
        kn_out = next(it)
    kn_s, vb_s, q2_s = [next(it) for _ in range(3)]
    s_scr = None
    if lat:
        ckb_s, cvb_s = next(it), next(it)
        s_scr = [[next(it)], [next(it)]]

    hd = GQA_HEAD_DIM
    nq = GQA_HEADS * hd
    nk = GQA_KV_HEADS * LANE
    qrows = LAT_Q_ROWS if lat else Q_BLOCK
    nblk = seq // qrows
    wide = 2 * LANE
    li = _iota((wide, wide), 0)
    lj = _iota((wide, wide), 1)
    same_head = (li // hd == lj // hd).astype(BF16)
    if lat:
        dist = hd // 4
        swap = (li == jnp.where((lj % (2 * dist)) < dist, lj + dist, lj - dist)).astype(BF16)
        cos = cos_ref[...]
        sin = sin_ref[...]
    lo_q = _iota((qrows, LANE), 1) < hd

    def prepared(x_ref, g_ref, b):
        sl = slice(b * wide, (b + 1) * wide)
        x = x_ref[:, sl]
        ss = _dot_x2(x * x, same_head)
        xn = x * lax.rsqrt(ss * (1.0 / hd) + EPS) * g_ref[:, sl]
        if not lat:
            return xn, xn
        return xn, xn * cos + _dot_x2(xn, swap) * sin

    for b in range(nk // wide):
        sl = slice(b * wide, (b + 1) * wide)
        kn, kr = prepared(k_ref, kn_ref, b)
        if not lat:
            kn_out[:, sl] = kn
        kn_s[:, sl] = kr.astype(BF16)
    if lat:
        ckb_s[...] = ck_ref[0].astype(BF16)
        cvb_s[...] = cv_ref[0].astype(BF16)
    vb_s[...] = v_ref[...].astype(BF16)
    for b in range(nq // wide):
        _, qr = prepared(q_ref, qn_ref, b)
        for half in range(wide // LANE):
            p = b * (wide // LANE) + half
            for i in range(nblk):
                blk = qr[i * qrows:(i + 1) * qrows, half * LANE:(half + 1) * LANE]
                q2_s[p, 2 * i * qrows:(2 * i + 1) * qrows, :] = jnp.where(lo_q, blk, 0.0).astype(BF16)
                q2_s[p, (2 * i + 1) * qrows:(2 * i + 2) * qrows, :] = jnp.where(lo_q, 0.0, blk).astype(BF16)

    for j in range(GQA_HEADS // 2):
        pair = slice(j * LANE, (j + 1) * LANE)
        g = (2 * j) // (GQA_HEADS // GQA_KV_HEADS)
        gs = slice(g * LANE, (g + 1) * LANE)

        def emit(r0, outs, pair=pair):
            o = outs[0]
            o_ref[pl.ds(r0, qrows), pair] = jnp.where(lo_q, o[:qrows], o[qrows:]).astype(o_ref.dtype)

        keys = [lambda gs=gs: kn_s[:, gs]]
        vals = [lambda gs=gs: vb_s[:, gs]]
        if lat:
            keys = [lambda gs=gs: ckb_s[:, gs]] + keys
            vals = [lambda gs=gs: cvb_s[:, gs]] + vals
        _attend_blocks(seq, [(q2_s.at[j], 2 * qrows, keys, vals)], emit, s_scr, qrows)


def _gqa(q, k, v, qn, kn, cache, tables, *, seq):
    t = q.shape[0]
    n_seq = t // seq
    lat = cache is not None
    nq = GQA_HEADS * GQA_HEAD_DIM
    nk = GQA_KV_HEADS * LANE
    tok = lambda w: pl.BlockSpec((seq, w), lambda b: (b, 0))
    full = lambda a: pl.BlockSpec(a.shape, lambda b: (0,) * a.ndim)
    args = [q, k, v, qn, kn]
    in_specs = [tok(nq), tok(nk), tok(nk), full(qn), full(kn)]
    past = 0
    if lat:
        past = cache[0].shape[1]
        in_specs += [pl.BlockSpec((1, past, nk), lambda b: (b, 0, 0))] * 2 + [full(tables[0]), full(tables[1])]
        args += [cache[0], cache[1], tables[0], tables[1]]
    out_shape = [jax.ShapeDtypeStruct((t, nq), BF16)]
    out_specs = [tok(nq)]
    if not lat:
        out_shape += [jax.ShapeDtypeStruct((t, nk), F32)]
        out_specs += [pl.BlockSpec((seq, nk), lambda b: (b, 0))]
    scratch = [pltpu.VMEM((seq, nk), BF16),
               pltpu.VMEM((seq, nk), BF16),
               pltpu.VMEM((GQA_HEADS // 2, 2 * seq, LANE), BF16)]
    if lat:
        scratch += [pltpu.VMEM((past, nk), BF16), pltpu.VMEM((past, nk), BF16)]
        scratch += [pltpu.VMEM((2 * LAT_Q_ROWS, past + seq), F32)] * 2

    return pl.pallas_call(
        functools.partial(_gqa_kernel, seq=seq, lat=lat),
        grid=(n_seq,),
        in_specs=in_specs,
        out_specs=out_specs,
        out_shape=out_shape,
        scratch_shapes=scratch,
        compiler_params=_cparams(1),
        name="gqa_lat" if lat else "gqa_ctx",
    )(*args)


def _route(sel, sc):
    gs = []
    for g in range(N_GROUPS):
        r = sel[g * GROUP_SIZE:(g + 1) * GROUP_SIZE]
        best = None
        for i in range(GROUP_SIZE):
            for j in range(i + 1, GROUP_SIZE):
                s = r[i] + r[j]
                best = s if best is None else jnp.maximum(best, s)
        gs.append(best)
    cur = gs[0]
    grp = jnp.zeros_like(cur, dtype=jnp.int32)
    for g in range(1, N_GROUPS):
        upd = gs[g] > cur
        grp = jnp.where(upd, g, grp)
        cur = jnp.where(upd, gs[g], cur)

    def pick(rows, i):
        v = rows[i]
        for g in range(1, N_GROUPS):
            v = jnp.where(grp == g, rows[g * GROUP_SIZE + i], v)
        return v

    cand = [pick(sel, i) for i in range(GROUP_SIZE)]
    csc = [pick(sc, i) for i in range(GROUP_SIZE)]
    v1, i1, w1 = cand[0], jnp.zeros_like(grp), csc[0]
    for i in range(1, GROUP_SIZE):
        upd = cand[i] > v1
        v1 = jnp.where(upd, cand[i], v1)
        i1 = jnp.where(upd, i, i1)
        w1 = jnp.where(upd, csc[i], w1)
    v2 = i2 = w2 = None
    for i in range(GROUP_SIZE):
        ok = i1 != i
        if v2 is None:
            v2 = jnp.where(ok, cand[i], -jnp.inf)
            i2 = jnp.zeros_like(grp)
            w2 = csc[i]
        else:
            upd = ok & (cand[i] > v2)
            v2 = jnp.where(upd, cand[i], v2)
            i2 = jnp.where(upd, i, i2)
            w2 = jnp.where(upd, csc[i], w2)
    tot = w1 + w2
    local = [jnp.where(i1 == i, w1 / tot, 0.0) + jnp.where(i2 == i, w2 / tot, 0.0)
             for i in range(GROUP_SIZE)]
    onehot = [jnp.where(grp == g, 1.0, 0.0) for g in range(N_GROUPS)]
    return local, onehot


def _outproj_kernel(*refs, n_y):
    x_ref, mod_ref = refs[0], refs[1]
    y_refs = refs[2:2 + n_y]
    w_ref, g_ref, wr_ref, rb_ref, x1_ref, h2_ref, gate_ref, cnt_ref = refs[2 + n_y:]
    m = mod_ref[0]
    off = 0
    out = None
    for y_ref in y_refs:
        k = y_ref.shape[1]
        part = _dot(y_ref[...], w_ref[off:off + k, :])
        out = part if out is None else out + part
        off += k
    x1 = x_ref[...] + m[:, 2 * D:3 * D] * out
    x1_ref[...] = x1
    h2 = _norm_mod(x1, g_ref[...], m[:, 3 * D:4 * D], m[:, 4 * D:5 * D])
    h2_hi = h2.astype(BF16)
    h2_ref[...] = h2_hi
    h2_mid = (h2 - h2_hi.astype(F32)).astype(BF16)
    half = h2.shape[0] // 2
    parts = jnp.concatenate([_dot(h2_hi[r], wr_ref[...]) + _dot(h2_mid[r], wr_ref[...])
                             for r in (slice(0, half), slice(half, None))], axis=0)
    logits = (parts + pltpu.roll(parts, LANE - N_EXPERTS, axis=1)
              + pltpu.roll(parts, LANE - 2 * N_EXPERTS, axis=1))
    sc_t = _sigmoid(logits.T[0:N_EXPERTS, :])
    sel_t = sc_t + rb_ref[...]
    local, onehot = _route([sel_t[e:e + 1] for e in range(N_EXPERTS)],
                           [sc_t[e:e + 1] for e in range(N_EXPERTS)])
    t = x1.shape[0]
    rows_t = (local + [jnp.zeros((GROUP_LANE - GROUP_SIZE, t), F32)] + onehot
              + [jnp.zeros((LANE - GROUP_LANE - N_GROUPS, t), F32)])
    routed = jnp.concatenate(rows_t, axis=0).T
    gate_ref[...] = routed
    cnt = jnp.sum(routed, axis=0, keepdims=True).astype(jnp.int32)
    cnt_ref[0] = jnp.broadcast_to(cnt, (8, LANE))


def _outproj(x, mod, ys, w, g, wr, rb):
    t = x.shape[0]
    full = lambda a: pl.BlockSpec(a.shape, lambda i: (0,) * a.ndim)
    return pl.pallas_call(
        functools.partial(_outproj_kernel, n_y=len(ys)),
        grid=(t // TOKEN_TILE,),
        in_specs=[pl.BlockSpec((TOKEN_TILE, D), lambda i: (i, 0)),
                  _mod_spec(mod, t, TOKEN_TILE)]
                 + [pl.BlockSpec((TOKEN_TILE, y.shape[1]), lambda i: (i, 0)) for y in ys]
                 + [full(w), full(g), full(wr), full(rb)],
        out_specs=[pl.BlockSpec((TOKEN_TILE, D), lambda i: (i, 0)),
                   pl.BlockSpec((TOKEN_TILE, D), lambda i: (i, 0)),
                   pl.BlockSpec((TOKEN_TILE, LANE), lambda i: (i, 0)),
                   pl.BlockSpec((1, 8, LANE), lambda i: (i, 0, 0))],
        out_shape=[jax.ShapeDtypeStruct((t, D), F32),
                   jax.ShapeDtypeStruct((t, D), BF16),
                   jax.ShapeDtypeStruct((t, LANE), F32),
                   jax.ShapeDtypeStruct((t // TOKEN_TILE, 8, LANE), jnp.int32)],
        compiler_params=_cparams(1),
        name="outproj_router",
    )(x, mod, *ys, w, g, wr, rb)


def _pad_to(n, shift):
    return ((n + (1 << shift) - 1) >> shift) << shift


SEG_SHIFT = SEG_ALIGN.bit_length() - 1
TILE_SHIFT = EXPERT_TILE.bit_length() - 1


def _start_segments(plan_ref, i, nt, make):
    for g in range(N_GROUPS):
        n = plan_ref[i * N_GROUPS + g]
        lo = plan_ref[(nt + i) * N_GROUPS + g]
        base = plan_ref[(2 * nt + i) * N_GROUPS + g]

        def body(c, carry, lo=lo, base=base):
            for cp in make(pl.multiple_of(lo + c * SEG_ALIGN, SEG_ALIGN),
                           pl.multiple_of(base + c * SEG_ALIGN, SEG_ALIGN)):
                cp.start()
            return carry

        lax.fori_loop(0, (n + SEG_ALIGN - 1) >> SEG_SHIFT, body, 0)


def _wait_segments(plan_ref, i, make):
    total = jnp.int32(0)
    for g in range(N_GROUPS):
        total = total + ((plan_ref[i * N_GROUPS + g] + SEG_ALIGN - 1) >> SEG_SHIFT)

    def body(c, carry):
        for cp in make(0, 0):
            cp.wait()
        return carry

    lax.fori_loop(0, total, body, 0)


def _dispatch_kernel(cnt_ref, h_ref, r_ref, xs_hbm, dest_ref, plan_ref,
                     xs_l, zx, tril_s, fill_s, sem, *, nt, n_rows):
    i = pl.program_id(0)
    t = TOKEN_TILE
    k_tiles = n_rows // EXPERT_TILE

    @pl.when(i == 0)
    def _():
        zx[...] = jnp.zeros_like(zx)
        tril_s[...] = (_iota((t, t), 0) >= _iota((t, t), 1)).astype(BF16)
        tots = []
        for g in range(N_GROUPS):
            tots.append(lax.fori_loop(
                0, nt, lambda tt, acc, g=g: acc + _pad_to(cnt_ref[tt * N_GROUPS + g], SEG_SHIFT), jnp.int32(0)))
        starts = []
        cur = jnp.int32(0)
        for g in range(N_GROUPS):
            starts.append(cur)
            fill_s[2 * g] = cur + tots[g]
            cur = cur + _pad_to(tots[g], TILE_SHIFT)
            fill_s[2 * g + 1] = cur
        fill_s[2 * N_GROUPS] = cur
        fill_s[2 * N_GROUPS + 1] = n_rows

        def per_tile(tt, run):
            lo = jnp.int32(0)
            new = []
            for g in range(N_GROUPS):
                n = cnt_ref[tt * N_GROUPS + g]
                plan_ref[tt * N_GROUPS + g] = n
                plan_ref[(nt + tt) * N_GROUPS + g] = lo
                plan_ref[(2 * nt + tt) * N_GROUPS + g] = run[g]
                seg = _pad_to(n, SEG_SHIFT)
                lo = lo + seg
                new.append(run[g] + seg)
            return tuple(new)

        lax.fori_loop(0, nt, per_tile, tuple(starts))
        for k in range(k_tiles):
            row = k * EXPERT_TILE
            grp = 0
            for g in range(1, N_GROUPS):
                grp = grp + (row >= starts[g]).astype(jnp.int32)
            plan_ref[3 * nt * N_GROUPS + k] = grp
        plan_ref[3 * nt * N_GROUPS + k_tiles] = cur >> TILE_SHIFT

    routed = r_ref[...]
    lane = _iota((t, LANE), 1)
    is_grp = (lane >= GROUP_LANE) & (lane < GROUP_LANE + N_GROUPS)
    onehot = jnp.where(is_grp, routed, 0.0)
    rank = _dot(tril_s[...], onehot.astype(BF16))
    seg_lo = jnp.zeros((1, LANE), F32)
    lane1 = _iota((1, LANE), 1)
    for g in range(N_GROUPS):
        lo = plan_ref[(nt + i) * N_GROUPS + g]
        seg_lo = jnp.where(lane1 == GROUP_LANE + g, (lo - 1).astype(F32), seg_lo)
    val = jnp.where(onehot > 0.0, rank + seg_lo, 0.0)
    dest = _dot_x2(val, jnp.ones((LANE, LANE), BF16))
    dest_ref[...] = dest
    dest_row = dest.T[0:1, :].astype(jnp.int32)
    perm = jnp.where(_iota((DISPATCH_ROWS, t), 0) == dest_row, 1.0, 0.0).astype(BF16)
    r_hi, r_mid, r_lo = _split3(routed)
    packed = (r_hi.astype(F32) + pltpu.roll(r_mid.astype(F32), GROUP_SIZE, axis=1)
              + pltpu.roll(r_lo.astype(F32), 2 * GROUP_SIZE, axis=1)).astype(BF16)
    slot = i % 2
    xs_l[slot] = _dot(perm, jnp.concatenate([h_ref[...], packed], axis=1)).astype(BF16)

    def to_buffer(slot):
        def make(lo, base):
            return (pltpu.make_async_copy(xs_l.at[slot, pl.ds(lo, SEG_ALIGN)],
                                          xs_hbm.at[pl.ds(base, SEG_ALIGN)], sem.at[slot]),)
        return make

    _start_segments(plan_ref, i, nt, to_buffer(slot))

    @pl.when(i > 0)
    def _():
        _wait_segments(plan_ref, i - 1, to_buffer(1 - slot))

    @pl.when(i == nt - 1)
    def _():
        _wait_segments(plan_ref, i, to_buffer(slot))
        def zero(base):
            return (pltpu.make_async_copy(zx, xs_hbm.at[pl.ds(base, SEG_ALIGN)], sem.at[0]),)

        n_fill = jnp.int32(0)
        for r in range(N_GROUPS + 1):
            lo = fill_s[2 * r]
            chunks = (fill_s[2 * r + 1] - lo) >> SEG_SHIFT

            def body(c, carry, lo=lo):
                for cp in zero(pl.multiple_of(lo + c * SEG_ALIGN, SEG_ALIGN)):
                    cp.start()
                return carry

            lax.fori_loop(0, chunks, body, 0)
            n_fill = n_fill + chunks

        def wait_fill(c, carry):
            for cp in zero(0):
                cp.wait()
            return carry

        lax.fori_loop(0, n_fill, wait_fill, 0)


def _moe_rows(n_tokens):
    nt = n_tokens // TOKEN_TILE
    worst = n_tokens + nt * N_GROUPS * (SEG_ALIGN - 1) + N_GROUPS * (EXPERT_TILE - SEG_ALIGN)
    return _pad_to(worst, TILE_SHIFT)


def _dispatch(counts, h2, routed):
    t = h2.shape[0]
    nt = t // TOKEN_TILE
    n_rows = _moe_rows(t)
    plan_len = 3 * nt * N_GROUPS + n_rows // EXPERT_TILE + 1
    grid_spec = pltpu.PrefetchScalarGridSpec(
        num_scalar_prefetch=1,
        grid=(nt,),
        in_specs=[pl.BlockSpec((TOKEN_TILE, D), lambda i, c: (i, 0)),
                  pl.BlockSpec((TOKEN_TILE, LANE), lambda i, c: (i, 0))],
        out_specs=[pl.BlockSpec(memory_space=pl.ANY),
                   pl.BlockSpec((TOKEN_TILE, LANE), lambda i, c: (i, 0)),
                   pl.BlockSpec(memory_space=pltpu.SMEM)],
        scratch_shapes=[pltpu.VMEM((2, DISPATCH_ROWS, SORTED_WIDTH), BF16),
                        pltpu.VMEM((SEG_ALIGN, SORTED_WIDTH), BF16),
                        pltpu.VMEM((TOKEN_TILE, TOKEN_TILE), BF16),
                        pltpu.SMEM((2 * N_GROUPS + 2,), jnp.int32),
                        pltpu.SemaphoreType.DMA((2,))])
    return pl.pallas_call(
        functools.partial(_dispatch_kernel, nt=nt, n_rows=n_rows),
        grid_spec=grid_spec,
        out_shape=[jax.ShapeDtypeStruct((n_rows, SORTED_WIDTH), BF16),
                   jax.ShapeDtypeStruct((t, LANE), F32),
                   jax.ShapeDtypeStruct((plan_len,), jnp.int32)],
        compiler_params=_cparams(1),
        name="moe_dispatch",
    )(counts, h2, routed)


def _experts_kernel(plan_ref, xs_ref, wg_ref, wu_ref, wd_ref, y_ref, wg_b, wu_b, wd_b,
                    *, group_at, used_at):
    k = pl.program_id(0)
    used = plan_ref[used_at]
    group = plan_ref[group_at + k]
    prev = plan_ref[group_at + jnp.maximum(k - 1, 0)]

    @pl.when((k == 0) | (group != prev))
    def _():
        wg_b[...] = wg_ref[...].astype(BF16)
        wu_b[...] = wu_ref[...].astype(BF16)
        wd_b[...] = wd_ref[...].astype(BF16)

    @pl.when(k < used)
    def _():
        rows = xs_ref[:, :D]
        terms = xs_ref[:, D:].astype(F32)
        gates = (terms + pltpu.roll(terms, LANE - GROUP_SIZE, axis=1)
                 + pltpu.roll(terms, LANE - 2 * GROUP_SIZE, axis=1))
        acc = None
        for e in range(GROUP_SIZE):
            a = _dot(rows, wg_b[e])
            u = _dot(rows, wu_b[e])
            act = (_silu(a) * u * gates[:, e:e + 1]).astype(BF16)
            part = _dot(act, wd_b[e])
            acc = part if acc is None else acc + part
        y_ref[...] = acc.astype(y_ref.dtype)

    @pl.when(k >= used)
    def _():
        y_ref[...] = jnp.zeros_like(y_ref)


def _experts(plan, xs, wg, wu, wd, layer, nt):
    n_rows = xs.shape[0]
    k_tiles = n_rows // EXPERT_TILE
    at = 3 * nt * N_GROUPS
    wspec = lambda a, b: pl.BlockSpec((None, GROUP_SIZE, a, b), lambda k, p: (layer, p[at + k], 0, 0))
    grid_spec = pltpu.PrefetchScalarGridSpec(
        num_scalar_prefetch=1,
        grid=(k_tiles,),
        in_specs=[pl.BlockSpec((EXPERT_TILE, SORTED_WIDTH), lambda k, p: (k, 0)),
                  wspec(D, D_EXPERT), wspec(D, D_EXPERT), wspec(D_EXPERT, D)],
        out_specs=pl.BlockSpec((EXPERT_TILE, D), lambda k, p: (k, 0)),
        scratch_shapes=[pltpu.VMEM((GROUP_SIZE, D, D_EXPERT), BF16),
                        pltpu.VMEM((GROUP_SIZE, D, D_EXPERT), BF16),
                        pltpu.VMEM((GROUP_SIZE, D_EXPERT, D), BF16)])
    return pl.pallas_call(
        functools.partial(_experts_kernel, group_at=at, used_at=at + k_tiles),
        grid_spec=grid_spec,
        out_shape=jax.ShapeDtypeStruct((n_rows, D), BF16),
        compiler_params=_cparams(1),
        name="moe_experts",
    )(plan, xs, wg, wu, wd)


def _combine_kernel(plan_ref, x_ref, mod_ref, dest_ref, ys_hbm, o_ref, y_l, sem, *, nt):
    i = pl.program_id(0)

    slot = i % 2

    def from_buffer(slot):
        def make(lo, base):
            return (pltpu.make_async_copy(ys_hbm.at[pl.ds(base, SEG_ALIGN)],
                                          y_l.at[slot, pl.ds(lo, SEG_ALIGN)], sem.at[slot]),)
        return make

    @pl.when(i == 0)
    def _():
        y_l[...] = jnp.zeros_like(y_l)
        _start_segments(plan_ref, i, nt, from_buffer(slot))

    @pl.when(i + 1 < nt)
    def _():
        _start_segments(plan_ref, i + 1, nt, from_buffer(1 - slot))

    _wait_segments(plan_ref, i, from_buffer(slot))
    dest = dest_ref[...].astype(jnp.int32)
    lane = _iota(dest.shape, 1)
    back = jnp.concatenate(
        [jnp.where(dest == lane + j * LANE, 1.0, 0.0).astype(BF16) for j in range(DISPATCH_ROWS // LANE)], axis=1)
    y = _dot(back, y_l[slot])
    o_ref[...] = x_ref[...] + mod_ref[0][:, 5 * D:6 * D] * y


def _combine(plan, x1, mod, dest, ys):
    t = x1.shape[0]
    nt = t // TOKEN_TILE
    per_row = t // mod.shape[0]
    grid_spec = pltpu.PrefetchScalarGridSpec(
        num_scalar_prefetch=1,
        grid=(nt,),
        in_specs=[pl.BlockSpec((TOKEN_TILE, D), lambda i, p: (i, 0)),
                  pl.BlockSpec((1, 1, 6 * D), lambda i, p: ((i * TOKEN_TILE) // per_row, 0, 0)),
                  pl.BlockSpec((TOKEN_TILE, LANE), lambda i, p: (i, 0)),
                  pl.BlockSpec(memory_space=pl.ANY)],
        out_specs=pl.BlockSpec((TOKEN_TILE, D), lambda i, p: (i, 0)),
        scratch_shapes=[pltpu.VMEM((2, DISPATCH_ROWS, D), BF16),
                        pltpu.SemaphoreType.DMA((2,))])
    return pl.pallas_call(
        functools.partial(_combine_kernel, nt=nt),
        grid_spec=grid_spec,
        out_shape=jax.ShapeDtypeStruct((t, D), F32),
        compiler_params=_cparams(1),
        name="moe_combine",
    )(plan, x1, mod, dest, ys)


def _moe(x1, mod, h2, routed, cnt, wg, wu, wd, layer):
    nt = x1.shape[0] // TOKEN_TILE
    counts = cnt[:, 0, GROUP_LANE:GROUP_LANE + N_GROUPS].reshape(nt * N_GROUPS)
    xs, dest, plan = _dispatch(counts, h2, routed)
    ys = _experts(plan, xs, wg, wu, wd, layer, nt)
    return _combine(plan, x1, mod, dest, ys)


def _rope_angles(pos, half):
    inv = jnp.power(ROPE_THETA, -jnp.arange(half, dtype=F32) / half)
    ang = pos.astype(F32)[:, None] * inv[None, :]
    return jnp.cos(ang), jnp.sin(ang)


def _rope_tables(seq, rot, lead, lane_tiles):
    rows = seq // GRID_W
    row = jnp.repeat(jnp.arange(rows), GRID_W)
    col = jnp.tile(jnp.arange(GRID_W), rows)
    cr, sr = _rope_angles(row, rot // 4)
    cc, sc = _rope_angles(col, rot // 4)
    cos = jnp.concatenate([cr, cr, cc, cc], axis=1)
    sin = jnp.concatenate([-sr, sr, -sc, sc], axis=1)
    width = LANE // lane_tiles
    pad = width - lead - rot
    cos = jnp.concatenate([jnp.ones((seq, lead), F32), cos, jnp.ones((seq, pad), F32)], axis=1)
    sin = jnp.concatenate([jnp.zeros((seq, lead), F32), sin, jnp.zeros((seq, pad), F32)], axis=1)
    return jnp.tile(cos, (1, lane_tiles)), jnp.tile(sin, (1, lane_tiles))


def _pad_lanes(a, width):
    return jnp.pad(a, [(0, 0)] * (a.ndim - 1) + [(0, width - a.shape[-1])])


def _mla_weights(w_q_b, w_kv_b):
    wq = _pad_lanes(w_q_b.reshape(MLA_Q_LORA, MLA_HEADS, MLA_QK), LANE).reshape(MLA_Q_LORA, MLA_HEADS * LANE)
    kvb = w_kv_b.reshape(MLA_KV_LORA, MLA_HEADS, MLA_NOPE + MLA_V)
    wk = _pad_lanes(kvb[:, :, :MLA_NOPE], LANE).reshape(MLA_KV_LORA, MLA_HEADS * LANE)
    wv = kvb[:, :, MLA_NOPE:].reshape(MLA_KV_LORA, MLA_HEADS * MLA_V)
    return wq.astype(BF16), wk.astype(BF16), wv.astype(BF16)


def _dup_heads(a):
    s = a.shape[:-1]
    a = a.reshape(s + (GQA_KV_HEADS, 1, GQA_HEAD_DIM))
    return jnp.broadcast_to(a, s + (GQA_KV_HEADS, 2, GQA_HEAD_DIM)).reshape(s + (GQA_KV_HEADS * LANE,))


def _undup_heads(a):
    s = a.shape[:-1]
    return a.reshape(s + (GQA_KV_HEADS, 2, GQA_HEAD_DIM))[..., 0, :]


def _state_to_kernel(h0):
    b = h0.shape[0]
    t = jnp.transpose(h0, (0, 3, 1, 2)).reshape(b, SSD_STATE, SSD_INNER)
    half = SSD_INNER // SSD_GROUPS
    col = jnp.arange(SSD_INNER) // half
    parts = [jnp.where(col == g, t, 0.0) for g in range(SSD_GROUPS)]
    return jnp.concatenate(parts, axis=1)


def _state_from_kernel(st):
    return st.reshape(st.shape[0], SSD_HEADS, SSD_HEADDIM, SSD_STATE)


def kernel(x_prompt, x_sample, c, state_ssd_fwd, state_ssd_bwd, cache_mla_ckv, cache_mla_kpe, cache_gqa_k, cache_gqa_v, c_ctx, g_mix, g_ffn, w_ada, b_ada, w_router, router_bias, w_exp_gate, w_exp_up, w_exp_down, w_in_even, ssd_conv_w, ssd_conv_b, ssd_a_log_fwd, ssd_a_log_bwd, ssd_dt_bias_fwd, ssd_dt_bias_bwd, ssd_d, ssd_norm, mla_q_a_norm, mla_w_q_b, mla_kv_a_norm, mla_w_kv_b, mla_q_norm, mla_k_norm, w_out_even, w_in_odd, gqa_q_norm, gqa_k_norm, w_out_odd):
    nb, ls, _ = x_prompt.shape
    db, dl, _ = x_sample.shape
    depth = w_ada.shape[0]
    n_ctx = nb * ls

    xs = [x_prompt.reshape(n_ctx, D), x_sample.reshape(db * dl, D)]
    seqs = (ls, dl)
    rows = 16
    cond = jnp.concatenate([c_ctx[None, :], c, jnp.zeros((rows - 1 - db, D), F32)], axis=0)
    mod_all = _ada_mod(cond, w_ada, b_ada)

    wr = _pad_lanes(jnp.concatenate(_split3(w_router), axis=1), LANE)
    rb = router_bias.reshape(N_EXPERTS, 1)
    wg_all, wu_all, wd_all = w_exp_gate, w_exp_up, w_exp_down

    outs = {}
    for i in range(depth):
        j = i // 2
        mods = [mod_all[i, 0:1].reshape(1, 1, 6 * D), mod_all[i, 1:1 + db].reshape(db, 1, 6 * D)]
        g1 = g_mix[i].reshape(1, D)
        g2 = g_ffn[i].reshape(1, D)
        if i % 2 == 0:
            wq, wk, wv = _mla_weights(mla_w_q_b[j], mla_w_kv_b[j])
            cw = jnp.pad(ssd_conv_w[j], ((0, 5), (0, 0)))
            cb = ssd_conv_b[j].reshape(1, SSD_CONV_DIM)
            hp = _pad_lanes(jnp.stack([ssd_a_log_fwd[j], ssd_a_log_bwd[j], ssd_dt_bias_fwd[j],
                                       ssd_dt_bias_bwd[j], ssd_d[j]]), LANE)
            hp = jnp.pad(hp, ((0, 3), (0, 0)))
            nw = ssd_norm[j].reshape(1, SSD_INNER)
            qan = mla_q_a_norm[j].reshape(1, MLA_Q_LORA)
            kvn = mla_kv_a_norm[j].reshape(1, MLA_KV_LORA)
            qn = _pad_lanes(mla_q_norm[j].reshape(1, MLA_QK) * (MLA_QK ** -0.5 * LOG2E), LANE)
            kn = _pad_lanes(mla_k_norm[j].reshape(1, MLA_QK), LANE)
            h0 = (_state_to_kernel(state_ssd_fwd[:, j]), _state_to_kernel(state_ssd_bwd[:, j]))
            ckpe = jnp.pad(cache_mla_kpe[:, j], ((0, 0), (0, 0), (MLA_NOPE, LANE - MLA_QK)))
            tables = tuple(jnp.tile(tb, (1, 2)) for tb in _rope_tables(dl, MLA_ROPE, MLA_NOPE, 1))
            w_out = w_out_even[j].astype(BF16)
            ys = []
            for s in range(2):
                z, xbc, qa, kv, dt = _inproj(xs[s], mods[s], g1, w_in_even[j], EVEN_WIDTHS, EVEN_PIECES)
                if s == 0:
                    y, sf, sb = _ssd(z, xbc, dt, cw, cb, hp, nw, None, seq=seqs[s])
                    o, ckv_new, kpe_new = _mla(qa, kv, qan, kvn, wq, wk, wv, qn, kn, None, None, seq=seqs[s])
                    outs.setdefault("ssd_f", []).append(_state_from_kernel(sf))
                    outs.setdefault("ssd_b", []).append(_state_from_kernel(sb))
                    outs.setdefault("ckv", []).append(ckv_new.reshape(nb, ls, MLA_KV_LORA))
                    outs.setdefault("kpe", []).append(kpe_new[:, MLA_NOPE:MLA_QK].reshape(nb, ls, MLA_ROPE))
                else:
                    y, = _ssd(z, xbc, dt, cw, cb, hp, nw, h0, seq=seqs[s])
                    o, = _mla(qa, kv, qan, kvn, wq, wk, wv, qn, kn, (cache_mla_ckv[:, j], ckpe), tables,
                              seq=seqs[s])
                ys.append((y, o))
        else:
            nq = GQA_HEADS * GQA_HEAD_DIM
            nkv = GQA_KV_HEADS * GQA_HEAD_DIM
            nk = GQA_KV_HEADS * LANE
            qscale = GQA_HEAD_DIM ** -0.5 * LOG2E
            qn = jnp.tile(gqa_q_norm[j].reshape(1, GQA_HEAD_DIM) * qscale, (1, nq // GQA_HEAD_DIM))
            kn = jnp.tile(gqa_k_norm[j].reshape(1, GQA_HEAD_DIM), (1, nk // GQA_HEAD_DIM))
            cache = (_dup_heads(cache_gqa_k[:, j].reshape(db, -1, nkv)),
                     _dup_heads(cache_gqa_v[:, j].reshape(db, -1, nkv)))
            tables = tuple(jnp.tile(tb, (1, 2)) for tb in _rope_tables(dl, GQA_HEAD_DIM, 0, 2))
            w_out = w_out_odd[j].astype(BF16)
            ys = []
            for s in range(2):
                q, k, v = _inproj(xs[s], mods[s], g1, w_in_odd[j], ODD_WIDTHS, ODD_PIECES)
                if s == 0:
                    o, k_new = _gqa(q, k, v, qn, kn, None, None, seq=seqs[s])
                    outs.setdefault("gk", []).append(
                        _undup_heads(k_new).reshape(nb, ls, GQA_KV_HEADS, GQA_HEAD_DIM))
                    outs.setdefault("gv", []).append(
                        _undup_heads(v).reshape(nb, ls, GQA_KV_HEADS, GQA_HEAD_DIM))
                else:
                    o, = _gqa(q, k, v, qn, kn, cache, tables, seq=seqs[s])
                ys.append((o,))
        for s in range(2):
            x1, h2, routed, cnt = _outproj(xs[s], mods[s], ys[s], w_out, g2, wr, rb)
            xs[s] = _moe(x1, mods[s], h2, routed, cnt, wg_all, wu_all, wd_all, i)

    stack = lambda key: jnp.stack(outs[key], axis=1)
    return (xs[0].reshape(nb, ls, D), xs[1].reshape(db, dl, D),
            stack("ssd_f"), stack("ssd_b"), stack("ckv"), stack("kpe"), stack("gk"), stack("gv"))
```

```python
import functools

import numpy as np

import jax
import jax.numpy as jnp
from jax import lax
from jax.experimental import pallas as pl
from jax.experimental.pallas import tpu as pltpu

F32 = jnp.float32
BF16 = jnp.bfloat16

D = 1024
EPS = 1e-6
GRID_W = 64
ROPE_THETA = 10000.0
SSD_HEADS = 16
SSD_HEADDIM = 64
SSD_INNER = 1024
SSD_STATE = 64
SSD_GROUPS = 2
SSD_CONV_DIM = 1280
SSD_CHUNK = 128
MLA_HEADS = 8
MLA_Q_LORA = 256
MLA_KV_LORA = 128
MLA_NOPE = 64
MLA_ROPE = 32
MLA_V = 64
MLA_QK = 96
GQA_HEADS = 16
GQA_KV_HEADS = 4
GQA_HEAD_DIM = 64
N_EXPERTS = 16
N_GROUPS = 4
GROUP_SIZE = 4
D_EXPERT = 256

LANE = 128
Q_BLOCK = 128
LAT_Q_ROWS = 256
LOG2E = 1.4426950408889634
TOKEN_TILE = 512
EXPERT_TILE = 512
SEG_ALIGN = 16
GROUP_LANE = 16
DISPATCH_ROWS = TOKEN_TILE + LANE
SORTED_WIDTH = D + LANE
ADA_TILE = 1536
VMEM_LIMIT = 56 * 1024 * 1024


def _cparams(n_axes):
    return pltpu.CompilerParams(dimension_semantics=("arbitrary",) * n_axes,
                                vmem_limit_bytes=VMEM_LIMIT)


def _dot(a, b):
    return jnp.dot(a, b, preferred_element_type=F32)


def _dot_nt(a, b):
    return lax.dot_general(a, b, (((1,), (1,)), ((), ())), preferred_element_type=F32)


def _split3(x):
    hi = x.astype(BF16)
    r1 = x - hi.astype(F32)
    mid = r1.astype(BF16)
    lo = (r1 - mid.astype(F32)).astype(BF16)
    return hi, mid, lo


def _dot_x3(x, e):
    hi, mid, lo = _split3(x)
    return _dot(hi, e) + _dot(mid, e) + _dot(lo, e)


def _dot_x2(x, e):
    hi = x.astype(BF16)
    mid = (x - hi.astype(F32)).astype(BF16)
    return _dot(hi, e) + _dot(mid, e)


def _dot_3x(e, x):
    hi, mid, lo = _split3(x)
    return _dot(e, hi) + _dot(e, mid) + _dot(e, lo)


def _dot_f32(a, b):
    a0, a1, a2 = _split3(a)
    b0, b1, b2 = _split3(b)
    return (_dot(a0, b0) + (_dot(a0, b1) + _dot(a1, b0))
            + (_dot(a1, b1) + _dot(a0, b2) + _dot(a2, b0)))


def _sigmoid(x):
    return 1.0 / (1.0 + jnp.exp(-x))


def _silu(x):
    return x * _sigmoid(x)


def _iota(shape, axis):
    return lax.broadcasted_iota(jnp.int32, shape, axis)


def _ada_kernel(c_ref, w_ref, b_ref, o_ref):
    a = _silu(c_ref[...]).astype(BF16)
    o_ref[0] = _dot(a, w_ref[0].astype(BF16)) + b_ref[0]


def _ada_mod(cond, w_ada, b_ada):
    depth, _, n = w_ada.shape
    rows = cond.shape[0]
    return pl.pallas_call(
        _ada_kernel,
        grid=(depth, n // ADA_TILE),
        in_specs=[pl.BlockSpec((rows, D), lambda l, j: (0, 0)),
                  pl.BlockSpec((1, D, ADA_TILE), lambda l, j: (l, 0, j)),
                  pl.BlockSpec((1, 1, ADA_TILE), lambda l, j: (l, 0, j))],
        out_specs=pl.BlockSpec((1, rows, ADA_TILE), lambda l, j: (l, 0, j)),
        out_shape=jax.ShapeDtypeStruct((depth, rows, n), F32),
        compiler_params=_cparams(2),
        name="ada_mod",
    )(cond, w_ada, b_ada.reshape(depth, 1, n))


def _mod_spec(mod, n_tokens, tile):
    _, layer, row0, n_rows = mod
    per_row = n_tokens // n_rows
    return pl.BlockSpec((None, 1, 1, 6 * D), lambda i, *_: (layer, row0 + (i * tile) // per_row, 0, 0))


def _layer_spec(a, layer):
    return pl.BlockSpec((None,) + a.shape[1:], lambda i, *_: (layer, 0, 0))


def _norm_mod(x, g, shift, scale):
    ms = jnp.mean(x * x, axis=-1, keepdims=True)
    y = x * lax.rsqrt(ms + EPS) * g
    return y * (1.0 + scale) + shift


def _inproj_kernel(x_ref, mod_ref, g_ref, w_ref, *refs, widths, pieces):
    o_refs, wb = refs[:-1], refs[-1]

    @pl.when(pl.program_id(0) == 0)
    def _():
        wb[...] = jnp.zeros_like(wb)
        for src, width, dst in pieces:
            wb[:, dst:dst + width] = w_ref[:, src:src + width].astype(BF16)

    m = mod_ref[0]
    h = _norm_mod(x_ref[...], g_ref[...], m[:, 0:D], m[:, D:2 * D]).astype(BF16)
    off = 0
    for o_ref, wd in zip(o_refs, widths):
        o_ref[...] = _dot(h, wb[:, off:off + wd]).astype(o_ref.dtype)
        off += wd


def _inproj(x, mod, g, layer, w, widths, pieces):
    t = x.shape[0]
    return pl.pallas_call(
        functools.partial(_inproj_kernel, widths=widths, pieces=pieces),
        grid=(t // TOKEN_TILE,),
        in_specs=[pl.BlockSpec((TOKEN_TILE, D), lambda i: (i, 0)),
                  _mod_spec(mod, t, TOKEN_TILE),
                  _layer_spec(g, layer),
                  pl.BlockSpec(w.shape, lambda i: (0, 0))],
        out_specs=[pl.BlockSpec((TOKEN_TILE, wd), lambda i: (i, 0)) for wd in widths],
        out_shape=[jax.ShapeDtypeStruct((t, wd), F32) for wd in widths],
        scratch_shapes=[pltpu.VMEM((D, sum(widths)), BF16)],
        compiler_params=_cparams(1),
        name="inproj",
    )(x, mod[0], g, w)


EVEN_WIDTHS = (SSD_INNER, SSD_CONV_DIM, MLA_Q_LORA, 2 * LANE, LANE)
_XBC_END = SSD_INNER + SSD_CONV_DIM
_QA0 = _XBC_END + SSD_HEADS
_KV0 = _QA0 + MLA_Q_LORA
EVEN_PIECES = ((0, _XBC_END, 0),
               (_QA0, MLA_Q_LORA, _XBC_END),
               (_KV0, MLA_KV_LORA, _XBC_END + MLA_Q_LORA),
               (_KV0 + MLA_KV_LORA, MLA_ROPE, _XBC_END + MLA_Q_LORA + LANE + MLA_NOPE),
               (_XBC_END, SSD_HEADS, _XBC_END + MLA_Q_LORA + 2 * LANE))
_NQ = GQA_HEADS * GQA_HEAD_DIM
_NKV = GQA_KV_HEADS * GQA_HEAD_DIM
ODD_WIDTHS = (_NQ, GQA_KV_HEADS * LANE, GQA_KV_HEADS * LANE)
ODD_PIECES = ((0, _NQ, 0),) + tuple(
    (_NQ + part * _NKV + h * GQA_HEAD_DIM, GQA_HEAD_DIM,
     _NQ + part * GQA_KV_HEADS * LANE + h * LANE + rep * GQA_HEAD_DIM)
    for part in range(2) for h in range(GQA_KV_HEADS) for rep in range(2))


def _ssd_kernel(*refs, seq, has_h0, want_state):
    it = iter(refs)
    z_ref, xbc_ref, dt_ref, cw_ref, cb_ref, hp_ref, nw_ref = [next(it) for _ in range(7)]
    h0_refs = (next(it), next(it)) if has_h0 else None
    y_ref = next(it)
    s_refs = (next(it), next(it)) if want_state else None
    xs_ref, bc_ref, yacc_ref, st_ref = [next(it) for _ in range(4)]

    q = SSD_CHUNK
    nc = seq // q
    unrolled = nc <= 2
    gn = SSD_GROUPS * SSD_STATE

    cw = cw_ref[...]
    cb = cb_ref[...]
    rid = _iota((q, SSD_CONV_DIM), 0)

    def rows(c):
        return pl.ds(c * q, q) if isinstance(c, int) else pl.ds(pl.multiple_of(c * q, q), q)

    def conv_chunk(c):
        xc = xbc_ref[rows(c), :]
        if isinstance(c, int):
            prev = xbc_ref[c * q - 1:c * q, :] if c > 0 else jnp.zeros((1, SSD_CONV_DIM), F32)
            nxt = xbc_ref[(c + 1) * q:(c + 1) * q + 1, :] if c < nc - 1 else jnp.zeros((1, SSD_CONV_DIM), F32)
        else:
            prev = xbc_ref[pl.ds(jnp.maximum(c * q - 1, 0), 1), :]
            nxt = xbc_ref[pl.ds(jnp.minimum(c * q + q, seq - 1), 1), :]
            prev = jnp.where(c > 0, prev, 0.0)
            nxt = jnp.where(c < nc - 1, nxt, 0.0)
        xp = jnp.where(rid == 0, prev, pltpu.roll(xc, 1, axis=0))
        xn = jnp.where(rid == q - 1, nxt, pltpu.roll(xc, q - 1, axis=0))
        v = _silu(xp * cw[0:1] + xc * cw[1:2] + xn * cw[2:3] + cb)
        xs_ref[rows(c), :] = v[:, :SSD_INNER]
        bc_ref[rows(c), :] = v[:, SSD_INNER:]

    hp = hp_ref[...]
    expand = (_iota((LANE, SSD_INNER), 1) // SSD_HEADDIM == _iota((LANE, SSD_INNER), 0)).astype(BF16)
    blockmask = (_iota((gn, SSD_INNER), 0) // SSD_STATE
                 == _iota((gn, SSD_INNER), 1) // (SSD_INNER // SSD_GROUPS))
    ri = _iota((q, q), 0)
    ci = _iota((q, q), 1)
    lane = _iota((q, LANE), 1)
    lo_half = lane < SSD_STATE
    d_e = _dot_x3(jnp.broadcast_to(hp[4:5], (8, LANE)), expand)[0:1]
    nw = nw_ref[...]

    def scan_chunk(c, fwd, a, bias, causal, tri):
        xs = xs_ref[rows(c), :]
        bc = bc_ref[rows(c), :]
        bm = bc[:, :gn]
        cm = bc[:, gn:]
        raw = dt_ref[rows(c), :] + bias
        dt = jnp.maximum(raw, 0.0) + jnp.log1p(jnp.exp(-jnp.abs(raw)))
        cum = _dot_3x(tri, dt * a)
        cum_t = cum.T
        dt_t = dt.T
        last = cum[q - 1:q] if fwd else cum[0:1]
        ecum = jnp.exp(cum)
        ex = _dot_x2(jnp.concatenate([ecum, dt * jnp.exp(last - cum)], axis=0), expand)
        ecum_e, w_e = ex[0:q], ex[q:2 * q]
        cdec_e = ecum_e[q - 1:q] if fwd else ecum_e[0:1]
        st = st_ref[...]
        cmb = cm.astype(BF16)
        bmb = bm.astype(BF16)
        y = _dot(cmb, st.astype(BF16)) * ecum_e
        cb0 = _dot_nt(jnp.where(lo_half, cm, 0.0).astype(BF16), bmb)
        cb1 = _dot_nt(jnp.where(lo_half, 0.0, cm).astype(BF16), bmb)
        pairs = []
        for j in range(SSD_HEADS // 2):
            cbg = cb0 if j < SSD_HEADS // 4 else cb1
            xp = xs[:, j * LANE:(j + 1) * LANE]
            scs = []
            for half in range(2):
                h = 2 * j + half
                seg = cum[:, h:h + 1] - cum_t[h:h + 1, :]
                dec = jnp.exp(jnp.where(causal, seg, -jnp.inf))
                scs.append((cbg * dec * dt_t[h:h + 1, :]).astype(BF16))
            rhs = jnp.concatenate([jnp.where(lo_half, xp, 0.0), jnp.where(lo_half, 0.0, xp)], axis=0)
            pairs.append(_dot(jnp.concatenate(scs, axis=1), rhs.astype(BF16)))
        y = y + jnp.concatenate(pairs, axis=1)
        new = _dot(bm.T.astype(BF16), (xs * w_e).astype(BF16))
        st_ref[...] = jnp.where(blockmask, st * cdec_e + new, 0.0)
        if fwd:
            yacc_ref[rows(c), :] = y
        else:
            y = y + yacc_ref[rows(c), :] + xs * d_e
            y = y * _silu(z_ref[rows(c), :])
            ms = jnp.mean(y * y, axis=-1, keepdims=True)
            y_ref[rows(c), :] = (y * lax.rsqrt(ms + EPS) * nw).astype(y_ref.dtype)

    def scan_pass(fwd):
        a = -jnp.exp(hp[0:1] if fwd else hp[1:2])
        bias = hp[2:3] if fwd else hp[3:4]
        causal = (ri >= ci) if fwd else (ri <= ci)
        tri = causal.astype(BF16)
        if has_h0:
            st_ref[...] = h0_refs[0 if fwd else 1][0]
        else:
            st_ref[...] = jnp.zeros((gn, SSD_INNER), F32)
        if unrolled:
            for k in range(nc):
                scan_chunk(k if fwd else nc - 1 - k, fwd, a, bias, causal, tri)
        else:
            def body(k, carry):
                c = k if fwd else nc - 1 - k
                scan_chunk(c, fwd, a, bias, causal, tri)
                if fwd:
                    conv_chunk(jnp.minimum(c + 1, nc - 1))
                return carry

            lax.fori_loop(0, nc, body, 0)
        if want_state:
            for j in range(SSD_INNER // LANE):
                g = (j * LANE) // (SSD_INNER // SSD_GROUPS)
                t = st_ref[:, j * LANE:(j + 1) * LANE].T
                s_refs[0 if fwd else 1][0, j * LANE:(j + 1) * LANE, :] = t[:, g * SSD_STATE:(g + 1) * SSD_STATE]

    if unrolled:
        for c in range(nc):
            conv_chunk(c)
    else:
        conv_chunk(0)
    scan_pass(True)
    scan_pass(False)


def _ssd(z, xbc, dt, cw, cb, hp, nw, h0, *, seq):
    t = z.shape[0]
    n_seq = t // seq
    has_h0 = h0 is not None
    want_state = not has_h0
    gn = SSD_GROUPS * SSD_STATE
    tok = lambda w: pl.BlockSpec((seq, w), lambda b: (b, 0))
    full = lambda a: pl.BlockSpec(a.shape, lambda b: (0,) * a.ndim)
    in_specs = [tok(SSD_INNER), tok(SSD_CONV_DIM), tok(LANE), full(cw), full(cb), full(hp), full(nw)]
    args = [z, xbc, dt, cw, cb, hp, nw]
    if has_h0:
        in_specs += [pl.BlockSpec((1, gn, SSD_INNER), lambda b: (b, 0, 0))] * 2
        args += list(h0)
    out_shape = [jax.ShapeDtypeStruct((t, SSD_INNER), BF16)]
    out_specs = [tok(SSD_INNER)]
    if want_state:
        out_shape += [jax.ShapeDtypeStruct((n_seq, SSD_INNER, SSD_STATE), F32)] * 2
        out_specs += [pl.BlockSpec((1, SSD_INNER, SSD_STATE), lambda b: (b, 0, 0))] * 2

    return pl.pallas_call(
        functools.partial(_ssd_kernel, seq=seq, has_h0=has_h0, want_state=want_state),
        grid=(n_seq,),
        in_specs=in_specs,
        out_specs=out_specs,
        out_shape=out_shape,
        scratch_shapes=[pltpu.VMEM((seq, SSD_INNER), F32),
                        pltpu.VMEM((seq, 2 * gn), F32),
                        pltpu.VMEM((seq, SSD_INNER), F32),
                        pltpu.VMEM((gn, SSD_INNER), F32)],
        compiler_params=_cparams(1),
        name="ssd_lat" if has_h0 else "ssd_ctx",
    )(*args)


def _softmax_pv(ss, vals):
    m = ss[0].max(axis=-1, keepdims=True)
    for s in ss[1:]:
        m = jnp.maximum(m, s.max(axis=-1, keepdims=True))
    ps = [jnp.exp2(s - m) for s in ss]
    den = ps[0].sum(axis=-1, keepdims=True)
    for p in ps[1:]:
        den = den + p.sum(axis=-1, keepdims=True)
    o = _dot(ps[0].astype(BF16), vals[0]())
    for p, v in zip(ps[1:], vals[1:]):
        o = o + _dot(p.astype(BF16), v())
    return o / den


def _attend_blocks(seq, units, emit, s_scr, qrows):
    nblk = seq // qrows

    def scores(u, blk, static):
        q_ref, rows, keys, _ = units[u]
        r = blk * rows if static else pl.multiple_of(blk * rows, rows)
        qb = q_ref[pl.ds(r, rows), :]
        return [_dot_nt(qb, k()) for k in keys]

    if s_scr is None:
        for i in range(nblk):
            emit(i * qrows, [_softmax_pv(scores(u, i, True), units[u][3]) for u in range(len(units))])
        return

    def put(slot, blk):
        for u in range(len(units)):
            off = 0
            for s in scores(u, blk, isinstance(blk, int)):
                s_scr[slot][u][:, off:off + s.shape[1]] = s
                off += s.shape[1]

    def take(slot, blk):
        outs = []
        for u, (_, _, keys, vals) in enumerate(units):
            off = 0
            ss = []
            for k in keys:
                n = k().shape[0]
                ss.append(s_scr[slot][u][:, off:off + n])
                off += n
            outs.append(_softmax_pv(ss, vals))
        emit(blk * qrows if isinstance(blk, int) else pl.multiple_of(blk * qrows, qrows), outs)

    put(0, 0)

    def body(ii, carry):
        b0 = 2 * ii
        put(1, b0 + 1)
        take(0, b0)

        def more():
            put(0, b0 + 2)
            take(1, b0 + 1)

        def last():
            take(1, b0 + 1)

        lax.cond(b0 + 2 < nblk, more, last)
        return carry

    lax.fori_loop(0, nblk // 2, body, 0)


def _mla_kernel(*refs, seq, lat):
    it = iter(refs)
    qa_ref, kv_ref, qan_ref, kvn_ref, wq_ref, wk_ref, wv_ref, qn_ref, kn_ref = [next(it) for _ in range(9)]
    if lat:
        cckv_ref, ckpe_ref, cos_ref, sin_ref = [next(it) for _ in range(4)]
    o_ref = next(it)
    if not lat:
        ckv_out, kpe_out = next(it), next(it)
    q_s, k_s, v_s, qh0_s, qh1_s, kh0_s, kh1_s = [next(it) for _ in range(7)]
    s_scr = None
    if lat:
        kc_s, vc_s, kch0_s, kch1_s = [next(it) for _ in range(4)]
        s_scr = [[next(it), next(it)], [next(it), next(it)]]

    kv = kv_ref[...]
    kpe = kv[:, LANE:]
    ckv_raw = kv[:, :LANE]
    ckv = ckv_raw * lax.rsqrt(jnp.mean(ckv_raw * ckv_raw, axis=-1, keepdims=True) + EPS) * kvn_ref[...]
    qa = qa_ref[...]
    qa = qa * lax.rsqrt(jnp.mean(qa * qa, axis=-1, keepdims=True) + EPS) * qan_ref[...]
    if not lat:
        ckv_out[...] = ckv
        kpe_out[...] = kpe
    ckvb = ckv.astype(BF16)
    q_s[...] = _dot(qa.astype(BF16), wq_ref[...])
    k_s[...] = _dot(ckvb, wk_ref[...])
    v_s[...] = _dot(ckvb, wv_ref[...]).astype(BF16)
    if lat:
        ccb = cckv_ref[0].astype(BF16)
        kc_s[...] = _dot(ccb, wk_ref[...])
        vc_s[...] = _dot(ccb, wv_ref[...]).astype(BF16)
        ckpe = ckpe_ref[0]
        cos = cos_ref[...]
        sin = sin_ref[...]
        dist = MLA_ROPE // 4
        li = _iota((2 * LANE, 2 * LANE), 0)
        lj = _iota((2 * LANE, 2 * LANE), 1)
        swap = (li == jnp.where((lj % (2 * dist)) < dist, lj + dist, lj - dist)).astype(BF16)
    qn = qn_ref[...]
    kn = kn_ref[...]
    qrows = LAT_Q_ROWS if lat else Q_BLOCK
    lo_lanes = _iota((qrows, LANE), 1) < MLA_V

    def head_norm(x, g):
        return x * lax.rsqrt(jnp.sum(x * x, axis=-1, keepdims=True) * (1.0 / MLA_QK) + EPS) * g

    def rotate(x2):
        return x2 * cos + _dot_x2(x2, swap) * sin

    for j in range(MLA_HEADS // 2):
        pair = slice(j * LANE, (j + 1) * LANE)
        sls = [slice((2 * j + half) * LANE, (2 * j + half + 1) * LANE) for half in range(2)]
        q2 = jnp.concatenate([head_norm(q_s[:, sl], qn) for sl in sls], axis=1)
        k2 = jnp.concatenate([head_norm(k_s[:, sl] + kpe, kn) for sl in sls], axis=1)
        if lat:
            q2 = rotate(q2)
            k2 = rotate(k2)
            for half, kch_s in enumerate((kch0_s, kch1_s)):
                kch_s[...] = head_norm(kc_s[:, sls[half]] + ckpe, kn).astype(BF16)
        for half, (qh_s, kh_s) in enumerate(((qh0_s, kh0_s), (qh1_s, kh1_s))):
            qh_s[...] = q2[:, half * LANE:(half + 1) * LANE].astype(BF16)
            kh_s[...] = k2[:, half * LANE:(half + 1) * LANE].astype(BF16)

        def emit(r0, outs, pair=pair):
            o_ref[pl.ds(r0, qrows), pair] = jnp.where(lo_lanes, outs[0], outs[1]).astype(o_ref.dtype)

        units = []
        for half, (qh_s, kh_s) in enumerate(((qh0_s, kh0_s), (qh1_s, kh1_s))):
            keys = [lambda r=kh_s: r[...]]
            vals = [lambda pair=pair: v_s[:, pair]]
            if lat:
                keys = [lambda r=(kch0_s, kch1_s)[half]: r[...]] + keys
                vals = [lambda pair=pair: vc_s[:, pair]] + vals
            units.append((qh_s, qrows, keys, vals))
        _attend_blocks(seq, units, emit, s_scr, qrows)


def _mla(qa, kv, qan, kvn, wq, wk, wv, qn, kn, cache, tables, *, seq):
    t = qa.shape[0]
    n_seq = t // seq
    lat = cache is not None
    hv = MLA_HEADS * MLA_V
    tok = lambda w: pl.BlockSpec((seq, w), lambda b: (b, 0))
    full = lambda a: pl.BlockSpec(a.shape, lambda b: (0,) * a.ndim)
    args = [qa, kv, qan, kvn, wq, wk, wv, qn, kn]
    in_specs = [tok(MLA_Q_LORA), tok(2 * LANE)] + [full(a) for a in args[2:]]
    past = 0
    if lat:
        past = cache[0].shape[1]
        in_specs += [pl.BlockSpec((1, past, LANE), lambda b: (b, 0, 0))] * 2 + [full(tables[0]), full(tables[1])]
        args += [cache[0], cache[1], tables[0], tables[1]]
    out_shape = [jax.ShapeDtypeStruct((t, hv), BF16)]
    out_specs = [tok(hv)]
    if not lat:
        out_shape += [jax.ShapeDtypeStruct((t, LANE), F32)] * 2
        out_specs += [pl.BlockSpec((seq, LANE), lambda b: (b, 0))] * 2
    scratch = [pltpu.VMEM((seq, MLA_HEADS * LANE), F32),
               pltpu.VMEM((seq, MLA_HEADS * LANE), F32),
               pltpu.VMEM((seq, hv), BF16)] + [pltpu.VMEM((seq, LANE), BF16)] * 4
    if lat:
        scratch += [pltpu.VMEM((past, MLA_HEADS * LANE), F32),
                    pltpu.VMEM((past, hv), BF16),
                    pltpu.VMEM((past, LANE), BF16),
                    pltpu.VMEM((past, LANE), BF16)]
        scratch += [pltpu.VMEM((LAT_Q_ROWS, past + seq), F32)] * 4

    return pl.pallas_call(
        functools.partial(_mla_kernel, seq=seq, lat=lat),
        grid=(n_seq,),
        in_specs=in_specs,
        out_specs=out_specs,
        out_shape=out_shape,
        scratch_shapes=scratch,
        compiler_params=_cparams(1),
        name="mla_lat" if lat else "mla_ctx",
    )(*args)


def _gqa_kernel(*refs, seq, lat):
    it = iter(refs)
    q_ref, k_ref, v_ref, qn_ref, kn_ref = [next(it) for _ in range(5)]
    if lat:
        ck_ref, cv_ref, cos_ref, sin_ref = [next(it) for _ in range(4)]
    o_ref = next(it)
    if not lat:
        kn_out = next(it)
    kn_s, vb_s, q2_s = [next(it) for _ in range(3)]
    s_scr = None
    if lat:
        ckb_s, cvb_s = next(it), next(it)
        s_scr = [[next(it)], [next(it)]]

    hd = GQA_HEAD_DIM
    nq = GQA_HEADS * hd
    nk = GQA_KV_HEADS * LANE
    qrows = LAT_Q_ROWS if lat else Q_BLOCK
    nblk = seq // qrows
    wide = 2 * LANE
    li = _iota((wide, wide), 0)
    lj = _iota((wide, wide), 1)
    same_head = (li // hd == lj // hd).astype(BF16)
    if lat:
        dist = hd // 4
        swap = (li == jnp.where((lj % (2 * dist)) < dist, lj + dist, lj - dist)).astype(BF16)
        cos = cos_ref[...]
        sin = sin_ref[...]
    lo_q = _iota((qrows, LANE), 1) < hd

    def prepared(x_ref, g_ref, b):
        sl = slice(b * wide, (b + 1) * wide)
        x = x_ref[:, sl]
        ss = _dot_x2(x * x, same_head)
        xn = x * lax.rsqrt(ss * (1.0 / hd) + EPS) * g_ref[:, sl]
        if not lat:
            return xn, xn
        return xn, xn * cos + _dot_x2(xn, swap) * sin

    for b in range(nk // wide):
        sl = slice(b * wide, (b + 1) * wide)
        kn, kr = prepared(k_ref, kn_ref, b)
        if not lat:
            kn_out[:, sl] = kn
        kn_s[:, sl] = kr.astype(BF16)
    if lat:
        for src_ref, dst in ((ck_ref, ckb_s), (cv_ref, cvb_s)):
            for h in range(GQA_KV_HEADS):
                head = src_ref[0, :, h * hd:(h + 1) * hd].astype(BF16)
                for rep in range(LANE // hd):
                    dst[:, h * LANE + rep * hd:h * LANE + (rep + 1) * hd] = head
    vb_s[...] = v_ref[...].astype(BF16)
    for b in range(nq // wide):
        _, qr = prepared(q_ref, qn_ref, b)
        for half in range(wide // LANE):
            p = b * (wide // LANE) + half
            for i in range(nblk):
                blk = qr[i * qrows:(i + 1) * qrows, half * LANE:(half + 1) * LANE]
                q2_s[p, 2 * i * qrows:(2 * i + 1) * qrows, :] = jnp.where(lo_q, blk, 0.0).astype(BF16)
                q2_s[p, (2 * i + 1) * qrows:(2 * i + 2) * qrows, :] = jnp.where(lo_q, 0.0, blk).astype(BF16)

    for j in range(GQA_HEADS // 2):
        pair = slice(j * LANE, (j + 1) * LANE)
        g = (2 * j) // (GQA_HEADS // GQA_KV_HEADS)
        gs = slice(g * LANE, (g + 1) * LANE)

        def emit(r0, outs, pair=pair):
            o = outs[0]
            o_ref[pl.ds(r0, qrows), pair] = jnp.where(lo_q, o[:qrows], o[qrows:]).astype(o_ref.dtype)

        keys = [lambda gs=gs: kn_s[:, gs]]
        vals = [lambda gs=gs: vb_s[:, gs]]
        if lat:
            keys = [lambda gs=gs: ckb_s[:, gs]] + keys
            vals = [lambda gs=gs: cvb_s[:, gs]] + vals
        _attend_blocks(seq, [(q2_s.at[j], 2 * qrows, keys, vals)], emit, s_scr, qrows)


def _gqa(q, k, v, qn, kn, cache, tables, *, seq):
    t = q.shape[0]
    n_seq = t // seq
    lat = cache is not None
    nq = GQA_HEADS * GQA_HEAD_DIM
    nk = GQA_KV_HEADS * LANE
    tok = lambda w: pl.BlockSpec((seq, w), lambda b: (b, 0))
    full = lambda a: pl.BlockSpec(a.shape, lambda b: (0,) * a.ndim)
    args = [q, k, v, qn, kn]
    in_specs = [tok(nq), tok(nk), tok(nk), full(qn), full(kn)]
    past = 0
    if lat:
        past = cache[0].shape[1]
        nkv = cache[0].shape[2]
        in_specs += [pl.BlockSpec((1, past, nkv), lambda b: (b, 0, 0))] * 2 + [full(tables[0]), full(tables[1])]
        args += [cache[0], cache[1], tables[0], tables[1]]
    out_shape = [jax.ShapeDtypeStruct((t, nq), BF16)]
    out_specs = [tok(nq)]
    if not lat:
        out_shape += [jax.ShapeDtypeStruct((t, nk), F32)]
        out_specs += [pl.BlockSpec((seq, nk), lambda b: (b, 0))]
    scratch = [pltpu.VMEM((seq, nk), BF16),
               pltpu.VMEM((seq, nk), BF16),
               pltpu.VMEM((GQA_HEADS // 2, 2 * seq, LANE), BF16)]
    if lat:
        scratch += [pltpu.VMEM((past, nk), BF16), pltpu.VMEM((past, nk), BF16)]
        scratch += [pltpu.VMEM((2 * LAT_Q_ROWS, past + seq), F32)] * 2

    return pl.pallas_call(
        functools.partial(_gqa_kernel, seq=seq, lat=lat),
        grid=(n_seq,),
        in_specs=in_specs,
        out_specs=out_specs,
        out_shape=out_shape,
        scratch_shapes=scratch,
        compiler_params=_cparams(1),
        name="gqa_lat" if lat else "gqa_ctx",
    )(*args)


def _route(sel, sc):
    gs = []
    for g in range(N_GROUPS):
        r = sel[g * GROUP_SIZE:(g + 1) * GROUP_SIZE]
        best = None
        for i in range(GROUP_SIZE):
            for j in range(i + 1, GROUP_SIZE):
                s = r[i] + r[j]
                best = s if best is None else jnp.maximum(best, s)
        gs.append(best)
    cur = gs[0]
    grp = jnp.zeros_like(cur, dtype=jnp.int32)
    for g in range(1, N_GROUPS):
        upd = gs[g] > cur
        grp = jnp.where(upd, g, grp)
        cur = jnp.where(upd, gs[g], cur)

    def pick(rows, i):
        v = rows[i]
        for g in range(1, N_GROUPS):
            v = jnp.where(grp == g, rows[g * GROUP_SIZE + i], v)
        return v

    cand = [pick(sel, i) for i in range(GROUP_SIZE)]
    csc = [pick(sc, i) for i in range(GROUP_SIZE)]
    v1, i1, w1 = cand[0], jnp.zeros_like(grp), csc[0]
    for i in range(1, GROUP_SIZE):
        upd = cand[i] > v1
        v1 = jnp.where(upd, cand[i], v1)
        i1 = jnp.where(upd, i, i1)
        w1 = jnp.where(upd, csc[i], w1)
    v2 = i2 = w2 = None
    for i in range(GROUP_SIZE):
        ok = i1 != i
        if v2 is None:
            v2 = jnp.where(ok, cand[i], -jnp.inf)
            i2 = jnp.zeros_like(grp)
            w2 = csc[i]
        else:
            upd = ok & (cand[i] > v2)
            v2 = jnp.where(upd, cand[i], v2)
            i2 = jnp.where(upd, i, i2)
            w2 = jnp.where(upd, csc[i], w2)
    tot = w1 + w2
    local = [jnp.where(i1 == i, w1 / tot, 0.0) + jnp.where(i2 == i, w2 / tot, 0.0)
             for i in range(GROUP_SIZE)]
    onehot = [jnp.where(grp == g, 1.0, 0.0) for g in range(N_GROUPS)]
    return local, onehot


def _outproj_kernel(*refs, n_y):
    x_ref, mod_ref = refs[0], refs[1]
    y_refs = refs[2:2 + n_y]
    w_ref, g_ref, wr_ref, rb_ref, x1_ref, h2_ref, gate_ref, cnt_ref, w_b, wr_b = refs[2 + n_y:]

    @pl.when(pl.program_id(0) == 0)
    def _():
        w_b[...] = w_ref[...].astype(BF16)
        wr_b[...] = jnp.zeros_like(wr_b)
        for j, term in enumerate(_split3(wr_ref[...])):
            wr_b[:, j * N_EXPERTS:(j + 1) * N_EXPERTS] = term

    m = mod_ref[0]
    off = 0
    out = None
    for y_ref in y_refs:
        k = y_ref.shape[1]
        part = _dot(y_ref[...], w_b[off:off + k, :])
        out = part if out is None else out + part
        off += k
    x1 = x_ref[...] + m[:, 2 * D:3 * D] * out
    x1_ref[...] = x1
    h2 = _norm_mod(x1, g_ref[...], m[:, 3 * D:4 * D], m[:, 4 * D:5 * D])
    h2_hi = h2.astype(BF16)
    h2_ref[...] = h2_hi
    h2_mid = (h2 - h2_hi.astype(F32)).astype(BF16)
    half = h2.shape[0] // 2
    parts = jnp.concatenate([_dot(h2_hi[r], wr_b[...]) + _dot(h2_mid[r], wr_b[...])
                             for r in (slice(0, half), slice(half, None))], axis=0)
    logits = (parts + pltpu.roll(parts, LANE - N_EXPERTS, axis=1)
              + pltpu.roll(parts, LANE - 2 * N_EXPERTS, axis=1))
    sc_t = _sigmoid(logits.T[0:N_EXPERTS, :])
    sel_t = sc_t + rb_ref[...]
    local, onehot = _route([sel_t[e:e + 1] for e in range(N_EXPERTS)],
                           [sc_t[e:e + 1] for e in range(N_EXPERTS)])
    t = x1.shape[0]
    rows_t = (local + [jnp.zeros((GROUP_LANE - GROUP_SIZE, t), F32)] + onehot
              + [jnp.zeros((LANE - GROUP_LANE - N_GROUPS, t), F32)])
    routed = jnp.concatenate(rows_t, axis=0).T
    gate_ref[...] = routed
    cnt = jnp.sum(routed, axis=0, keepdims=True).astype(jnp.int32)
    cnt_ref[0] = jnp.broadcast_to(cnt, (8, LANE))


def _outproj(x, mod, ys, w, g, layer, wr, rb):
    t = x.shape[0]
    full = lambda a: pl.BlockSpec(a.shape, lambda i: (0,) * a.ndim)
    return pl.pallas_call(
        functools.partial(_outproj_kernel, n_y=len(ys)),
        grid=(t // TOKEN_TILE,),
        in_specs=[pl.BlockSpec((TOKEN_TILE, D), lambda i: (i, 0)),
                  _mod_spec(mod, t, TOKEN_TILE)]
                 + [pl.BlockSpec((TOKEN_TILE, y.shape[1]), lambda i: (i, 0)) for y in ys]
                 + [full(w), _layer_spec(g, layer), full(wr), full(rb)],
        out_specs=[pl.BlockSpec((TOKEN_TILE, D), lambda i: (i, 0)),
                   pl.BlockSpec((TOKEN_TILE, D), lambda i: (i, 0)),
                   pl.BlockSpec((TOKEN_TILE, LANE), lambda i: (i, 0)),
                   pl.BlockSpec((1, 8, LANE), lambda i: (i, 0, 0))],
        out_shape=[jax.ShapeDtypeStruct((t, D), F32),
                   jax.ShapeDtypeStruct((t, D), BF16),
                   jax.ShapeDtypeStruct((t, LANE), F32),
                   jax.ShapeDtypeStruct((t // TOKEN_TILE, 8, LANE), jnp.int32)],
        scratch_shapes=[pltpu.VMEM(w.shape, BF16), pltpu.VMEM((D, LANE), BF16)],
        compiler_params=_cparams(1),
        name="outproj_router",
    )(x, mod[0], *ys, w, g, wr, rb)


def _pad_to(n, shift):
    return ((n + (1 << shift) - 1) >> shift) << shift


SEG_SHIFT = SEG_ALIGN.bit_length() - 1
TILE_SHIFT = EXPERT_TILE.bit_length() - 1


def _start_segments(plan_ref, i, nt, make):
    for g in range(N_GROUPS):
        n = plan_ref[i * N_GROUPS + g]
        lo = plan_ref[(nt + i) * N_GROUPS + g]
        base = plan_ref[(2 * nt + i) * N_GROUPS + g]

        def body(c, carry, lo=lo, base=base):
            for cp in make(pl.multiple_of(lo + c * SEG_ALIGN, SEG_ALIGN),
                           pl.multiple_of(base + c * SEG_ALIGN, SEG_ALIGN)):
                cp.start()
            return carry

        lax.fori_loop(0, (n + SEG_ALIGN - 1) >> SEG_SHIFT, body, 0)


def _wait_segments(plan_ref, i, make):
    total = jnp.int32(0)
    for g in range(N_GROUPS):
        total = total + ((plan_ref[i * N_GROUPS + g] + SEG_ALIGN - 1) >> SEG_SHIFT)

    def body(c, carry):
        for cp in make(0, 0):
            cp.wait()
        return carry

    lax.fori_loop(0, total, body, 0)


def _dispatch_kernel(cnt_ref, h_ref, r_ref, xs_hbm, dest_ref, plan_ref,
                     xs_l, zx, tril_s, fill_s, sem, *, nt, n_rows):
    i = pl.program_id(0)
    t = TOKEN_TILE
    k_tiles = n_rows // EXPERT_TILE

    @pl.when(i == 0)
    def _():
        zx[...] = jnp.zeros_like(zx)
        tril_s[...] = (_iota((t, t), 0) >= _iota((t, t), 1)).astype(BF16)
        tots = []
        for g in range(N_GROUPS):
            tots.append(lax.fori_loop(
                0, nt, lambda tt, acc, g=g: acc + _pad_to(cnt_ref[tt * N_GROUPS + g], SEG_SHIFT), jnp.int32(0)))
        starts = []
        cur = jnp.int32(0)
        for g in range(N_GROUPS):
            starts.append(cur)
            fill_s[2 * g] = cur + tots[g]
            cur = cur + _pad_to(tots[g], TILE_SHIFT)
            fill_s[2 * g + 1] = cur
        fill_s[2 * N_GROUPS] = cur
        fill_s[2 * N_GROUPS + 1] = n_rows

        def per_tile(tt, run):
            lo = jnp.int32(0)
            new = []
            for g in range(N_GROUPS):
                n = cnt_ref[tt * N_GROUPS + g]
                plan_ref[tt * N_GROUPS + g] = n
                plan_ref[(nt + tt) * N_GROUPS + g] = lo
                plan_ref[(2 * nt + tt) * N_GROUPS + g] = run[g]
                seg = _pad_to(n, SEG_SHIFT)
                lo = lo + seg
                new.append(run[g] + seg)
            return tuple(new)

        lax.fori_loop(0, nt, per_tile, tuple(starts))
        for k in range(k_tiles):
            row = k * EXPERT_TILE
            grp = 0
            for g in range(1, N_GROUPS):
                grp = grp + (row >= starts[g]).astype(jnp.int32)
            plan_ref[3 * nt * N_GROUPS + k] = grp
        plan_ref[3 * nt * N_GROUPS + k_tiles] = cur >> TILE_SHIFT

    routed = r_ref[...]
    lane = _iota((t, LANE), 1)
    is_grp = (lane >= GROUP_LANE) & (lane < GROUP_LANE + N_GROUPS)
    onehot = jnp.where(is_grp, routed, 0.0)
    rank = _dot(tril_s[...], onehot.astype(BF16))
    seg_lo = jnp.zeros((1, LANE), F32)
    lane1 = _iota((1, LANE), 1)
    for g in range(N_GROUPS):
        lo = plan_ref[(nt + i) * N_GROUPS + g]
        seg_lo = jnp.where(lane1 == GROUP_LANE + g, (lo - 1).astype(F32), seg_lo)
    val = jnp.where(onehot > 0.0, rank + seg_lo, 0.0)
    dest = _dot_x2(val, jnp.ones((LANE, LANE), BF16))
    dest_ref[...] = dest
    dest_row = dest.T[0:1, :].astype(jnp.int32)
    perm = jnp.where(_iota((DISPATCH_ROWS, t), 0) == dest_row, 1.0, 0.0).astype(BF16)
    r_hi, r_mid, r_lo = _split3(routed)
    packed = (r_hi.astype(F32) + pltpu.roll(r_mid.astype(F32), GROUP_SIZE, axis=1)
              + pltpu.roll(r_lo.astype(F32), 2 * GROUP_SIZE, axis=1)).astype(BF16)
    slot = i % 2
    xs_l[slot] = _dot(perm, jnp.concatenate([h_ref[...], packed], axis=1)).astype(BF16)

    def to_buffer(slot):
        def make(lo, base):
            return (pltpu.make_async_copy(xs_l.at[slot, pl.ds(lo, SEG_ALIGN)],
                                          xs_hbm.at[pl.ds(base, SEG_ALIGN)], sem.at[slot]),)
        return make

    _start_segments(plan_ref, i, nt, to_buffer(slot))

    @pl.when(i > 0)
    def _():
        _wait_segments(plan_ref, i - 1, to_buffer(1 - slot))

    @pl.when(i == nt - 1)
    def _():
        _wait_segments(plan_ref, i, to_buffer(slot))
        def zero(base):
            return (pltpu.make_async_copy(zx, xs_hbm.at[pl.ds(base, SEG_ALIGN)], sem.at[0]),)

        n_fill = jnp.int32(0)
        for r in range(N_GROUPS + 1):
            lo = fill_s[2 * r]
            chunks = (fill_s[2 * r + 1] - lo) >> SEG_SHIFT

            def body(c, carry, lo=lo):
                for cp in zero(pl.multiple_of(lo + c * SEG_ALIGN, SEG_ALIGN)):
                    cp.start()
                return carry

            lax.fori_loop(0, chunks, body, 0)
            n_fill = n_fill + chunks

        def wait_fill(c, carry):
            for cp in zero(0):
                cp.wait()
            return carry

        lax.fori_loop(0, n_fill, wait_fill, 0)


def _moe_rows(n_tokens):
    nt = n_tokens // TOKEN_TILE
    worst = n_tokens + nt * N_GROUPS * (SEG_ALIGN - 1) + N_GROUPS * (EXPERT_TILE - SEG_ALIGN)
    return _pad_to(worst, TILE_SHIFT)


def _dispatch(counts, h2, routed):
    t = h2.shape[0]
    nt = t // TOKEN_TILE
    n_rows = _moe_rows(t)
    plan_len = 3 * nt * N_GROUPS + n_rows // EXPERT_TILE + 1
    grid_spec = pltpu.PrefetchScalarGridSpec(
        num_scalar_prefetch=1,
        grid=(nt,),
        in_specs=[pl.BlockSpec((TOKEN_TILE, D), lambda i, c: (i, 0)),
                  pl.BlockSpec((TOKEN_TILE, LANE), lambda i, c: (i, 0))],
        out_specs=[pl.BlockSpec(memory_space=pl.ANY),
                   pl.BlockSpec((TOKEN_TILE, LANE), lambda i, c: (i, 0)),
                   pl.BlockSpec(memory_space=pltpu.SMEM)],
        scratch_shapes=[pltpu.VMEM((2, DISPATCH_ROWS, SORTED_WIDTH), BF16),
                        pltpu.VMEM((SEG_ALIGN, SORTED_WIDTH), BF16),
                        pltpu.VMEM((TOKEN_TILE, TOKEN_TILE), BF16),
                        pltpu.SMEM((2 * N_GROUPS + 2,), jnp.int32),
                        pltpu.SemaphoreType.DMA((2,))])
    return pl.pallas_call(
        functools.partial(_dispatch_kernel, nt=nt, n_rows=n_rows),
        grid_spec=grid_spec,
        out_shape=[jax.ShapeDtypeStruct((n_rows, SORTED_WIDTH), BF16),
                   jax.ShapeDtypeStruct((t, LANE), F32),
                   jax.ShapeDtypeStruct((plan_len,), jnp.int32)],
        compiler_params=_cparams(1),
        name="moe_dispatch",
    )(counts, h2, routed)


def _experts_kernel(plan_ref, xs_ref, wg_ref, wu_ref, wd_ref, y_ref, wg_b, wu_b, wd_b,
                    *, group_at, used_at):
    k = pl.program_id(0)
    used = plan_ref[used_at]
    group = plan_ref[group_at + k]
    prev = plan_ref[group_at + jnp.maximum(k - 1, 0)]

    @pl.when((k == 0) | (group != prev))
    def _():
        wg_b[...] = wg_ref[...].astype(BF16)
        wu_b[...] = wu_ref[...].astype(BF16)
        wd_b[...] = wd_ref[...].astype(BF16)

    @pl.when(k < used)
    def _():
        rows = xs_ref[:, :D]
        terms = xs_ref[:, D:].astype(F32)
        gates = (terms + pltpu.roll(terms, LANE - GROUP_SIZE, axis=1)
                 + pltpu.roll(terms, LANE - 2 * GROUP_SIZE, axis=1))
        acc = None
        for e in range(GROUP_SIZE):
            a = _dot(rows, wg_b[e])
            u = _dot(rows, wu_b[e])
            act = (_silu(a) * u * gates[:, e:e + 1]).astype(BF16)
            part = _dot(act, wd_b[e])
            acc = part if acc is None else acc + part
        y_ref[...] = acc.astype(y_ref.dtype)

    @pl.when(k >= used)
    def _():
        y_ref[...] = jnp.zeros_like(y_ref)


def _experts(plan, xs, wg, wu, wd, layer, nt):
    n_rows = xs.shape[0]
    k_tiles = n_rows // EXPERT_TILE
    at = 3 * nt * N_GROUPS
    wspec = lambda a, b: pl.BlockSpec((None, GROUP_SIZE, a, b), lambda k, p: (layer, p[at + k], 0, 0))
    grid_spec = pltpu.PrefetchScalarGridSpec(
        num_scalar_prefetch=1,
        grid=(k_tiles,),
        in_specs=[pl.BlockSpec((EXPERT_TILE, SORTED_WIDTH), lambda k, p: (k, 0)),
                  wspec(D, D_EXPERT), wspec(D, D_EXPERT), wspec(D_EXPERT, D)],
        out_specs=pl.BlockSpec((EXPERT_TILE, D), lambda k, p: (k, 0)),
        scratch_shapes=[pltpu.VMEM((GROUP_SIZE, D, D_EXPERT), BF16),
                        pltpu.VMEM((GROUP_SIZE, D, D_EXPERT), BF16),
                        pltpu.VMEM((GROUP_SIZE, D_EXPERT, D), BF16)])
    return pl.pallas_call(
        functools.partial(_experts_kernel, group_at=at, used_at=at + k_tiles),
        grid_spec=grid_spec,
        out_shape=jax.ShapeDtypeStruct((n_rows, D), BF16),
        compiler_params=_cparams(1),
        name="moe_experts",
    )(plan, xs, wg, wu, wd)


def _combine_kernel(plan_ref, x_ref, mod_ref, dest_ref, ys_hbm, o_ref, y_l, sem, *, nt):
    i = pl.program_id(0)

    slot = i % 2

    def from_buffer(slot):
        def make(lo, base):
            return (pltpu.make_async_copy(ys_hbm.at[pl.ds(base, SEG_ALIGN)],
                                          y_l.at[slot, pl.ds(lo, SEG_ALIGN)], sem.at[slot]),)
        return make

    @pl.when(i == 0)
    def _():
        y_l[...] = jnp.zeros_like(y_l)
        _start_segments(plan_ref, i, nt, from_buffer(slot))

    @pl.when(i + 1 < nt)
    def _():
        _start_segments(plan_ref, i + 1, nt, from_buffer(1 - slot))

    _wait_segments(plan_ref, i, from_buffer(slot))
    dest = dest_ref[...].astype(jnp.int32)
    lane = _iota(dest.shape, 1)
    back = jnp.concatenate(
        [jnp.where(dest == lane + j * LANE, 1.0, 0.0).astype(BF16) for j in range(DISPATCH_ROWS // LANE)], axis=1)
    y = _dot(back, y_l[slot])
    o_ref[...] = x_ref[...] + mod_ref[0][:, 5 * D:6 * D] * y


def _combine(plan, x1, mod, dest, ys):
    t = x1.shape[0]
    nt = t // TOKEN_TILE
    grid_spec = pltpu.PrefetchScalarGridSpec(
        num_scalar_prefetch=1,
        grid=(nt,),
        in_specs=[pl.BlockSpec((TOKEN_TILE, D), lambda i, p: (i, 0)),
                  _mod_spec(mod, t, TOKEN_TILE),
                  pl.BlockSpec((TOKEN_TILE, LANE), lambda i, p: (i, 0)),
                  pl.BlockSpec(memory_space=pl.ANY)],
        out_specs=pl.BlockSpec((TOKEN_TILE, D), lambda i, p: (i, 0)),
        scratch_shapes=[pltpu.VMEM((2, DISPATCH_ROWS, D), BF16),
                        pltpu.SemaphoreType.DMA((2,))])
    return pl.pallas_call(
        functools.partial(_combine_kernel, nt=nt),
        grid_spec=grid_spec,
        out_shape=jax.ShapeDtypeStruct((t, D), F32),
        compiler_params=_cparams(1),
        name="moe_combine",
    )(plan, x1, mod[0], dest, ys)


def _moe(x1, mod, h2, routed, cnt, wg, wu, wd, layer):
    nt = x1.shape[0] // TOKEN_TILE
    counts = cnt[:, 0, GROUP_LANE:GROUP_LANE + N_GROUPS].reshape(nt * N_GROUPS)
    xs, dest, plan = _dispatch(counts, h2, routed)
    ys = _experts(plan, xs, wg, wu, wd, layer, nt)
    return _combine(plan, x1, mod, dest, ys)


def _rope_angles(pos, half):
    inv = np.power(np.float32(ROPE_THETA), -np.arange(half, dtype=np.float32) / np.float32(half))
    ang = pos.astype(np.float32)[:, None] * inv[None, :]
    return np.cos(ang), np.sin(ang)


def _rope_tables(seq, rot, lead, lane_tiles):
    rows = seq // GRID_W
    row = np.repeat(np.arange(rows), GRID_W)
    col = np.tile(np.arange(GRID_W), rows)
    cr, sr = _rope_angles(row, rot // 4)
    cc, sc = _rope_angles(col, rot // 4)
    cos = np.concatenate([cr, cr, cc, cc], axis=1)
    sin = np.concatenate([-sr, sr, -sc, sc], axis=1)
    width = LANE // lane_tiles
    pad = width - lead - rot
    cos = np.concatenate([np.ones((seq, lead), np.float32), cos, np.ones((seq, pad), np.float32)], axis=1)
    sin = np.concatenate([np.zeros((seq, lead), np.float32), sin, np.zeros((seq, pad), np.float32)], axis=1)
    return np.tile(cos, (1, lane_tiles)), np.tile(sin, (1, lane_tiles))


def _pad_lanes(a, width):
    return jnp.pad(a, [(0, 0)] * (a.ndim - 1) + [(0, width - a.shape[-1])])


def _mla_weights(w_q_b, w_kv_b):
    wq = _pad_lanes(w_q_b.reshape(MLA_Q_LORA, MLA_HEADS, MLA_QK), LANE).reshape(MLA_Q_LORA, MLA_HEADS * LANE)
    kvb = w_kv_b.reshape(MLA_KV_LORA, MLA_HEADS, MLA_NOPE + MLA_V)
    wk = _pad_lanes(kvb[:, :, :MLA_NOPE], LANE).reshape(MLA_KV_LORA, MLA_HEADS * LANE)
    wv = kvb[:, :, MLA_NOPE:].reshape(MLA_KV_LORA, MLA_HEADS * MLA_V)
    return wq.astype(BF16), wk.astype(BF16), wv.astype(BF16)


def _undup_heads(a):
    s = a.shape[:-1]
    return a.reshape(s + (GQA_KV_HEADS, 2, GQA_HEAD_DIM))[..., 0, :]


def _state_to_kernel(h0):
    b = h0.shape[0]
    t = jnp.transpose(h0, (0, 3, 1, 2)).reshape(b, SSD_STATE, SSD_INNER)
    half = SSD_INNER // SSD_GROUPS
    col = jnp.arange(SSD_INNER) // half
    parts = [jnp.where(col == g, t, 0.0) for g in range(SSD_GROUPS)]
    return jnp.concatenate(parts, axis=1)


def _state_from_kernel(st):
    return st.reshape(st.shape[0], SSD_HEADS, SSD_HEADDIM, SSD_STATE)


def kernel(x_prompt, x_sample, c, state_ssd_fwd, state_ssd_bwd, cache_mla_ckv, cache_mla_kpe, cache_gqa_k, cache_gqa_v, c_ctx, g_mix, g_ffn, w_ada, b_ada, w_router, router_bias, w_exp_gate, w_exp_up, w_exp_down, w_in_even, ssd_conv_w, ssd_conv_b, ssd_a_log_fwd, ssd_a_log_bwd, ssd_dt_bias_fwd, ssd_dt_bias_bwd, ssd_d, ssd_norm, mla_q_a_norm, mla_w_q_b, mla_kv_a_norm, mla_w_kv_b, mla_q_norm, mla_k_norm, w_out_even, w_in_odd, gqa_q_norm, gqa_k_norm, w_out_odd):
    nb, ls, _ = x_prompt.shape
    db, dl, _ = x_sample.shape
    depth = w_ada.shape[0]
    n_ctx = nb * ls

    xs = [x_prompt.reshape(n_ctx, D), x_sample.reshape(db * dl, D)]
    seqs = (ls, dl)
    rows = 16
    cond = jnp.concatenate([c_ctx[None, :], c, jnp.zeros((rows - 1 - db, D), F32)], axis=0)
    mod_all = _ada_mod(cond, w_ada, b_ada).reshape(depth, rows, 1, 6 * D)
    g_mix3 = g_mix.reshape(depth, 1, D)
    g_ffn3 = g_ffn.reshape(depth, 1, D)

    wr = w_router
    rb = router_bias.reshape(N_EXPERTS, 1)
    wg_all, wu_all, wd_all = w_exp_gate, w_exp_up, w_exp_down

    outs = {}
    for i in range(depth):
        j = i // 2
        mods = [(mod_all, i, 0, 1), (mod_all, i, 1, db)]
        if i % 2 == 0:
            wq, wk, wv = _mla_weights(mla_w_q_b[j], mla_w_kv_b[j])
            cw = jnp.pad(ssd_conv_w[j], ((0, 5), (0, 0)))
            cb = ssd_conv_b[j].reshape(1, SSD_CONV_DIM)
            hp = _pad_lanes(jnp.stack([ssd_a_log_fwd[j], ssd_a_log_bwd[j], ssd_dt_bias_fwd[j],
                                       ssd_dt_bias_bwd[j], ssd_d[j]]), LANE)
            hp = jnp.pad(hp, ((0, 3), (0, 0)))
            nw = ssd_norm[j].reshape(1, SSD_INNER)
            qan = mla_q_a_norm[j].reshape(1, MLA_Q_LORA)
            kvn = mla_kv_a_norm[j].reshape(1, MLA_KV_LORA)
            qn = _pad_lanes(mla_q_norm[j].reshape(1, MLA_QK) * (MLA_QK ** -0.5 * LOG2E), LANE)
            kn = _pad_lanes(mla_k_norm[j].reshape(1, MLA_QK), LANE)
            h0 = (_state_to_kernel(state_ssd_fwd[:, j]), _state_to_kernel(state_ssd_bwd[:, j]))
            ckpe = jnp.pad(cache_mla_kpe[:, j], ((0, 0), (0, 0), (MLA_NOPE, LANE - MLA_QK)))
            tables = tuple(jnp.asarray(np.tile(tb, (1, 2))) for tb in _rope_tables(dl, MLA_ROPE, MLA_NOPE, 1))
            w_out = w_out_even[j]
            ys = []
            for s in range(2):
                z, xbc, qa, kv, dt = _inproj(xs[s], mods[s], g_mix3, i, w_in_even[j], EVEN_WIDTHS, EVEN_PIECES)
                if s == 0:
                    y, sf, sb = _ssd(z, xbc, dt, cw, cb, hp, nw, None, seq=seqs[s])
                    o, ckv_new, kpe_new = _mla(qa, kv, qan, kvn, wq, wk, wv, qn, kn, None, None, seq=seqs[s])
                    outs.setdefault("ssd_f", []).append(_state_from_kernel(sf))
                    outs.setdefault("ssd_b", []).append(_state_from_kernel(sb))
                    outs.setdefault("ckv", []).append(ckv_new.reshape(nb, ls, MLA_KV_LORA))
                    outs.setdefault("kpe", []).append(kpe_new[:, MLA_NOPE:MLA_QK].reshape(nb, ls, MLA_ROPE))
                else:
                    y, = _ssd(z, xbc, dt, cw, cb, hp, nw, h0, seq=seqs[s])
                    o, = _mla(qa, kv, qan, kvn, wq, wk, wv, qn, kn, (cache_mla_ckv[:, j], ckpe), tables,
                              seq=seqs[s])
                ys.append((y, o))
        else:
            nq = GQA_HEADS * GQA_HEAD_DIM
            nkv = GQA_KV_HEADS * GQA_HEAD_DIM
            nk = GQA_KV_HEADS * LANE
            qscale = GQA_HEAD_DIM ** -0.5 * LOG2E
            qn = jnp.tile(gqa_q_norm[j].reshape(1, GQA_HEAD_DIM) * qscale, (1, nq // GQA_HEAD_DIM))
            kn = jnp.tile(gqa_k_norm[j].reshape(1, GQA_HEAD_DIM), (1, nk // GQA_HEAD_DIM))
            cache = (cache_gqa_k[:, j].reshape(db, -1, nkv), cache_gqa_v[:, j].reshape(db, -1, nkv))
            tables = tuple(jnp.asarray(np.tile(tb, (1, 2))) for tb in _rope_tables(dl, GQA_HEAD_DIM, 0, 2))
            w_out = w_out_odd[j]
            ys = []
            for s in range(2):
                q, k, v = _inproj(xs[s], mods[s], g_mix3, i, w_in_odd[j], ODD_WIDTHS, ODD_PIECES)
                if s == 0:
                    o, k_new = _gqa(q, k, v, qn, kn, None, None, seq=seqs[s])
                    outs.setdefault("gk", []).append(
                        _undup_heads(k_new).reshape(nb, ls, GQA_KV_HEADS, GQA_HEAD_DIM))
                    outs.setdefault("gv", []).append(
                        _undup_heads(v).reshape(nb, ls, GQA_KV_HEADS, GQA_HEAD_DIM))
                else:
                    o, = _gqa(q, k, v, qn, kn, cache, tables, seq=seqs[s])
                ys.append((o,))
        for s in range(2):
            x1, h2, routed, cnt = _outproj(xs[s], mods[s], ys[s], w_out, g_ffn3, i, wr, rb)
            xs[s] = _moe(x1, mods[s], h2, routed, cnt, wg_all, wu_all, wd_all, i)

    stack = lambda key: jnp.stack(outs[key], axis=1)
    return (xs[0].reshape(nb, ls, D), xs[1].reshape(db, dl, D),
            stack("ssd_f"), stack("ssd_b"), stack("ckv"), stack("kpe"), stack("gk"), stack("gv"))
```

```python
import functools

import numpy as np

import jax
import jax.numpy as jnp
from jax import lax
from jax.experimental import pallas as pl
from jax.experimental.pallas import tpu as pltpu

F32 = jnp.float32
BF16 = jnp.bfloat16

D = 1024
EPS = 1e-6
GRID_W = 64
ROPE_THETA = 10000.0
SSD_HEADS = 16
SSD_HEADDIM = 64
SSD_INNER = 1024
SSD_STATE = 64
SSD_GROUPS = 2
SSD_CONV_DIM = 1280
SSD_CHUNK = 128
MLA_HEADS = 8
MLA_Q_LORA = 256
MLA_KV_LORA = 128
MLA_NOPE = 64
MLA_ROPE = 32
MLA_V = 64
MLA_QK = 96
GQA_HEADS = 16
GQA_KV_HEADS = 4
GQA_HEAD_DIM = 64
N_EXPERTS = 16
N_GROUPS = 4
GROUP_SIZE = 4
D_EXPERT = 256

LANE = 128
Q_BLOCK = 128
LAT_Q_ROWS = 256
LOG2E = 1.4426950408889634
TOKEN_TILE = 512
EXPERT_TILE = 512
SEG_ALIGN = 16
GROUP_LANE = 16
DISPATCH_ROWS = TOKEN_TILE + LANE
SORTED_WIDTH = D + LANE
ADA_TILE = 1536
VMEM_LIMIT = 56 * 1024 * 1024


def _cparams(n_axes):
    return pltpu.CompilerParams(dimension_semantics=("arbitrary",) * n_axes,
                                vmem_limit_bytes=VMEM_LIMIT)


def _dot(a, b):
    return jnp.dot(a, b, preferred_element_type=F32)


def _dot_nt(a, b):
    return lax.dot_general(a, b, (((1,), (1,)), ((), ())), preferred_element_type=F32)


def _split3(x):
    hi = x.astype(BF16)
    r1 = x - hi.astype(F32)
    mid = r1.astype(BF16)
    lo = (r1 - mid.astype(F32)).astype(BF16)
    return hi, mid, lo


def _dot_x3(x, e):
    hi, mid, lo = _split3(x)
    return _dot(hi, e) + _dot(mid, e) + _dot(lo, e)


def _dot_x2(x, e):
    hi = x.astype(BF16)
    mid = (x - hi.astype(F32)).astype(BF16)
    return _dot(hi, e) + _dot(mid, e)


def _dot_3x(e, x):
    hi, mid, lo = _split3(x)
    return _dot(e, hi) + _dot(e, mid) + _dot(e, lo)


def _dot_f32(a, b):
    a0, a1, a2 = _split3(a)
    b0, b1, b2 = _split3(b)
    return (_dot(a0, b0) + (_dot(a0, b1) + _dot(a1, b0))
            + (_dot(a1, b1) + _dot(a0, b2) + _dot(a2, b0)))


def _sigmoid(x):
    return 1.0 / (1.0 + jnp.exp(-x))


def _silu(x):
    return x * _sigmoid(x)


def _iota(shape, axis):
    return lax.broadcasted_iota(jnp.int32, shape, axis)


def _ada_kernel(c_ref, w_ref, b_ref, o_ref):
    a = _silu(c_ref[...]).astype(BF16)
    o_ref[0] = _dot(a, w_ref[0].astype(BF16)) + b_ref[0]


def _ada_mod(cond, w_ada, b_ada):
    depth, _, n = w_ada.shape
    rows = cond.shape[0]
    return pl.pallas_call(
        _ada_kernel,
        grid=(depth, n // ADA_TILE),
        in_specs=[pl.BlockSpec((rows, D), lambda l, j: (0, 0)),
                  pl.BlockSpec((1, D, ADA_TILE), lambda l, j: (l, 0, j)),
                  pl.BlockSpec((1, 1, ADA_TILE), lambda l, j: (l, 0, j))],
        out_specs=pl.BlockSpec((1, rows, ADA_TILE), lambda l, j: (l, 0, j)),
        out_shape=jax.ShapeDtypeStruct((depth, rows, n), F32),
        compiler_params=_cparams(2),
        name="ada_mod",
    )(cond, w_ada, b_ada.reshape(depth, 1, n))


def _mod_spec(mod, n_tokens, tile):
    _, layer, row0, n_rows = mod
    per_row = n_tokens // n_rows
    return pl.BlockSpec((None, 1, 1, 6 * D), lambda i, *_: (layer, row0 + (i * tile) // per_row, 0, 0))


def _layer_spec(a, layer):
    return pl.BlockSpec((None,) + a.shape[1:], lambda i, *_: (layer, 0, 0))


def _norm_mod(x, g, shift, scale):
    ms = jnp.mean(x * x, axis=-1, keepdims=True)
    y = x * lax.rsqrt(ms + EPS) * g
    return y * (1.0 + scale) + shift


def _inproj_kernel(x_ref, mod_ref, g_ref, w_ref, *refs, widths, pieces):
    o_refs, wb = refs[:-1], refs[-1]

    @pl.when(pl.program_id(0) == 0)
    def _():
        wb[...] = jnp.zeros_like(wb)
        for src, width, dst in pieces:
            wb[:, dst:dst + width] = w_ref[:, src:src + width].astype(BF16)

    m = mod_ref[0]
    h = _norm_mod(x_ref[...], g_ref[...], m[:, 0:D], m[:, D:2 * D]).astype(BF16)
    off = 0
    for o_ref, wd in zip(o_refs, widths):
        o_ref[...] = _dot(h, wb[:, off:off + wd]).astype(o_ref.dtype)
        off += wd


def _inproj(x, mod, g, layer, w, j, widths, pieces):
    t = x.shape[0]
    return pl.pallas_call(
        functools.partial(_inproj_kernel, widths=widths, pieces=pieces),
        grid=(t // TOKEN_TILE,),
        in_specs=[pl.BlockSpec((TOKEN_TILE, D), lambda i: (i, 0)),
                  _mod_spec(mod, t, TOKEN_TILE),
                  _layer_spec(g, layer),
                  pl.BlockSpec((None,) + w.shape[1:], lambda i: (j, 0, 0))],
        out_specs=[pl.BlockSpec((TOKEN_TILE, wd), lambda i: (i, 0)) for wd in widths],
        out_shape=[jax.ShapeDtypeStruct((t, wd), F32) for wd in widths],
        scratch_shapes=[pltpu.VMEM((D, sum(widths)), BF16)],
        compiler_params=_cparams(1),
        name="inproj",
    )(x, mod[0], g, w)


EVEN_WIDTHS = (SSD_INNER, SSD_CONV_DIM, MLA_Q_LORA, 2 * LANE)
_XBC_END = SSD_INNER + SSD_CONV_DIM
_QA0 = _XBC_END + SSD_HEADS
_KV0 = _QA0 + MLA_Q_LORA
EVEN_PIECES = ((0, _XBC_END, 0),
               (_QA0, MLA_Q_LORA, _XBC_END),
               (_KV0, MLA_KV_LORA, _XBC_END + MLA_Q_LORA),
               (_KV0 + MLA_KV_LORA, MLA_ROPE, _XBC_END + MLA_Q_LORA + LANE + MLA_NOPE),
               (_XBC_END, SSD_HEADS, _XBC_END + MLA_Q_LORA + LANE))
_NQ = GQA_HEADS * GQA_HEAD_DIM
_NKV = GQA_KV_HEADS * GQA_HEAD_DIM
ODD_WIDTHS = (_NQ, GQA_KV_HEADS * LANE, GQA_KV_HEADS * LANE)
ODD_PIECES = ((0, _NQ, 0),) + tuple(
    (_NQ + part * _NKV + h * GQA_HEAD_DIM, GQA_HEAD_DIM,
     _NQ + part * GQA_KV_HEADS * LANE + h * LANE + rep * GQA_HEAD_DIM)
    for part in range(2) for h in range(GQA_KV_HEADS) for rep in range(2))


def _ssd_kernel(*refs, seq, has_h0, want_state):
    it = iter(refs)
    z_ref, xbc_ref, dt_ref, cw_ref, cb_ref, hp_ref, nw_ref = [next(it) for _ in range(7)]
    h0_refs = (next(it), next(it)) if has_h0 else None
    y_ref = next(it)
    s_refs = (next(it), next(it)) if want_state else None
    xs_ref, bc_ref, yacc_ref, st_ref = [next(it) for _ in range(4)]

    q = SSD_CHUNK
    nc = seq // q
    unrolled = nc <= 2
    gn = SSD_GROUPS * SSD_STATE

    cw = cw_ref[...]
    cb = cb_ref[...]
    rid = _iota((q, SSD_CONV_DIM), 0)

    def rows(c):
        return pl.ds(c * q, q) if isinstance(c, int) else pl.ds(pl.multiple_of(c * q, q), q)

    def conv_chunk(c):
        xc = xbc_ref[rows(c), :]
        if isinstance(c, int):
            prev = xbc_ref[c * q - 1:c * q, :] if c > 0 else jnp.zeros((1, SSD_CONV_DIM), F32)
            nxt = xbc_ref[(c + 1) * q:(c + 1) * q + 1, :] if c < nc - 1 else jnp.zeros((1, SSD_CONV_DIM), F32)
        else:
            prev = xbc_ref[pl.ds(jnp.maximum(c * q - 1, 0), 1), :]
            nxt = xbc_ref[pl.ds(jnp.minimum(c * q + q, seq - 1), 1), :]
            prev = jnp.where(c > 0, prev, 0.0)
            nxt = jnp.where(c < nc - 1, nxt, 0.0)
        xp = jnp.where(rid == 0, prev, pltpu.roll(xc, 1, axis=0))
        xn = jnp.where(rid == q - 1, nxt, pltpu.roll(xc, q - 1, axis=0))
        v = _silu(xp * cw[0:1] + xc * cw[1:2] + xn * cw[2:3] + cb)
        xs_ref[rows(c), :] = v[:, :SSD_INNER]
        bc_ref[rows(c), :] = v[:, SSD_INNER:]

    hp = hp_ref[...]
    expand = (_iota((LANE, SSD_INNER), 1) // SSD_HEADDIM == _iota((LANE, SSD_INNER), 0)).astype(BF16)
    blockmask = (_iota((gn, SSD_INNER), 0) // SSD_STATE
                 == _iota((gn, SSD_INNER), 1) // (SSD_INNER // SSD_GROUPS))
    ri = _iota((q, q), 0)
    ci = _iota((q, q), 1)
    lane = _iota((q, LANE), 1)
    lo_half = lane < SSD_STATE
    d_e = _dot_x3(jnp.broadcast_to(hp[4:5], (8, LANE)), expand)[0:1]
    nw = nw_ref[...]

    def scan_chunk(c, fwd, a, bias, causal, tri):
        xs = xs_ref[rows(c), :]
        bc = bc_ref[rows(c), :]
        bm = bc[:, :gn]
        cm = bc[:, gn:]
        raw = dt_ref[rows(c), :] + bias
        dt = jnp.maximum(raw, 0.0) + jnp.log1p(jnp.exp(-jnp.abs(raw)))
        cum = _dot_3x(tri, dt * a)
        cum_t = cum.T
        dt_t = dt.T
        last = cum[q - 1:q] if fwd else cum[0:1]
        ecum = jnp.exp(cum)
        ex = _dot_x2(jnp.concatenate([ecum, dt * jnp.exp(last - cum)], axis=0), expand)
        ecum_e, w_e = ex[0:q], ex[q:2 * q]
        cdec_e = ecum_e[q - 1:q] if fwd else ecum_e[0:1]
        st = st_ref[...]
        cmb = cm.astype(BF16)
        bmb = bm.astype(BF16)
        y = _dot(cmb, st.astype(BF16)) * ecum_e
        cb0 = _dot_nt(jnp.where(lo_half, cm, 0.0).astype(BF16), bmb)
        cb1 = _dot_nt(jnp.where(lo_half, 0.0, cm).astype(BF16), bmb)
        pairs = []
        for j in range(SSD_HEADS // 2):
            cbg = cb0 if j < SSD_HEADS // 4 else cb1
            xp = xs[:, j * LANE:(j + 1) * LANE]
            scs = []
            for half in range(2):
                h = 2 * j + half
                seg = cum[:, h:h + 1] - cum_t[h:h + 1, :]
                dec = jnp.exp(jnp.where(causal, seg, -jnp.inf))
                scs.append((cbg * dec * dt_t[h:h + 1, :]).astype(BF16))
            rhs = jnp.concatenate([jnp.where(lo_half, xp, 0.0), jnp.where(lo_half, 0.0, xp)], axis=0)
            pairs.append(_dot(jnp.concatenate(scs, axis=1), rhs.astype(BF16)))
        y = y + jnp.concatenate(pairs, axis=1)
        new = _dot(bm.T.astype(BF16), (xs * w_e).astype(BF16))
        st_ref[...] = jnp.where(blockmask, st * cdec_e + new, 0.0)
        if fwd:
            yacc_ref[rows(c), :] = y
        else:
            y = y + yacc_ref[rows(c), :] + xs * d_e
            y = y * _silu(z_ref[rows(c), :])
            ms = jnp.mean(y * y, axis=-1, keepdims=True)
            y_ref[rows(c), :] = (y * lax.rsqrt(ms + EPS) * nw).astype(y_ref.dtype)

    def scan_pass(fwd):
        a = -jnp.exp(hp[0:1] if fwd else hp[1:2])
        bias = hp[2:3] if fwd else hp[3:4]
        causal = (ri >= ci) if fwd else (ri <= ci)
        tri = causal.astype(BF16)
        if has_h0:
            st_ref[...] = h0_refs[0 if fwd else 1][0]
        else:
            st_ref[...] = jnp.zeros((gn, SSD_INNER), F32)
        if unrolled:
            for k in range(nc):
                scan_chunk(k if fwd else nc - 1 - k, fwd, a, bias, causal, tri)
        else:
            def body(k, carry):
                c = k if fwd else nc - 1 - k
                scan_chunk(c, fwd, a, bias, causal, tri)
                if fwd:
                    conv_chunk(jnp.minimum(c + 1, nc - 1))
                return carry

            lax.fori_loop(0, nc, body, 0)
        if want_state:
            for j in range(SSD_INNER // LANE):
                g = (j * LANE) // (SSD_INNER // SSD_GROUPS)
                t = st_ref[:, j * LANE:(j + 1) * LANE].T
                s_refs[0 if fwd else 1][0, j * LANE:(j + 1) * LANE, :] = t[:, g * SSD_STATE:(g + 1) * SSD_STATE]

    if unrolled:
        for c in range(nc):
            conv_chunk(c)
    else:
        conv_chunk(0)
    scan_pass(True)
    scan_pass(False)


def _ssd(z, xbc, dt, cw, cb, hp, nw, h0, *, seq):
    t = z.shape[0]
    n_seq = t // seq
    has_h0 = h0 is not None
    want_state = not has_h0
    gn = SSD_GROUPS * SSD_STATE
    tok = lambda w: pl.BlockSpec((seq, w), lambda b: (b, 0))
    full = lambda a: pl.BlockSpec(a.shape, lambda b: (0,) * a.ndim)
    dt_spec = pl.BlockSpec((seq, LANE), lambda b: (b, 1))
    in_specs = [tok(SSD_INNER), tok(SSD_CONV_DIM), dt_spec, full(cw), full(cb), full(hp), full(nw)]
    args = [z, xbc, dt, cw, cb, hp, nw]
    if has_h0:
        in_specs += [pl.BlockSpec((1, gn, SSD_INNER), lambda b: (b, 0, 0))] * 2
        args += list(h0)
    out_shape = [jax.ShapeDtypeStruct((t, SSD_INNER), BF16)]
    out_specs = [tok(SSD_INNER)]
    if want_state:
        out_shape += [jax.ShapeDtypeStruct((n_seq, SSD_INNER, SSD_STATE), F32)] * 2
        out_specs += [pl.BlockSpec((1, SSD_INNER, SSD_STATE), lambda b: (b, 0, 0))] * 2

    return pl.pallas_call(
        functools.partial(_ssd_kernel, seq=seq, has_h0=has_h0, want_state=want_state),
        grid=(n_seq,),
        in_specs=in_specs,
        out_specs=out_specs,
        out_shape=out_shape,
        scratch_shapes=[pltpu.VMEM((seq, SSD_INNER), F32),
                        pltpu.VMEM((seq, 2 * gn), F32),
                        pltpu.VMEM((seq, SSD_INNER), F32),
                        pltpu.VMEM((gn, SSD_INNER), F32)],
        compiler_params=_cparams(1),
        name="ssd_lat" if has_h0 else "ssd_ctx",
    )(*args)


def _softmax_pv(ss, vals):
    m = ss[0].max(axis=-1, keepdims=True)
    for s in ss[1:]:
        m = jnp.maximum(m, s.max(axis=-1, keepdims=True))
    ps = [jnp.exp2(s - m) for s in ss]
    den = ps[0].sum(axis=-1, keepdims=True)
    for p in ps[1:]:
        den = den + p.sum(axis=-1, keepdims=True)
    o = _dot(ps[0].astype(BF16), vals[0]())
    for p, v in zip(ps[1:], vals[1:]):
        o = o + _dot(p.astype(BF16), v())
    return o / den


def _attend_blocks(seq, units, emit, s_scr, qrows):
    nblk = seq // qrows

    def scores(u, blk, static):
        q_ref, rows, keys, _ = units[u]
        r = blk * rows if static else pl.multiple_of(blk * rows, rows)
        qb = q_ref[pl.ds(r, rows), :]
        return [_dot_nt(qb, k()) for k in keys]

    if s_scr is None:
        for i in range(nblk):
            emit(i * qrows, [_softmax_pv(scores(u, i, True), units[u][3]) for u in range(len(units))])
        return

    def put(slot, blk):
        for u in range(len(units)):
            off = 0
            for s in scores(u, blk, isinstance(blk, int)):
                s_scr[slot][u][:, off:off + s.shape[1]] = s
                off += s.shape[1]

    def take(slot, blk):
        outs = []
        for u, (_, _, keys, vals) in enumerate(units):
            off = 0
            ss = []
            for k in keys:
                n = k().shape[0]
                ss.append(s_scr[slot][u][:, off:off + n])
                off += n
            outs.append(_softmax_pv(ss, vals))
        emit(blk * qrows if isinstance(blk, int) else pl.multiple_of(blk * qrows, qrows), outs)

    put(0, 0)

    def body(ii, carry):
        b0 = 2 * ii
        put(1, b0 + 1)
        take(0, b0)

        def more():
            put(0, b0 + 2)
            take(1, b0 + 1)

        def last():
            take(1, b0 + 1)

        lax.cond(b0 + 2 < nblk, more, last)
        return carry

    lax.fori_loop(0, nblk // 2, body, 0)


def _mla_kernel(*refs, seq, lat):
    it = iter(refs)
    qa_ref, kv_ref, qan_ref, kvn_ref, wq_ref, wk_ref, wv_ref, qn_ref, kn_ref = [next(it) for _ in range(9)]
    if lat:
        cckv_ref, ckpe_ref, cos_ref, sin_ref = [next(it) for _ in range(4)]
    o_ref = next(it)
    if not lat:
        ckv_out, kpe_out = next(it), next(it)
    q_s, k_s, v_s, qh0_s, qh1_s, kh0_s, kh1_s = [next(it) for _ in range(7)]
    s_scr = None
    if lat:
        kc_s, vc_s, kch0_s, kch1_s = [next(it) for _ in range(4)]
        s_scr = [[next(it), next(it)], [next(it), next(it)]]

    kv = kv_ref[...]
    kpe = jnp.where(_iota((seq, LANE), 1) >= MLA_NOPE, kv[:, LANE:], 0.0)
    ckv_raw = kv[:, :LANE]
    ckv = ckv_raw * lax.rsqrt(jnp.mean(ckv_raw * ckv_raw, axis=-1, keepdims=True) + EPS) * kvn_ref[...]
    qa = qa_ref[...]
    qa = qa * lax.rsqrt(jnp.mean(qa * qa, axis=-1, keepdims=True) + EPS) * qan_ref[...]
    if not lat:
        ckv_out[...] = ckv
        kpe_out[...] = kpe
    ckvb = ckv.astype(BF16)
    q_s[...] = _dot(qa.astype(BF16), wq_ref[...])
    k_s[...] = _dot(ckvb, wk_ref[...])
    v_s[...] = _dot(ckvb, wv_ref[...]).astype(BF16)
    if lat:
        ccb = cckv_ref[0].astype(BF16)
        kc_s[...] = _dot(ccb, wk_ref[...])
        vc_s[...] = _dot(ccb, wv_ref[...]).astype(BF16)
        ckpe = ckpe_ref[0]
        cos = cos_ref[...]
        sin = sin_ref[...]
        dist = MLA_ROPE // 4
        li = _iota((2 * LANE, 2 * LANE), 0)
        lj = _iota((2 * LANE, 2 * LANE), 1)
        swap = (li == jnp.where((lj % (2 * dist)) < dist, lj + dist, lj - dist)).astype(BF16)
    qn = qn_ref[...]
    kn = kn_ref[...]
    qrows = LAT_Q_ROWS if lat else Q_BLOCK
    lo_lanes = _iota((qrows, LANE), 1) < MLA_V

    def head_norm(x, g):
        return x * lax.rsqrt(jnp.sum(x * x, axis=-1, keepdims=True) * (1.0 / MLA_QK) + EPS) * g

    def rotate(x2):
        return x2 * cos + _dot_x2(x2, swap) * sin

    for j in range(MLA_HEADS // 2):
        pair = slice(j * LANE, (j + 1) * LANE)
        sls = [slice((2 * j + half) * LANE, (2 * j + half + 1) * LANE) for half in range(2)]
        q2 = jnp.concatenate([head_norm(q_s[:, sl], qn) for sl in sls], axis=1)
        k2 = jnp.concatenate([head_norm(k_s[:, sl] + kpe, kn) for sl in sls], axis=1)
        if lat:
            q2 = rotate(q2)
            k2 = rotate(k2)
            for half, kch_s in enumerate((kch0_s, kch1_s)):
                kch_s[...] = head_norm(kc_s[:, sls[half]] + ckpe, kn).astype(BF16)
        for half, (qh_s, kh_s) in enumerate(((qh0_s, kh0_s), (qh1_s, kh1_s))):
            qh_s[...] = q2[:, half * LANE:(half + 1) * LANE].astype(BF16)
            kh_s[...] = k2[:, half * LANE:(half + 1) * LANE].astype(BF16)

        def emit(r0, outs, pair=pair):
            o_ref[pl.ds(r0, qrows), pair] = jnp.where(lo_lanes, outs[0], outs[1]).astype(o_ref.dtype)

        units = []
        for half, (qh_s, kh_s) in enumerate(((qh0_s, kh0_s), (qh1_s, kh1_s))):
            keys = [lambda r=kh_s: r[...]]
            vals = [lambda pair=pair: v_s[:, pair]]
            if lat:
                keys = [lambda r=(kch0_s, kch1_s)[half]: r[...]] + keys
                vals = [lambda pair=pair: vc_s[:, pair]] + vals
            units.append((qh_s, qrows, keys, vals))
        _attend_blocks(seq, units, emit, s_scr, qrows)


def _mla(qa, kv, qan, kvn, wq, wk, wv, qn, kn, cache, tables, *, seq):
    t = qa.shape[0]
    n_seq = t // seq
    lat = cache is not None
    hv = MLA_HEADS * MLA_V
    tok = lambda w: pl.BlockSpec((seq, w), lambda b: (b, 0))
    full = lambda a: pl.BlockSpec(a.shape, lambda b: (0,) * a.ndim)
    args = [qa, kv, qan, kvn, wq, wk, wv, qn, kn]
    in_specs = [tok(MLA_Q_LORA), tok(2 * LANE)] + [full(a) for a in args[2:]]
    past = 0
    if lat:
        past = cache[0].shape[1]
        in_specs += [pl.BlockSpec((1, past, LANE), lambda b: (b, 0, 0))] * 2 + [full(tables[0]), full(tables[1])]
        args += [cache[0], cache[1], tables[0], tables[1]]
    out_shape = [jax.ShapeDtypeStruct((t, hv), BF16)]
    out_specs = [tok(hv)]
    if not lat:
        out_shape += [jax.ShapeDtypeStruct((t, LANE), F32)] * 2
        out_specs += [pl.BlockSpec((seq, LANE), lambda b: (b, 0))] * 2
    scratch = [pltpu.VMEM((seq, MLA_HEADS * LANE), F32),
               pltpu.VMEM((seq, MLA_HEADS * LANE), F32),
               pltpu.VMEM((seq, hv), BF16)] + [pltpu.VMEM((seq, LANE), BF16)] * 4
    if lat:
        scratch += [pltpu.VMEM((past, MLA_HEADS * LANE), F32),
                    pltpu.VMEM((past, hv), BF16),
                    pltpu.VMEM((past, LANE), BF16),
                    pltpu.VMEM((past, LANE), BF16)]
        scratch += [pltpu.VMEM((LAT_Q_ROWS, past + seq), F32)] * 4

    return pl.pallas_call(
        functools.partial(_mla_kernel, seq=seq, lat=lat),
        grid=(n_seq,),
        in_specs=in_specs,
        out_specs=out_specs,
        out_shape=out_shape,
        scratch_shapes=scratch,
        compiler_params=_cparams(1),
        name="mla_lat" if lat else "mla_ctx",
    )(*args)


def _gqa_kernel(*refs, seq, lat):
    it = iter(refs)
    q_ref, k_ref, v_ref, qn_ref, kn_ref = [next(it) for _ in range(5)]
    if lat:
        ck_ref, cv_ref, cos_ref, sin_ref = [next(it) for _ in range(4)]
    o_ref = next(it)
    if not lat:
        kn_out = next(it)
    kn_s, vb_s, q2_s = [next(it) for _ in range(3)]
    s_scr = None
    if lat:
        ckb_s, cvb_s = next(it), next(it)
        s_scr = [[next(it)], [next(it)]]

    hd = GQA_HEAD_DIM
    nq = GQA_HEADS * hd
    nk = GQA_KV_HEADS * LANE
    qrows = LAT_Q_ROWS if lat else Q_BLOCK
    nblk = seq // qrows
    wide = 2 * LANE
    li = _iota((wide, wide), 0)
    lj = _iota((wide, wide), 1)
    same_head = (li // hd == lj // hd).astype(BF16)
    if lat:
        dist = hd // 4
        swap = (li == jnp.where((lj % (2 * dist)) < dist, lj + dist, lj - dist)).astype(BF16)
        cos = cos_ref[...]
        sin = sin_ref[...]
    lo_q = _iota((qrows, LANE), 1) < hd

    def prepared(x_ref, g_ref, b):
        sl = slice(b * wide, (b + 1) * wide)
        x = x_ref[:, sl]
        ss = _dot_x2(x * x, same_head)
        xn = x * lax.rsqrt(ss * (1.0 / hd) + EPS) * g_ref[:, sl]
        if not lat:
            return xn, xn
        return xn, xn * cos + _dot_x2(xn, swap) * sin

    for b in range(nk // wide):
        sl = slice(b * wide, (b + 1) * wide)
        kn, kr = prepared(k_ref, kn_ref, b)
        if not lat:
            kn_out[:, sl] = kn
        kn_s[:, sl] = kr.astype(BF16)
    if lat:
        for src_ref, dst in ((ck_ref, ckb_s), (cv_ref, cvb_s)):
            for h in range(GQA_KV_HEADS):
                head = src_ref[0, :, h * hd:(h + 1) * hd].astype(BF16)
                for rep in range(LANE // hd):
                    dst[:, h * LANE + rep * hd:h * LANE + (rep + 1) * hd] = head
    vb_s[...] = v_ref[...].astype(BF16)
    for b in range(nq // wide):
        _, qr = prepared(q_ref, qn_ref, b)
        for half in range(wide // LANE):
            p = b * (wide // LANE) + half
            for i in range(nblk):
                blk = qr[i * qrows:(i + 1) * qrows, half * LANE:(half + 1) * LANE]
                q2_s[p, 2 * i * qrows:(2 * i + 1) * qrows, :] = jnp.where(lo_q, blk, 0.0).astype(BF16)
                q2_s[p, (2 * i + 1) * qrows:(2 * i + 2) * qrows, :] = jnp.where(lo_q, 0.0, blk).astype(BF16)

    for j in range(GQA_HEADS // 2):
        pair = slice(j * LANE, (j + 1) * LANE)
        g = (2 * j) // (GQA_HEADS // GQA_KV_HEADS)
        gs = slice(g * LANE, (g + 1) * LANE)

        def emit(r0, outs, pair=pair):
            o = outs[0]
            o_ref[pl.ds(r0, qrows), pair] = jnp.where(lo_q, o[:qrows], o[qrows:]).astype(o_ref.dtype)

        keys = [lambda gs=gs: kn_s[:, gs]]
        vals = [lambda gs=gs: vb_s[:, gs]]
        if lat:
            keys = [lambda gs=gs: ckb_s[:, gs]] + keys
            vals = [lambda gs=gs: cvb_s[:, gs]] + vals
        _attend_blocks(seq, [(q2_s.at[j], 2 * qrows, keys, vals)], emit, s_scr, qrows)


def _gqa(q, k, v, qn, kn, cache, tables, *, seq):
    t = q.shape[0]
    n_seq = t // seq
    lat = cache is not None
    nq = GQA_HEADS * GQA_HEAD_DIM
    nk = GQA_KV_HEADS * LANE
    tok = lambda w: pl.BlockSpec((seq, w), lambda b: (b, 0))
    full = lambda a: pl.BlockSpec(a.shape, lambda b: (0,) * a.ndim)
    args = [q, k, v, qn, kn]
    in_specs = [tok(nq), tok(nk), tok(nk), full(qn), full(kn)]
    past = 0
    if lat:
        past = cache[0].shape[1]
        nkv = cache[0].shape[2]
        in_specs += [pl.BlockSpec((1, past, nkv), lambda b: (b, 0, 0))] * 2 + [full(tables[0]), full(tables[1])]
        args += [cache[0], cache[1], tables[0], tables[1]]
    out_shape = [jax.ShapeDtypeStruct((t, nq), BF16)]
    out_specs = [tok(nq)]
    if not lat:
        out_shape += [jax.ShapeDtypeStruct((t, nk), F32)]
        out_specs += [pl.BlockSpec((seq, nk), lambda b: (b, 0))]
    scratch = [pltpu.VMEM((seq, nk), BF16),
               pltpu.VMEM((seq, nk), BF16),
               pltpu.VMEM((GQA_HEADS // 2, 2 * seq, LANE), BF16)]
    if lat:
        scratch += [pltpu.VMEM((past, nk), BF16), pltpu.VMEM((past, nk), BF16)]
        scratch += [pltpu.VMEM((2 * LAT_Q_ROWS, past + seq), F32)] * 2

    return pl.pallas_call(
        functools.partial(_gqa_kernel, seq=seq, lat=lat),
        grid=(n_seq,),
        in_specs=in_specs,
        out_specs=out_specs,
        out_shape=out_shape,
        scratch_shapes=scratch,
        compiler_params=_cparams(1),
        name="gqa_lat" if lat else "gqa_ctx",
    )(*args)


def _route(sel, sc):
    gs = []
    for g in range(N_GROUPS):
        r = sel[g * GROUP_SIZE:(g + 1) * GROUP_SIZE]
        best = None
        for i in range(GROUP_SIZE):
            for j in range(i + 1, GROUP_SIZE):
                s = r[i] + r[j]
                best = s if best is None else jnp.maximum(best, s)
        gs.append(best)
    cur = gs[0]
    grp = jnp.zeros_like(cur, dtype=jnp.int32)
    for g in range(1, N_GROUPS):
        upd = gs[g] > cur
        grp = jnp.where(upd, g, grp)
        cur = jnp.where(upd, gs[g], cur)

    def pick(rows, i):
        v = rows[i]
        for g in range(1, N_GROUPS):
            v = jnp.where(grp == g, rows[g * GROUP_SIZE + i], v)
        return v

    cand = [pick(sel, i) for i in range(GROUP_SIZE)]
    csc = [pick(sc, i) for i in range(GROUP_SIZE)]
    v1, i1, w1 = cand[0], jnp.zeros_like(grp), csc[0]
    for i in range(1, GROUP_SIZE):
        upd = cand[i] > v1
        v1 = jnp.where(upd, cand[i], v1)
        i1 = jnp.where(upd, i, i1)
        w1 = jnp.where(upd, csc[i], w1)
    v2 = i2 = w2 = None
    for i in range(GROUP_SIZE):
        ok = i1 != i
        if v2 is None:
            v2 = jnp.where(ok, cand[i], -jnp.inf)
            i2 = jnp.zeros_like(grp)
            w2 = csc[i]
        else:
            upd = ok & (cand[i] > v2)
            v2 = jnp.where(upd, cand[i], v2)
            i2 = jnp.where(upd, i, i2)
            w2 = jnp.where(upd, csc[i], w2)
    tot = w1 + w2
    local = [jnp.where(i1 == i, w1 / tot, 0.0) + jnp.where(i2 == i, w2 / tot, 0.0)
             for i in range(GROUP_SIZE)]
    onehot = [jnp.where(grp == g, 1.0, 0.0) for g in range(N_GROUPS)]
    return local, onehot


def _outproj_kernel(*refs, n_y):
    x_ref, mod_ref = refs[0], refs[1]
    y_refs = refs[2:2 + n_y]
    w_ref, g_ref, wr_ref, rb_ref, x1_ref, h2_ref, gate_ref, cnt_ref, w_b, wr_b = refs[2 + n_y:]

    @pl.when(pl.program_id(0) == 0)
    def _():
        w_b[...] = w_ref[...].astype(BF16)
        wr_b[...] = jnp.zeros_like(wr_b)
        for j, term in enumerate(_split3(wr_ref[...])):
            wr_b[:, j * N_EXPERTS:(j + 1) * N_EXPERTS] = term

    m = mod_ref[0]
    off = 0
    out = None
    for y_ref in y_refs:
        k = y_ref.shape[1]
        part = _dot(y_ref[...], w_b[off:off + k, :])
        out = part if out is None else out + part
        off += k
    x1 = x_ref[...] + m[:, 2 * D:3 * D] * out
    x1_ref[...] = x1
    h2 = _norm_mod(x1, g_ref[...], m[:, 3 * D:4 * D], m[:, 4 * D:5 * D])
    h2_hi = h2.astype(BF16)
    h2_ref[...] = h2_hi
    h2_mid = (h2 - h2_hi.astype(F32)).astype(BF16)
    half = h2.shape[0] // 2
    parts = jnp.concatenate([_dot(h2_hi[r], wr_b[...]) + _dot(h2_mid[r], wr_b[...])
                             for r in (slice(0, half), slice(half, None))], axis=0)
    logits = (parts + pltpu.roll(parts, LANE - N_EXPERTS, axis=1)
              + pltpu.roll(parts, LANE - 2 * N_EXPERTS, axis=1))
    sc_t = _sigmoid(logits.T[0:N_EXPERTS, :])
    sel_t = sc_t + rb_ref[...]
    local, onehot = _route([sel_t[e:e + 1] for e in range(N_EXPERTS)],
                           [sc_t[e:e + 1] for e in range(N_EXPERTS)])
    t = x1.shape[0]
    rows_t = (local + [jnp.zeros((GROUP_LANE - GROUP_SIZE, t), F32)] + onehot
              + [jnp.zeros((LANE - GROUP_LANE - N_GROUPS, t), F32)])
    routed = jnp.concatenate(rows_t, axis=0).T
    gate_ref[...] = routed
    cnt = jnp.sum(routed, axis=0, keepdims=True).astype(jnp.int32)
    cnt_ref[0] = jnp.broadcast_to(cnt, (8, LANE))


def _outproj(x, mod, ys, w, j, g, layer, wr, rb):
    t = x.shape[0]
    full = lambda a: pl.BlockSpec(a.shape, lambda i: (0,) * a.ndim)
    return pl.pallas_call(
        functools.partial(_outproj_kernel, n_y=len(ys)),
        grid=(t // TOKEN_TILE,),
        in_specs=[pl.BlockSpec((TOKEN_TILE, D), lambda i: (i, 0)),
                  _mod_spec(mod, t, TOKEN_TILE)]
                 + [pl.BlockSpec((TOKEN_TILE, y.shape[1]), lambda i: (i, 0)) for y in ys]
                 + [pl.BlockSpec((None,) + w.shape[1:], lambda i: (j, 0, 0)), _layer_spec(g, layer),
                    full(wr), full(rb)],
        out_specs=[pl.BlockSpec((TOKEN_TILE, D), lambda i: (i, 0)),
                   pl.BlockSpec((TOKEN_TILE, D), lambda i: (i, 0)),
                   pl.BlockSpec((TOKEN_TILE, LANE), lambda i: (i, 0)),
                   pl.BlockSpec((1, 8, LANE), lambda i: (i, 0, 0))],
        out_shape=[jax.ShapeDtypeStruct((t, D), F32),
                   jax.ShapeDtypeStruct((t, D), BF16),
                   jax.ShapeDtypeStruct((t, LANE), F32),
                   jax.ShapeDtypeStruct((t // TOKEN_TILE, 8, LANE), jnp.int32)],
        scratch_shapes=[pltpu.VMEM(w.shape[1:], BF16), pltpu.VMEM((D, LANE), BF16)],
        compiler_params=_cparams(1),
        name="outproj_router",
    )(x, mod[0], *ys, w, g, wr, rb)


def _pad_to(n, shift):
    return ((n + (1 << shift) - 1) >> shift) << shift


SEG_SHIFT = SEG_ALIGN.bit_length() - 1
TILE_SHIFT = EXPERT_TILE.bit_length() - 1


def _start_segments(plan_ref, i, nt, make):
    for g in range(N_GROUPS):
        n = plan_ref[i * N_GROUPS + g]
        lo = plan_ref[(nt + i) * N_GROUPS + g]
        base = plan_ref[(2 * nt + i) * N_GROUPS + g]

        def body(c, carry, lo=lo, base=base):
            for cp in make(pl.multiple_of(lo + c * SEG_ALIGN, SEG_ALIGN),
                           pl.multiple_of(base + c * SEG_ALIGN, SEG_ALIGN)):
                cp.start()
            return carry

        lax.fori_loop(0, (n + SEG_ALIGN - 1) >> SEG_SHIFT, body, 0)


def _wait_segments(plan_ref, i, make):
    total = jnp.int32(0)
    for g in range(N_GROUPS):
        total = total + ((plan_ref[i * N_GROUPS + g] + SEG_ALIGN - 1) >> SEG_SHIFT)

    def body(c, carry):
        for cp in make(0, 0):
            cp.wait()
        return carry

    lax.fori_loop(0, total, body, 0)


def _dispatch_kernel(cnt_ref, h_ref, r_ref, xs_hbm, dest_ref, plan_ref,
                     xs_l, zx, tril_s, fill_s, sem, *, nt, n_rows):
    i = pl.program_id(0)
    t = TOKEN_TILE
    k_tiles = n_rows // EXPERT_TILE

    @pl.when(i == 0)
    def _():
        zx[...] = jnp.zeros_like(zx)
        tril_s[...] = (_iota((t, t), 0) >= _iota((t, t), 1)).astype(BF16)
        tots = []
        for g in range(N_GROUPS):
            tots.append(lax.fori_loop(
                0, nt, lambda tt, acc, g=g: acc + _pad_to(cnt_ref[tt * N_GROUPS + g], SEG_SHIFT), jnp.int32(0)))
        starts = []
        cur = jnp.int32(0)
        for g in range(N_GROUPS):
            starts.append(cur)
            fill_s[2 * g] = cur + tots[g]
            cur = cur + _pad_to(tots[g], TILE_SHIFT)
            fill_s[2 * g + 1] = cur
        fill_s[2 * N_GROUPS] = cur
        fill_s[2 * N_GROUPS + 1] = n_rows

        def per_tile(tt, run):
            lo = jnp.int32(0)
            new = []
            for g in range(N_GROUPS):
                n = cnt_ref[tt * N_GROUPS + g]
                plan_ref[tt * N_GROUPS + g] = n
                plan_ref[(nt + tt) * N_GROUPS + g] = lo
                plan_ref[(2 * nt + tt) * N_GROUPS + g] = run[g]
                seg = _pad_to(n, SEG_SHIFT)
                lo = lo + seg
                new.append(run[g] + seg)
            return tuple(new)

        lax.fori_loop(0, nt, per_tile, tuple(starts))
        for k in range(k_tiles):
            row = k * EXPERT_TILE
            grp = 0
            for g in range(1, N_GROUPS):
                grp = grp + (row >= starts[g]).astype(jnp.int32)
            plan_ref[3 * nt * N_GROUPS + k] = grp
        plan_ref[3 * nt * N_GROUPS + k_tiles] = cur >> TILE_SHIFT

    routed = r_ref[...]
    lane = _iota((t, LANE), 1)
    is_grp = (lane >= GROUP_LANE) & (lane < GROUP_LANE + N_GROUPS)
    onehot = jnp.where(is_grp, routed, 0.0)
    rank = _dot(tril_s[...], onehot.astype(BF16))
    seg_lo = jnp.zeros((1, LANE), F32)
    lane1 = _iota((1, LANE), 1)
    for g in range(N_GROUPS):
        lo = plan_ref[(nt + i) * N_GROUPS + g]
        seg_lo = jnp.where(lane1 == GROUP_LANE + g, (lo - 1).astype(F32), seg_lo)
    val = jnp.where(onehot > 0.0, rank + seg_lo, 0.0)
    dest = _dot_x2(val, jnp.ones((LANE, LANE), BF16))
    dest_ref[...] = dest
    dest_row = dest.T[0:1, :].astype(jnp.int32)
    perm = jnp.where(_iota((DISPATCH_ROWS, t), 0) == dest_row, 1.0, 0.0).astype(BF16)
    r_hi, r_mid, r_lo = _split3(routed)
    packed = (r_hi.astype(F32) + pltpu.roll(r_mid.astype(F32), GROUP_SIZE, axis=1)
              + pltpu.roll(r_lo.astype(F32), 2 * GROUP_SIZE, axis=1)).astype(BF16)
    slot = i % 2
    xs_l[slot] = _dot(perm, jnp.concatenate([h_ref[...], packed], axis=1)).astype(BF16)

    def to_buffer(slot):
        def make(lo, base):
            return (pltpu.make_async_copy(xs_l.at[slot, pl.ds(lo, SEG_ALIGN)],
                                          xs_hbm.at[pl.ds(base, SEG_ALIGN)], sem.at[slot]),)
        return make

    _start_segments(plan_ref, i, nt, to_buffer(slot))

    @pl.when(i > 0)
    def _():
        _wait_segments(plan_ref, i - 1, to_buffer(1 - slot))

    @pl.when(i == nt - 1)
    def _():
        _wait_segments(plan_ref, i, to_buffer(slot))
        def zero(base):
            return (pltpu.make_async_copy(zx, xs_hbm.at[pl.ds(base, SEG_ALIGN)], sem.at[0]),)

        n_fill = jnp.int32(0)
        for r in range(N_GROUPS + 1):
            lo = fill_s[2 * r]
            chunks = (fill_s[2 * r + 1] - lo) >> SEG_SHIFT

            def body(c, carry, lo=lo):
                for cp in zero(pl.multiple_of(lo + c * SEG_ALIGN, SEG_ALIGN)):
                    cp.start()
                return carry

            lax.fori_loop(0, chunks, body, 0)
            n_fill = n_fill + chunks

        def wait_fill(c, carry):
            for cp in zero(0):
                cp.wait()
            return carry

        lax.fori_loop(0, n_fill, wait_fill, 0)


def _moe_rows(n_tokens):
    nt = n_tokens // TOKEN_TILE
    worst = n_tokens + nt * N_GROUPS * (SEG_ALIGN - 1) + N_GROUPS * (EXPERT_TILE - SEG_ALIGN)
    return _pad_to(worst, TILE_SHIFT)


def _dispatch(counts, h2, routed):
    t = h2.shape[0]
    nt = t // TOKEN_TILE
    n_rows = _moe_rows(t)
    plan_len = 3 * nt * N_GROUPS + n_rows // EXPERT_TILE + 1
    grid_spec = pltpu.PrefetchScalarGridSpec(
        num_scalar_prefetch=1,
        grid=(nt,),
        in_specs=[pl.BlockSpec((TOKEN_TILE, D), lambda i, c: (i, 0)),
                  pl.BlockSpec((TOKEN_TILE, LANE), lambda i, c: (i, 0))],
        out_specs=[pl.BlockSpec(memory_space=pl.ANY),
                   pl.BlockSpec((TOKEN_TILE, LANE), lambda i, c: (i, 0)),
                   pl.BlockSpec(memory_space=pltpu.SMEM)],
        scratch_shapes=[pltpu.VMEM((2, DISPATCH_ROWS, SORTED_WIDTH), BF16),
                        pltpu.VMEM((SEG_ALIGN, SORTED_WIDTH), BF16),
                        pltpu.VMEM((TOKEN_TILE, TOKEN_TILE), BF16),
                        pltpu.SMEM((2 * N_GROUPS + 2,), jnp.int32),
                        pltpu.SemaphoreType.DMA((2,))])
    return pl.pallas_call(
        functools.partial(_dispatch_kernel, nt=nt, n_rows=n_rows),
        grid_spec=grid_spec,
        out_shape=[jax.ShapeDtypeStruct((n_rows, SORTED_WIDTH), BF16),
                   jax.ShapeDtypeStruct((t, LANE), F32),
                   jax.ShapeDtypeStruct((plan_len,), jnp.int32)],
        compiler_params=_cparams(1),
        name="moe_dispatch",
    )(counts, h2, routed)


def _experts_kernel(plan_ref, xs_ref, wg_ref, wu_ref, wd_ref, y_ref, wg_b, wu_b, wd_b,
                    *, group_at, used_at):
    k = pl.program_id(0)
    used = plan_ref[used_at]
    group = plan_ref[group_at + k]
    prev = plan_ref[group_at + jnp.maximum(k - 1, 0)]

    @pl.when((k == 0) | (group != prev))
    def _():
        wg_b[...] = wg_ref[...].astype(BF16)
        wu_b[...] = wu_ref[...].astype(BF16)
        wd_b[...] = wd_ref[...].astype(BF16)

    @pl.when(k < used)
    def _():
        rows = xs_ref[:, :D]
        terms = xs_ref[:, D:].astype(F32)
        gates = (terms + pltpu.roll(terms, LANE - GROUP_SIZE, axis=1)
                 + pltpu.roll(terms, LANE - 2 * GROUP_SIZE, axis=1))
        acc = None
        for e in range(GROUP_SIZE):
            a = _dot(rows, wg_b[e])
            u = _dot(rows, wu_b[e])
            act = (_silu(a) * u * gates[:, e:e + 1]).astype(BF16)
            part = _dot(act, wd_b[e])
            acc = part if acc is None else acc + part
        y_ref[...] = acc.astype(y_ref.dtype)

    @pl.when(k >= used)
    def _():
        y_ref[...] = jnp.zeros_like(y_ref)


def _experts(plan, xs, wg, wu, wd, layer, nt):
    n_rows = xs.shape[0]
    k_tiles = n_rows // EXPERT_TILE
    at = 3 * nt * N_GROUPS
    wspec = lambda a, b: pl.BlockSpec((None, GROUP_SIZE, a, b), lambda k, p: (layer, p[at + k], 0, 0))
    grid_spec = pltpu.PrefetchScalarGridSpec(
        num_scalar_prefetch=1,
        grid=(k_tiles,),
        in_specs=[pl.BlockSpec((EXPERT_TILE, SORTED_WIDTH), lambda k, p: (k, 0)),
                  wspec(D, D_EXPERT), wspec(D, D_EXPERT), wspec(D_EXPERT, D)],
        out_specs=pl.BlockSpec((EXPERT_TILE, D), lambda k, p: (k, 0)),
        scratch_shapes=[pltpu.VMEM((GROUP_SIZE, D, D_EXPERT), BF16),
                        pltpu.VMEM((GROUP_SIZE, D, D_EXPERT), BF16),
                        pltpu.VMEM((GROUP_SIZE, D_EXPERT, D), BF16)])
    return pl.pallas_call(
        functools.partial(_experts_kernel, group_at=at, used_at=at + k_tiles),
        grid_spec=grid_spec,
        out_shape=jax.ShapeDtypeStruct((n_rows, D), BF16),
        compiler_params=_cparams(1),
        name="moe_experts",
    )(plan, xs, wg, wu, wd)


def _combine_kernel(plan_ref, x_ref, mod_ref, dest_ref, ys_hbm, o_ref, y_l, sem, *, nt):
    i = pl.program_id(0)

    slot = i % 2

    def from_buffer(slot):
        def make(lo, base):
            return (pltpu.make_async_copy(ys_hbm.at[pl.ds(base, SEG_ALIGN)],
                                          y_l.at[slot, pl.ds(lo, SEG_ALIGN)], sem.at[slot]),)
        return make

    @pl.when(i == 0)
    def _():
        y_l[...] = jnp.zeros_like(y_l)
        _start_segments(plan_ref, i, nt, from_buffer(slot))

    @pl.when(i + 1 < nt)
    def _():
        _start_segments(plan_ref, i + 1, nt, from_buffer(1 - slot))

    _wait_segments(plan_ref, i, from_buffer(slot))
    dest = dest_ref[...].astype(jnp.int32)
    lane = _iota(dest.shape, 1)
    back = jnp.concatenate(
        [jnp.where(dest == lane + j * LANE, 1.0, 0.0).astype(BF16) for j in range(DISPATCH_ROWS // LANE)], axis=1)
    y = _dot(back, y_l[slot])
    o_ref[...] = x_ref[...] + mod_ref[0][:, 5 * D:6 * D] * y


def _combine(plan, x1, mod, dest, ys):
    t = x1.shape[0]
    nt = t // TOKEN_TILE
    grid_spec = pltpu.PrefetchScalarGridSpec(
        num_scalar_prefetch=1,
        grid=(nt,),
        in_specs=[pl.BlockSpec((TOKEN_TILE, D), lambda i, p: (i, 0)),
                  _mod_spec(mod, t, TOKEN_TILE),
                  pl.BlockSpec((TOKEN_TILE, LANE), lambda i, p: (i, 0)),
                  pl.BlockSpec(memory_space=pl.ANY)],
        out_specs=pl.BlockSpec((TOKEN_TILE, D), lambda i, p: (i, 0)),
        scratch_shapes=[pltpu.VMEM((2, DISPATCH_ROWS, D), BF16),
                        pltpu.SemaphoreType.DMA((2,))])
    return pl.pallas_call(
        functools.partial(_combine_kernel, nt=nt),
        grid_spec=grid_spec,
        out_shape=jax.ShapeDtypeStruct((t, D), F32),
        compiler_params=_cparams(1),
        name="moe_combine",
    )(plan, x1, mod[0], dest, ys)


def _moe(x1, mod, h2, routed, cnt, wg, wu, wd, layer):
    nt = x1.shape[0] // TOKEN_TILE
    counts = cnt[:, 0, GROUP_LANE:GROUP_LANE + N_GROUPS].reshape(nt * N_GROUPS)
    xs, dest, plan = _dispatch(counts, h2, routed)
    ys = _experts(plan, xs, wg, wu, wd, layer, nt)
    return _combine(plan, x1, mod, dest, ys)


def _rope_angles(pos, half):
    inv = np.power(np.float32(ROPE_THETA), -np.arange(half, dtype=np.float32) / np.float32(half))
    ang = pos.astype(np.float32)[:, None] * inv[None, :]
    return np.cos(ang), np.sin(ang)


def _rope_tables(seq, rot, lead, lane_tiles):
    rows = seq // GRID_W
    row = np.repeat(np.arange(rows), GRID_W)
    col = np.tile(np.arange(GRID_W), rows)
    cr, sr = _rope_angles(row, rot // 4)
    cc, sc = _rope_angles(col, rot // 4)
    cos = np.concatenate([cr, cr, cc, cc], axis=1)
    sin = np.concatenate([-sr, sr, -sc, sc], axis=1)
    width = LANE // lane_tiles
    pad = width - lead - rot
    cos = np.concatenate([np.ones((seq, lead), np.float32), cos, np.ones((seq, pad), np.float32)], axis=1)
    sin = np.concatenate([np.zeros((seq, lead), np.float32), sin, np.zeros((seq, pad), np.float32)], axis=1)
    return np.tile(cos, (1, lane_tiles)), np.tile(sin, (1, lane_tiles))


def _pad_lanes(a, width):
    return jnp.pad(a, [(0, 0)] * (a.ndim - 1) + [(0, width - a.shape[-1])])


def _mla_weights(w_q_b, w_kv_b):
    wq = _pad_lanes(w_q_b.reshape(MLA_Q_LORA, MLA_HEADS, MLA_QK), LANE).reshape(MLA_Q_LORA, MLA_HEADS * LANE)
    kvb = w_kv_b.reshape(MLA_KV_LORA, MLA_HEADS, MLA_NOPE + MLA_V)
    wk = _pad_lanes(kvb[:, :, :MLA_NOPE], LANE).reshape(MLA_KV_LORA, MLA_HEADS * LANE)
    wv = kvb[:, :, MLA_NOPE:].reshape(MLA_KV_LORA, MLA_HEADS * MLA_V)
    return wq.astype(BF16), wk.astype(BF16), wv.astype(BF16)


def _undup_heads(a):
    s = a.shape[:-1]
    return a.reshape(s + (GQA_KV_HEADS, 2, GQA_HEAD_DIM))[..., 0, :]


def _state_to_kernel(h0):
    b = h0.shape[0]
    t = jnp.transpose(h0, (0, 3, 1, 2)).reshape(b, SSD_STATE, SSD_INNER)
    half = SSD_INNER // SSD_GROUPS
    col = jnp.arange(SSD_INNER) // half
    parts = [jnp.where(col == g, t, 0.0) for g in range(SSD_GROUPS)]
    return jnp.concatenate(parts, axis=1)


def _state_from_kernel(st):
    return st.reshape(st.shape[0], SSD_HEADS, SSD_HEADDIM, SSD_STATE)


def kernel(x_prompt, x_sample, c, state_ssd_fwd, state_ssd_bwd, cache_mla_ckv, cache_mla_kpe, cache_gqa_k, cache_gqa_v, c_ctx, g_mix, g_ffn, w_ada, b_ada, w_router, router_bias, w_exp_gate, w_exp_up, w_exp_down, w_in_even, ssd_conv_w, ssd_conv_b, ssd_a_log_fwd, ssd_a_log_bwd, ssd_dt_bias_fwd, ssd_dt_bias_bwd, ssd_d, ssd_norm, mla_q_a_norm, mla_w_q_b, mla_kv_a_norm, mla_w_kv_b, mla_q_norm, mla_k_norm, w_out_even, w_in_odd, gqa_q_norm, gqa_k_norm, w_out_odd):
    nb, ls, _ = x_prompt.shape
    db, dl, _ = x_sample.shape
    depth = w_ada.shape[0]
    n_ctx = nb * ls

    xs = [x_prompt.reshape(n_ctx, D), x_sample.reshape(db * dl, D)]
    seqs = (ls, dl)
    rows = 16
    cond = jnp.concatenate([c_ctx[None, :], c, jnp.zeros((rows - 1 - db, D), F32)], axis=0)
    mod_all = _ada_mod(cond, w_ada, b_ada).reshape(depth, rows, 1, 6 * D)
    g_mix3 = g_mix.reshape(depth, 1, D)
    g_ffn3 = g_ffn.reshape(depth, 1, D)

    wr = w_router
    rb = router_bias.reshape(N_EXPERTS, 1)
    wg_all, wu_all, wd_all = w_exp_gate, w_exp_up, w_exp_down

    outs = {}
    for i in range(depth):
        j = i // 2
        mods = [(mod_all, i, 0, 1), (mod_all, i, 1, db)]
        if i % 2 == 0:
            wq, wk, wv = _mla_weights(mla_w_q_b[j], mla_w_kv_b[j])
            cw = jnp.pad(ssd_conv_w[j], ((0, 5), (0, 0)))
            cb = ssd_conv_b[j].reshape(1, SSD_CONV_DIM)
            hp = _pad_lanes(jnp.stack([ssd_a_log_fwd[j], ssd_a_log_bwd[j], ssd_dt_bias_fwd[j],
                                       ssd_dt_bias_bwd[j], ssd_d[j]]), LANE)
            hp = jnp.pad(hp, ((0, 3), (0, 0)))
            nw = ssd_norm[j].reshape(1, SSD_INNER)
            qan = mla_q_a_norm[j].reshape(1, MLA_Q_LORA)
            kvn = mla_kv_a_norm[j].reshape(1, MLA_KV_LORA)
            qn = _pad_lanes(mla_q_norm[j].reshape(1, MLA_QK) * (MLA_QK ** -0.5 * LOG2E), LANE)
            kn = _pad_lanes(mla_k_norm[j].reshape(1, MLA_QK), LANE)
            h0 = (_state_to_kernel(state_ssd_fwd[:, j]), _state_to_kernel(state_ssd_bwd[:, j]))
            ckpe = jnp.pad(cache_mla_kpe[:, j], ((0, 0), (0, 0), (MLA_NOPE, LANE - MLA_QK)))
            tables = tuple(jnp.asarray(np.tile(tb, (1, 2))) for tb in _rope_tables(dl, MLA_ROPE, MLA_NOPE, 1))
            w_out = w_out_even
            ys = []
            for s in range(2):
                z, xbc, qa, kv = _inproj(xs[s], mods[s], g_mix3, i, w_in_even, j, EVEN_WIDTHS, EVEN_PIECES)
                dt = kv
                if s == 0:
                    y, sf, sb = _ssd(z, xbc, dt, cw, cb, hp, nw, None, seq=seqs[s])
                    o, ckv_new, kpe_new = _mla(qa, kv, qan, kvn, wq, wk, wv, qn, kn, None, None, seq=seqs[s])
                    outs.setdefault("ssd_f", []).append(_state_from_kernel(sf))
                    outs.setdefault("ssd_b", []).append(_state_from_kernel(sb))
                    outs.setdefault("ckv", []).append(ckv_new.reshape(nb, ls, MLA_KV_LORA))
                    outs.setdefault("kpe", []).append(kpe_new[:, MLA_NOPE:MLA_QK].reshape(nb, ls, MLA_ROPE))
                else:
                    y, = _ssd(z, xbc, dt, cw, cb, hp, nw, h0, seq=seqs[s])
                    o, = _mla(qa, kv, qan, kvn, wq, wk, wv, qn, kn, (cache_mla_ckv[:, j], ckpe), tables,
                              seq=seqs[s])
                ys.append((y, o))
        else:
            nq = GQA_HEADS * GQA_HEAD_DIM
            nkv = GQA_KV_HEADS * GQA_HEAD_DIM
            nk = GQA_KV_HEADS * LANE
            qscale = GQA_HEAD_DIM ** -0.5 * LOG2E
            qn = jnp.tile(gqa_q_norm[j].reshape(1, GQA_HEAD_DIM) * qscale, (1, nq // GQA_HEAD_DIM))
            kn = jnp.tile(gqa_k_norm[j].reshape(1, GQA_HEAD_DIM), (1, nk // GQA_HEAD_DIM))
            cache = (cache_gqa_k[:, j].reshape(db, -1, nkv), cache_gqa_v[:, j].reshape(db, -1, nkv))
            tables = tuple(jnp.asarray(np.tile(tb, (1, 2))) for tb in _rope_tables(dl, GQA_HEAD_DIM, 0, 2))
            w_out = w_out_odd
            ys = []
            for s in range(2):
                q, k, v = _inproj(xs[s], mods[s], g_mix3, i, w_in_odd, j, ODD_WIDTHS, ODD_PIECES)
                if s == 0:
                    o, k_new = _gqa(q, k, v, qn, kn, None, None, seq=seqs[s])
                    outs.setdefault("gk", []).append(
                        _undup_heads(k_new).reshape(nb, ls, GQA_KV_HEADS, GQA_HEAD_DIM))
                    outs.setdefault("gv", []).append(
                        _undup_heads(v).reshape(nb, ls, GQA_KV_HEADS, GQA_HEAD_DIM))
                else:
                    o, = _gqa(q, k, v, qn, kn, cache, tables, seq=seqs[s])
                ys.append((o,))
        for s in range(2):
            x1, h2, routed, cnt = _outproj(xs[s], mods[s], ys[s], w_out, j, g_ffn3, i, wr, rb)
            xs[s] = _moe(x1, mods[s], h2, routed, cnt, wg_all, wu_all, wd_all, i)

    stack = lambda key: jnp.stack(outs[key], axis=1)
    return (xs[0].reshape(nb, ls, D), xs[1].reshape(db, dl, D),
            stack("ssd_f"), stack("ssd_b"), stack("ckv"), stack("kpe"), stack("gk"), stack("gv"))
```

```python
import functools

import numpy as np

import jax
import jax.numpy as jnp
from jax import lax
from jax.experimental import pallas as pl
from jax.experimental.pallas import tpu as pltpu

F32 = jnp.float32
BF16 = jnp.bfloat16

D = 1024
EPS = 1e-6
GRID_W = 64
ROPE_THETA = 10000.0
SSD_HEADS = 16
SSD_HEADDIM = 64
SSD_INNER = 1024
SSD_STATE = 64
SSD_GROUPS = 2
SSD_CONV_DIM = 1280
SSD_CHUNK = 128
MLA_HEADS = 8
MLA_Q_LORA = 256
MLA_KV_LORA = 128
MLA_NOPE = 64
MLA_ROPE = 32
MLA_V = 64
MLA_QK = 96
GQA_HEADS = 16
GQA_KV_HEADS = 4
GQA_HEAD_DIM = 64
N_EXPERTS = 16
N_GROUPS = 4
GROUP_SIZE = 4
D_EXPERT = 256

LANE = 128
Q_BLOCK = 128
LAT_Q_ROWS = 256
LOG2E = 1.4426950408889634
TOKEN_TILE = 512
EXPERT_TILE = 512
SEG_ALIGN = 16
GROUP_LANE = 16
DISPATCH_ROWS = TOKEN_TILE + LANE
SORTED_WIDTH = D + LANE
ADA_TILE = 1536
COND_ROWS = 16
VMEM_LIMIT = 56 * 1024 * 1024


def _cparams(n_axes):
    return pltpu.CompilerParams(dimension_semantics=("arbitrary",) * n_axes,
                                vmem_limit_bytes=VMEM_LIMIT)


def _dot(a, b):
    return jnp.dot(a, b, preferred_element_type=F32)


def _dot_nt(a, b):
    return lax.dot_general(a, b, (((1,), (1,)), ((), ())), preferred_element_type=F32)


def _split3(x):
    hi = x.astype(BF16)
    r1 = x - hi.astype(F32)
    mid = r1.astype(BF16)
    lo = (r1 - mid.astype(F32)).astype(BF16)
    return hi, mid, lo


def _dot_x3(x, e):
    hi, mid, lo = _split3(x)
    return _dot(hi, e) + _dot(mid, e) + _dot(lo, e)


def _dot_x2(x, e):
    hi = x.astype(BF16)
    mid = (x - hi.astype(F32)).astype(BF16)
    return _dot(hi, e) + _dot(mid, e)


def _dot_3x(e, x):
    hi, mid, lo = _split3(x)
    return _dot(e, hi) + _dot(e, mid) + _dot(e, lo)


def _dot_f32(a, b):
    a0, a1, a2 = _split3(a)
    b0, b1, b2 = _split3(b)
    return (_dot(a0, b0) + (_dot(a0, b1) + _dot(a1, b0))
            + (_dot(a1, b1) + _dot(a0, b2) + _dot(a2, b0)))


def _sigmoid(x):
    return 1.0 / (1.0 + jnp.exp(-x))


def _silu(x):
    return x * _sigmoid(x)


def _iota(shape, axis):
    return lax.broadcasted_iota(jnp.int32, shape, axis)


def _ada_kernel(c_ref, w_ref, b_ref, o_ref):
    a = _silu(c_ref[...]).astype(BF16)
    o_ref[0] = _dot(a, w_ref[0].astype(BF16)) + b_ref[0]


def _ada_mod(cond, w_ada, b_ada):
    depth, _, n = w_ada.shape
    rows = cond.shape[0]
    return pl.pallas_call(
        _ada_kernel,
        grid=(depth, n // ADA_TILE),
        in_specs=[pl.BlockSpec((rows, D), lambda l, j: (0, 0)),
                  pl.BlockSpec((1, D, ADA_TILE), lambda l, j: (l, 0, j)),
                  pl.BlockSpec((1, 1, ADA_TILE), lambda l, j: (l, 0, j))],
        out_specs=pl.BlockSpec((1, rows, ADA_TILE), lambda l, j: (l, 0, j)),
        out_shape=jax.ShapeDtypeStruct((depth, rows, n), F32),
        compiler_params=_cparams(2),
        name="ada_mod",
    )(cond, w_ada, b_ada.reshape(depth, 1, n))


def _mod_spec(mod, n_tokens, tile):
    _, layer, row0, n_rows = mod
    per_row = n_tokens // n_rows
    return pl.BlockSpec((None, 1, 1, 6 * D), lambda i, *_: (layer, row0 + (i * tile) // per_row, 0, 0))


def _layer_spec(a, layer):
    return pl.BlockSpec((None,) + a.shape[1:], lambda i, *_: (layer, 0, 0))


def _norm_mod(x, g, shift, scale):
    ms = jnp.mean(x * x, axis=-1, keepdims=True)
    y = x * lax.rsqrt(ms + EPS) * g
    return y * (1.0 + scale) + shift


def _inproj_kernel(x_ref, mod_ref, g_ref, w_ref, *refs, widths, pieces, transposed):
    o_refs, wb = refs[:-1], refs[-1]

    @pl.when(pl.program_id(0) == 0)
    def _():
        wb[...] = jnp.zeros_like(wb)
        for src, width, dst in pieces:
            if transposed:
                wb[dst:dst + width, :] = w_ref[src:src + width, :].astype(BF16)
            else:
                wb[:, dst:dst + width] = w_ref[:, src:src + width].astype(BF16)

    m = mod_ref[0]
    h = _norm_mod(x_ref[...], g_ref[...], m[:, 0:D], m[:, D:2 * D]).astype(BF16)
    off = 0
    for o_ref, wd in zip(o_refs, widths):
        if transposed:
            o_ref[...] = _dot_nt(h, wb[off:off + wd, :]).astype(o_ref.dtype)
        else:
            o_ref[...] = _dot(h, wb[:, off:off + wd]).astype(o_ref.dtype)
        off += wd


def _inproj(x, mod, g, layer, w, j, widths, pieces, transposed=False):
    t = x.shape[0]
    n_out = sum(widths)
    return pl.pallas_call(
        functools.partial(_inproj_kernel, widths=widths, pieces=pieces, transposed=transposed),
        grid=(t // TOKEN_TILE,),
        in_specs=[pl.BlockSpec((TOKEN_TILE, D), lambda i: (i, 0)),
                  _mod_spec(mod, t, TOKEN_TILE),
                  _layer_spec(g, layer),
                  pl.BlockSpec((None,) + w.shape[1:], lambda i: (j, 0, 0))],
        out_specs=[pl.BlockSpec((TOKEN_TILE, wd), lambda i: (i, 0)) for wd in widths],
        out_shape=[jax.ShapeDtypeStruct((t, wd), F32) for wd in widths],
        scratch_shapes=[pltpu.VMEM((n_out, D) if transposed else (D, n_out), BF16)],
        compiler_params=_cparams(1),
        name="inproj",
    )(x, mod[0], g, w)


EVEN_WIDTHS = (SSD_INNER, SSD_CONV_DIM, MLA_Q_LORA, 2 * LANE)
_XBC_END = SSD_INNER + SSD_CONV_DIM
_QA0 = _XBC_END + SSD_HEADS
_KV0 = _QA0 + MLA_Q_LORA
EVEN_PIECES = ((0, _XBC_END, 0),
               (_QA0, MLA_Q_LORA, _XBC_END),
               (_KV0, MLA_KV_LORA, _XBC_END + MLA_Q_LORA),
               (_KV0 + MLA_KV_LORA, MLA_ROPE, _XBC_END + MLA_Q_LORA + LANE + MLA_NOPE),
               (_XBC_END, SSD_HEADS, _XBC_END + MLA_Q_LORA + LANE))
_NQ = GQA_HEADS * GQA_HEAD_DIM
_NKV = GQA_KV_HEADS * GQA_HEAD_DIM
ODD_WIDTHS = (_NQ, GQA_KV_HEADS * LANE, GQA_KV_HEADS * LANE)
ODD_PIECES = ((0, _NQ, 0),) + tuple(
    (_NQ + part * _NKV + h * GQA_HEAD_DIM, GQA_HEAD_DIM,
     _NQ + part * GQA_KV_HEADS * LANE + h * LANE + rep * GQA_HEAD_DIM)
    for part in range(2) for h in range(GQA_KV_HEADS) for rep in range(2))


def _ssd_kernel(*refs, seq, has_h0, want_state):
    it = iter(refs)
    z_ref, xbc_ref, dt_ref, cw_ref, cb_ref, hp_ref, nw_ref = [next(it) for _ in range(7)]
    h0_refs = (next(it), next(it)) if has_h0 else None
    y_ref = next(it)
    s_refs = (next(it), next(it)) if want_state else None
    xs_ref, bc_ref, yacc_ref, st_ref = [next(it) for _ in range(4)]

    q = SSD_CHUNK
    nc = seq // q
    unrolled = nc <= 2
    gn = SSD_GROUPS * SSD_STATE

    cw = cw_ref[...]
    cb = cb_ref[...]
    rid = _iota((q, SSD_CONV_DIM), 0)

    def rows(c):
        return pl.ds(c * q, q) if isinstance(c, int) else pl.ds(pl.multiple_of(c * q, q), q)

    def conv_chunk(c):
        xc = xbc_ref[rows(c), :]
        if isinstance(c, int):
            prev = xbc_ref[c * q - 1:c * q, :] if c > 0 else jnp.zeros((1, SSD_CONV_DIM), F32)
            nxt = xbc_ref[(c + 1) * q:(c + 1) * q + 1, :] if c < nc - 1 else jnp.zeros((1, SSD_CONV_DIM), F32)
        else:
            prev = xbc_ref[pl.ds(jnp.maximum(c * q - 1, 0), 1), :]
            nxt = xbc_ref[pl.ds(jnp.minimum(c * q + q, seq - 1), 1), :]
            prev = jnp.where(c > 0, prev, 0.0)
            nxt = jnp.where(c < nc - 1, nxt, 0.0)
        xp = jnp.where(rid == 0, prev, pltpu.roll(xc, 1, axis=0))
        xn = jnp.where(rid == q - 1, nxt, pltpu.roll(xc, q - 1, axis=0))
        v = _silu(xp * cw[0:1] + xc * cw[1:2] + xn * cw[2:3] + cb)
        xs_ref[rows(c), :] = v[:, :SSD_INNER]
        bc_ref[rows(c), :] = v[:, SSD_INNER:]

    hp = hp_ref[...]
    expand = (_iota((LANE, SSD_INNER), 1) // SSD_HEADDIM == _iota((LANE, SSD_INNER), 0)).astype(BF16)
    blockmask = (_iota((gn, SSD_INNER), 0) // SSD_STATE
                 == _iota((gn, SSD_INNER), 1) // (SSD_INNER // SSD_GROUPS))
    ri = _iota((q, q), 0)
    ci = _iota((q, q), 1)
    lane = _iota((q, LANE), 1)
    lo_half = lane < SSD_STATE
    d_e = _dot_x3(jnp.broadcast_to(hp[4:5], (8, LANE)), expand)[0:1]
    nw = nw_ref[...]

    def scan_chunk(c, fwd, a, bias, causal, tri):
        xs = xs_ref[rows(c), :]
        bc = bc_ref[rows(c), :]
        bm = bc[:, :gn]
        cm = bc[:, gn:]
        raw = dt_ref[rows(c), :] + bias
        dt = jnp.maximum(raw, 0.0) + jnp.log1p(jnp.exp(-jnp.abs(raw)))
        cum = _dot_3x(tri, dt * a)
        cum_t = cum.T
        dt_t = dt.T
        last = cum[q - 1:q] if fwd else cum[0:1]
        ecum = jnp.exp(cum)
        ex = _dot_x2(jnp.concatenate([ecum, dt * jnp.exp(last - cum)], axis=0), expand)
        ecum_e, w_e = ex[0:q], ex[q:2 * q]
        cdec_e = ecum_e[q - 1:q] if fwd else ecum_e[0:1]
        st = st_ref[...]
        cmb = cm.astype(BF16)
        bmb = bm.astype(BF16)
        y = _dot(cmb, st.astype(BF16)) * ecum_e
        cb0 = _dot_nt(jnp.where(lo_half, cm, 0.0).astype(BF16), bmb)
        cb1 = _dot_nt(jnp.where(lo_half, 0.0, cm).astype(BF16), bmb)
        pairs = []
        for j in range(SSD_HEADS // 2):
            cbg = cb0 if j < SSD_HEADS // 4 else cb1
            xp = xs[:, j * LANE:(j + 1) * LANE]
            scs = []
            for half in range(2):
                h = 2 * j + half
                seg = cum[:, h:h + 1] - cum_t[h:h + 1, :]
                dec = jnp.exp(jnp.where(causal, seg, -jnp.inf))
                scs.append((cbg * dec * dt_t[h:h + 1, :]).astype(BF16))
            rhs = jnp.concatenate([jnp.where(lo_half, xp, 0.0), jnp.where(lo_half, 0.0, xp)], axis=0)
            pairs.append(_dot(jnp.concatenate(scs, axis=1), rhs.astype(BF16)))
        y = y + jnp.concatenate(pairs, axis=1)
        new = _dot(bm.T.astype(BF16), (xs * w_e).astype(BF16))
        st_ref[...] = jnp.where(blockmask, st * cdec_e + new, 0.0)
        if fwd:
            yacc_ref[rows(c), :] = y
        else:
            y = y + yacc_ref[rows(c), :] + xs * d_e
            y = y * _silu(z_ref[rows(c), :])
            ms = jnp.mean(y * y, axis=-1, keepdims=True)
            y_ref[rows(c), :] = (y * lax.rsqrt(ms + EPS) * nw).astype(y_ref.dtype)

    def scan_pass(fwd):
        a = -jnp.exp(hp[0:1] if fwd else hp[1:2])
        bias = hp[2:3] if fwd else hp[3:4]
        causal = (ri >= ci) if fwd else (ri <= ci)
        tri = causal.astype(BF16)
        if has_h0:
            st_ref[...] = h0_refs[0 if fwd else 1][0]
        else:
            st_ref[...] = jnp.zeros((gn, SSD_INNER), F32)
        if unrolled:
            for k in range(nc):
                scan_chunk(k if fwd else nc - 1 - k, fwd, a, bias, causal, tri)
        else:
            def body(k, carry):
                c = k if fwd else nc - 1 - k
                scan_chunk(c, fwd, a, bias, causal, tri)
                if fwd:
                    conv_chunk(jnp.minimum(c + 1, nc - 1))
                return carry

            lax.fori_loop(0, nc, body, 0)
        if want_state:
            for j in range(SSD_INNER // LANE):
                g = (j * LANE) // (SSD_INNER // SSD_GROUPS)
                t = st_ref[:, j * LANE:(j + 1) * LANE].T
                s_refs[0 if fwd else 1][0, j * LANE:(j + 1) * LANE, :] = t[:, g * SSD_STATE:(g + 1) * SSD_STATE]

    if unrolled:
        for c in range(nc):
            conv_chunk(c)
    else:
        conv_chunk(0)
    scan_pass(True)
    scan_pass(False)


def _ssd(z, xbc, dt, cw, cb, hp, nw, h0, *, seq):
    t = z.shape[0]
    n_seq = t // seq
    has_h0 = h0 is not None
    want_state = not has_h0
    gn = SSD_GROUPS * SSD_STATE
    tok = lambda w: pl.BlockSpec((seq, w), lambda b: (b, 0))
    full = lambda a: pl.BlockSpec(a.shape, lambda b: (0,) * a.ndim)
    dt_spec = pl.BlockSpec((seq, LANE), lambda b: (b, 1))
    in_specs = [tok(SSD_INNER), tok(SSD_CONV_DIM), dt_spec, full(cw), full(cb), full(hp), full(nw)]
    args = [z, xbc, dt, cw, cb, hp, nw]
    if has_h0:
        in_specs += [pl.BlockSpec((1, gn, SSD_INNER), lambda b: (b, 0, 0))] * 2
        args += list(h0)
    out_shape = [jax.ShapeDtypeStruct((t, SSD_INNER), BF16)]
    out_specs = [tok(SSD_INNER)]
    if want_state:
        out_shape += [jax.ShapeDtypeStruct((n_seq, SSD_INNER, SSD_STATE), F32)] * 2
        out_specs += [pl.BlockSpec((1, SSD_INNER, SSD_STATE), lambda b: (b, 0, 0))] * 2

    return pl.pallas_call(
        functools.partial(_ssd_kernel, seq=seq, has_h0=has_h0, want_state=want_state),
        grid=(n_seq,),
        in_specs=in_specs,
        out_specs=out_specs,
        out_shape=out_shape,
        scratch_shapes=[pltpu.VMEM((seq, SSD_INNER), F32),
                        pltpu.VMEM((seq, 2 * gn), F32),
                        pltpu.VMEM((seq, SSD_INNER), F32),
                        pltpu.VMEM((gn, SSD_INNER), F32)],
        compiler_params=_cparams(1),
        name="ssd_lat" if has_h0 else "ssd_ctx",
    )(*args)


def _softmax_pv(ss, vals):
    m = ss[0].max(axis=-1, keepdims=True)
    for s in ss[1:]:
        m = jnp.maximum(m, s.max(axis=-1, keepdims=True))
    ps = [jnp.exp2(s - m) for s in ss]
    den = ps[0].sum(axis=-1, keepdims=True)
    for p in ps[1:]:
        den = den + p.sum(axis=-1, keepdims=True)
    o = _dot(ps[0].astype(BF16), vals[0]())
    for p, v in zip(ps[1:], vals[1:]):
        o = o + _dot(p.astype(BF16), v())
    return o / den


def _attend_blocks(seq, units, emit, s_scr, qrows):
    nblk = seq // qrows

    def scores(u, blk, static):
        q_ref, rows, keys, _ = units[u]
        r = blk * rows if static else pl.multiple_of(blk * rows, rows)
        qb = q_ref[pl.ds(r, rows), :]
        return [_dot_nt(qb, k()) for k in keys]

    if s_scr is None:
        for i in range(nblk):
            emit(i * qrows, [_softmax_pv(scores(u, i, True), units[u][3]) for u in range(len(units))])
        return

    def put(slot, blk):
        for u in range(len(units)):
            off = 0
            for s in scores(u, blk, isinstance(blk, int)):
                s_scr[slot][u][:, off:off + s.shape[1]] = s
                off += s.shape[1]

    def take(slot, blk):
        outs = []
        for u, (_, _, keys, vals) in enumerate(units):
            off = 0
            ss = []
            for k in keys:
                n = k().shape[0]
                ss.append(s_scr[slot][u][:, off:off + n])
                off += n
            outs.append(_softmax_pv(ss, vals))
        emit(blk * qrows if isinstance(blk, int) else pl.multiple_of(blk * qrows, qrows), outs)

    put(0, 0)

    def body(ii, carry):
        b0 = 2 * ii
        put(1, b0 + 1)
        take(0, b0)

        def more():
            put(0, b0 + 2)
            take(1, b0 + 1)

        def last():
            take(1, b0 + 1)

        lax.cond(b0 + 2 < nblk, more, last)
        return carry

    lax.fori_loop(0, nblk // 2, body, 0)


def _mla_kernel(*refs, seq, lat):
    it = iter(refs)
    qa_ref, kv_ref, qan_ref, kvn_ref, wq_ref, wk_ref, wv_ref, qn_ref, kn_ref = [next(it) for _ in range(9)]
    if lat:
        cckv_ref, ckpe_ref, cos_ref, sin_ref = [next(it) for _ in range(4)]
    o_ref = next(it)
    if not lat:
        ckv_out, kpe_out = next(it), next(it)
    q_s, k_s, v_s, qh0_s, qh1_s, kh0_s, kh1_s = [next(it) for _ in range(7)]
    s_scr = None
    if lat:
        kc_s, vc_s, kch0_s, kch1_s = [next(it) for _ in range(4)]
        s_scr = [[next(it), next(it)], [next(it), next(it)]]

    kv = kv_ref[...]
    kpe = jnp.where(_iota((seq, LANE), 1) >= MLA_NOPE, kv[:, LANE:], 0.0)
    ckv_raw = kv[:, :LANE]
    ckv = ckv_raw * lax.rsqrt(jnp.mean(ckv_raw * ckv_raw, axis=-1, keepdims=True) + EPS) * kvn_ref[...]
    qa = qa_ref[...]
    qa = qa * lax.rsqrt(jnp.mean(qa * qa, axis=-1, keepdims=True) + EPS) * qan_ref[...]
    if not lat:
        ckv_out[...] = ckv
        kpe_out[...] = kpe
    ckvb = ckv.astype(BF16)
    q_s[...] = _dot(qa.astype(BF16), wq_ref[...])
    k_s[...] = _dot(ckvb, wk_ref[...])
    v_s[...] = _dot(ckvb, wv_ref[...]).astype(BF16)
    if lat:
        ccb = cckv_ref[0].astype(BF16)
        kc_s[...] = _dot(ccb, wk_ref[...])
        vc_s[...] = _dot(ccb, wv_ref[...]).astype(BF16)
        ckpe = ckpe_ref[0]
        cos = cos_ref[...]
        sin = sin_ref[...]
        dist = MLA_ROPE // 4
        li = _iota((2 * LANE, 2 * LANE), 0)
        lj = _iota((2 * LANE, 2 * LANE), 1)
        swap = (li == jnp.where((lj % (2 * dist)) < dist, lj + dist, lj - dist)).astype(BF16)
    qn = qn_ref[...]
    kn = kn_ref[...]
    qrows = LAT_Q_ROWS if lat else Q_BLOCK
    lo_lanes = _iota((qrows, LANE), 1) < MLA_V

    def head_norm(x, g):
        return x * lax.rsqrt(jnp.sum(x * x, axis=-1, keepdims=True) * (1.0 / MLA_QK) + EPS) * g

    def rotate(x2):
        return x2 * cos + _dot_x2(x2, swap) * sin

    for j in range(MLA_HEADS // 2):
        pair = slice(j * LANE, (j + 1) * LANE)
        sls = [slice((2 * j + half) * LANE, (2 * j + half + 1) * LANE) for half in range(2)]
        q2 = jnp.concatenate([head_norm(q_s[:, sl], qn) for sl in sls], axis=1)
        k2 = jnp.concatenate([head_norm(k_s[:, sl] + kpe, kn) for sl in sls], axis=1)
        if lat:
            q2 = rotate(q2)
            k2 = rotate(k2)
            for half, kch_s in enumerate((kch0_s, kch1_s)):
                kch_s[...] = head_norm(kc_s[:, sls[half]] + ckpe, kn).astype(BF16)
        for half, (qh_s, kh_s) in enumerate(((qh0_s, kh0_s), (qh1_s, kh1_s))):
            qh_s[...] = q2[:, half * LANE:(half + 1) * LANE].astype(BF16)
            kh_s[...] = k2[:, half * LANE:(half + 1) * LANE].astype(BF16)

        def emit(r0, outs, pair=pair):
            o_ref[pl.ds(r0, qrows), pair] = jnp.where(lo_lanes, outs[0], outs[1]).astype(o_ref.dtype)

        units = []
        for half, (qh_s, kh_s) in enumerate(((qh0_s, kh0_s), (qh1_s, kh1_s))):
            keys = [lambda r=kh_s: r[...]]
            vals = [lambda pair=pair: v_s[:, pair]]
            if lat:
                keys = [lambda r=(kch0_s, kch1_s)[half]: r[...]] + keys
                vals = [lambda pair=pair: vc_s[:, pair]] + vals
            units.append((qh_s, qrows, keys, vals))
        _attend_blocks(seq, units, emit, s_scr, qrows)


def _mla(qa, kv, qan, kvn, wq, wk, wv, qn, kn, cache, tables, *, seq):
    t = qa.shape[0]
    n_seq = t // seq
    lat = cache is not None
    hv = MLA_HEADS * MLA_V
    tok = lambda w: pl.BlockSpec((seq, w), lambda b: (b, 0))
    full = lambda a: pl.BlockSpec(a.shape, lambda b: (0,) * a.ndim)
    args = [qa, kv, qan, kvn, wq, wk, wv, qn, kn]
    in_specs = [tok(MLA_Q_LORA), tok(2 * LANE)] + [full(a) for a in args[2:]]
    past = 0
    if lat:
        past = cache[0].shape[1]
        in_specs += [pl.BlockSpec((1, past, LANE), lambda b: (b, 0, 0))] * 2 + [full(tables[0]), full(tables[1])]
        args += [cache[0], cache[1], tables[0], tables[1]]
    out_shape = [jax.ShapeDtypeStruct((t, hv), BF16)]
    out_specs = [tok(hv)]
    if not lat:
        out_shape += [jax.ShapeDtypeStruct((t, LANE), F32)] * 2
        out_specs += [pl.BlockSpec((seq, LANE), lambda b: (b, 0))] * 2
    scratch = [pltpu.VMEM((seq, MLA_HEADS * LANE), F32),
               pltpu.VMEM((seq, MLA_HEADS * LANE), F32),
               pltpu.VMEM((seq, hv), BF16)] + [pltpu.VMEM((seq, LANE), BF16)] * 4
    if lat:
        scratch += [pltpu.VMEM((past, MLA_HEADS * LANE), F32),
                    pltpu.VMEM((past, hv), BF16),
                    pltpu.VMEM((past, LANE), BF16),
                    pltpu.VMEM((past, LANE), BF16)]
        scratch += [pltpu.VMEM((LAT_Q_ROWS, past + seq), F32)] * 4

    return pl.pallas_call(
        functools.partial(_mla_kernel, seq=seq, lat=lat),
        grid=(n_seq,),
        in_specs=in_specs,
        out_specs=out_specs,
        out_shape=out_shape,
        scratch_shapes=scratch,
        compiler_params=_cparams(1),
        name="mla_lat" if lat else "mla_ctx",
    )(*args)


def _gqa_kernel(*refs, seq, lat):
    it = iter(refs)
    q_ref, k_ref, v_ref, qn_ref, kn_ref = [next(it) for _ in range(5)]
    if lat:
        ck_ref, cv_ref, cos_ref, sin_ref = [next(it) for _ in range(4)]
    o_ref = next(it)
    if not lat:
        kn_out = next(it)
    kn_s, vb_s, q2_s = [next(it) for _ in range(3)]
    s_scr = None
    if lat:
        ckb_s, cvb_s = next(it), next(it)
        s_scr = [[next(it)], [next(it)]]

    hd = GQA_HEAD_DIM
    nq = GQA_HEADS * hd
    nk = GQA_KV_HEADS * LANE
    qrows = LAT_Q_ROWS if lat else Q_BLOCK
    nblk = seq // qrows
    wide = 2 * LANE
    li = _iota((wide, wide), 0)
    lj = _iota((wide, wide), 1)
    same_head = (li // hd == lj // hd).astype(BF16)
    if lat:
        dist = hd // 4
        swap = (li == jnp.where((lj % (2 * dist)) < dist, lj + dist, lj - dist)).astype(BF16)
        cos = cos_ref[...]
        sin = sin_ref[...]
    lo_q = _iota((qrows, LANE), 1) < hd

    def prepared(x_ref, g_ref, b):
        sl = slice(b * wide, (b + 1) * wide)
        x = x_ref[:, sl]
        ss = _dot_x2(x * x, same_head)
        xn = x * lax.rsqrt(ss * (1.0 / hd) + EPS) * g_ref[:, sl]
        if not lat:
            return xn, xn
        return xn, xn * cos + _dot_x2(xn, swap) * sin

    for b in range(nk // wide):
        sl = slice(b * wide, (b + 1) * wide)
        kn, kr = prepared(k_ref, kn_ref, b)
        if not lat:
            kn_out[:, sl] = kn
        kn_s[:, sl] = kr.astype(BF16)
    if lat:
        for src_ref, dst in ((ck_ref, ckb_s), (cv_ref, cvb_s)):
            for h in range(GQA_KV_HEADS):
                head = src_ref[0, :, h * hd:(h + 1) * hd].astype(BF16)
                for rep in range(LANE // hd):
                    dst[:, h * LANE + rep * hd:h * LANE + (rep + 1) * hd] = head
    vb_s[...] = v_ref[...].astype(BF16)
    for b in range(nq // wide):
        _, qr = prepared(q_ref, qn_ref, b)
        for half in range(wide // LANE):
            p = b * (wide // LANE) + half
            for i in range(nblk):
                blk = qr[i * qrows:(i + 1) * qrows, half * LANE:(half + 1) * LANE]
                q2_s[p, 2 * i * qrows:(2 * i + 1) * qrows, :] = jnp.where(lo_q, blk, 0.0).astype(BF16)
                q2_s[p, (2 * i + 1) * qrows:(2 * i + 2) * qrows, :] = jnp.where(lo_q, 0.0, blk).astype(BF16)

    for j in range(GQA_HEADS // 2):
        pair = slice(j * LANE, (j + 1) * LANE)
        g = (2 * j) // (GQA_HEADS // GQA_KV_HEADS)
        gs = slice(g * LANE, (g + 1) * LANE)

        def emit(r0, outs, pair=pair):
            o = outs[0]
            o_ref[pl.ds(r0, qrows), pair] = jnp.where(lo_q, o[:qrows], o[qrows:]).astype(o_ref.dtype)

        keys = [lambda gs=gs: kn_s[:, gs]]
        vals = [lambda gs=gs: vb_s[:, gs]]
        if lat:
            keys = [lambda gs=gs: ckb_s[:, gs]] + keys
            vals = [lambda gs=gs: cvb_s[:, gs]] + vals
        _attend_blocks(seq, [(q2_s.at[j], 2 * qrows, keys, vals)], emit, s_scr, qrows)


def _gqa(q, k, v, qn, kn, cache, tables, *, seq):
    t = q.shape[0]
    n_seq = t // seq
    lat = cache is not None
    nq = GQA_HEADS * GQA_HEAD_DIM
    nk = GQA_KV_HEADS * LANE
    tok = lambda w: pl.BlockSpec((seq, w), lambda b: (b, 0))
    full = lambda a: pl.BlockSpec(a.shape, lambda b: (0,) * a.ndim)
    args = [q, k, v, qn, kn]
    in_specs = [tok(nq), tok(nk), tok(nk), full(qn), full(kn)]
    past = 0
    if lat:
        past = cache[0].shape[1]
        nkv = cache[0].shape[2]
        in_specs += [pl.BlockSpec((1, past, nkv), lambda b: (b, 0, 0))] * 2 + [full(tables[0]), full(tables[1])]
        args += [cache[0], cache[1], tables[0], tables[1]]
    out_shape = [jax.ShapeDtypeStruct((t, nq), BF16)]
    out_specs = [tok(nq)]
    if not lat:
        out_shape += [jax.ShapeDtypeStruct((t, nk), F32)]
        out_specs += [pl.BlockSpec((seq, nk), lambda b: (b, 0))]
    scratch = [pltpu.VMEM((seq, nk), BF16),
               pltpu.VMEM((seq, nk), BF16),
               pltpu.VMEM((GQA_HEADS // 2, 2 * seq, LANE), BF16)]
    if lat:
        scratch += [pltpu.VMEM((past, nk), BF16), pltpu.VMEM((past, nk), BF16)]
        scratch += [pltpu.VMEM((2 * LAT_Q_ROWS, past + seq), F32)] * 2

    return pl.pallas_call(
        functools.partial(_gqa_kernel, seq=seq, lat=lat),
        grid=(n_seq,),
        in_specs=in_specs,
        out_specs=out_specs,
        out_shape=out_shape,
        scratch_shapes=scratch,
        compiler_params=_cparams(1),
        name="gqa_lat" if lat else "gqa_ctx",
    )(*args)


def _route(sel, sc):
    gs = []
    for g in range(N_GROUPS):
        r = sel[g * GROUP_SIZE:(g + 1) * GROUP_SIZE]
        best = None
        for i in range(GROUP_SIZE):
            for j in range(i + 1, GROUP_SIZE):
                s = r[i] + r[j]
                best = s if best is None else jnp.maximum(best, s)
        gs.append(best)
    cur = gs[0]
    grp = jnp.zeros_like(cur, dtype=jnp.int32)
    for g in range(1, N_GROUPS):
        upd = gs[g] > cur
        grp = jnp.where(upd, g, grp)
        cur = jnp.where(upd, gs[g], cur)

    def pick(rows, i):
        v = rows[i]
        for g in range(1, N_GROUPS):
            v = jnp.where(grp == g, rows[g * GROUP_SIZE + i], v)
        return v

    cand = [pick(sel, i) for i in range(GROUP_SIZE)]
    csc = [pick(sc, i) for i in range(GROUP_SIZE)]
    v1, i1, w1 = cand[0], jnp.zeros_like(grp), csc[0]
    for i in range(1, GROUP_SIZE):
        upd = cand[i] > v1
        v1 = jnp.where(upd, cand[i], v1)
        i1 = jnp.where(upd, i, i1)
        w1 = jnp.where(upd, csc[i], w1)
    v2 = i2 = w2 = None
    for i in range(GROUP_SIZE):
        ok = i1 != i
        if v2 is None:
            v2 = jnp.where(ok, cand[i], -jnp.inf)
            i2 = jnp.zeros_like(grp)
            w2 = csc[i]
        else:
            upd = ok & (cand[i] > v2)
            v2 = jnp.where(upd, cand[i], v2)
            i2 = jnp.where(upd, i, i2)
            w2 = jnp.where(upd, csc[i], w2)
    tot = w1 + w2
    local = [jnp.where(i1 == i, w1 / tot, 0.0) + jnp.where(i2 == i, w2 / tot, 0.0)
             for i in range(GROUP_SIZE)]
    onehot = [jnp.where(grp == g, 1.0, 0.0) for g in range(N_GROUPS)]
    return local, onehot


def _outproj_kernel(*refs, n_y):
    x_ref, mod_ref = refs[0], refs[1]
    y_refs = refs[2:2 + n_y]
    w_ref, g_ref, wr_ref, rb_ref, x1_ref, h2_ref, gate_ref, cnt_ref, w_b, wr_b = refs[2 + n_y:]

    @pl.when(pl.program_id(0) == 0)
    def _():
        w_b[...] = w_ref[...].astype(BF16)
        wr_b[...] = jnp.zeros_like(wr_b)
        for j, term in enumerate(_split3(wr_ref[...])):
            wr_b[:, j * N_EXPERTS:(j + 1) * N_EXPERTS] = term

    m = mod_ref[0]
    off = 0
    out = None
    for y_ref in y_refs:
        k = y_ref.shape[1]
        part = _dot(y_ref[...], w_b[off:off + k, :])
        out = part if out is None else out + part
        off += k
    x1 = x_ref[...] + m[:, 2 * D:3 * D] * out
    x1_ref[...] = x1
    h2 = _norm_mod(x1, g_ref[...], m[:, 3 * D:4 * D], m[:, 4 * D:5 * D])
    h2_hi = h2.astype(BF16)
    h2_ref[...] = h2_hi
    h2_mid = (h2 - h2_hi.astype(F32)).astype(BF16)
    half = h2.shape[0] // 2
    parts = jnp.concatenate([_dot(h2_hi[r], wr_b[...]) + _dot(h2_mid[r], wr_b[...])
                             for r in (slice(0, half), slice(half, None))], axis=0)
    logits = (parts + pltpu.roll(parts, LANE - N_EXPERTS, axis=1)
              + pltpu.roll(parts, LANE - 2 * N_EXPERTS, axis=1))
    sc_t = _sigmoid(logits.T[0:N_EXPERTS, :])
    sel_t = sc_t + rb_ref[...]
    local, onehot = _route([sel_t[e:e + 1] for e in range(N_EXPERTS)],
                           [sc_t[e:e + 1] for e in range(N_EXPERTS)])
    t = x1.shape[0]
    rows_t = (local + [jnp.zeros((GROUP_LANE - GROUP_SIZE, t), F32)] + onehot
              + [jnp.zeros((LANE - GROUP_LANE - N_GROUPS, t), F32)])
    routed = jnp.concatenate(rows_t, axis=0).T
    gate_ref[...] = routed
    cnt = jnp.sum(routed, axis=0, keepdims=True).astype(jnp.int32)
    cnt_ref[0] = jnp.broadcast_to(cnt, (8, LANE))


def _outproj(x, mod, ys, w, j, g, layer, wr, rb):
    t = x.shape[0]
    full = lambda a: pl.BlockSpec(a.shape, lambda i: (0,) * a.ndim)
    return pl.pallas_call(
        functools.partial(_outproj_kernel, n_y=len(ys)),
        grid=(t // TOKEN_TILE,),
        in_specs=[pl.BlockSpec((TOKEN_TILE, D), lambda i: (i, 0)),
                  _mod_spec(mod, t, TOKEN_TILE)]
                 + [pl.BlockSpec((TOKEN_TILE, y.shape[1]), lambda i: (i, 0)) for y in ys]
                 + [pl.BlockSpec((None,) + w.shape[1:], lambda i: (j, 0, 0)), _layer_spec(g, layer),
                    full(wr), full(rb)],
        out_specs=[pl.BlockSpec((TOKEN_TILE, D), lambda i: (i, 0)),
                   pl.BlockSpec((TOKEN_TILE, D), lambda i: (i, 0)),
                   pl.BlockSpec((TOKEN_TILE, LANE), lambda i: (i, 0)),
                   pl.BlockSpec((1, 8, LANE), lambda i: (i, 0, 0))],
        out_shape=[jax.ShapeDtypeStruct((t, D), F32),
                   jax.ShapeDtypeStruct((t, D), BF16),
                   jax.ShapeDtypeStruct((t, LANE), F32),
                   jax.ShapeDtypeStruct((t // TOKEN_TILE, 8, LANE), jnp.int32)],
        scratch_shapes=[pltpu.VMEM(w.shape[1:], BF16), pltpu.VMEM((D, LANE), BF16)],
        compiler_params=_cparams(1),
        name="outproj_router",
    )(x, mod[0], *ys, w, g, wr, rb)


def _pad_to(n, shift):
    return ((n + (1 << shift) - 1) >> shift) << shift


SEG_SHIFT = SEG_ALIGN.bit_length() - 1
TILE_SHIFT = EXPERT_TILE.bit_length() - 1


def _start_segments(plan_ref, i, nt, make):
    for g in range(N_GROUPS):
        n = plan_ref[i * N_GROUPS + g]
        lo = plan_ref[(nt + i) * N_GROUPS + g]
        base = plan_ref[(2 * nt + i) * N_GROUPS + g]

        def body(c, carry, lo=lo, base=base):
            for cp in make(pl.multiple_of(lo + c * SEG_ALIGN, SEG_ALIGN),
                           pl.multiple_of(base + c * SEG_ALIGN, SEG_ALIGN)):
                cp.start()
            return carry

        lax.fori_loop(0, (n + SEG_ALIGN - 1) >> SEG_SHIFT, body, 0)


def _wait_segments(plan_ref, i, make):
    total = jnp.int32(0)
    for g in range(N_GROUPS):
        total = total + ((plan_ref[i * N_GROUPS + g] + SEG_ALIGN - 1) >> SEG_SHIFT)

    def body(c, carry):
        for cp in make(0, 0):
            cp.wait()
        return carry

    lax.fori_loop(0, total, body, 0)


def _dispatch_kernel(cnt_ref, h_ref, r_ref, xs_hbm, dest_ref, plan_ref,
                     xs_l, zx, tril_s, fill_s, sem, *, nt, n_rows):
    i = pl.program_id(0)
    t = TOKEN_TILE
    k_tiles = n_rows // EXPERT_TILE

    @pl.when(i == 0)
    def _():
        zx[...] = jnp.zeros_like(zx)
        tril_s[...] = (_iota((t, t), 0) >= _iota((t, t), 1)).astype(BF16)
        tots = []
        for g in range(N_GROUPS):
            tots.append(lax.fori_loop(
                0, nt, lambda tt, acc, g=g: acc + _pad_to(cnt_ref[tt * N_GROUPS + g], SEG_SHIFT), jnp.int32(0)))
        starts = []
        cur = jnp.int32(0)
        for g in range(N_GROUPS):
            starts.append(cur)
            fill_s[2 * g] = cur + tots[g]
            cur = cur + _pad_to(tots[g], TILE_SHIFT)
            fill_s[2 * g + 1] = cur
        fill_s[2 * N_GROUPS] = cur
        fill_s[2 * N_GROUPS + 1] = n_rows

        def per_tile(tt, run):
            lo = jnp.int32(0)
            new = []
            for g in range(N_GROUPS):
                n = cnt_ref[tt * N_GROUPS + g]
                plan_ref[tt * N_GROUPS + g] = n
                plan_ref[(nt + tt) * N_GROUPS + g] = lo
                plan_ref[(2 * nt + tt) * N_GROUPS + g] = run[g]
                seg = _pad_to(n, SEG_SHIFT)
                lo = lo + seg
                new.append(run[g] + seg)
            return tuple(new)

        lax.fori_loop(0, nt, per_tile, tuple(starts))
        for k in range(k_tiles):
            row = k * EXPERT_TILE
            grp = 0
            for g in range(1, N_GROUPS):
                grp = grp + (row >= starts[g]).astype(jnp.int32)
            plan_ref[3 * nt * N_GROUPS + k] = grp
        plan_ref[3 * nt * N_GROUPS + k_tiles] = cur >> TILE_SHIFT

    routed = r_ref[...]
    lane = _iota((t, LANE), 1)
    is_grp = (lane >= GROUP_LANE) & (lane < GROUP_LANE + N_GROUPS)
    onehot = jnp.where(is_grp, routed, 0.0)
    rank = _dot(tril_s[...], onehot.astype(BF16))
    seg_lo = jnp.zeros((1, LANE), F32)
    lane1 = _iota((1, LANE), 1)
    for g in range(N_GROUPS):
        lo = plan_ref[(nt + i) * N_GROUPS + g]
        seg_lo = jnp.where(lane1 == GROUP_LANE + g, (lo - 1).astype(F32), seg_lo)
    val = jnp.where(onehot > 0.0, rank + seg_lo, 0.0)
    dest = _dot_x2(val, jnp.ones((LANE, LANE), BF16))
    dest_ref[...] = dest
    dest_row = dest.T[0:1, :].astype(jnp.int32)
    perm = jnp.where(_iota((DISPATCH_ROWS, t), 0) == dest_row, 1.0, 0.0).astype(BF16)
    r_hi, r_mid, r_lo = _split3(routed)
    packed = (r_hi.astype(F32) + pltpu.roll(r_mid.astype(F32), GROUP_SIZE, axis=1)
              + pltpu.roll(r_lo.astype(F32), 2 * GROUP_SIZE, axis=1)).astype(BF16)
    slot = i % 2
    xs_l[slot] = _dot(perm, jnp.concatenate([h_ref[...], packed], axis=1)).astype(BF16)

    def to_buffer(slot):
        def make(lo, base):
            return (pltpu.make_async_copy(xs_l.at[slot, pl.ds(lo, SEG_ALIGN)],
                                          xs_hbm.at[pl.ds(base, SEG_ALIGN)], sem.at[slot]),)
        return make

    _start_segments(plan_ref, i, nt, to_buffer(slot))

    @pl.when(i > 0)
    def _():
        _wait_segments(plan_ref, i - 1, to_buffer(1 - slot))

    @pl.when(i == nt - 1)
    def _():
        _wait_segments(plan_ref, i, to_buffer(slot))
        def zero(base):
            return (pltpu.make_async_copy(zx, xs_hbm.at[pl.ds(base, SEG_ALIGN)], sem.at[0]),)

        n_fill = jnp.int32(0)
        for r in range(N_GROUPS + 1):
            lo = fill_s[2 * r]
            chunks = (fill_s[2 * r + 1] - lo) >> SEG_SHIFT

            def body(c, carry, lo=lo):
                for cp in zero(pl.multiple_of(lo + c * SEG_ALIGN, SEG_ALIGN)):
                    cp.start()
                return carry

            lax.fori_loop(0, chunks, body, 0)
            n_fill = n_fill + chunks

        def wait_fill(c, carry):
            for cp in zero(0):
                cp.wait()
            return carry

        lax.fori_loop(0, n_fill, wait_fill, 0)


def _moe_rows(n_tokens):
    nt = n_tokens // TOKEN_TILE
    worst = n_tokens + nt * N_GROUPS * (SEG_ALIGN - 1) + N_GROUPS * (EXPERT_TILE - SEG_ALIGN)
    return _pad_to(worst, TILE_SHIFT)


def _dispatch(counts, h2, routed):
    t = h2.shape[0]
    nt = t // TOKEN_TILE
    n_rows = _moe_rows(t)
    plan_len = 3 * nt * N_GROUPS + n_rows // EXPERT_TILE + 1
    grid_spec = pltpu.PrefetchScalarGridSpec(
        num_scalar_prefetch=1,
        grid=(nt,),
        in_specs=[pl.BlockSpec((TOKEN_TILE, D), lambda i, c: (i, 0)),
                  pl.BlockSpec((TOKEN_TILE, LANE), lambda i, c: (i, 0))],
        out_specs=[pl.BlockSpec(memory_space=pl.ANY),
                   pl.BlockSpec((TOKEN_TILE, LANE), lambda i, c: (i, 0)),
                   pl.BlockSpec(memory_space=pltpu.SMEM)],
        scratch_shapes=[pltpu.VMEM((2, DISPATCH_ROWS, SORTED_WIDTH), BF16),
                        pltpu.VMEM((SEG_ALIGN, SORTED_WIDTH), BF16),
                        pltpu.VMEM((TOKEN_TILE, TOKEN_TILE), BF16),
                        pltpu.SMEM((2 * N_GROUPS + 2,), jnp.int32),
                        pltpu.SemaphoreType.DMA((2,))])
    return pl.pallas_call(
        functools.partial(_dispatch_kernel, nt=nt, n_rows=n_rows),
        grid_spec=grid_spec,
        out_shape=[jax.ShapeDtypeStruct((n_rows, SORTED_WIDTH), BF16),
                   jax.ShapeDtypeStruct((t, LANE), F32),
                   jax.ShapeDtypeStruct((plan_len,), jnp.int32)],
        compiler_params=_cparams(1),
        name="moe_dispatch",
    )(counts, h2, routed)


def _experts_kernel(plan_ref, xs_ref, wg_ref, wu_ref, wd_ref, y_ref, wg_b, wu_b, wd_b,
                    *, group_at, used_at):
    k = pl.program_id(0)
    used = plan_ref[used_at]
    group = plan_ref[group_at + k]
    prev = plan_ref[group_at + jnp.maximum(k - 1, 0)]

    @pl.when((k == 0) | (group != prev))
    def _():
        wg_b[...] = wg_ref[...].astype(BF16)
        wu_b[...] = wu_ref[...].astype(BF16)
        wd_b[...] = wd_ref[...].astype(BF16)

    @pl.when(k < used)
    def _():
        rows = xs_ref[:, :D]
        terms = xs_ref[:, D:].astype(F32)
        gates = (terms + pltpu.roll(terms, LANE - GROUP_SIZE, axis=1)
                 + pltpu.roll(terms, LANE - 2 * GROUP_SIZE, axis=1))
        acts = []
        for e in range(GROUP_SIZE):
            a = _dot(rows, wg_b[e])
            u = _dot(rows, wu_b[e])
            acts.append((_silu(a) * u * gates[:, e:e + 1]).astype(BF16))
        down = wd_b[...].reshape(GROUP_SIZE * D_EXPERT, D)
        y_ref[...] = _dot(jnp.concatenate(acts, axis=1), down).astype(y_ref.dtype)

    @pl.when(k >= used)
    def _():
        y_ref[...] = jnp.zeros_like(y_ref)


def _experts(plan, xs, wg, wu, wd, layer, nt):
    n_rows = xs.shape[0]
    k_tiles = n_rows // EXPERT_TILE
    at = 3 * nt * N_GROUPS
    wspec = lambda a, b: pl.BlockSpec((None, GROUP_SIZE, a, b), lambda k, p: (layer, p[at + k], 0, 0))
    grid_spec = pltpu.PrefetchScalarGridSpec(
        num_scalar_prefetch=1,
        grid=(k_tiles,),
        in_specs=[pl.BlockSpec((EXPERT_TILE, SORTED_WIDTH), lambda k, p: (k, 0)),
                  wspec(D, D_EXPERT), wspec(D, D_EXPERT), wspec(D_EXPERT, D)],
        out_specs=pl.BlockSpec((EXPERT_TILE, D), lambda k, p: (k, 0)),
        scratch_shapes=[pltpu.VMEM((GROUP_SIZE, D, D_EXPERT), BF16),
                        pltpu.VMEM((GROUP_SIZE, D, D_EXPERT), BF16),
                        pltpu.VMEM((GROUP_SIZE, D_EXPERT, D), BF16)])
    return pl.pallas_call(
        functools.partial(_experts_kernel, group_at=at, used_at=at + k_tiles),
        grid_spec=grid_spec,
        out_shape=jax.ShapeDtypeStruct((n_rows, D), BF16),
        compiler_params=_cparams(1),
        name="moe_experts",
    )(plan, xs, wg, wu, wd)


def _combine_kernel(plan_ref, x_ref, mod_ref, dest_ref, ys_hbm, o_ref, y_l, sem, *, nt):
    i = pl.program_id(0)

    slot = i % 2

    def from_buffer(slot):
        def make(lo, base):
            return (pltpu.make_async_copy(ys_hbm.at[pl.ds(base, SEG_ALIGN)],
                                          y_l.at[slot, pl.ds(lo, SEG_ALIGN)], sem.at[slot]),)
        return make

    @pl.when(i == 0)
    def _():
        y_l[...] = jnp.zeros_like(y_l)
        _start_segments(plan_ref, i, nt, from_buffer(slot))

    @pl.when(i + 1 < nt)
    def _():
        _start_segments(plan_ref, i + 1, nt, from_buffer(1 - slot))

    _wait_segments(plan_ref, i, from_buffer(slot))
    dest = dest_ref[...].astype(jnp.int32)
    lane = _iota(dest.shape, 1)
    back = jnp.concatenate(
        [jnp.where(dest == lane + j * LANE, 1.0, 0.0).astype(BF16) for j in range(DISPATCH_ROWS // LANE)], axis=1)
    y = _dot(back, y_l[slot])
    o_ref[...] = x_ref[...] + mod_ref[0][:, 5 * D:6 * D] * y


def _combine(plan, x1, mod, dest, ys):
    t = x1.shape[0]
    nt = t // TOKEN_TILE
    grid_spec = pltpu.PrefetchScalarGridSpec(
        num_scalar_prefetch=1,
        grid=(nt,),
        in_specs=[pl.BlockSpec((TOKEN_TILE, D), lambda i, p: (i, 0)),
                  _mod_spec(mod, t, TOKEN_TILE),
                  pl.BlockSpec((TOKEN_TILE, LANE), lambda i, p: (i, 0)),
                  pl.BlockSpec(memory_space=pl.ANY)],
        out_specs=pl.BlockSpec((TOKEN_TILE, D), lambda i, p: (i, 0)),
        scratch_shapes=[pltpu.VMEM((2, DISPATCH_ROWS, D), BF16),
                        pltpu.SemaphoreType.DMA((2,))])
    return pl.pallas_call(
        functools.partial(_combine_kernel, nt=nt),
        grid_spec=grid_spec,
        out_shape=jax.ShapeDtypeStruct((t, D), F32),
        compiler_params=_cparams(1),
        name="moe_combine",
    )(plan, x1, mod[0], dest, ys)


def _moe(x1, mod, h2, routed, cnt, wg, wu, wd, layer):
    nt = x1.shape[0] // TOKEN_TILE
    counts = cnt[:, 0, GROUP_LANE:GROUP_LANE + N_GROUPS].reshape(nt * N_GROUPS)
    xs, dest, plan = _dispatch(counts, h2, routed)
    ys = _experts(plan, xs, wg, wu, wd, layer, nt)
    return _combine(plan, x1, mod, dest, ys)


def _rope_angles(pos, half):
    inv = np.power(np.float32(ROPE_THETA), -np.arange(half, dtype=np.float32) / np.float32(half))
    ang = pos.astype(np.float32)[:, None] * inv[None, :]
    return np.cos(ang), np.sin(ang)


def _rope_tables(seq, rot, lead, lane_tiles):
    rows = seq // GRID_W
    row = np.repeat(np.arange(rows), GRID_W)
    col = np.tile(np.arange(GRID_W), rows)
    cr, sr = _rope_angles(row, rot // 4)
    cc, sc = _rope_angles(col, rot // 4)
    cos = np.concatenate([cr, cr, cc, cc], axis=1)
    sin = np.concatenate([-sr, sr, -sc, sc], axis=1)
    width = LANE // lane_tiles
    pad = width - lead - rot
    cos = np.concatenate([np.ones((seq, lead), np.float32), cos, np.ones((seq, pad), np.float32)], axis=1)
    sin = np.concatenate([np.zeros((seq, lead), np.float32), sin, np.zeros((seq, pad), np.float32)], axis=1)
    return np.tile(cos, (1, lane_tiles)), np.tile(sin, (1, lane_tiles))


def _pad_lanes(a, width):
    return jnp.pad(a, [(0, 0)] * (a.ndim - 1) + [(0, width - a.shape[-1])])


def _mla_weights(w_q_b, w_kv_b):
    wq = _pad_lanes(w_q_b.reshape(MLA_Q_LORA, MLA_HEADS, MLA_QK), LANE).reshape(MLA_Q_LORA, MLA_HEADS * LANE)
    kvb = w_kv_b.reshape(MLA_KV_LORA, MLA_HEADS, MLA_NOPE + MLA_V)
    wk = _pad_lanes(kvb[:, :, :MLA_NOPE], LANE).reshape(MLA_KV_LORA, MLA_HEADS * LANE)
    wv = kvb[:, :, MLA_NOPE:].reshape(MLA_KV_LORA, MLA_HEADS * MLA_V)
    return wq.astype(BF16), wk.astype(BF16), wv.astype(BF16)


def _undup_heads(a):
    s = a.shape[:-1]
    return a.reshape(s + (GQA_KV_HEADS, 2, GQA_HEAD_DIM))[..., 0, :]


def _state_to_kernel(h0):
    b = h0.shape[0]
    t = jnp.transpose(h0, (0, 3, 1, 2)).reshape(b, SSD_STATE, SSD_INNER)
    half = SSD_INNER // SSD_GROUPS
    col = jnp.arange(SSD_INNER) // half
    parts = [jnp.where(col == g, t, 0.0) for g in range(SSD_GROUPS)]
    return jnp.concatenate(parts, axis=1)


def _state_from_kernel(st):
    return st.reshape(st.shape[0], SSD_HEADS, SSD_HEADDIM, SSD_STATE)


def kernel(x_prompt, x_sample, c, state_ssd_fwd, state_ssd_bwd, cache_mla_ckv, cache_mla_kpe, cache_gqa_k, cache_gqa_v, c_ctx, g_mix, g_ffn, w_ada, b_ada, w_router, router_bias, w_exp_gate, w_exp_up, w_exp_down, w_in_even, ssd_conv_w, ssd_conv_b, ssd_a_log_fwd, ssd_a_log_bwd, ssd_dt_bias_fwd, ssd_dt_bias_bwd, ssd_d, ssd_norm, mla_q_a_norm, mla_w_q_b, mla_kv_a_norm, mla_w_kv_b, mla_q_norm, mla_k_norm, w_out_even, w_in_odd, gqa_q_norm, gqa_k_norm, w_out_odd):
    nb, ls, _ = x_prompt.shape
    db, dl, _ = x_sample.shape
    depth = w_ada.shape[0]
    n_ctx = nb * ls

    xs = [x_prompt.reshape(n_ctx, D), x_sample.reshape(db * dl, D)]
    seqs = (ls, dl)
    rows = COND_ROWS
    cond = jnp.concatenate([c_ctx[None, :], c, jnp.zeros((rows - 1 - db, D), F32)], axis=0)
    mod_all = _ada_mod(cond, w_ada, b_ada).reshape(depth, rows, 1, 6 * D)
    g_mix3 = g_mix.reshape(depth, 1, D)
    w_in_even_t = jnp.swapaxes(w_in_even, 1, 2)
    g_ffn3 = g_ffn.reshape(depth, 1, D)

    wr = w_router
    rb = router_bias.reshape(N_EXPERTS, 1)
    wg_all, wu_all, wd_all = w_exp_gate, w_exp_up, w_exp_down

    outs = {}
    for i in range(depth):
        j = i // 2
        mods = [(mod_all, i, 0, 1), (mod_all, i, 1, db)]
        if i % 2 == 0:
            wq, wk, wv = _mla_weights(mla_w_q_b[j], mla_w_kv_b[j])
            cw = jnp.pad(ssd_conv_w[j], ((0, 5), (0, 0)))
            cb = ssd_conv_b[j].reshape(1, SSD_CONV_DIM)
            hp = _pad_lanes(jnp.stack([ssd_a_log_fwd[j], ssd_a_log_bwd[j], ssd_dt_bias_fwd[j],
                                       ssd_dt_bias_bwd[j], ssd_d[j]]), LANE)
            hp = jnp.pad(hp, ((0, 3), (0, 0)))
            nw = ssd_norm[j].reshape(1, SSD_INNER)
            qan = mla_q_a_norm[j].reshape(1, MLA_Q_LORA)
            kvn = mla_kv_a_norm[j].reshape(1, MLA_KV_LORA)
            qn = _pad_lanes(mla_q_norm[j].reshape(1, MLA_QK) * (MLA_QK ** -0.5 * LOG2E), LANE)
            kn = _pad_lanes(mla_k_norm[j].reshape(1, MLA_QK), LANE)
            h0 = (_state_to_kernel(state_ssd_fwd[:, j]), _state_to_kernel(state_ssd_bwd[:, j]))
            ckpe = jnp.pad(cache_mla_kpe[:, j], ((0, 0), (0, 0), (MLA_NOPE, LANE - MLA_QK)))
            tables = tuple(jnp.asarray(np.tile(tb, (1, 2))) for tb in _rope_tables(dl, MLA_ROPE, MLA_NOPE, 1))
            w_out = w_out_even
            ys = []
            for s in range(2):
                z, xbc, qa, kv = _inproj(xs[s], mods[s], g_mix3, i, w_in_even_t, j, EVEN_WIDTHS, EVEN_PIECES, transposed=True)
                dt = kv
                if s == 0:
                    y, sf, sb = _ssd(z, xbc, dt, cw, cb, hp, nw, None, seq=seqs[s])
                    o, ckv_new, kpe_new = _mla(qa, kv, qan, kvn, wq, wk, wv, qn, kn, None, None, seq=seqs[s])
                    outs.setdefault("ssd_f", []).append(_state_from_kernel(sf))
                    outs.setdefault("ssd_b", []).append(_state_from_kernel(sb))
                    outs.setdefault("ckv", []).append(ckv_new.reshape(nb, ls, MLA_KV_LORA))
                    outs.setdefault("kpe", []).append(kpe_new[:, MLA_NOPE:MLA_QK].reshape(nb, ls, MLA_ROPE))
                else:
                    y, = _ssd(z, xbc, dt, cw, cb, hp, nw, h0, seq=seqs[s])
                    o, = _mla(qa, kv, qan, kvn, wq, wk, wv, qn, kn, (cache_mla_ckv[:, j], ckpe), tables,
                              seq=seqs[s])
                ys.append((y, o))
        else:
            nq = GQA_HEADS * GQA_HEAD_DIM
            nkv = GQA_KV_HEADS * GQA_HEAD_DIM
            nk = GQA_KV_HEADS * LANE
            qscale = GQA_HEAD_DIM ** -0.5 * LOG2E
            qn = jnp.tile(gqa_q_norm[j].reshape(1, GQA_HEAD_DIM) * qscale, (1, nq // GQA_HEAD_DIM))
            kn = jnp.tile(gqa_k_norm[j].reshape(1, GQA_HEAD_DIM), (1, nk // GQA_HEAD_DIM))
            cache = (cache_gqa_k[:, j].reshape(db, -1, nkv), cache_gqa_v[:, j].reshape(db, -1, nkv))
            tables = tuple(jnp.asarray(np.tile(tb, (1, 2))) for tb in _rope_tables(dl, GQA_HEAD_DIM, 0, 2))
            w_out = w_out_odd
            ys = []
            for s in range(2):
                q, k, v = _inproj(xs[s], mods[s], g_mix3, i, w_in_odd, j, ODD_WIDTHS, ODD_PIECES)
                if s == 0:
                    o, k_new = _gqa(q, k, v, qn, kn, None, None, seq=seqs[s])
                    outs.setdefault("gk", []).append(
                        _undup_heads(k_new).reshape(nb, ls, GQA_KV_HEADS, GQA_HEAD_DIM))
                    outs.setdefault("gv", []).append(
                        _undup_heads(v).reshape(nb, ls, GQA_KV_HEADS, GQA_HEAD_DIM))
                else:
                    o, = _gqa(q, k, v, qn, kn, cache, tables, seq=seqs[s])
                ys.append((o,))
        for s in range(2):
            x1, h2, routed, cnt = _outproj(xs[s], mods[s], ys[s], w_out, j, g_ffn3, i, wr, rb)
            xs[s] = _moe(x1, mods[s], h2, routed, cnt, wg_all, wu_all, wd_all, i)

    stack = lambda key: jnp.stack(outs[key], axis=1)
    return (xs[0].reshape(nb, ls, D), xs[1].reshape(db, dl, D),
            stack("ssd_f"), stack("ssd_b"), stack("ckv"), stack("kpe"), stack("gk"), stack("gv"))
```

```python
import functools

import numpy as np

import jax
import jax.numpy as jnp
from jax import lax
from jax.experimental import pallas as pl
from jax.experimental.pallas import tpu as pltpu

F32 = jnp.float32
BF16 = jnp.bfloat16

D = 1024
EPS = 1e-6
GRID_W = 64
ROPE_THETA = 10000.0
SSD_HEADS = 16
SSD_HEADDIM = 64
SSD_INNER = 1024
SSD_STATE = 64
SSD_GROUPS = 2
SSD_CONV_DIM = 1280
SSD_CHUNK = 128
MLA_HEADS = 8
MLA_Q_LORA = 256
MLA_KV_LORA = 128
MLA_NOPE = 64
MLA_ROPE = 32
MLA_V = 64
MLA_QK = 96
GQA_HEADS = 16
GQA_KV_HEADS = 4
GQA_HEAD_DIM = 64
N_EXPERTS = 16
N_GROUPS = 4
GROUP_SIZE = 4
D_EXPERT = 256

LANE = 128
Q_BLOCK = 128
LAT_Q_ROWS = 256
LOG2E = 1.4426950408889634
TOKEN_TILE = 512
EXPERT_TILE = 512
SEG_ALIGN = 16
GROUP_LANE = 16
DISPATCH_ROWS = TOKEN_TILE + LANE
SORTED_WIDTH = D + LANE
ADA_TILE = 1536
COND_ROWS = 16
VMEM_LIMIT = 56 * 1024 * 1024


def _cparams(n_axes):
    return pltpu.CompilerParams(dimension_semantics=("arbitrary",) * n_axes,
                                vmem_limit_bytes=VMEM_LIMIT)


def _dot(a, b):
    return jnp.dot(a, b, preferred_element_type=F32)


def _dot_nt(a, b):
    return lax.dot_general(a, b, (((1,), (1,)), ((), ())), preferred_element_type=F32)


def _split3(x):
    hi = x.astype(BF16)
    r1 = x - hi.astype(F32)
    mid = r1.astype(BF16)
    lo = (r1 - mid.astype(F32)).astype(BF16)
    return hi, mid, lo


def _dot_x3(x, e):
    hi, mid, lo = _split3(x)
    return _dot(hi, e) + _dot(mid, e) + _dot(lo, e)


def _dot_x2(x, e):
    hi = x.astype(BF16)
    mid = (x - hi.astype(F32)).astype(BF16)
    return _dot(hi, e) + _dot(mid, e)


def _dot_3x(e, x):
    hi, mid, lo = _split3(x)
    return _dot(e, hi) + _dot(e, mid) + _dot(e, lo)


def _dot_f32(a, b):
    a0, a1, a2 = _split3(a)
    b0, b1, b2 = _split3(b)
    return (_dot(a0, b0) + (_dot(a0, b1) + _dot(a1, b0))
            + (_dot(a1, b1) + _dot(a0, b2) + _dot(a2, b0)))


def _sigmoid(x):
    return 1.0 / (1.0 + jnp.exp(-x))


def _silu(x):
    return x * _sigmoid(x)


def _iota(shape, axis):
    return lax.broadcasted_iota(jnp.int32, shape, axis)


def _ada_kernel(c_ref, w_ref, b_ref, o_ref):
    a = _silu(c_ref[...]).astype(BF16)
    o_ref[0] = _dot(a, w_ref[0].astype(BF16)) + b_ref[0]


def _ada_mod(cond, w_ada, b_ada):
    depth, _, n = w_ada.shape
    rows = cond.shape[0]
    return pl.pallas_call(
        _ada_kernel,
        grid=(depth, n // ADA_TILE),
        in_specs=[pl.BlockSpec((rows, D), lambda l, j: (0, 0)),
                  pl.BlockSpec((1, D, ADA_TILE), lambda l, j: (l, 0, j)),
                  pl.BlockSpec((1, 1, ADA_TILE), lambda l, j: (l, 0, j))],
        out_specs=pl.BlockSpec((1, rows, ADA_TILE), lambda l, j: (l, 0, j)),
        out_shape=jax.ShapeDtypeStruct((depth, rows, n), F32),
        compiler_params=_cparams(2),
        name="ada_mod",
    )(cond, w_ada, b_ada.reshape(depth, 1, n))


def _mod_spec(mod, n_tokens, tile):
    _, layer, row0, n_rows = mod
    per_row = n_tokens // n_rows
    return pl.BlockSpec((None, 1, 1, 6 * D), lambda i, *_: (layer, row0 + (i * tile) // per_row, 0, 0))


def _layer_spec(a, layer):
    return pl.BlockSpec((None,) + a.shape[1:], lambda i, *_: (layer, 0, 0))


def _norm_mod(x, g, shift, scale):
    ms = jnp.mean(x * x, axis=-1, keepdims=True)
    y = x * lax.rsqrt(ms + EPS) * g
    return y * (1.0 + scale) + shift


def _inproj_kernel(x_ref, mod_ref, g_ref, w_ref, *refs, widths, pieces, transposed):
    o_refs, wb = refs[:-1], refs[-1]

    @pl.when(pl.program_id(0) == 0)
    def _():
        wb[...] = jnp.zeros_like(wb)
        for src, width, dst in pieces:
            if transposed:
                wb[dst:dst + width, :] = w_ref[src:src + width, :].astype(BF16)
            else:
                wb[:, dst:dst + width] = w_ref[:, src:src + width].astype(BF16)

    m = mod_ref[0]
    h = _norm_mod(x_ref[...], g_ref[...], m[:, 0:D], m[:, D:2 * D]).astype(BF16)
    off = 0
    for o_ref, wd in zip(o_refs, widths):
        if transposed:
            o_ref[...] = _dot_nt(h, wb[off:off + wd, :]).astype(o_ref.dtype)
        else:
            o_ref[...] = _dot(h, wb[:, off:off + wd]).astype(o_ref.dtype)
        off += wd


def _inproj(x, mod, g, layer, w, j, widths, pieces, transposed=False):
    t = x.shape[0]
    n_out = sum(widths)
    return pl.pallas_call(
        functools.partial(_inproj_kernel, widths=widths, pieces=pieces, transposed=transposed),
        grid=(t // TOKEN_TILE,),
        in_specs=[pl.BlockSpec((TOKEN_TILE, D), lambda i: (i, 0)),
                  _mod_spec(mod, t, TOKEN_TILE),
                  _layer_spec(g, layer),
                  pl.BlockSpec((None,) + w.shape[1:], lambda i: (j, 0, 0))],
        out_specs=[pl.BlockSpec((TOKEN_TILE, wd), lambda i: (i, 0)) for wd in widths],
        out_shape=[jax.ShapeDtypeStruct((t, wd), F32) for wd in widths],
        scratch_shapes=[pltpu.VMEM((n_out, D) if transposed else (D, n_out), BF16)],
        compiler_params=_cparams(1),
        name="inproj",
    )(x, mod[0], g, w)


EVEN_WIDTHS = (SSD_INNER, SSD_CONV_DIM, MLA_Q_LORA, 2 * LANE)
_XBC_END = SSD_INNER + SSD_CONV_DIM
_QA0 = _XBC_END + SSD_HEADS
_KV0 = _QA0 + MLA_Q_LORA
EVEN_PIECES = ((0, _XBC_END, 0),
               (_QA0, MLA_Q_LORA, _XBC_END),
               (_KV0, MLA_KV_LORA, _XBC_END + MLA_Q_LORA),
               (_KV0 + MLA_KV_LORA, MLA_ROPE, _XBC_END + MLA_Q_LORA + LANE + MLA_NOPE),
               (_XBC_END, SSD_HEADS, _XBC_END + MLA_Q_LORA + LANE))
_NQ = GQA_HEADS * GQA_HEAD_DIM
_NKV = GQA_KV_HEADS * GQA_HEAD_DIM
ODD_WIDTHS = (_NQ, GQA_KV_HEADS * LANE, GQA_KV_HEADS * LANE)
ODD_PIECES = ((0, _NQ, 0),) + tuple(
    (_NQ + part * _NKV + h * GQA_HEAD_DIM, GQA_HEAD_DIM,
     _NQ + part * GQA_KV_HEADS * LANE + h * LANE + rep * GQA_HEAD_DIM)
    for part in range(2) for h in range(GQA_KV_HEADS) for rep in range(2))


def _ssd_kernel(*refs, seq, has_h0, want_state):
    it = iter(refs)
    z_ref, xbc_ref, dt_ref, cw_ref, cb_ref, hp_ref, nw_ref = [next(it) for _ in range(7)]
    h0_refs = (next(it), next(it)) if has_h0 else None
    y_ref = next(it)
    s_refs = (next(it), next(it)) if want_state else None
    xs_ref, bc_ref, yacc_ref, st_ref = [next(it) for _ in range(4)]

    q = SSD_CHUNK
    nc = seq // q
    unrolled = nc <= 2
    gn = SSD_GROUPS * SSD_STATE

    cw = cw_ref[...]
    cb = cb_ref[...]
    rid = _iota((q, SSD_CONV_DIM), 0)

    def rows(c):
        return pl.ds(c * q, q) if isinstance(c, int) else pl.ds(pl.multiple_of(c * q, q), q)

    def conv_chunk(c):
        xc = xbc_ref[rows(c), :]
        if isinstance(c, int):
            prev = xbc_ref[c * q - 1:c * q, :] if c > 0 else jnp.zeros((1, SSD_CONV_DIM), F32)
            nxt = xbc_ref[(c + 1) * q:(c + 1) * q + 1, :] if c < nc - 1 else jnp.zeros((1, SSD_CONV_DIM), F32)
        else:
            prev = xbc_ref[pl.ds(jnp.maximum(c * q - 1, 0), 1), :]
            nxt = xbc_ref[pl.ds(jnp.minimum(c * q + q, seq - 1), 1), :]
            prev = jnp.where(c > 0, prev, 0.0)
            nxt = jnp.where(c < nc - 1, nxt, 0.0)
        xp = jnp.where(rid == 0, prev, pltpu.roll(xc, 1, axis=0))
        xn = jnp.where(rid == q - 1, nxt, pltpu.roll(xc, q - 1, axis=0))
        v = _silu(xp * cw[0:1] + xc * cw[1:2] + xn * cw[2:3] + cb)
        xs_ref[rows(c), :] = v[:, :SSD_INNER]
        bc_ref[rows(c), :] = v[:, SSD_INNER:]

    hp = hp_ref[...]
    expand = (_iota((LANE, SSD_INNER), 1) // SSD_HEADDIM == _iota((LANE, SSD_INNER), 0)).astype(BF16)
    blockmask = (_iota((gn, SSD_INNER), 0) // SSD_STATE
                 == _iota((gn, SSD_INNER), 1) // (SSD_INNER // SSD_GROUPS))
    ri = _iota((q, q), 0)
    ci = _iota((q, q), 1)
    lane = _iota((q, LANE), 1)
    lo_half = lane < SSD_STATE
    d_e = _dot_x3(jnp.broadcast_to(hp[4:5], (8, LANE)), expand)[0:1]
    nw = nw_ref[...]

    def scan_chunk(c, fwd, a, bias, causal, tri):
        xs = xs_ref[rows(c), :]
        bc = bc_ref[rows(c), :]
        bm = bc[:, :gn]
        cm = bc[:, gn:]
        raw = dt_ref[rows(c), :] + bias
        dt = jnp.maximum(raw, 0.0) + jnp.log1p(jnp.exp(-jnp.abs(raw)))
        cum = _dot_3x(tri, dt * a)
        cum_t = cum.T
        dt_t = dt.T
        last = cum[q - 1:q] if fwd else cum[0:1]
        ecum = jnp.exp(cum)
        ex = _dot_x2(jnp.concatenate([ecum, dt * jnp.exp(last - cum)], axis=0), expand)
        ecum_e, w_e = ex[0:q], ex[q:2 * q]
        cdec_e = ecum_e[q - 1:q] if fwd else ecum_e[0:1]
        st = st_ref[...]
        cmb = cm.astype(BF16)
        bmb = bm.astype(BF16)
        y = _dot(cmb, st.astype(BF16)) * ecum_e
        cb0 = _dot_nt(jnp.where(lo_half, cm, 0.0).astype(BF16), bmb)
        cb1 = _dot_nt(jnp.where(lo_half, 0.0, cm).astype(BF16), bmb)
        pairs = []
        for j in range(SSD_HEADS // 2):
            cbg = cb0 if j < SSD_HEADS // 4 else cb1
            xp = xs[:, j * LANE:(j + 1) * LANE]
            scs = []
            for half in range(2):
                h = 2 * j + half
                seg = cum[:, h:h + 1] - cum_t[h:h + 1, :]
                dec = jnp.exp(jnp.where(causal, seg, -jnp.inf))
                scs.append((cbg * dec * dt_t[h:h + 1, :]).astype(BF16))
            rhs = jnp.concatenate([jnp.where(lo_half, xp, 0.0), jnp.where(lo_half, 0.0, xp)], axis=0)
            pairs.append(_dot(jnp.concatenate(scs, axis=1), rhs.astype(BF16)))
        y = y + jnp.concatenate(pairs, axis=1)
        new = _dot(bm.T.astype(BF16), (xs * w_e).astype(BF16))
        st_ref[...] = jnp.where(blockmask, st * cdec_e + new, 0.0)
        if fwd:
            yacc_ref[rows(c), :] = y
        else:
            y = y + yacc_ref[rows(c), :] + xs * d_e
            y = y * _silu(z_ref[rows(c), :])
            ms = jnp.mean(y * y, axis=-1, keepdims=True)
            y_ref[rows(c), :] = (y * lax.rsqrt(ms + EPS) * nw).astype(y_ref.dtype)

    def scan_pass(fwd):
        a = -jnp.exp(hp[0:1] if fwd else hp[1:2])
        bias = hp[2:3] if fwd else hp[3:4]
        causal = (ri >= ci) if fwd else (ri <= ci)
        tri = causal.astype(BF16)
        if has_h0:
            st_ref[...] = h0_refs[0 if fwd else 1][0]
        else:
            st_ref[...] = jnp.zeros((gn, SSD_INNER), F32)
        if unrolled:
            for k in range(nc):
                scan_chunk(k if fwd else nc - 1 - k, fwd, a, bias, causal, tri)
        else:
            def body(k, carry):
                c = k if fwd else nc - 1 - k
                scan_chunk(c, fwd, a, bias, causal, tri)
                if fwd:
                    conv_chunk(jnp.minimum(c + 1, nc - 1))
                return carry

            lax.fori_loop(0, nc, body, 0)
        if want_state:
            for j in range(SSD_INNER // LANE):
                g = (j * LANE) // (SSD_INNER // SSD_GROUPS)
                t = st_ref[:, j * LANE:(j + 1) * LANE].T
                s_refs[0 if fwd else 1][0, j * LANE:(j + 1) * LANE, :] = t[:, g * SSD_STATE:(g + 1) * SSD_STATE]

    if unrolled:
        for c in range(nc):
            conv_chunk(c)
    else:
        conv_chunk(0)
    scan_pass(True)
    scan_pass(False)


def _ssd(z, xbc, dt, cw, cb, hp, nw, h0, *, seq):
    t = z.shape[0]
    n_seq = t // seq
    has_h0 = h0 is not None
    want_state = not has_h0
    gn = SSD_GROUPS * SSD_STATE
    tok = lambda w: pl.BlockSpec((seq, w), lambda b: (b, 0))
    full = lambda a: pl.BlockSpec(a.shape, lambda b: (0,) * a.ndim)
    dt_spec = pl.BlockSpec((seq, LANE), lambda b: (b, 1))
    in_specs = [tok(SSD_INNER), tok(SSD_CONV_DIM), dt_spec, full(cw), full(cb), full(hp), full(nw)]
    args = [z, xbc, dt, cw, cb, hp, nw]
    if has_h0:
        in_specs += [pl.BlockSpec((1, gn, SSD_INNER), lambda b: (b, 0, 0))] * 2
        args += list(h0)
    out_shape = [jax.ShapeDtypeStruct((t, SSD_INNER), BF16)]
    out_specs = [tok(SSD_INNER)]
    if want_state:
        out_shape += [jax.ShapeDtypeStruct((n_seq, SSD_INNER, SSD_STATE), F32)] * 2
        out_specs += [pl.BlockSpec((1, SSD_INNER, SSD_STATE), lambda b: (b, 0, 0))] * 2

    return pl.pallas_call(
        functools.partial(_ssd_kernel, seq=seq, has_h0=has_h0, want_state=want_state),
        grid=(n_seq,),
        in_specs=in_specs,
        out_specs=out_specs,
        out_shape=out_shape,
        scratch_shapes=[pltpu.VMEM((seq, SSD_INNER), F32),
                        pltpu.VMEM((seq, 2 * gn), F32),
                        pltpu.VMEM((seq, SSD_INNER), F32),
                        pltpu.VMEM((gn, SSD_INNER), F32)],
        compiler_params=_cparams(1),
        name="ssd_lat" if has_h0 else "ssd_ctx",
    )(*args)


def _softmax_pv(ss, vals):
    m = ss[0].max(axis=-1, keepdims=True)
    for s in ss[1:]:
        m = jnp.maximum(m, s.max(axis=-1, keepdims=True))
    ps = [jnp.exp2(s - m) for s in ss]
    den = ps[0].sum(axis=-1, keepdims=True)
    for p in ps[1:]:
        den = den + p.sum(axis=-1, keepdims=True)
    o = _dot(ps[0].astype(BF16), vals[0]())
    for p, v in zip(ps[1:], vals[1:]):
        o = o + _dot(p.astype(BF16), v())
    return o / den


def _attend_blocks(seq, units, emit, s_scr, qrows):
    nblk = seq // qrows

    def scores(u, blk, static):
        q_ref, rows, keys, _ = units[u]
        r = blk * rows if static else pl.multiple_of(blk * rows, rows)
        qb = q_ref[pl.ds(r, rows), :]
        return [_dot_nt(qb, k()) for k in keys]

    if s_scr is None:
        for i in range(nblk):
            emit(i * qrows, [_softmax_pv(scores(u, i, True), units[u][3]) for u in range(len(units))])
        return

    def put(slot, blk):
        for u in range(len(units)):
            off = 0
            for s in scores(u, blk, isinstance(blk, int)):
                s_scr[slot][u][:, off:off + s.shape[1]] = s
                off += s.shape[1]

    def take(slot, blk):
        outs = []
        for u, (_, _, keys, vals) in enumerate(units):
            off = 0
            ss = []
            for k in keys:
                n = k().shape[0]
                ss.append(s_scr[slot][u][:, off:off + n])
                off += n
            outs.append(_softmax_pv(ss, vals))
        emit(blk * qrows if isinstance(blk, int) else pl.multiple_of(blk * qrows, qrows), outs)

    put(0, 0)

    def body(ii, carry):
        b0 = 2 * ii
        put(1, b0 + 1)
        take(0, b0)

        def more():
            put(0, b0 + 2)
            take(1, b0 + 1)

        def last():
            take(1, b0 + 1)

        lax.cond(b0 + 2 < nblk, more, last)
        return carry

    lax.fori_loop(0, nblk // 2, body, 0)


def _mla_kernel(*refs, seq, lat):
    it = iter(refs)
    qa_ref, kv_ref, qan_ref, kvn_ref, wq_ref, wk_ref, wv_ref, qn_ref, kn_ref = [next(it) for _ in range(9)]
    if lat:
        cckv_ref, ckpe_ref, cos_ref, sin_ref = [next(it) for _ in range(4)]
    o_ref = next(it)
    if not lat:
        ckv_out, kpe_out = next(it), next(it)
    q_s, k_s, v_s, qh0_s, qh1_s, kh0_s, kh1_s = [next(it) for _ in range(7)]
    s_scr = None
    if lat:
        kc_s, vc_s, kch0_s, kch1_s = [next(it) for _ in range(4)]
        s_scr = [[next(it), next(it)], [next(it), next(it)]]

    kv = kv_ref[...]
    kpe = jnp.where(_iota((seq, LANE), 1) >= MLA_NOPE, kv[:, LANE:], 0.0)
    ckv_raw = kv[:, :LANE]
    ckv = ckv_raw * lax.rsqrt(jnp.mean(ckv_raw * ckv_raw, axis=-1, keepdims=True) + EPS) * kvn_ref[...]
    qa = qa_ref[...]
    qa = qa * lax.rsqrt(jnp.mean(qa * qa, axis=-1, keepdims=True) + EPS) * qan_ref[...]
    if not lat:
        ckv_out[...] = ckv
        kpe_out[...] = kpe
    ckvb = ckv.astype(BF16)
    q_s[...] = _dot(qa.astype(BF16), wq_ref[...])
    k_s[...] = _dot(ckvb, wk_ref[...])
    v_s[...] = _dot(ckvb, wv_ref[...]).astype(BF16)
    if lat:
        ccb = cckv_ref[0].astype(BF16)
        kc_s[...] = _dot(ccb, wk_ref[...])
        vc_s[...] = _dot(ccb, wv_ref[...]).astype(BF16)
        ckpe = ckpe_ref[0]
        cos = cos_ref[...]
        sin = sin_ref[...]
        dist = MLA_ROPE // 4
        li = _iota((2 * LANE, 2 * LANE), 0)
        lj = _iota((2 * LANE, 2 * LANE), 1)
        swap = (li == jnp.where((lj % (2 * dist)) < dist, lj + dist, lj - dist)).astype(BF16)
    qn = qn_ref[...]
    kn = kn_ref[...]
    qrows = LAT_Q_ROWS if lat else Q_BLOCK
    lo_lanes = _iota((qrows, LANE), 1) < MLA_V

    def head_norm(x, g):
        return x * lax.rsqrt(jnp.sum(x * x, axis=-1, keepdims=True) * (1.0 / MLA_QK) + EPS) * g

    def rotate(x2):
        return x2 * cos + _dot_x2(x2, swap) * sin

    for j in range(MLA_HEADS // 2):
        pair = slice(j * LANE, (j + 1) * LANE)
        sls = [slice((2 * j + half) * LANE, (2 * j + half + 1) * LANE) for half in range(2)]
        q2 = jnp.concatenate([head_norm(q_s[:, sl], qn) for sl in sls], axis=1)
        k2 = jnp.concatenate([head_norm(k_s[:, sl] + kpe, kn) for sl in sls], axis=1)
        if lat:
            q2 = rotate(q2)
            k2 = rotate(k2)
            for half, kch_s in enumerate((kch0_s, kch1_s)):
                kch_s[...] = head_norm(kc_s[:, sls[half]] + ckpe, kn).astype(BF16)
        for half, (qh_s, kh_s) in enumerate(((qh0_s, kh0_s), (qh1_s, kh1_s))):
            qh_s[...] = q2[:, half * LANE:(half + 1) * LANE].astype(BF16)
            kh_s[...] = k2[:, half * LANE:(half + 1) * LANE].astype(BF16)

        def emit(r0, outs, pair=pair):
            o_ref[pl.ds(r0, qrows), pair] = jnp.where(lo_lanes, outs[0], outs[1]).astype(o_ref.dtype)

        units = []
        for half, (qh_s, kh_s) in enumerate(((qh0_s, kh0_s), (qh1_s, kh1_s))):
            keys = [lambda r=kh_s: r[...]]
            vals = [lambda pair=pair: v_s[:, pair]]
            if lat:
                keys = [lambda r=(kch0_s, kch1_s)[half]: r[...]] + keys
                vals = [lambda pair=pair: vc_s[:, pair]] + vals
            units.append((qh_s, qrows, keys, vals))
        _attend_blocks(seq, units, emit, s_scr, qrows)


def _mla(qa, kv, qan, kvn, wq, wk, wv, qn, kn, cache, tables, *, seq):
    t = qa.shape[0]
    n_seq = t // seq
    lat = cache is not None
    hv = MLA_HEADS * MLA_V
    tok = lambda w: pl.BlockSpec((seq, w), lambda b: (b, 0))
    full = lambda a: pl.BlockSpec(a.shape, lambda b: (0,) * a.ndim)
    args = [qa, kv, qan, kvn, wq, wk, wv, qn, kn]
    in_specs = [tok(MLA_Q_LORA), tok(2 * LANE)] + [full(a) for a in args[2:]]
    past = 0
    if lat:
        past = cache[0].shape[1]
        in_specs += [pl.BlockSpec((1, past, LANE), lambda b: (b, 0, 0))] * 2 + [full(tables[0]), full(tables[1])]
        args += [cache[0], cache[1], tables[0], tables[1]]
    out_shape = [jax.ShapeDtypeStruct((t, hv), BF16)]
    out_specs = [tok(hv)]
    if not lat:
        out_shape += [jax.ShapeDtypeStruct((t, LANE), F32)] * 2
        out_specs += [pl.BlockSpec((seq, LANE), lambda b: (b, 0))] * 2
    scratch = [pltpu.VMEM((seq, MLA_HEADS * LANE), F32),
               pltpu.VMEM((seq, MLA_HEADS * LANE), F32),
               pltpu.VMEM((seq, hv), BF16)] + [pltpu.VMEM((seq, LANE), BF16)] * 4
    if lat:
        scratch += [pltpu.VMEM((past, MLA_HEADS * LANE), F32),
                    pltpu.VMEM((past, hv), BF16),
                    pltpu.VMEM((past, LANE), BF16),
                    pltpu.VMEM((past, LANE), BF16)]
        scratch += [pltpu.VMEM((LAT_Q_ROWS, past + seq), F32)] * 4

    return pl.pallas_call(
        functools.partial(_mla_kernel, seq=seq, lat=lat),
        grid=(n_seq,),
        in_specs=in_specs,
        out_specs=out_specs,
        out_shape=out_shape,
        scratch_shapes=scratch,
        compiler_params=_cparams(1),
        name="mla_lat" if lat else "mla_ctx",
    )(*args)


def _gqa_kernel(*refs, seq, lat):
    it = iter(refs)
    q_ref, k_ref, v_ref, qn_ref, kn_ref = [next(it) for _ in range(5)]
    if lat:
        ck_ref, cv_ref, cos_ref, sin_ref = [next(it) for _ in range(4)]
    o_ref = next(it)
    if not lat:
        kn_out = next(it)
    kn_s, vb_s, q2_s = [next(it) for _ in range(3)]
    s_scr = None
    if lat:
        ckb_s, cvb_s = next(it), next(it)
        s_scr = [[next(it)], [next(it)]]

    hd = GQA_HEAD_DIM
    nq = GQA_HEADS * hd
    nk = GQA_KV_HEADS * LANE
    qrows = LAT_Q_ROWS if lat else Q_BLOCK
    nblk = seq // qrows
    wide = 2 * LANE
    li = _iota((wide, wide), 0)
    lj = _iota((wide, wide), 1)
    same_head = (li // hd == lj // hd).astype(BF16)
    if lat:
        dist = hd // 4
        swap = (li == jnp.where((lj % (2 * dist)) < dist, lj + dist, lj - dist)).astype(BF16)
        cos = cos_ref[...]
        sin = sin_ref[...]
    lo_q = _iota((qrows, LANE), 1) < hd

    def prepared(x_ref, g_ref, b):
        sl = slice(b * wide, (b + 1) * wide)
        x = x_ref[:, sl]
        ss = _dot_x2(x * x, same_head)
        xn = x * lax.rsqrt(ss * (1.0 / hd) + EPS) * g_ref[:, sl]
        if not lat:
            return xn, xn
        return xn, xn * cos + _dot_x2(xn, swap) * sin

    for b in range(nk // wide):
        sl = slice(b * wide, (b + 1) * wide)
        kn, kr = prepared(k_ref, kn_ref, b)
        if not lat:
            kn_out[:, sl] = kn
        kn_s[:, sl] = kr.astype(BF16)
    if lat:
        for src_ref, dst in ((ck_ref, ckb_s), (cv_ref, cvb_s)):
            for h in range(GQA_KV_HEADS):
                head = src_ref[0, :, h * hd:(h + 1) * hd].astype(BF16)
                for rep in range(LANE // hd):
                    dst[:, h * LANE + rep * hd:h * LANE + (rep + 1) * hd] = head
    vb_s[...] = v_ref[...].astype(BF16)
    for b in range(nq // wide):
        _, qr = prepared(q_ref, qn_ref, b)
        for half in range(wide // LANE):
            p = b * (wide // LANE) + half
            for i in range(nblk):
                blk = qr[i * qrows:(i + 1) * qrows, half * LANE:(half + 1) * LANE]
                q2_s[p, 2 * i * qrows:(2 * i + 1) * qrows, :] = jnp.where(lo_q, blk, 0.0).astype(BF16)
                q2_s[p, (2 * i + 1) * qrows:(2 * i + 2) * qrows, :] = jnp.where(lo_q, 0.0, blk).astype(BF16)

    for j in range(GQA_HEADS // 2):
        pair = slice(j * LANE, (j + 1) * LANE)
        g = (2 * j) // (GQA_HEADS // GQA_KV_HEADS)
        gs = slice(g * LANE, (g + 1) * LANE)

        def emit(r0, outs, pair=pair):
            o = outs[0]
            o_ref[pl.ds(r0, qrows), pair] = jnp.where(lo_q, o[:qrows], o[qrows:]).astype(o_ref.dtype)

        keys = [lambda gs=gs: kn_s[:, gs]]
        vals = [lambda gs=gs: vb_s[:, gs]]
        if lat:
            keys = [lambda gs=gs: ckb_s[:, gs]] + keys
            vals = [lambda gs=gs: cvb_s[:, gs]] + vals
        _attend_blocks(seq, [(q2_s.at[j], 2 * qrows, keys, vals)], emit, s_scr, qrows)


def _gqa(q, k, v, qn, kn, cache, tables, *, seq):
    t = q.shape[0]
    n_seq = t // seq
    lat = cache is not None
    nq = GQA_HEADS * GQA_HEAD_DIM
    nk = GQA_KV_HEADS * LANE
    tok = lambda w: pl.BlockSpec((seq, w), lambda b: (b, 0))
    full = lambda a: pl.BlockSpec(a.shape, lambda b: (0,) * a.ndim)
    args = [q, k, v, qn, kn]
    in_specs = [tok(nq), tok(nk), tok(nk), full(qn), full(kn)]
    past = 0
    if lat:
        past = cache[0].shape[1]
        nkv = cache[0].shape[2]
        in_specs += [pl.BlockSpec((1, past, nkv), lambda b: (b, 0, 0))] * 2 + [full(tables[0]), full(tables[1])]
        args += [cache[0], cache[1], tables[0], tables[1]]
    out_shape = [jax.ShapeDtypeStruct((t, nq), BF16)]
    out_specs = [tok(nq)]
    if not lat:
        out_shape += [jax.ShapeDtypeStruct((t, nk), F32)]
        out_specs += [pl.BlockSpec((seq, nk), lambda b: (b, 0))]
    scratch = [pltpu.VMEM((seq, nk), BF16),
               pltpu.VMEM((seq, nk), BF16),
               pltpu.VMEM((GQA_HEADS // 2, 2 * seq, LANE), BF16)]
    if lat:
        scratch += [pltpu.VMEM((past, nk), BF16), pltpu.VMEM((past, nk), BF16)]
        scratch += [pltpu.VMEM((2 * LAT_Q_ROWS, past + seq), F32)] * 2

    return pl.pallas_call(
        functools.partial(_gqa_kernel, seq=seq, lat=lat),
        grid=(n_seq,),
        in_specs=in_specs,
        out_specs=out_specs,
        out_shape=out_shape,
        scratch_shapes=scratch,
        compiler_params=_cparams(1),
        name="gqa_lat" if lat else "gqa_ctx",
    )(*args)


def _route(sel, sc):
    gs = []
    for g in range(N_GROUPS):
        r = sel[g * GROUP_SIZE:(g + 1) * GROUP_SIZE]
        best = None
        for i in range(GROUP_SIZE):
            for j in range(i + 1, GROUP_SIZE):
                s = r[i] + r[j]
                best = s if best is None else jnp.maximum(best, s)
        gs.append(best)
    cur = gs[0]
    grp = jnp.zeros_like(cur, dtype=jnp.int32)
    for g in range(1, N_GROUPS):
        upd = gs[g] > cur
        grp = jnp.where(upd, g, grp)
        cur = jnp.where(upd, gs[g], cur)

    def pick(rows, i):
        v = rows[i]
        for g in range(1, N_GROUPS):
            v = jnp.where(grp == g, rows[g * GROUP_SIZE + i], v)
        return v

    cand = [pick(sel, i) for i in range(GROUP_SIZE)]
    csc = [pick(sc, i) for i in range(GROUP_SIZE)]
    v1, i1, w1 = cand[0], jnp.zeros_like(grp), csc[0]
    for i in range(1, GROUP_SIZE):
        upd = cand[i] > v1
        v1 = jnp.where(upd, cand[i], v1)
        i1 = jnp.where(upd, i, i1)
        w1 = jnp.where(upd, csc[i], w1)
    v2 = i2 = w2 = None
    for i in range(GROUP_SIZE):
        ok = i1 != i
        if v2 is None:
            v2 = jnp.where(ok, cand[i], -jnp.inf)
            i2 = jnp.zeros_like(grp)
            w2 = csc[i]
        else:
            upd = ok & (cand[i] > v2)
            v2 = jnp.where(upd, cand[i], v2)
            i2 = jnp.where(upd, i, i2)
            w2 = jnp.where(upd, csc[i], w2)
    tot = w1 + w2
    local = [jnp.where(i1 == i, w1 / tot, 0.0) + jnp.where(i2 == i, w2 / tot, 0.0)
             for i in range(GROUP_SIZE)]
    onehot = [jnp.where(grp == g, 1.0, 0.0) for g in range(N_GROUPS)]
    return local, onehot


def _outproj_kernel(*refs, n_y):
    x_ref, mod_ref = refs[0], refs[1]
    y_refs = refs[2:2 + n_y]
    w_ref, g_ref, wr_ref, rb_ref, x1_ref, h2_ref, gate_ref, cnt_ref, w_b, wr_b = refs[2 + n_y:]

    @pl.when(pl.program_id(0) == 0)
    def _():
        w_b[...] = w_ref[...].astype(BF16)
        wr_b[...] = jnp.zeros_like(wr_b)
        for j, term in enumerate(_split3(wr_ref[...])):
            wr_b[:, j * N_EXPERTS:(j + 1) * N_EXPERTS] = term

    m = mod_ref[0]
    off = 0
    out = None
    for y_ref in y_refs:
        k = y_ref.shape[1]
        part = _dot(y_ref[...], w_b[off:off + k, :])
        out = part if out is None else out + part
        off += k
    x1 = x_ref[...] + m[:, 2 * D:3 * D] * out
    x1_ref[...] = x1
    h2 = _norm_mod(x1, g_ref[...], m[:, 3 * D:4 * D], m[:, 4 * D:5 * D])
    h2_hi = h2.astype(BF16)
    h2_ref[...] = h2_hi
    h2_mid = (h2 - h2_hi.astype(F32)).astype(BF16)
    half = h2.shape[0] // 2
    parts = jnp.concatenate([_dot(h2_hi[r], wr_b[...]) + _dot(h2_mid[r], wr_b[...])
                             for r in (slice(0, half), slice(half, None))], axis=0)
    logits = (parts + pltpu.roll(parts, LANE - N_EXPERTS, axis=1)
              + pltpu.roll(parts, LANE - 2 * N_EXPERTS, axis=1))
    sc_t = _sigmoid(logits.T[0:N_EXPERTS, :])
    sel_t = sc_t + rb_ref[...]
    local, onehot = _route([sel_t[e:e + 1] for e in range(N_EXPERTS)],
                           [sc_t[e:e + 1] for e in range(N_EXPERTS)])
    t = x1.shape[0]
    rows_t = (local + [jnp.zeros((GROUP_LANE - GROUP_SIZE, t), F32)] + onehot
              + [jnp.zeros((LANE - GROUP_LANE - N_GROUPS, t), F32)])
    routed = jnp.concatenate(rows_t, axis=0).T
    gate_ref[...] = routed
    cnt = jnp.sum(routed, axis=0, keepdims=True).astype(jnp.int32)
    cnt_ref[0] = jnp.broadcast_to(cnt, (8, LANE))


def _outproj(x, mod, ys, w, j, g, layer, wr, rb):
    t = x.shape[0]
    full = lambda a: pl.BlockSpec(a.shape, lambda i: (0,) * a.ndim)
    return pl.pallas_call(
        functools.partial(_outproj_kernel, n_y=len(ys)),
        grid=(t // TOKEN_TILE,),
        in_specs=[pl.BlockSpec((TOKEN_TILE, D), lambda i: (i, 0)),
                  _mod_spec(mod, t, TOKEN_TILE)]
                 + [pl.BlockSpec((TOKEN_TILE, y.shape[1]), lambda i: (i, 0)) for y in ys]
                 + [pl.BlockSpec((None,) + w.shape[1:], lambda i: (j, 0, 0)), _layer_spec(g, layer),
                    full(wr), full(rb)],
        out_specs=[pl.BlockSpec((TOKEN_TILE, D), lambda i: (i, 0)),
                   pl.BlockSpec((TOKEN_TILE, D), lambda i: (i, 0)),
                   pl.BlockSpec((TOKEN_TILE, LANE), lambda i: (i, 0)),
                   pl.BlockSpec((1, 8, LANE), lambda i: (i, 0, 0))],
        out_shape=[jax.ShapeDtypeStruct((t, D), F32),
                   jax.ShapeDtypeStruct((t, D), BF16),
                   jax.ShapeDtypeStruct((t, LANE), F32),
                   jax.ShapeDtypeStruct((t // TOKEN_TILE, 8, LANE), jnp.int32)],
        scratch_shapes=[pltpu.VMEM(w.shape[1:], BF16), pltpu.VMEM((D, LANE), BF16)],
        compiler_params=_cparams(1),
        name="outproj_router",
    )(x, mod[0], *ys, w, g, wr, rb)


def _pad_to(n, shift):
    return ((n + (1 << shift) - 1) >> shift) << shift


SEG_SHIFT = SEG_ALIGN.bit_length() - 1
TILE_SHIFT = EXPERT_TILE.bit_length() - 1


def _start_segments(plan_ref, i, nt, make):
    for g in range(N_GROUPS):
        n = plan_ref[i * N_GROUPS + g]
        lo = plan_ref[(nt + i) * N_GROUPS + g]
        base = plan_ref[(2 * nt + i) * N_GROUPS + g]

        def body(c, carry, lo=lo, base=base, g=g):
            for cp in make(pl.multiple_of(lo + c * SEG_ALIGN, SEG_ALIGN),
                           pl.multiple_of(base + c * SEG_ALIGN, SEG_ALIGN)):
                cp.start(priority=g % 2)
            return carry

        lax.fori_loop(0, (n + SEG_ALIGN - 1) >> SEG_SHIFT, body, 0)


def _wait_segments(plan_ref, i, make):
    total = jnp.int32(0)
    for g in range(N_GROUPS):
        total = total + ((plan_ref[i * N_GROUPS + g] + SEG_ALIGN - 1) >> SEG_SHIFT)

    def body(c, carry):
        for cp in make(0, 0):
            cp.wait()
        return carry

    lax.fori_loop(0, total, body, 0)


def _dispatch_kernel(cnt_ref, h_ref, r_ref, xs_hbm, dest_ref, plan_ref,
                     xs_l, zx, tril_s, fill_s, sem, *, nt, n_rows):
    i = pl.program_id(0)
    t = TOKEN_TILE
    k_tiles = n_rows // EXPERT_TILE

    @pl.when(i == 0)
    def _():
        zx[...] = jnp.zeros_like(zx)
        tril_s[...] = (_iota((t, t), 0) >= _iota((t, t), 1)).astype(BF16)
        tots = []
        for g in range(N_GROUPS):
            tots.append(lax.fori_loop(
                0, nt, lambda tt, acc, g=g: acc + _pad_to(cnt_ref[tt * N_GROUPS + g], SEG_SHIFT), jnp.int32(0)))
        starts = []
        cur = jnp.int32(0)
        for g in range(N_GROUPS):
            starts.append(cur)
            fill_s[2 * g] = cur + tots[g]
            cur = cur + _pad_to(tots[g], TILE_SHIFT)
            fill_s[2 * g + 1] = cur
        fill_s[2 * N_GROUPS] = cur
        fill_s[2 * N_GROUPS + 1] = n_rows

        def per_tile(tt, run):
            lo = jnp.int32(0)
            new = []
            for g in range(N_GROUPS):
                n = cnt_ref[tt * N_GROUPS + g]
                plan_ref[tt * N_GROUPS + g] = n
                plan_ref[(nt + tt) * N_GROUPS + g] = lo
                plan_ref[(2 * nt + tt) * N_GROUPS + g] = run[g]
                seg = _pad_to(n, SEG_SHIFT)
                lo = lo + seg
                new.append(run[g] + seg)
            return tuple(new)

        lax.fori_loop(0, nt, per_tile, tuple(starts))
        for k in range(k_tiles):
            row = k * EXPERT_TILE
            grp = 0
            for g in range(1, N_GROUPS):
                grp = grp + (row >= starts[g]).astype(jnp.int32)
            plan_ref[3 * nt * N_GROUPS + k] = grp
        plan_ref[3 * nt * N_GROUPS + k_tiles] = cur >> TILE_SHIFT

    routed = r_ref[...]
    lane = _iota((t, LANE), 1)
    is_grp = (lane >= GROUP_LANE) & (lane < GROUP_LANE + N_GROUPS)
    onehot = jnp.where(is_grp, routed, 0.0)
    rank = _dot(tril_s[...], onehot.astype(BF16))
    seg_lo = jnp.zeros((1, LANE), F32)
    lane1 = _iota((1, LANE), 1)
    for g in range(N_GROUPS):
        lo = plan_ref[(nt + i) * N_GROUPS + g]
        seg_lo = jnp.where(lane1 == GROUP_LANE + g, (lo - 1).astype(F32), seg_lo)
    val = jnp.where(onehot > 0.0, rank + seg_lo, 0.0)
    dest = _dot_x2(val, jnp.ones((LANE, LANE), BF16))
    dest_ref[...] = dest
    dest_row = dest.T[0:1, :].astype(jnp.int32)
    perm = jnp.where(_iota((DISPATCH_ROWS, t), 0) == dest_row, 1.0, 0.0).astype(BF16)
    r_hi, r_mid, r_lo = _split3(routed)
    packed = (r_hi.astype(F32) + pltpu.roll(r_mid.astype(F32), GROUP_SIZE, axis=1)
              + pltpu.roll(r_lo.astype(F32), 2 * GROUP_SIZE, axis=1)).astype(BF16)
    slot = i % 2
    xs_l[slot] = _dot(perm, jnp.concatenate([h_ref[...], packed], axis=1)).astype(BF16)

    def to_buffer(slot):
        def make(lo, base):
            return (pltpu.make_async_copy(xs_l.at[slot, pl.ds(lo, SEG_ALIGN)],
                                          xs_hbm.at[pl.ds(base, SEG_ALIGN)], sem.at[slot]),)
        return make

    _start_segments(plan_ref, i, nt, to_buffer(slot))

    @pl.when(i > 0)
    def _():
        _wait_segments(plan_ref, i - 1, to_buffer(1 - slot))

    @pl.when(i == nt - 1)
    def _():
        _wait_segments(plan_ref, i, to_buffer(slot))
        def zero(base):
            return (pltpu.make_async_copy(zx, xs_hbm.at[pl.ds(base, SEG_ALIGN)], sem.at[0]),)

        n_fill = jnp.int32(0)
        for r in range(N_GROUPS + 1):
            lo = fill_s[2 * r]
            chunks = (fill_s[2 * r + 1] - lo) >> SEG_SHIFT

            def body(c, carry, lo=lo):
                for cp in zero(pl.multiple_of(lo + c * SEG_ALIGN, SEG_ALIGN)):
                    cp.start()
                return carry

            lax.fori_loop(0, chunks, body, 0)
            n_fill = n_fill + chunks

        def wait_fill(c, carry):
            for cp in zero(0):
                cp.wait()
            return carry

        lax.fori_loop(0, n_fill, wait_fill, 0)


def _moe_rows(n_tokens):
    nt = n_tokens // TOKEN_TILE
    worst = n_tokens + nt * N_GROUPS * (SEG_ALIGN - 1) + N_GROUPS * (EXPERT_TILE - SEG_ALIGN)
    return _pad_to(worst, TILE_SHIFT)


def _dispatch(counts, h2, routed):
    t = h2.shape[0]
    nt = t // TOKEN_TILE
    n_rows = _moe_rows(t)
    plan_len = 3 * nt * N_GROUPS + n_rows // EXPERT_TILE + 1
    grid_spec = pltpu.PrefetchScalarGridSpec(
        num_scalar_prefetch=1,
        grid=(nt,),
        in_specs=[pl.BlockSpec((TOKEN_TILE, D), lambda i, c: (i, 0)),
                  pl.BlockSpec((TOKEN_TILE, LANE), lambda i, c: (i, 0))],
        out_specs=[pl.BlockSpec(memory_space=pl.ANY),
                   pl.BlockSpec((TOKEN_TILE, LANE), lambda i, c: (i, 0)),
                   pl.BlockSpec(memory_space=pltpu.SMEM)],
        scratch_shapes=[pltpu.VMEM((2, DISPATCH_ROWS, SORTED_WIDTH), BF16),
                        pltpu.VMEM((SEG_ALIGN, SORTED_WIDTH), BF16),
                        pltpu.VMEM((TOKEN_TILE, TOKEN_TILE), BF16),
                        pltpu.SMEM((2 * N_GROUPS + 2,), jnp.int32),
                        pltpu.SemaphoreType.DMA((2,))])
    return pl.pallas_call(
        functools.partial(_dispatch_kernel, nt=nt, n_rows=n_rows),
        grid_spec=grid_spec,
        out_shape=[jax.ShapeDtypeStruct((n_rows, SORTED_WIDTH), BF16),
                   jax.ShapeDtypeStruct((t, LANE), F32),
                   jax.ShapeDtypeStruct((plan_len,), jnp.int32)],
        compiler_params=_cparams(1),
        name="moe_dispatch",
    )(counts, h2, routed)


def _experts_kernel(plan_ref, xs_ref, wg_ref, wu_ref, wd_ref, y_ref, wg_b, wu_b, wd_b,
                    *, group_at, used_at):
    k = pl.program_id(0)
    used = plan_ref[used_at]
    group = plan_ref[group_at + k]
    prev = plan_ref[group_at + jnp.maximum(k - 1, 0)]

    @pl.when((k == 0) | (group != prev))
    def _():
        wg_b[...] = wg_ref[...].astype(BF16)
        wu_b[...] = wu_ref[...].astype(BF16)
        wd_b[...] = wd_ref[...].astype(BF16)

    @pl.when(k < used)
    def _():
        rows = xs_ref[:, :D]
        terms = xs_ref[:, D:].astype(F32)
        gates = (terms + pltpu.roll(terms, LANE - GROUP_SIZE, axis=1)
                 + pltpu.roll(terms, LANE - 2 * GROUP_SIZE, axis=1))
        acts = []
        for e in range(GROUP_SIZE):
            a = _dot(rows, wg_b[e])
            u = _dot(rows, wu_b[e])
            acts.append((_silu(a) * u * gates[:, e:e + 1]).astype(BF16))
        down = wd_b[...].reshape(GROUP_SIZE * D_EXPERT, D)
        y_ref[...] = _dot(jnp.concatenate(acts, axis=1), down).astype(y_ref.dtype)

    @pl.when(k >= used)
    def _():
        y_ref[...] = jnp.zeros_like(y_ref)


def _experts(plan, xs, wg, wu, wd, layer, nt):
    n_rows = xs.shape[0]
    k_tiles = n_rows // EXPERT_TILE
    at = 3 * nt * N_GROUPS
    wspec = lambda a, b: pl.BlockSpec((None, GROUP_SIZE, a, b), lambda k, p: (layer, p[at + k], 0, 0))
    grid_spec = pltpu.PrefetchScalarGridSpec(
        num_scalar_prefetch=1,
        grid=(k_tiles,),
        in_specs=[pl.BlockSpec((EXPERT_TILE, SORTED_WIDTH), lambda k, p: (k, 0)),
                  wspec(D, D_EXPERT), wspec(D, D_EXPERT), wspec(D_EXPERT, D)],
        out_specs=pl.BlockSpec((EXPERT_TILE, D), lambda k, p: (k, 0)),
        scratch_shapes=[pltpu.VMEM((GROUP_SIZE, D, D_EXPERT), BF16),
                        pltpu.VMEM((GROUP_SIZE, D, D_EXPERT), BF16),
                        pltpu.VMEM((GROUP_SIZE, D_EXPERT, D), BF16)])
    return pl.pallas_call(
        functools.partial(_experts_kernel, group_at=at, used_at=at + k_tiles),
        grid_spec=grid_spec,
        out_shape=jax.ShapeDtypeStruct((n_rows, D), BF16),
        compiler_params=_cparams(1),
        name="moe_experts",
    )(plan, xs, wg, wu, wd)


def _combine_kernel(plan_ref, x_ref, mod_ref, dest_ref, ys_hbm, o_ref, y_l, sem, *, nt):
    i = pl.program_id(0)

    slot = i % 2

    def from_buffer(slot):
        def make(lo, base):
            return (pltpu.make_async_copy(ys_hbm.at[pl.ds(base, SEG_ALIGN)],
                                          y_l.at[slot, pl.ds(lo, SEG_ALIGN)], sem.at[slot]),)
        return make

    @pl.when(i == 0)
    def _():
        y_l[...] = jnp.zeros_like(y_l)
        _start_segments(plan_ref, i, nt, from_buffer(slot))

    @pl.when(i + 1 < nt)
    def _():
        _start_segments(plan_ref, i + 1, nt, from_buffer(1 - slot))

    _wait_segments(plan_ref, i, from_buffer(slot))
    dest = dest_ref[...].astype(jnp.int32)
    lane = _iota(dest.shape, 1)
    back = jnp.concatenate(
        [jnp.where(dest == lane + j * LANE, 1.0, 0.0).astype(BF16) for j in range(DISPATCH_ROWS // LANE)], axis=1)
    y = _dot(back, y_l[slot])
    o_ref[...] = x_ref[...] + mod_ref[0][:, 5 * D:6 * D] * y


def _combine(plan, x1, mod, dest, ys):
    t = x1.shape[0]
    nt = t // TOKEN_TILE
    grid_spec = pltpu.PrefetchScalarGridSpec(
        num_scalar_prefetch=1,
        grid=(nt,),
        in_specs=[pl.BlockSpec((TOKEN_TILE, D), lambda i, p: (i, 0)),
                  _mod_spec(mod, t, TOKEN_TILE),
                  pl.BlockSpec((TOKEN_TILE, LANE), lambda i, p: (i, 0)),
                  pl.BlockSpec(memory_space=pl.ANY)],
        out_specs=pl.BlockSpec((TOKEN_TILE, D), lambda i, p: (i, 0)),
        scratch_shapes=[pltpu.VMEM((2, DISPATCH_ROWS, D), BF16),
                        pltpu.SemaphoreType.DMA((2,))])
    return pl.pallas_call(
        functools.partial(_combine_kernel, nt=nt),
        grid_spec=grid_spec,
        out_shape=jax.ShapeDtypeStruct((t, D), F32),
        compiler_params=_cparams(1),
        name="moe_combine",
    )(plan, x1, mod[0], dest, ys)


def _moe(x1, mod, h2, routed, cnt, wg, wu, wd, layer):
    nt = x1.shape[0] // TOKEN_TILE
    counts = cnt[:, 0, GROUP_LANE:GROUP_LANE + N_GROUPS].reshape(nt * N_GROUPS)
    xs, dest, plan = _dispatch(counts, h2, routed)
    ys = _experts(plan, xs, wg, wu, wd, layer, nt)
    return _combine(plan, x1, mod, dest, ys)


def _rope_angles(pos, half):
    inv = np.power(np.float32(ROPE_THETA), -np.arange(half, dtype=np.float32) / np.float32(half))
    ang = pos.astype(np.float32)[:, None] * inv[None, :]
    return np.cos(ang), np.sin(ang)


def _rope_tables(seq, rot, lead, lane_tiles):
    rows = seq // GRID_W
    row = np.repeat(np.arange(rows), GRID_W)
    col = np.tile(np.arange(GRID_W), rows)
    cr, sr = _rope_angles(row, rot // 4)
    cc, sc = _rope_angles(col, rot // 4)
    cos = np.concatenate([cr, cr, cc, cc], axis=1)
    sin = np.concatenate([-sr, sr, -sc, sc], axis=1)
    width = LANE // lane_tiles
    pad = width - lead - rot
    cos = np.concatenate([np.ones((seq, lead), np.float32), cos, np.ones((seq, pad), np.float32)], axis=1)
    sin = np.concatenate([np.zeros((seq, lead), np.float32), sin, np.zeros((seq, pad), np.float32)], axis=1)
    return np.tile(cos, (1, lane_tiles)), np.tile(sin, (1, lane_tiles))


def _pad_lanes(a, width):
    return jnp.pad(a, [(0, 0)] * (a.ndim - 1) + [(0, width - a.shape[-1])])


def _mla_weights(w_q_b, w_kv_b):
    wq = _pad_lanes(w_q_b.reshape(MLA_Q_LORA, MLA_HEADS, MLA_QK), LANE).reshape(MLA_Q_LORA, MLA_HEADS * LANE)
    kvb = w_kv_b.reshape(MLA_KV_LORA, MLA_HEADS, MLA_NOPE + MLA_V)
    wk = _pad_lanes(kvb[:, :, :MLA_NOPE], LANE).reshape(MLA_KV_LORA, MLA_HEADS * LANE)
    wv = kvb[:, :, MLA_NOPE:].reshape(MLA_KV_LORA, MLA_HEADS * MLA_V)
    return wq.astype(BF16), wk.astype(BF16), wv.astype(BF16)


def _undup_heads(a):
    s = a.shape[:-1]
    return a.reshape(s + (GQA_KV_HEADS, 2, GQA_HEAD_DIM))[..., 0, :]


def _state_to_kernel(h0):
    b = h0.shape[0]
    t = jnp.transpose(h0, (0, 3, 1, 2)).reshape(b, SSD_STATE, SSD_INNER)
    half = SSD_INNER // SSD_GROUPS
    col = jnp.arange(SSD_INNER) // half
    parts = [jnp.where(col == g, t, 0.0) for g in range(SSD_GROUPS)]
    return jnp.concatenate(parts, axis=1)


def _state_from_kernel(st):
    return st.reshape(st.shape[0], SSD_HEADS, SSD_HEADDIM, SSD_STATE)


def kernel(x_prompt, x_sample, c, state_ssd_fwd, state_ssd_bwd, cache_mla_ckv, cache_mla_kpe, cache_gqa_k, cache_gqa_v, c_ctx, g_mix, g_ffn, w_ada, b_ada, w_router, router_bias, w_exp_gate, w_exp_up, w_exp_down, w_in_even, ssd_conv_w, ssd_conv_b, ssd_a_log_fwd, ssd_a_log_bwd, ssd_dt_bias_fwd, ssd_dt_bias_bwd, ssd_d, ssd_norm, mla_q_a_norm, mla_w_q_b, mla_kv_a_norm, mla_w_kv_b, mla_q_norm, mla_k_norm, w_out_even, w_in_odd, gqa_q_norm, gqa_k_norm, w_out_odd):
    nb, ls, _ = x_prompt.shape
    db, dl, _ = x_sample.shape
    depth = w_ada.shape[0]
    n_ctx = nb * ls

    xs = [x_prompt.reshape(n_ctx, D), x_sample.reshape(db * dl, D)]
    seqs = (ls, dl)
    rows = COND_ROWS
    cond = jnp.concatenate([c_ctx[None, :], c, jnp.zeros((rows - 1 - db, D), F32)], axis=0)
    mod_all = _ada_mod(cond, w_ada, b_ada).reshape(depth, rows, 1, 6 * D)
    g_mix3 = g_mix.reshape(depth, 1, D)
    w_in_even_t = jnp.swapaxes(w_in_even, 1, 2)
    g_ffn3 = g_ffn.reshape(depth, 1, D)

    wr = w_router
    rb = router_bias.reshape(N_EXPERTS, 1)
    wg_all, wu_all, wd_all = w_exp_gate, w_exp_up, w_exp_down

    outs = {}
    for i in range(depth):
        j = i // 2
        mods = [(mod_all, i, 0, 1), (mod_all, i, 1, db)]
        if i % 2 == 0:
            wq, wk, wv = _mla_weights(mla_w_q_b[j], mla_w_kv_b[j])
            cw = jnp.pad(ssd_conv_w[j], ((0, 5), (0, 0)))
            cb = ssd_conv_b[j].reshape(1, SSD_CONV_DIM)
            hp = _pad_lanes(jnp.stack([ssd_a_log_fwd[j], ssd_a_log_bwd[j], ssd_dt_bias_fwd[j],
                                       ssd_dt_bias_bwd[j], ssd_d[j]]), LANE)
            hp = jnp.pad(hp, ((0, 3), (0, 0)))
            nw = ssd_norm[j].reshape(1, SSD_INNER)
            qan = mla_q_a_norm[j].reshape(1, MLA_Q_LORA)
            kvn = mla_kv_a_norm[j].reshape(1, MLA_KV_LORA)
            qn = _pad_lanes(mla_q_norm[j].reshape(1, MLA_QK) * (MLA_QK ** -0.5 * LOG2E), LANE)
            kn = _pad_lanes(mla_k_norm[j].reshape(1, MLA_QK), LANE)
            h0 = (_state_to_kernel(state_ssd_fwd[:, j]), _state_to_kernel(state_ssd_bwd[:, j]))
            ckpe = jnp.pad(cache_mla_kpe[:, j], ((0, 0), (0, 0), (MLA_NOPE, LANE - MLA_QK)))
            tables = tuple(jnp.asarray(np.tile(tb, (1, 2))) for tb in _rope_tables(dl, MLA_ROPE, MLA_NOPE, 1))
            w_out = w_out_even
            ys = []
            for s in range(2):
                z, xbc, qa, kv = _inproj(xs[s], mods[s], g_mix3, i, w_in_even_t, j, EVEN_WIDTHS, EVEN_PIECES, transposed=True)
                dt = kv
                if s == 0:
                    y, sf, sb = _ssd(z, xbc, dt, cw, cb, hp, nw, None, seq=seqs[s])
                    o, ckv_new, kpe_new = _mla(qa, kv, qan, kvn, wq, wk, wv, qn, kn, None, None, seq=seqs[s])
                    outs.setdefault("ssd_f", []).append(_state_from_kernel(sf))
                    outs.setdefault("ssd_b", []).append(_state_from_kernel(sb))
                    outs.setdefault("ckv", []).append(ckv_new.reshape(nb, ls, MLA_KV_LORA))
                    outs.setdefault("kpe", []).append(kpe_new[:, MLA_NOPE:MLA_QK].reshape(nb, ls, MLA_ROPE))
                else:
                    y, = _ssd(z, xbc, dt, cw, cb, hp, nw, h0, seq=seqs[s])
                    o, = _mla(qa, kv, qan, kvn, wq, wk, wv, qn, kn, (cache_mla_ckv[:, j], ckpe), tables,
                              seq=seqs[s])
                ys.append((y, o))
        else:
            nq = GQA_HEADS * GQA_HEAD_DIM
            nkv = GQA_KV_HEADS * GQA_HEAD_DIM
            nk = GQA_KV_HEADS * LANE
            qscale = GQA_HEAD_DIM ** -0.5 * LOG2E
            qn = jnp.tile(gqa_q_norm[j].reshape(1, GQA_HEAD_DIM) * qscale, (1, nq // GQA_HEAD_DIM))
            kn = jnp.tile(gqa_k_norm[j].reshape(1, GQA_HEAD_DIM), (1, nk // GQA_HEAD_DIM))
            cache = (cache_gqa_k[:, j].reshape(db, -1, nkv), cache_gqa_v[:, j].reshape(db, -1, nkv))
            tables = tuple(jnp.asarray(np.tile(tb, (1, 2))) for tb in _rope_tables(dl, GQA_HEAD_DIM, 0, 2))
            w_out = w_out_odd
            ys = []
            for s in range(2):
                q, k, v = _inproj(xs[s], mods[s], g_mix3, i, w_in_odd, j, ODD_WIDTHS, ODD_PIECES)
                if s == 0:
                    o, k_new = _gqa(q, k, v, qn, kn, None, None, seq=seqs[s])
                    outs.setdefault("gk", []).append(
                        _undup_heads(k_new).reshape(nb, ls, GQA_KV_HEADS, GQA_HEAD_DIM))
                    outs.setdefault("gv", []).append(
                        _undup_heads(v).reshape(nb, ls, GQA_KV_HEADS, GQA_HEAD_DIM))
                else:
                    o, = _gqa(q, k, v, qn, kn, cache, tables, seq=seqs[s])
                ys.append((o,))
        for s in range(2):
            x1, h2, routed, cnt = _outproj(xs[s], mods[s], ys[s], w_out, j, g_ffn3, i, wr, rb)
            xs[s] = _moe(x1, mods[s], h2, routed, cnt, wg_all, wu_all, wd_all, i)

    stack = lambda key: jnp.stack(outs[key], axis=1)
    return (xs[0].reshape(nb, ls, D), xs[1].reshape(db, dl, D),
            stack("ssd_f"), stack("ssd_b"), stack("ckv"), stack("kpe"), stack("gk"), stack("gv"))
```

```python
import functools

import numpy as np

import jax
import jax.numpy as jnp
from jax import lax
from jax.experimental import pallas as pl
from jax.experimental.pallas import tpu as pltpu

F32 = jnp.float32
BF16 = jnp.bfloat16

D = 1024
EPS = 1e-6
GRID_W = 64
ROPE_THETA = 10000.0
SSD_HEADS = 16
SSD_HEADDIM = 64
SSD_INNER = 1024
SSD_STATE = 64
SSD_GROUPS = 2
SSD_CONV_DIM = 1280
SSD_CHUNK = 128
MLA_HEADS = 8
MLA_Q_LORA = 256
MLA_KV_LORA = 128
MLA_NOPE = 64
MLA_ROPE = 32
MLA_V = 64
MLA_QK = 96
GQA_HEADS = 16
GQA_KV_HEADS = 4
GQA_HEAD_DIM = 64
N_EXPERTS = 16
N_GROUPS = 4
GROUP_SIZE = 4
D_EXPERT = 256

LANE = 128
Q_BLOCK = 128
LAT_Q_ROWS = 256
GQA_LAT_Q_ROWS = 128
LOG2E = 1.4426950408889634
TOKEN_TILE = 512
EXPERT_TILE = 512
SEG_ALIGN = 16
GROUP_LANE = 16
DISPATCH_ROWS = TOKEN_TILE + LANE
SORTED_WIDTH = D + LANE
ADA_TILE = 1536
COND_ROWS = 16
VMEM_LIMIT = 56 * 1024 * 1024


def _cparams(n_axes):
    return pltpu.CompilerParams(dimension_semantics=("arbitrary",) * n_axes,
                                vmem_limit_bytes=VMEM_LIMIT)


def _dot(a, b):
    return jnp.dot(a, b, preferred_element_type=F32)


def _dot_nt(a, b):
    return lax.dot_general(a, b, (((1,), (1,)), ((), ())), preferred_element_type=F32)


def _split3(x):
    hi = x.astype(BF16)
    r1 = x - hi.astype(F32)
    mid = r1.astype(BF16)
    lo = (r1 - mid.astype(F32)).astype(BF16)
    return hi, mid, lo


def _dot_x3(x, e):
    hi, mid, lo = _split3(x)
    return _dot(hi, e) + _dot(mid, e) + _dot(lo, e)


def _dot_x2(x, e):
    hi = x.astype(BF16)
    mid = (x - hi.astype(F32)).astype(BF16)
    return _dot(hi, e) + _dot(mid, e)


def _dot_3x(e, x):
    hi, mid, lo = _split3(x)
    return _dot(e, hi) + _dot(e, mid) + _dot(e, lo)


def _dot_f32(a, b):
    a0, a1, a2 = _split3(a)
    b0, b1, b2 = _split3(b)
    return (_dot(a0, b0) + (_dot(a0, b1) + _dot(a1, b0))
            + (_dot(a1, b1) + _dot(a0, b2) + _dot(a2, b0)))


def _sigmoid(x):
    return 1.0 / (1.0 + jnp.exp(-x))


def _silu(x):
    return x * _sigmoid(x)


def _iota(shape, axis):
    return lax.broadcasted_iota(jnp.int32, shape, axis)


def _ada_kernel(c_ref, w_ref, b_ref, o_ref):
    a = _silu(c_ref[...]).astype(BF16)
    o_ref[0] = _dot(a, w_ref[0].astype(BF16)) + b_ref[0]


def _ada_mod(cond, w_ada, b_ada):
    depth, _, n = w_ada.shape
    rows = cond.shape[0]
    return pl.pallas_call(
        _ada_kernel,
        grid=(depth, n // ADA_TILE),
        in_specs=[pl.BlockSpec((rows, D), lambda l, j: (0, 0)),
                  pl.BlockSpec((1, D, ADA_TILE), lambda l, j: (l, 0, j)),
                  pl.BlockSpec((1, 1, ADA_TILE), lambda l, j: (l, 0, j))],
        out_specs=pl.BlockSpec((1, rows, ADA_TILE), lambda l, j: (l, 0, j)),
        out_shape=jax.ShapeDtypeStruct((depth, rows, n), F32),
        compiler_params=_cparams(2),
        name="ada_mod",
    )(cond, w_ada, b_ada.reshape(depth, 1, n))


def _mod_spec(mod, n_tokens, tile):
    _, layer, row0, n_rows = mod
    per_row = n_tokens // n_rows
    return pl.BlockSpec((None, 1, 1, 6 * D), lambda i, *_: (layer, row0 + (i * tile) // per_row, 0, 0))


def _layer_spec(a, layer):
    return pl.BlockSpec((None,) + a.shape[1:], lambda i, *_: (layer, 0, 0))


def _norm_mod(x, g, shift, scale):
    ms = jnp.mean(x * x, axis=-1, keepdims=True)
    y = x * lax.rsqrt(ms + EPS) * g
    return y * (1.0 + scale) + shift


def _inproj_kernel(x_ref, mod_ref, g_ref, w_ref, *refs, widths, pieces, transposed):
    o_refs, wb = refs[:-1], refs[-1]

    @pl.when(pl.program_id(0) == 0)
    def _():
        wb[...] = jnp.zeros_like(wb)
        for src, width, dst in pieces:
            if transposed:
                wb[dst:dst + width, :] = w_ref[src:src + width, :].astype(BF16)
            else:
                wb[:, dst:dst + width] = w_ref[:, src:src + width].astype(BF16)

    m = mod_ref[0]
    h = _norm_mod(x_ref[...], g_ref[...], m[:, 0:D], m[:, D:2 * D]).astype(BF16)
    off = 0
    for o_ref, wd in zip(o_refs, widths):
        if transposed:
            o_ref[...] = _dot_nt(h, wb[off:off + wd, :]).astype(o_ref.dtype)
        else:
            o_ref[...] = _dot(h, wb[:, off:off + wd]).astype(o_ref.dtype)
        off += wd


def _inproj(x, mod, g, layer, w, j, widths, pieces, transposed=False):
    t = x.shape[0]
    n_out = sum(widths)
    return pl.pallas_call(
        functools.partial(_inproj_kernel, widths=widths, pieces=pieces, transposed=transposed),
        grid=(t // TOKEN_TILE,),
        in_specs=[pl.BlockSpec((TOKEN_TILE, D), lambda i: (i, 0)),
                  _mod_spec(mod, t, TOKEN_TILE),
                  _layer_spec(g, layer),
                  pl.BlockSpec((None,) + w.shape[1:], lambda i: (j, 0, 0))],
        out_specs=[pl.BlockSpec((TOKEN_TILE, wd), lambda i: (i, 0)) for wd in widths],
        out_shape=[jax.ShapeDtypeStruct((t, wd), F32) for wd in widths],
        scratch_shapes=[pltpu.VMEM((n_out, D) if transposed else (D, n_out), BF16)],
        compiler_params=_cparams(1),
        name="inproj",
    )(x, mod[0], g, w)


EVEN_WIDTHS = (SSD_INNER, SSD_CONV_DIM, MLA_Q_LORA, 2 * LANE)
_XBC_END = SSD_INNER + SSD_CONV_DIM
_QA0 = _XBC_END + SSD_HEADS
_KV0 = _QA0 + MLA_Q_LORA
EVEN_PIECES = ((0, _XBC_END, 0),
               (_QA0, MLA_Q_LORA, _XBC_END),
               (_KV0, MLA_KV_LORA, _XBC_END + MLA_Q_LORA),
               (_KV0 + MLA_KV_LORA, MLA_ROPE, _XBC_END + MLA_Q_LORA + LANE + MLA_NOPE),
               (_XBC_END, SSD_HEADS, _XBC_END + MLA_Q_LORA + LANE))
_NQ = GQA_HEADS * GQA_HEAD_DIM
_NKV = GQA_KV_HEADS * GQA_HEAD_DIM
ODD_WIDTHS = (_NQ, GQA_KV_HEADS * LANE, GQA_KV_HEADS * LANE)
ODD_PIECES = ((0, _NQ, 0),) + tuple(
    (_NQ + part * _NKV + h * GQA_HEAD_DIM, GQA_HEAD_DIM,
     _NQ + part * GQA_KV_HEADS * LANE + h * LANE + rep * GQA_HEAD_DIM)
    for part in range(2) for h in range(GQA_KV_HEADS) for rep in range(2))


def _ssd_kernel(*refs, seq, has_h0, want_state):
    it = iter(refs)
    z_ref, xbc_ref, dt_ref, cw_ref, cb_ref, hp_ref, nw_ref = [next(it) for _ in range(7)]
    h0_refs = (next(it), next(it)) if has_h0 else None
    y_ref = next(it)
    s_refs = (next(it), next(it)) if want_state else None
    xs_ref, bc_ref, yacc_ref, st_ref = [next(it) for _ in range(4)]

    q = SSD_CHUNK
    nc = seq // q
    unrolled = nc <= 2
    gn = SSD_GROUPS * SSD_STATE

    cw = cw_ref[...]
    cb = cb_ref[...]
    rid = _iota((q, SSD_CONV_DIM), 0)

    def rows(c):
        return pl.ds(c * q, q) if isinstance(c, int) else pl.ds(pl.multiple_of(c * q, q), q)

    def conv_chunk(c):
        xc = xbc_ref[rows(c), :]
        if isinstance(c, int):
            prev = xbc_ref[c * q - 1:c * q, :] if c > 0 else jnp.zeros((1, SSD_CONV_DIM), F32)
            nxt = xbc_ref[(c + 1) * q:(c + 1) * q + 1, :] if c < nc - 1 else jnp.zeros((1, SSD_CONV_DIM), F32)
        else:
            prev = xbc_ref[pl.ds(jnp.maximum(c * q - 1, 0), 1), :]
            nxt = xbc_ref[pl.ds(jnp.minimum(c * q + q, seq - 1), 1), :]
            prev = jnp.where(c > 0, prev, 0.0)
            nxt = jnp.where(c < nc - 1, nxt, 0.0)
        xp = jnp.where(rid == 0, prev, pltpu.roll(xc, 1, axis=0))
        xn = jnp.where(rid == q - 1, nxt, pltpu.roll(xc, q - 1, axis=0))
        v = _silu(xp * cw[0:1] + xc * cw[1:2] + xn * cw[2:3] + cb)
        xs_ref[rows(c), :] = v[:, :SSD_INNER]
        bc_ref[rows(c), :] = v[:, SSD_INNER:]

    hp = hp_ref[...]
    expand = (_iota((LANE, SSD_INNER), 1) // SSD_HEADDIM == _iota((LANE, SSD_INNER), 0)).astype(BF16)
    blockmask = (_iota((gn, SSD_INNER), 0) // SSD_STATE
                 == _iota((gn, SSD_INNER), 1) // (SSD_INNER // SSD_GROUPS))
    ri = _iota((q, q), 0)
    ci = _iota((q, q), 1)
    lane = _iota((q, LANE), 1)
    lo_half = lane < SSD_STATE
    d_e = _dot_x3(jnp.broadcast_to(hp[4:5], (8, LANE)), expand)[0:1]
    nw = nw_ref[...]

    def scan_chunk(c, fwd, a, bias, causal, tri):
        xs = xs_ref[rows(c), :]
        bc = bc_ref[rows(c), :]
        bm = bc[:, :gn]
        cm = bc[:, gn:]
        raw = dt_ref[rows(c), :] + bias
        dt = jnp.maximum(raw, 0.0) + jnp.log1p(jnp.exp(-jnp.abs(raw)))
        cum = _dot_3x(tri, dt * a)
        cum_t = cum.T
        dt_t = dt.T
        last = cum[q - 1:q] if fwd else cum[0:1]
        ecum = jnp.exp(cum)
        ex = _dot_x2(jnp.concatenate([ecum, dt * jnp.exp(last - cum)], axis=0), expand)
        ecum_e, w_e = ex[0:q], ex[q:2 * q]
        cdec_e = ecum_e[q - 1:q] if fwd else ecum_e[0:1]
        st = st_ref[...]
        cmb = cm.astype(BF16)
        bmb = bm.astype(BF16)
        y = _dot(cmb, st.astype(BF16)) * ecum_e
        cb0 = _dot_nt(jnp.where(lo_half, cm, 0.0).astype(BF16), bmb)
        cb1 = _dot_nt(jnp.where(lo_half, 0.0, cm).astype(BF16), bmb)
        pairs = []
        for j in range(SSD_HEADS // 2):
            cbg = cb0 if j < SSD_HEADS // 4 else cb1
            xp = xs[:, j * LANE:(j + 1) * LANE]
            scs = []
            for half in range(2):
                h = 2 * j + half
                seg = cum[:, h:h + 1] - cum_t[h:h + 1, :]
                dec = jnp.exp(jnp.where(causal, seg, -jnp.inf))
                scs.append((cbg * dec * dt_t[h:h + 1, :]).astype(BF16))
            rhs = jnp.concatenate([jnp.where(lo_half, xp, 0.0), jnp.where(lo_half, 0.0, xp)], axis=0)
            pairs.append(_dot(jnp.concatenate(scs, axis=1), rhs.astype(BF16)))
        y = y + jnp.concatenate(pairs, axis=1)
        new = _dot(bm.T.astype(BF16), (xs * w_e).astype(BF16))
        st_ref[...] = jnp.where(blockmask, st * cdec_e + new, 0.0)
        if fwd:
            yacc_ref[rows(c), :] = y
        else:
            y = y + yacc_ref[rows(c), :] + xs * d_e
            y = y * _silu(z_ref[rows(c), :])
            ms = jnp.mean(y * y, axis=-1, keepdims=True)
            y_ref[rows(c), :] = (y * lax.rsqrt(ms + EPS) * nw).astype(y_ref.dtype)

    def scan_pass(fwd):
        a = -jnp.exp(hp[0:1] if fwd else hp[1:2])
        bias = hp[2:3] if fwd else hp[3:4]
        causal = (ri >= ci) if fwd else (ri <= ci)
        tri = causal.astype(BF16)
        if has_h0:
            st_ref[...] = h0_refs[0 if fwd else 1][0]
        else:
            st_ref[...] = jnp.zeros((gn, SSD_INNER), F32)
        if unrolled:
            for k in range(nc):
                scan_chunk(k if fwd else nc - 1 - k, fwd, a, bias, causal, tri)
        else:
            def body(k, carry):
                c = k if fwd else nc - 1 - k
                scan_chunk(c, fwd, a, bias, causal, tri)
                if fwd:
                    conv_chunk(jnp.minimum(c + 1, nc - 1))
                return carry

            lax.fori_loop(0, nc, body, 0)
        if want_state:
            for j in range(SSD_INNER // LANE):
                g = (j * LANE) // (SSD_INNER // SSD_GROUPS)
                t = st_ref[:, j * LANE:(j + 1) * LANE].T
                s_refs[0 if fwd else 1][0, j * LANE:(j + 1) * LANE, :] = t[:, g * SSD_STATE:(g + 1) * SSD_STATE]

    if unrolled:
        for c in range(nc):
            conv_chunk(c)
    else:
        conv_chunk(0)
    scan_pass(True)
    scan_pass(False)


def _ssd(z, xbc, dt, cw, cb, hp, nw, h0, *, seq):
    t = z.shape[0]
    n_seq = t // seq
    has_h0 = h0 is not None
    want_state = not has_h0
    gn = SSD_GROUPS * SSD_STATE
    tok = lambda w: pl.BlockSpec((seq, w), lambda b: (b, 0))
    full = lambda a: pl.BlockSpec(a.shape, lambda b: (0,) * a.ndim)
    dt_spec = pl.BlockSpec((seq, LANE), lambda b: (b, 1))
    in_specs = [tok(SSD_INNER), tok(SSD_CONV_DIM), dt_spec, full(cw), full(cb), full(hp), full(nw)]
    args = [z, xbc, dt, cw, cb, hp, nw]
    if has_h0:
        in_specs += [pl.BlockSpec((1, gn, SSD_INNER), lambda b: (b, 0, 0))] * 2
        args += list(h0)
    out_shape = [jax.ShapeDtypeStruct((t, SSD_INNER), BF16)]
    out_specs = [tok(SSD_INNER)]
    if want_state:
        out_shape += [jax.ShapeDtypeStruct((n_seq, SSD_INNER, SSD_STATE), F32)] * 2
        out_specs += [pl.BlockSpec((1, SSD_INNER, SSD_STATE), lambda b: (b, 0, 0))] * 2

    return pl.pallas_call(
        functools.partial(_ssd_kernel, seq=seq, has_h0=has_h0, want_state=want_state),
        grid=(n_seq,),
        in_specs=in_specs,
        out_specs=out_specs,
        out_shape=out_shape,
        scratch_shapes=[pltpu.VMEM((seq, SSD_INNER), F32),
                        pltpu.VMEM((seq, 2 * gn), F32),
                        pltpu.VMEM((seq, SSD_INNER), F32),
                        pltpu.VMEM((gn, SSD_INNER), F32)],
        compiler_params=_cparams(1),
        name="ssd_lat" if has_h0 else "ssd_ctx",
    )(*args)


def _softmax_pv(ss, vals):
    m = ss[0].max(axis=-1, keepdims=True)
    for s in ss[1:]:
        m = jnp.maximum(m, s.max(axis=-1, keepdims=True))
    ps = [jnp.exp2(s - m) for s in ss]
    den = ps[0].sum(axis=-1, keepdims=True)
    for p in ps[1:]:
        den = den + p.sum(axis=-1, keepdims=True)
    o = _dot(ps[0].astype(BF16), vals[0]())
    for p, v in zip(ps[1:], vals[1:]):
        o = o + _dot(p.astype(BF16), v())
    return o / den


def _attend_blocks(seq, units, emit, s_scr, qrows):
    nblk = seq // qrows

    def scores(u, blk, static):
        q_ref, rows, keys, _ = units[u]
        r = blk * rows if static else pl.multiple_of(blk * rows, rows)
        qb = q_ref[pl.ds(r, rows), :]
        return [_dot_nt(qb, k()) for k in keys]

    if s_scr is None:
        for i in range(nblk):
            emit(i * qrows, [_softmax_pv(scores(u, i, True), units[u][3]) for u in range(len(units))])
        return

    def put(slot, blk):
        for u in range(len(units)):
            off = 0
            for s in scores(u, blk, isinstance(blk, int)):
                s_scr[slot][u][:, off:off + s.shape[1]] = s
                off += s.shape[1]

    def take(slot, blk):
        outs = []
        for u, (_, _, keys, vals) in enumerate(units):
            off = 0
            ss = []
            for k in keys:
                n = k().shape[0]
                ss.append(s_scr[slot][u][:, off:off + n])
                off += n
            outs.append(_softmax_pv(ss, vals))
        emit(blk * qrows if isinstance(blk, int) else pl.multiple_of(blk * qrows, qrows), outs)

    put(0, 0)

    def body(ii, carry):
        b0 = 2 * ii
        put(1, b0 + 1)
        take(0, b0)

        def more():
            put(0, b0 + 2)
            take(1, b0 + 1)

        def last():
            take(1, b0 + 1)

        lax.cond(b0 + 2 < nblk, more, last)
        return carry

    lax.fori_loop(0, nblk // 2, body, 0)


def _mla_kernel(*refs, seq, lat):
    it = iter(refs)
    qa_ref, kv_ref, qan_ref, kvn_ref, wq_ref, wk_ref, wv_ref, qn_ref, kn_ref = [next(it) for _ in range(9)]
    if lat:
        cckv_ref, ckpe_ref, cos_ref, sin_ref = [next(it) for _ in range(4)]
    o_ref = next(it)
    if not lat:
        ckv_out, kpe_out = next(it), next(it)
    q_s, k_s, v_s, qh0_s, qh1_s, kh0_s, kh1_s = [next(it) for _ in range(7)]
    s_scr = None
    if lat:
        kc_s, vc_s, kch0_s, kch1_s = [next(it) for _ in range(4)]
        s_scr = [[next(it), next(it)], [next(it), next(it)]]

    kv = kv_ref[...]
    kpe = jnp.where(_iota((seq, LANE), 1) >= MLA_NOPE, kv[:, LANE:], 0.0)
    ckv_raw = kv[:, :LANE]
    ckv = ckv_raw * lax.rsqrt(jnp.mean(ckv_raw * ckv_raw, axis=-1, keepdims=True) + EPS) * kvn_ref[...]
    qa = qa_ref[...]
    qa = qa * lax.rsqrt(jnp.mean(qa * qa, axis=-1, keepdims=True) + EPS) * qan_ref[...]
    if not lat:
        ckv_out[...] = ckv
        kpe_out[...] = kpe
    ckvb = ckv.astype(BF16)
    q_s[...] = _dot(qa.astype(BF16), wq_ref[...])
    k_s[...] = _dot(ckvb, wk_ref[...])
    v_s[...] = _dot(ckvb, wv_ref[...]).astype(BF16)
    if lat:
        ccb = cckv_ref[0].astype(BF16)
        kc_s[...] = _dot(ccb, wk_ref[...])
        vc_s[...] = _dot(ccb, wv_ref[...]).astype(BF16)
        ckpe = ckpe_ref[0]
        cos = cos_ref[...]
        sin = sin_ref[...]
        dist = MLA_ROPE // 4
        li = _iota((2 * LANE, 2 * LANE), 0)
        lj = _iota((2 * LANE, 2 * LANE), 1)
        swap = (li == jnp.where((lj % (2 * dist)) < dist, lj + dist, lj - dist)).astype(BF16)
    qn = qn_ref[...]
    kn = kn_ref[...]
    qrows = LAT_Q_ROWS if lat else Q_BLOCK
    lo_lanes = _iota((qrows, LANE), 1) < MLA_V

    def head_norm(x, g):
        return x * lax.rsqrt(jnp.sum(x * x, axis=-1, keepdims=True) * (1.0 / MLA_QK) + EPS) * g

    def rotate(x2):
        return x2 * cos + _dot_x2(x2, swap) * sin

    for j in range(MLA_HEADS // 2):
        pair = slice(j * LANE, (j + 1) * LANE)
        sls = [slice((2 * j + half) * LANE, (2 * j + half + 1) * LANE) for half in range(2)]
        q2 = jnp.concatenate([head_norm(q_s[:, sl], qn) for sl in sls], axis=1)
        k2 = jnp.concatenate([head_norm(k_s[:, sl] + kpe, kn) for sl in sls], axis=1)
        if lat:
            q2 = rotate(q2)
            k2 = rotate(k2)
            for half, kch_s in enumerate((kch0_s, kch1_s)):
                kch_s[...] = head_norm(kc_s[:, sls[half]] + ckpe, kn).astype(BF16)
        for half, (qh_s, kh_s) in enumerate(((qh0_s, kh0_s), (qh1_s, kh1_s))):
            qh_s[...] = q2[:, half * LANE:(half + 1) * LANE].astype(BF16)
            kh_s[...] = k2[:, half * LANE:(half + 1) * LANE].astype(BF16)

        def emit(r0, outs, pair=pair):
            o_ref[pl.ds(r0, qrows), pair] = jnp.where(lo_lanes, outs[0], outs[1]).astype(o_ref.dtype)

        units = []
        for half, (qh_s, kh_s) in enumerate(((qh0_s, kh0_s), (qh1_s, kh1_s))):
            keys = [lambda r=kh_s: r[...]]
            vals = [lambda pair=pair: v_s[:, pair]]
            if lat:
                keys = [lambda r=(kch0_s, kch1_s)[half]: r[...]] + keys
                vals = [lambda pair=pair: vc_s[:, pair]] + vals
            units.append((qh_s, qrows, keys, vals))
        _attend_blocks(seq, units, emit, s_scr, qrows)


def _mla(qa, kv, qan, kvn, wq, wk, wv, qn, kn, cache, tables, *, seq):
    t = qa.shape[0]
    n_seq = t // seq
    lat = cache is not None
    hv = MLA_HEADS * MLA_V
    tok = lambda w: pl.BlockSpec((seq, w), lambda b: (b, 0))
    full = lambda a: pl.BlockSpec(a.shape, lambda b: (0,) * a.ndim)
    args = [qa, kv, qan, kvn, wq, wk, wv, qn, kn]
    in_specs = [tok(MLA_Q_LORA), tok(2 * LANE)] + [full(a) for a in args[2:]]
    past = 0
    if lat:
        past = cache[0].shape[1]
        in_specs += [pl.BlockSpec((1, past, LANE), lambda b: (b, 0, 0))] * 2 + [full(tables[0]), full(tables[1])]
        args += [cache[0], cache[1], tables[0], tables[1]]
    out_shape = [jax.ShapeDtypeStruct((t, hv), BF16)]
    out_specs = [tok(hv)]
    if not lat:
        out_shape += [jax.ShapeDtypeStruct((t, LANE), F32)] * 2
        out_specs += [pl.BlockSpec((seq, LANE), lambda b: (b, 0))] * 2
    scratch = [pltpu.VMEM((seq, MLA_HEADS * LANE), F32),
               pltpu.VMEM((seq, MLA_HEADS * LANE), F32),
               pltpu.VMEM((seq, hv), BF16)] + [pltpu.VMEM((seq, LANE), BF16)] * 4
    if lat:
        scratch += [pltpu.VMEM((past, MLA_HEADS * LANE), F32),
                    pltpu.VMEM((past, hv), BF16),
                    pltpu.VMEM((past, LANE), BF16),
                    pltpu.VMEM((past, LANE), BF16)]
        scratch += [pltpu.VMEM((LAT_Q_ROWS, past + seq), F32)] * 4

    return pl.pallas_call(
        functools.partial(_mla_kernel, seq=seq, lat=lat),
        grid=(n_seq,),
        in_specs=in_specs,
        out_specs=out_specs,
        out_shape=out_shape,
        scratch_shapes=scratch,
        compiler_params=_cparams(1),
        name="mla_lat" if lat else "mla_ctx",
    )(*args)


def _gqa_kernel(*refs, seq, lat):
    it = iter(refs)
    q_ref, k_ref, v_ref, qn_ref, kn_ref = [next(it) for _ in range(5)]
    if lat:
        ck_ref, cv_ref, cos_ref, sin_ref = [next(it) for _ in range(4)]
    o_ref = next(it)
    if not lat:
        kn_out = next(it)
    kn_s, vb_s, q2_s = [next(it) for _ in range(3)]
    s_scr = None
    if lat:
        ckb_s, cvb_s = next(it), next(it)
        s_scr = [[next(it)], [next(it)]]

    hd = GQA_HEAD_DIM
    nq = GQA_HEADS * hd
    nk = GQA_KV_HEADS * LANE
    qrows = GQA_LAT_Q_ROWS if lat else Q_BLOCK
    nblk = seq // qrows
    wide = 2 * LANE
    li = _iota((wide, wide), 0)
    lj = _iota((wide, wide), 1)
    same_head = (li // hd == lj // hd).astype(BF16)
    if lat:
        dist = hd // 4
        swap = (li == jnp.where((lj % (2 * dist)) < dist, lj + dist, lj - dist)).astype(BF16)
        cos = cos_ref[...]
        sin = sin_ref[...]
    lo_q = _iota((qrows, LANE), 1) < hd

    def prepared(x_ref, g_ref, b):
        sl = slice(b * wide, (b + 1) * wide)
        x = x_ref[:, sl]
        ss = _dot_x2(x * x, same_head)
        xn = x * lax.rsqrt(ss * (1.0 / hd) + EPS) * g_ref[:, sl]
        if not lat:
            return xn, xn
        return xn, xn * cos + _dot_x2(xn, swap) * sin

    for b in range(nk // wide):
        sl = slice(b * wide, (b + 1) * wide)
        kn, kr = prepared(k_ref, kn_ref, b)
        if not lat:
            kn_out[:, sl] = kn
        kn_s[:, sl] = kr.astype(BF16)
    if lat:
        for src_ref, dst in ((ck_ref, ckb_s), (cv_ref, cvb_s)):
            for h in range(GQA_KV_HEADS):
                head = src_ref[0, :, h * hd:(h + 1) * hd].astype(BF16)
                for rep in range(LANE // hd):
                    dst[:, h * LANE + rep * hd:h * LANE + (rep + 1) * hd] = head
    vb_s[...] = v_ref[...].astype(BF16)
    for b in range(nq // wide):
        _, qr = prepared(q_ref, qn_ref, b)
        for half in range(wide // LANE):
            p = b * (wide // LANE) + half
            for i in range(nblk):
                blk = qr[i * qrows:(i + 1) * qrows, half * LANE:(half + 1) * LANE]
                q2_s[p, 2 * i * qrows:(2 * i + 1) * qrows, :] = jnp.where(lo_q, blk, 0.0).astype(BF16)
                q2_s[p, (2 * i + 1) * qrows:(2 * i + 2) * qrows, :] = jnp.where(lo_q, 0.0, blk).astype(BF16)

    for j in range(GQA_HEADS // 2):
        pair = slice(j * LANE, (j + 1) * LANE)
        g = (2 * j) // (GQA_HEADS // GQA_KV_HEADS)
        gs = slice(g * LANE, (g + 1) * LANE)

        def emit(r0, outs, pair=pair):
            o = outs[0]
            o_ref[pl.ds(r0, qrows), pair] = jnp.where(lo_q, o[:qrows], o[qrows:]).astype(o_ref.dtype)

        keys = [lambda gs=gs: kn_s[:, gs]]
        vals = [lambda gs=gs: vb_s[:, gs]]
        if lat:
            keys = [lambda gs=gs: ckb_s[:, gs]] + keys
            vals = [lambda gs=gs: cvb_s[:, gs]] + vals
        _attend_blocks(seq, [(q2_s.at[j], 2 * qrows, keys, vals)], emit, s_scr, qrows)


def _gqa(q, k, v, qn, kn, cache, tables, *, seq):
    t = q.shape[0]
    n_seq = t // seq
    lat = cache is not None
    nq = GQA_HEADS * GQA_HEAD_DIM
    nk = GQA_KV_HEADS * LANE
    tok = lambda w: pl.BlockSpec((seq, w), lambda b: (b, 0))
    full = lambda a: pl.BlockSpec(a.shape, lambda b: (0,) * a.ndim)
    args = [q, k, v, qn, kn]
    in_specs = [tok(nq), tok(nk), tok(nk), full(qn), full(kn)]
    past = 0
    if lat:
        past = cache[0].shape[1]
        nkv = cache[0].shape[2]
        in_specs += [pl.BlockSpec((1, past, nkv), lambda b: (b, 0, 0))] * 2 + [full(tables[0]), full(tables[1])]
        args += [cache[0], cache[1], tables[0], tables[1]]
    out_shape = [jax.ShapeDtypeStruct((t, nq), BF16)]
    out_specs = [tok(nq)]
    if not lat:
        out_shape += [jax.ShapeDtypeStruct((t, nk), F32)]
        out_specs += [pl.BlockSpec((seq, nk), lambda b: (b, 0))]
    scratch = [pltpu.VMEM((seq, nk), BF16),
               pltpu.VMEM((seq, nk), BF16),
               pltpu.VMEM((GQA_HEADS // 2, 2 * seq, LANE), BF16)]
    if lat:
        scratch += [pltpu.VMEM((past, nk), BF16), pltpu.VMEM((past, nk), BF16)]
        scratch += [pltpu.VMEM((2 * GQA_LAT_Q_ROWS, past + seq), F32)] * 2

    return pl.pallas_call(
        functools.partial(_gqa_kernel, seq=seq, lat=lat),
        grid=(n_seq,),
        in_specs=in_specs,
        out_specs=out_specs,
        out_shape=out_shape,
        scratch_shapes=scratch,
        compiler_params=_cparams(1),
        name="gqa_lat" if lat else "gqa_ctx",
    )(*args)


def _route(sel, sc):
    gs = []
    for g in range(N_GROUPS):
        r = sel[g * GROUP_SIZE:(g + 1) * GROUP_SIZE]
        best = None
        for i in range(GROUP_SIZE):
            for j in range(i + 1, GROUP_SIZE):
                s = r[i] + r[j]
                best = s if best is None else jnp.maximum(best, s)
        gs.append(best)
    cur = gs[0]
    grp = jnp.zeros_like(cur, dtype=jnp.int32)
    for g in range(1, N_GROUPS):
        upd = gs[g] > cur
        grp = jnp.where(upd, g, grp)
        cur = jnp.where(upd, gs[g], cur)

    def pick(rows, i):
        v = rows[i]
        for g in range(1, N_GROUPS):
            v = jnp.where(grp == g, rows[g * GROUP_SIZE + i], v)
        return v

    cand = [pick(sel, i) for i in range(GROUP_SIZE)]
    csc = [pick(sc, i) for i in range(GROUP_SIZE)]
    v1, i1, w1 = cand[0], jnp.zeros_like(grp), csc[0]
    for i in range(1, GROUP_SIZE):
        upd = cand[i] > v1
        v1 = jnp.where(upd, cand[i], v1)
        i1 = jnp.where(upd, i, i1)
        w1 = jnp.where(upd, csc[i], w1)
    v2 = i2 = w2 = None
    for i in range(GROUP_SIZE):
        ok = i1 != i
        if v2 is None:
            v2 = jnp.where(ok, cand[i], -jnp.inf)
            i2 = jnp.zeros_like(grp)
            w2 = csc[i]
        else:
            upd = ok & (cand[i] > v2)
            v2 = jnp.where(upd, cand[i], v2)
            i2 = jnp.where(upd, i, i2)
            w2 = jnp.where(upd, csc[i], w2)
    tot = w1 + w2
    local = [jnp.where(i1 == i, w1 / tot, 0.0) + jnp.where(i2 == i, w2 / tot, 0.0)
             for i in range(GROUP_SIZE)]
    onehot = [jnp.where(grp == g, 1.0, 0.0) for g in range(N_GROUPS)]
    return local, onehot


def _outproj_kernel(*refs, n_y):
    x_ref, mod_ref = refs[0], refs[1]
    y_refs = refs[2:2 + n_y]
    w_ref, g_ref, wr_ref, rb_ref, x1_ref, h2_ref, gate_ref, cnt_ref, w_b, wr_b = refs[2 + n_y:]

    @pl.when(pl.program_id(0) == 0)
    def _():
        w_b[...] = w_ref[...].astype(BF16)
        wr_b[...] = jnp.zeros_like(wr_b)
        for j, term in enumerate(_split3(wr_ref[...])):
            wr_b[:, j * N_EXPERTS:(j + 1) * N_EXPERTS] = term

    m = mod_ref[0]
    off = 0
    out = None
    for y_ref in y_refs:
        k = y_ref.shape[1]
        part = _dot(y_ref[...], w_b[off:off + k, :])
        out = part if out is None else out + part
        off += k
    x1 = x_ref[...] + m[:, 2 * D:3 * D] * out
    x1_ref[...] = x1
    h2 = _norm_mod(x1, g_ref[...], m[:, 3 * D:4 * D], m[:, 4 * D:5 * D])
    h2_hi = h2.astype(BF16)
    h2_ref[...] = h2_hi
    h2_mid = (h2 - h2_hi.astype(F32)).astype(BF16)
    half = h2.shape[0] // 2
    parts = jnp.concatenate([_dot(h2_hi[r], wr_b[...]) + _dot(h2_mid[r], wr_b[...])
                             for r in (slice(0, half), slice(half, None))], axis=0)
    logits = (parts + pltpu.roll(parts, LANE - N_EXPERTS, axis=1)
              + pltpu.roll(parts, LANE - 2 * N_EXPERTS, axis=1))
    sc_t = _sigmoid(logits.T[0:N_EXPERTS, :])
    sel_t = sc_t + rb_ref[...]
    local, onehot = _route([sel_t[e:e + 1] for e in range(N_EXPERTS)],
                           [sc_t[e:e + 1] for e in range(N_EXPERTS)])
    t = x1.shape[0]
    rows_t = (local + [jnp.zeros((GROUP_LANE - GROUP_SIZE, t), F32)] + onehot
              + [jnp.zeros((LANE - GROUP_LANE - N_GROUPS, t), F32)])
    routed = jnp.concatenate(rows_t, axis=0).T
    gate_ref[...] = routed
    cnt = jnp.sum(routed, axis=0, keepdims=True).astype(jnp.int32)
    cnt_ref[0] = jnp.broadcast_to(cnt, (8, LANE))


def _outproj(x, mod, ys, w, j, g, layer, wr, rb):
    t = x.shape[0]
    full = lambda a: pl.BlockSpec(a.shape, lambda i: (0,) * a.ndim)
    return pl.pallas_call(
        functools.partial(_outproj_kernel, n_y=len(ys)),
        grid=(t // TOKEN_TILE,),
        in_specs=[pl.BlockSpec((TOKEN_TILE, D), lambda i: (i, 0)),
                  _mod_spec(mod, t, TOKEN_TILE)]
                 + [pl.BlockSpec((TOKEN_TILE, y.shape[1]), lambda i: (i, 0)) for y in ys]
                 + [pl.BlockSpec((None,) + w.shape[1:], lambda i: (j, 0, 0)), _layer_spec(g, layer),
                    full(wr), full(rb)],
        out_specs=[pl.BlockSpec((TOKEN_TILE, D), lambda i: (i, 0)),
                   pl.BlockSpec((TOKEN_TILE, D), lambda i: (i, 0)),
                   pl.BlockSpec((TOKEN_TILE, LANE), lambda i: (i, 0)),
                   pl.BlockSpec((1, 8, LANE), lambda i: (i, 0, 0))],
        out_shape=[jax.ShapeDtypeStruct((t, D), F32),
                   jax.ShapeDtypeStruct((t, D), BF16),
                   jax.ShapeDtypeStruct((t, LANE), F32),
                   jax.ShapeDtypeStruct((t // TOKEN_TILE, 8, LANE), jnp.int32)],
        scratch_shapes=[pltpu.VMEM(w.shape[1:], BF16), pltpu.VMEM((D, LANE), BF16)],
        compiler_params=_cparams(1),
        name="outproj_router",
    )(x, mod[0], *ys, w, g, wr, rb)


def _pad_to(n, shift):
    return ((n + (1 << shift) - 1) >> shift) << shift


SEG_SHIFT = SEG_ALIGN.bit_length() - 1
TILE_SHIFT = EXPERT_TILE.bit_length() - 1


def _start_segments(plan_ref, i, nt, make):
    for g in range(N_GROUPS):
        n = plan_ref[i * N_GROUPS + g]
        lo = plan_ref[(nt + i) * N_GROUPS + g]
        base = plan_ref[(2 * nt + i) * N_GROUPS + g]

        def body(c, carry, lo=lo, base=base):
            for cp in make(pl.multiple_of(lo + c * SEG_ALIGN, SEG_ALIGN),
                           pl.multiple_of(base + c * SEG_ALIGN, SEG_ALIGN)):
                cp.start()
            return carry

        lax.fori_loop(0, (n + SEG_ALIGN - 1) >> SEG_SHIFT, body, 0)


def _wait_segments(plan_ref, i, make):
    total = jnp.int32(0)
    for g in range(N_GROUPS):
        total = total + ((plan_ref[i * N_GROUPS + g] + SEG_ALIGN - 1) >> SEG_SHIFT)

    def body(c, carry):
        for cp in make(0, 0):
            cp.wait()
        return carry

    lax.fori_loop(0, total, body, 0)


def _dispatch_kernel(cnt_ref, h_ref, r_ref, xs_hbm, dest_ref, plan_ref,
                     xs_l, zx, tril_s, fill_s, sem, *, nt, n_rows):
    i = pl.program_id(0)
    t = TOKEN_TILE
    k_tiles = n_rows // EXPERT_TILE

    @pl.when(i == 0)
    def _():
        zx[...] = jnp.zeros_like(zx)
        tril_s[...] = (_iota((t, t), 0) >= _iota((t, t), 1)).astype(BF16)
        tots = []
        for g in range(N_GROUPS):
            tots.append(lax.fori_loop(
                0, nt, lambda tt, acc, g=g: acc + _pad_to(cnt_ref[tt * N_GROUPS + g], SEG_SHIFT), jnp.int32(0)))
        starts = []
        cur = jnp.int32(0)
        for g in range(N_GROUPS):
            starts.append(cur)
            fill_s[2 * g] = cur + tots[g]
            cur = cur + _pad_to(tots[g], TILE_SHIFT)
            fill_s[2 * g + 1] = cur
        fill_s[2 * N_GROUPS] = cur
        fill_s[2 * N_GROUPS + 1] = n_rows

        def per_tile(tt, run):
            lo = jnp.int32(0)
            new = []
            for g in range(N_GROUPS):
                n = cnt_ref[tt * N_GROUPS + g]
                plan_ref[tt * N_GROUPS + g] = n
                plan_ref[(nt + tt) * N_GROUPS + g] = lo
                plan_ref[(2 * nt + tt) * N_GROUPS + g] = run[g]
                seg = _pad_to(n, SEG_SHIFT)
                lo = lo + seg
                new.append(run[g] + seg)
            return tuple(new)

        lax.fori_loop(0, nt, per_tile, tuple(starts))
        for k in range(k_tiles):
            row = k * EXPERT_TILE
            grp = 0
            for g in range(1, N_GROUPS):
                grp = grp + (row >= starts[g]).astype(jnp.int32)
            plan_ref[3 * nt * N_GROUPS + k] = grp
        plan_ref[3 * nt * N_GROUPS + k_tiles] = cur >> TILE_SHIFT

    routed = r_ref[...]
    lane = _iota((t, LANE), 1)
    is_grp = (lane >= GROUP_LANE) & (lane < GROUP_LANE + N_GROUPS)
    onehot = jnp.where(is_grp, routed, 0.0)
    rank = _dot(tril_s[...], onehot.astype(BF16))
    seg_lo = jnp.zeros((1, LANE), F32)
    lane1 = _iota((1, LANE), 1)
    for g in range(N_GROUPS):
        lo = plan_ref[(nt + i) * N_GROUPS + g]
        seg_lo = jnp.where(lane1 == GROUP_LANE + g, (lo - 1).astype(F32), seg_lo)
    val = jnp.where(onehot > 0.0, rank + seg_lo, 0.0)
    dest = _dot_x2(val, jnp.ones((LANE, LANE), BF16))
    dest_ref[...] = dest
    dest_row = dest.T[0:1, :].astype(jnp.int32)
    perm = jnp.where(_iota((DISPATCH_ROWS, t), 0) == dest_row, 1.0, 0.0).astype(BF16)
    r_hi, r_mid, r_lo = _split3(routed)
    packed = (r_hi.astype(F32) + pltpu.roll(r_mid.astype(F32), GROUP_SIZE, axis=1)
              + pltpu.roll(r_lo.astype(F32), 2 * GROUP_SIZE, axis=1)).astype(BF16)
    slot = i % 2
    xs_l[slot] = _dot(perm, jnp.concatenate([h_ref[...], packed], axis=1)).astype(BF16)

    def to_buffer(slot):
        def make(lo, base):
            return (pltpu.make_async_copy(xs_l.at[slot, pl.ds(lo, SEG_ALIGN)],
                                          xs_hbm.at[pl.ds(base, SEG_ALIGN)], sem.at[slot]),)
        return make

    _start_segments(plan_ref, i, nt, to_buffer(slot))

    @pl.when(i > 0)
    def _():
        _wait_segments(plan_ref, i - 1, to_buffer(1 - slot))

    @pl.when(i == nt - 1)
    def _():
        _wait_segments(plan_ref, i, to_buffer(slot))
        def zero(base):
            return (pltpu.make_async_copy(zx, xs_hbm.at[pl.ds(base, SEG_ALIGN)], sem.at[0]),)

        n_fill = jnp.int32(0)
        for r in range(N_GROUPS + 1):
            lo = fill_s[2 * r]
            chunks = (fill_s[2 * r + 1] - lo) >> SEG_SHIFT

            def body(c, carry, lo=lo):
                for cp in zero(pl.multiple_of(lo + c * SEG_ALIGN, SEG_ALIGN)):
                    cp.start()
                return carry

            lax.fori_loop(0, chunks, body, 0)
            n_fill = n_fill + chunks

        def wait_fill(c, carry):
            for cp in zero(0):
                cp.wait()
            return carry

        lax.fori_loop(0, n_fill, wait_fill, 0)


def _moe_rows(n_tokens):
    nt = n_tokens // TOKEN_TILE
    worst = n_tokens + nt * N_GROUPS * (SEG_ALIGN - 1) + N_GROUPS * (EXPERT_TILE - SEG_ALIGN)
    return _pad_to(worst, TILE_SHIFT)


def _dispatch(counts, h2, routed):
    t = h2.shape[0]
    nt = t // TOKEN_TILE
    n_rows = _moe_rows(t)
    plan_len = 3 * nt * N_GROUPS + n_rows // EXPERT_TILE + 1
    grid_spec = pltpu.PrefetchScalarGridSpec(
        num_scalar_prefetch=1,
        grid=(nt,),
        in_specs=[pl.BlockSpec((TOKEN_TILE, D), lambda i, c: (i, 0)),
                  pl.BlockSpec((TOKEN_TILE, LANE), lambda i, c: (i, 0))],
        out_specs=[pl.BlockSpec(memory_space=pl.ANY),
                   pl.BlockSpec((TOKEN_TILE, LANE), lambda i, c: (i, 0)),
                   pl.BlockSpec(memory_space=pltpu.SMEM)],
        scratch_shapes=[pltpu.VMEM((2, DISPATCH_ROWS, SORTED_WIDTH), BF16),
                        pltpu.VMEM((SEG_ALIGN, SORTED_WIDTH), BF16),
                        pltpu.VMEM((TOKEN_TILE, TOKEN_TILE), BF16),
                        pltpu.SMEM((2 * N_GROUPS + 2,), jnp.int32),
                        pltpu.SemaphoreType.DMA((2,))])
    return pl.pallas_call(
        functools.partial(_dispatch_kernel, nt=nt, n_rows=n_rows),
        grid_spec=grid_spec,
        out_shape=[jax.ShapeDtypeStruct((n_rows, SORTED_WIDTH), BF16),
                   jax.ShapeDtypeStruct((t, LANE), F32),
                   jax.ShapeDtypeStruct((plan_len,), jnp.int32)],
        compiler_params=_cparams(1),
        name="moe_dispatch",
    )(counts, h2, routed)


def _experts_kernel(plan_ref, xs_ref, wg_ref, wu_ref, wd_ref, y_ref, wg_b, wu_b, wd_b,
                    *, group_at, used_at):
    k = pl.program_id(0)
    used = plan_ref[used_at]
    group = plan_ref[group_at + k]
    prev = plan_ref[group_at + jnp.maximum(k - 1, 0)]

    @pl.when((k == 0) | (group != prev))
    def _():
        wg_b[...] = wg_ref[...].astype(BF16)
        wu_b[...] = wu_ref[...].astype(BF16)
        wd_b[...] = wd_ref[...].astype(BF16)

    @pl.when(k < used)
    def _():
        rows = xs_ref[:, :D]
        terms = xs_ref[:, D:].astype(F32)
        gates = (terms + pltpu.roll(terms, LANE - GROUP_SIZE, axis=1)
                 + pltpu.roll(terms, LANE - 2 * GROUP_SIZE, axis=1))
        acts = []
        for e in range(GROUP_SIZE):
            a = _dot(rows, wg_b[e])
            u = _dot(rows, wu_b[e])
            acts.append((_silu(a) * u * gates[:, e:e + 1]).astype(BF16))
        down = wd_b[...].reshape(GROUP_SIZE * D_EXPERT, D)
        y_ref[...] = _dot(jnp.concatenate(acts, axis=1), down).astype(y_ref.dtype)

    @pl.when(k >= used)
    def _():
        y_ref[...] = jnp.zeros_like(y_ref)


def _experts(plan, xs, wg, wu, wd, layer, nt):
    n_rows = xs.shape[0]
    k_tiles = n_rows // EXPERT_TILE
    at = 3 * nt * N_GROUPS
    wspec = lambda a, b: pl.BlockSpec((None, GROUP_SIZE, a, b), lambda k, p: (layer, p[at + k], 0, 0))
    grid_spec = pltpu.PrefetchScalarGridSpec(
        num_scalar_prefetch=1,
        grid=(k_tiles,),
        in_specs=[pl.BlockSpec((EXPERT_TILE, SORTED_WIDTH), lambda k, p: (k, 0)),
                  wspec(D, D_EXPERT), wspec(D, D_EXPERT), wspec(D_EXPERT, D)],
        out_specs=pl.BlockSpec((EXPERT_TILE, D), lambda k, p: (k, 0)),
        scratch_shapes=[pltpu.VMEM((GROUP_SIZE, D, D_EXPERT), BF16),
                        pltpu.VMEM((GROUP_SIZE, D, D_EXPERT), BF16),
                        pltpu.VMEM((GROUP_SIZE, D_EXPERT, D), BF16)])
    return pl.pallas_call(
        functools.partial(_experts_kernel, group_at=at, used_at=at + k_tiles),
        grid_spec=grid_spec,
        out_shape=jax.ShapeDtypeStruct((n_rows, D), BF16),
        compiler_params=_cparams(1),
        name="moe_experts",
    )(plan, xs, wg, wu, wd)


def _combine_kernel(plan_ref, x_ref, mod_ref, dest_ref, ys_hbm, o_ref, y_l, sem, *, nt):
    i = pl.program_id(0)

    slot = i % 2

    def from_buffer(slot):
        def make(lo, base):
            return (pltpu.make_async_copy(ys_hbm.at[pl.ds(base, SEG_ALIGN)],
                                          y_l.at[slot, pl.ds(lo, SEG_ALIGN)], sem.at[slot]),)
        return make

    @pl.when(i == 0)
    def _():
        y_l[...] = jnp.zeros_like(y_l)
        _start_segments(plan_ref, i, nt, from_buffer(slot))

    @pl.when(i + 1 < nt)
    def _():
        _start_segments(plan_ref, i + 1, nt, from_buffer(1 - slot))

    _wait_segments(plan_ref, i, from_buffer(slot))
    dest = dest_ref[...].astype(jnp.int32)
    lane = _iota(dest.shape, 1)
    back = jnp.concatenate(
        [jnp.where(dest == lane + j * LANE, 1.0, 0.0).astype(BF16) for j in range(DISPATCH_ROWS // LANE)], axis=1)
    y = _dot(back, y_l[slot])
    o_ref[...] = x_ref[...] + mod_ref[0][:, 5 * D:6 * D] * y


def _combine(plan, x1, mod, dest, ys):
    t = x1.shape[0]
    nt = t // TOKEN_TILE
    grid_spec = pltpu.PrefetchScalarGridSpec(
        num_scalar_prefetch=1,
        grid=(nt,),
        in_specs=[pl.BlockSpec((TOKEN_TILE, D), lambda i, p: (i, 0)),
                  _mod_spec(mod, t, TOKEN_TILE),
                  pl.BlockSpec((TOKEN_TILE, LANE), lambda i, p: (i, 0)),
                  pl.BlockSpec(memory_space=pl.ANY)],
        out_specs=pl.BlockSpec((TOKEN_TILE, D), lambda i, p: (i, 0)),
        scratch_shapes=[pltpu.VMEM((2, DISPATCH_ROWS, D), BF16),
                        pltpu.SemaphoreType.DMA((2,))])
    return pl.pallas_call(
        functools.partial(_combine_kernel, nt=nt),
        grid_spec=grid_spec,
        out_shape=jax.ShapeDtypeStruct((t, D), F32),
        compiler_params=_cparams(1),
        name="moe_combine",
    )(plan, x1, mod[0], dest, ys)


def _moe(x1, mod, h2, routed, cnt, wg, wu, wd, layer):
    nt = x1.shape[0] // TOKEN_TILE
    counts = cnt[:, 0, GROUP_LANE:GROUP_LANE + N_GROUPS].reshape(nt * N_GROUPS)
    xs, dest, plan = _dispatch(counts, h2, routed)
    ys = _experts(plan, xs, wg, wu, wd, layer, nt)
    return _combine(plan, x1, mod, dest, ys)


def _rope_angles(pos, half):
    inv = np.power(np.float32(ROPE_THETA), -np.arange(half, dtype=np.float32) / np.float32(half))
    ang = pos.astype(np.float32)[:, None] * inv[None, :]
    return np.cos(ang), np.sin(ang)


def _rope_tables(seq, rot, lead, lane_tiles):
    rows = seq // GRID_W
    row = np.repeat(np.arange(rows), GRID_W)
    col = np.tile(np.arange(GRID_W), rows)
    cr, sr = _rope_angles(row, rot // 4)
    cc, sc = _rope_angles(col, rot // 4)
    cos = np.concatenate([cr, cr, cc, cc], axis=1)
    sin = np.concatenate([-sr, sr, -sc, sc], axis=1)
    width = LANE // lane_tiles
    pad = width - lead - rot
    cos = np.concatenate([np.ones((seq, lead), np.float32), cos, np.ones((seq, pad), np.float32)], axis=1)
    sin = np.concatenate([np.zeros((seq, lead), np.float32), sin, np.zeros((seq, pad), np.float32)], axis=1)
    return np.tile(cos, (1, lane_tiles)), np.tile(sin, (1, lane_tiles))


def _pad_lanes(a, width):
    return jnp.pad(a, [(0, 0)] * (a.ndim - 1) + [(0, width - a.shape[-1])])


def _mla_weights(w_q_b, w_kv_b):
    wq = _pad_lanes(w_q_b.reshape(MLA_Q_LORA, MLA_HEADS, MLA_QK), LANE).reshape(MLA_Q_LORA, MLA_HEADS * LANE)
    kvb = w_kv_b.reshape(MLA_KV_LORA, MLA_HEADS, MLA_NOPE + MLA_V)
    wk = _pad_lanes(kvb[:, :, :MLA_NOPE], LANE).reshape(MLA_KV_LORA, MLA_HEADS * LANE)
    wv = kvb[:, :, MLA_NOPE:].reshape(MLA_KV_LORA, MLA_HEADS * MLA_V)
    return wq.astype(BF16), wk.astype(BF16), wv.astype(BF16)


def _undup_heads(a):
    s = a.shape[:-1]
    return a.reshape(s + (GQA_KV_HEADS, 2, GQA_HEAD_DIM))[..., 0, :]


def _state_to_kernel(h0):
    b = h0.shape[0]
    t = jnp.transpose(h0, (0, 3, 1, 2)).reshape(b, SSD_STATE, SSD_INNER)
    half = SSD_INNER // SSD_GROUPS
    col = jnp.arange(SSD_INNER) // half
    parts = [jnp.where(col == g, t, 0.0) for g in range(SSD_GROUPS)]
    return jnp.concatenate(parts, axis=1)


def _state_from_kernel(st):
    return st.reshape(st.shape[0], SSD_HEADS, SSD_HEADDIM, SSD_STATE)


def kernel(x_prompt, x_sample, c, state_ssd_fwd, state_ssd_bwd, cache_mla_ckv, cache_mla_kpe, cache_gqa_k, cache_gqa_v, c_ctx, g_mix, g_ffn, w_ada, b_ada, w_router, router_bias, w_exp_gate, w_exp_up, w_exp_down, w_in_even, ssd_conv_w, ssd_conv_b, ssd_a_log_fwd, ssd_a_log_bwd, ssd_dt_bias_fwd, ssd_dt_bias_bwd, ssd_d, ssd_norm, mla_q_a_norm, mla_w_q_b, mla_kv_a_norm, mla_w_kv_b, mla_q_norm, mla_k_norm, w_out_even, w_in_odd, gqa_q_norm, gqa_k_norm, w_out_odd):
    nb, ls, _ = x_prompt.shape
    db, dl, _ = x_sample.shape
    depth = w_ada.shape[0]
    n_ctx = nb * ls

    xs = [x_prompt.reshape(n_ctx, D), x_sample.reshape(db * dl, D)]
    seqs = (ls, dl)
    rows = COND_ROWS
    cond = jnp.concatenate([c_ctx[None, :], c, jnp.zeros((rows - 1 - db, D), F32)], axis=0)
    mod_all = _ada_mod(cond, w_ada, b_ada).reshape(depth, rows, 1, 6 * D)
    g_mix3 = g_mix.reshape(depth, 1, D)
    w_in_even_t = jnp.swapaxes(w_in_even, 1, 2)
    g_ffn3 = g_ffn.reshape(depth, 1, D)

    wr = w_router
    rb = router_bias.reshape(N_EXPERTS, 1)
    wg_all, wu_all, wd_all = w_exp_gate, w_exp_up, w_exp_down

    outs = {}
    for i in range(depth):
        j = i // 2
        mods = [(mod_all, i, 0, 1), (mod_all, i, 1, db)]
        if i % 2 == 0:
            wq, wk, wv = _mla_weights(mla_w_q_b[j], mla_w_kv_b[j])
            cw = jnp.pad(ssd_conv_w[j], ((0, 5), (0, 0)))
            cb = ssd_conv_b[j].reshape(1, SSD_CONV_DIM)
            hp = _pad_lanes(jnp.stack([ssd_a_log_fwd[j], ssd_a_log_bwd[j], ssd_dt_bias_fwd[j],
                                       ssd_dt_bias_bwd[j], ssd_d[j]]), LANE)
            hp = jnp.pad(hp, ((0, 3), (0, 0)))
            nw = ssd_norm[j].reshape(1, SSD_INNER)
            qan = mla_q_a_norm[j].reshape(1, MLA_Q_LORA)
            kvn = mla_kv_a_norm[j].reshape(1, MLA_KV_LORA)
            qn = _pad_lanes(mla_q_norm[j].reshape(1, MLA_QK) * (MLA_QK ** -0.5 * LOG2E), LANE)
            kn = _pad_lanes(mla_k_norm[j].reshape(1, MLA_QK), LANE)
            h0 = (_state_to_kernel(state_ssd_fwd[:, j]), _state_to_kernel(state_ssd_bwd[:, j]))
            ckpe = jnp.pad(cache_mla_kpe[:, j], ((0, 0), (0, 0), (MLA_NOPE, LANE - MLA_QK)))
            tables = tuple(jnp.asarray(np.tile(tb, (1, 2))) for tb in _rope_tables(dl, MLA_ROPE, MLA_NOPE, 1))
            w_out = w_out_even
            ys = []
            for s in range(2):
                z, xbc, qa, kv = _inproj(xs[s], mods[s], g_mix3, i, w_in_even_t, j, EVEN_WIDTHS, EVEN_PIECES, transposed=True)
                dt = kv
                if s == 0:
                    y, sf, sb = _ssd(z, xbc, dt, cw, cb, hp, nw, None, seq=seqs[s])
                    o, ckv_new, kpe_new = _mla(qa, kv, qan, kvn, wq, wk, wv, qn, kn, None, None, seq=seqs[s])
                    outs.setdefault("ssd_f", []).append(_state_from_kernel(sf))
                    outs.setdefault("ssd_b", []).append(_state_from_kernel(sb))
                    outs.setdefault("ckv", []).append(ckv_new.reshape(nb, ls, MLA_KV_LORA))
                    outs.setdefault("kpe", []).append(kpe_new[:, MLA_NOPE:MLA_QK].reshape(nb, ls, MLA_ROPE))
                else:
                    y, = _ssd(z, xbc, dt, cw, cb, hp, nw, h0, seq=seqs[s])
                    o, = _mla(qa, kv, qan, kvn, wq, wk, wv, qn, kn, (cache_mla_ckv[:, j], ckpe), tables,
                              seq=seqs[s])
                ys.append((y, o))
        else:
            nq = GQA_HEADS * GQA_HEAD_DIM
            nkv = GQA_KV_HEADS * GQA_HEAD_DIM
            nk = GQA_KV_HEADS * LANE
            qscale = GQA_HEAD_DIM ** -0.5 * LOG2E
            qn = jnp.tile(gqa_q_norm[j].reshape(1, GQA_HEAD_DIM) * qscale, (1, nq // GQA_HEAD_DIM))
            kn = jnp.tile(gqa_k_norm[j].reshape(1, GQA_HEAD_DIM), (1, nk // GQA_HEAD_DIM))
            cache = (cache_gqa_k[:, j].reshape(db, -1, nkv), cache_gqa_v[:, j].reshape(db, -1, nkv))
            tables = tuple(jnp.asarray(np.tile(tb, (1, 2))) for tb in _rope_tables(dl, GQA_HEAD_DIM, 0, 2))
            w_out = w_out_odd
            ys = []
            for s in range(2):
                q, k, v = _inproj(xs[s], mods[s], g_mix3, i, w_in_odd, j, ODD_WIDTHS, ODD_PIECES)
                if s == 0:
                    o, k_new = _gqa(q, k, v, qn, kn, None, None, seq=seqs[s])
                    outs.setdefault("gk", []).append(
                        _undup_heads(k_new).reshape(nb, ls, GQA_KV_HEADS, GQA_HEAD_DIM))
                    outs.setdefault("gv", []).append(
                        _undup_heads(v).reshape(nb, ls, GQA_KV_HEADS, GQA_HEAD_DIM))
                else:
                    o, = _gqa(q, k, v, qn, kn, cache, tables, seq=seqs[s])
                ys.append((o,))
        for s in range(2):
            x1, h2, routed, cnt = _outproj(xs[s], mods[s], ys[s], w_out, j, g_ffn3, i, wr, rb)
            xs[s] = _moe(x1, mods[s], h2, routed, cnt, wg_all, wu_all, wd_all, i)

    stack = lambda key: jnp.stack(outs[key], axis=1)
    return (xs[0].reshape(nb, ls, D), xs[1].reshape(db, dl, D),
            stack("ssd_f"), stack("ssd_b"), stack("ckv"), stack("kpe"), stack("gk"), stack("gv"))
```
